```python
import math
import jax, jax.numpy as jnp
from jax import lax
import numpy as np

D_MODEL = 1024
BATCH = 4
SEQ = 4096
DEPTH = 1

PLE_DIM = 256
RET_HEADS = 8
RET_HEAD_DIM = 64
RET_WIDTH = RET_HEADS * RET_HEAD_DIM
RET_CHUNK = 128
ROPE_BASE = 10000.0
GN_EPS = 1e-6
NA_HEADS = 8
NA_HEAD_DIM = 64
NA_WIDTH = NA_HEADS * NA_HEAD_DIM
GRID_W = 64
NA_WIN_ROWS_MAX = 8
NA_WIN_COLS = 16
MIX_WIDTH = RET_WIDTH + NA_WIDTH
IN_PROJ_WIDTH = 4 * RET_WIDTH + 3 * NA_WIDTH
N_EXPERTS = 64
TOP_K = 8
N_GROUPS = 8
TOPK_GROUPS = 4
EXPERT_DIM = 256
SHARED_DIM = 256
ROUTED_SCALE = 2.5
MOE_BLOCK = 128
LN_EPS = 1e-5

kernel_name = "hybrid_retention_natten_moe_deepnorm"


def _layer_norm(x, g, b):
    xf = x.astype(jnp.float32)
    mu = jnp.mean(xf, -1, keepdims=True)
    var = jnp.mean(jnp.square(xf - mu), -1, keepdims=True)
    return ((xf - mu) * lax.rsqrt(var + LN_EPS) * g.astype(jnp.float32)
            + b.astype(jnp.float32)).astype(x.dtype)


def _rotary(t, pos):
    half = t.shape[-1] // 2
    inv = ROPE_BASE ** (-jnp.arange(half, dtype=jnp.float32) / half)
    ang = pos.astype(jnp.float32)[:, None] * inv[None, :]
    cos = jnp.cos(ang)[None, :, None, :].astype(t.dtype)
    sin = jnp.sin(ang)[None, :, None, :].astype(t.dtype)
    t1, t2 = t[..., :half], t[..., half:]
    return jnp.concatenate([t1 * cos - t2 * sin, t1 * sin + t2 * cos], -1)


def _retention_dir(q, k, v, log_gamma, strict):
    B, H, S, d = q.shape
    C = RET_CHUNK
    NC = S // C
    qc = q.reshape(B, H, NC, C, d)
    kc = k.reshape(B, H, NC, C, d)
    vc = v.reshape(B, H, NC, C, d)
    i = jnp.arange(C, dtype=jnp.float32)
    diff = i[:, None] - i[None, :]
    mask = (diff > 0) if strict else (diff >= 0)
    lg = log_gamma[:, None, None]
    d_intra = jnp.where(mask[None], jnp.exp(jnp.where(mask, diff, 0.0)[None] * lg), 0.0)
    scores = jnp.einsum('bhncd,bhnjd->bhncj', qc, kc) * d_intra[None, :, None]
    inner = jnp.einsum('bhncj,bhnje->bhnce', scores, vc)
    k_decay = jnp.exp((C - 1.0 - i)[None, :] * log_gamma[:, None])
    q_decay = jnp.exp((i + 1.0)[None, :] * log_gamma[:, None])
    chunk_decay = jnp.exp(C * log_gamma)[None, :, None, None]
    s_chunk = jnp.einsum('bhncd,hc,bhnce->nbhde', kc, k_decay, vc)

    def step(state, s_c):
        return chunk_decay * state + s_c, state

    _, prev = lax.scan(step, jnp.zeros_like(s_chunk[0]), s_chunk)
    cross = jnp.einsum('bhncd,hc,nbhde->bhnce', qc, q_decay, prev)
    return (inner + cross).reshape(B, H, S, d)


def _retention_group(rq, rk, rv, rg, pos, dec_f, dec_b, gn_gain):
    B, S, _ = rq.shape
    dt = rq.dtype
    q = _rotary(rq.reshape(B, S, RET_HEADS, RET_HEAD_DIM), pos)
    k = _rotary(rk.reshape(B, S, RET_HEADS, RET_HEAD_DIM), pos) * (RET_HEAD_DIM ** -0.5)
    v = rv.reshape(B, S, RET_HEADS, RET_HEAD_DIM)
    q, k, v = (t.transpose(0, 2, 1, 3).astype(jnp.float32) for t in (q, k, v))
    lg_f = jax.nn.log_sigmoid(dec_f.astype(jnp.float32))
    lg_b = jax.nn.log_sigmoid(dec_b.astype(jnp.float32))
    y_f = _retention_dir(q, k, v, lg_f, False)
    y_b = _retention_dir(q[:, :, ::-1], k[:, :, ::-1], v[:, :, ::-1], lg_b, True)[:, :, ::-1]
    y = y_f + y_b
    mu = jnp.mean(y, -1, keepdims=True)
    var = jnp.mean(jnp.square(y - mu), -1, keepdims=True)
    y = (y - mu) * lax.rsqrt(var + GN_EPS)
    y = y.transpose(0, 2, 1, 3).reshape(B, S, RET_WIDTH) * gn_gain.astype(jnp.float32)
    return jax.nn.silu(rg) * y.astype(dt)


def _neighbour_index(rows):
    wr = min(NA_WIN_ROWS_MAX, rows)
    wc = NA_WIN_COLS
    r = np.arange(rows)
    c = np.arange(GRID_W)
    rs = np.clip(r - wr // 2, 0, rows - wr)
    cs = np.clip(c - wc // 2, 0, GRID_W - wc)
    key_r = rs[:, None] + np.arange(wr)[None, :]
    key_c = cs[:, None] + np.arange(wc)[None, :]
    idx = key_r[:, None, :, None] * GRID_W + key_c[None, :, None, :]
    br = key_r - r[:, None] + (NA_WIN_ROWS_MAX - 1)
    bc = key_c - c[:, None] + (NA_WIN_COLS - 1)
    return (jnp.asarray(idx.reshape(rows, GRID_W, wr * wc), jnp.int32),
            jnp.asarray(br, jnp.int32), bc)


def _neighbourhood_attention(nq, nk, nv, rpb):
    B, S, _ = nq.shape
    H, d = NA_HEADS, NA_HEAD_DIM
    rows = S // GRID_W
    idx, br, bc = _neighbour_index(rows)
    qg = nq.reshape(B, rows, GRID_W, H, d).transpose(1, 0, 3, 2, 4)
    kh = nk.reshape(B, S, H, d).transpose(0, 2, 1, 3)
    vh = nv.reshape(B, S, H, d).transpose(0, 2, 1, 3)
    scale = d ** -0.5

    def row_block(args):
        q_r, idx_r, br_r = args
        k_r = kh[:, :, idx_r]
        v_r = vh[:, :, idx_r]
        bias = rpb[:, br_r][:, :, bc]
        bias = bias.transpose(0, 2, 1, 3).reshape(H, GRID_W, -1)
        s = (jnp.einsum('bhqd,bhqnd->bhqn', q_r, k_r).astype(jnp.float32) * scale
             + bias.astype(jnp.float32)[None])
        a = jax.nn.softmax(s, axis=-1).astype(v_r.dtype)
        return jnp.einsum('bhqn,bhqnd->bhqd', a, v_r)

    out = lax.map(row_block, (qg, idx, br))
    return out.transpose(1, 0, 3, 2, 4).reshape(B, S, NA_WIDTH)


def _hybrid_mixer(x, pos, w_in, dec_f, dec_b, gn_gain, rpb, w_out):
    proj = x @ w_in
    cuts = list(np.cumsum([RET_WIDTH] * 4 + [NA_WIDTH] * 2))
    rq, rk, rv, rg, nq, nk, nv = jnp.split(proj, cuts, axis=-1)
    ret_out = _retention_group(rq, rk, rv, rg, pos, dec_f, dec_b, gn_gain)
    na_out = _neighbourhood_attention(nq, nk, nv, rpb)
    return jnp.concatenate([ret_out, na_out], axis=-1) @ w_out


def _swiglu(h, w_gu, w_down):
    g, u = jnp.split(h @ w_gu, 2, axis=-1)
    return (jax.nn.silu(g) * u) @ w_down


def _route(xt, w_router, router_bias):
    T = xt.shape[0]
    scores = jax.nn.sigmoid(xt.astype(jnp.float32) @ w_router.astype(jnp.float32))
    biased = scores + router_bias.astype(jnp.float32)[None]
    grp = biased.reshape(T, N_GROUPS, N_EXPERTS // N_GROUPS)
    grp_score = lax.top_k(grp, 2)[0].sum(-1)
    _, top_g = lax.top_k(grp_score, TOPK_GROUPS)
    gmask = jnp.any(top_g[..., None] == jnp.arange(N_GROUPS)[None, None, :], axis=1)
    emask = jnp.repeat(gmask, N_EXPERTS // N_GROUPS, axis=-1)
    _, idx = lax.top_k(jnp.where(emask, biased, -jnp.inf), TOP_K)
    w = jnp.take_along_axis(scores, idx, axis=-1)
    w = w / jnp.sum(w, -1, keepdims=True) * ROUTED_SCALE
    return idx, w


def _routed_experts(xt, idx, w, w_gu, w_down):
    T, D = xt.shape
    A = T * TOP_K
    nb = -(-A // MOE_BLOCK) + N_EXPERTS
    n_slots = nb * MOE_BLOCK
    flat_e = idx.reshape(-1).astype(jnp.int32)
    flat_t = jnp.repeat(jnp.arange(T, dtype=jnp.int32), TOP_K)
    flat_w = w.reshape(-1)
    order = jnp.argsort(flat_e)
    e_sorted = flat_e[order]
    counts = jnp.bincount(flat_e, length=N_EXPERTS).astype(jnp.int32)
    start = jnp.cumsum(counts) - counts
    padded = (counts + MOE_BLOCK - 1) // MOE_BLOCK * MOE_BLOCK
    padded_end = jnp.cumsum(padded)
    padded_start = padded_end - padded
    dest = padded_start[e_sorted] + jnp.arange(A, dtype=jnp.int32) - start[e_sorted]
    slot_t = jnp.full((n_slots,), T, jnp.int32).at[dest].set(flat_t[order])
    slot_w = jnp.zeros((n_slots,), jnp.float32).at[dest].set(flat_w[order])
    block_e = jnp.minimum(
        jnp.searchsorted(padded_end, jnp.arange(nb, dtype=jnp.int32) * MOE_BLOCK, side='right'),
        N_EXPERTS - 1)
    x_pad = jnp.concatenate([xt, jnp.zeros((1, D), xt.dtype)], axis=0)

    def run_block(args):
        t_b, e_b = args
        return _swiglu(x_pad[t_b], w_gu[e_b], w_down[e_b])

    y = lax.map(run_block, (slot_t.reshape(nb, MOE_BLOCK), block_e)).reshape(n_slots, D)
    y = y * slot_w[:, None].astype(y.dtype)
    return jax.ops.segment_sum(y, slot_t, num_segments=T + 1)[:T]


def _moe(x, w_router, router_bias, w_e_gu, w_e_down, w_s_gu, w_s_down):
    B, S, D = x.shape
    xt = x.reshape(B * S, D)
    idx, w = _route(xt, w_router, router_bias)
    y = _routed_experts(xt, idx, w, w_e_gu, w_e_down) + _swiglu(xt, w_s_gu, w_s_down)
    return y.reshape(B, S, D)


def setup_inputs(seed: int = 0) -> dict:
    key = jax.random.key(seed)
    ks = jax.random.split(key, 20)
    f32 = jnp.float32
    beta = (8.0 * DEPTH) ** -0.25
    nrm = lambda k, s: jax.random.normal(k, s, f32)
    x = nrm(ks[0], (BATCH, SEQ, D_MODEL))
    p = nrm(ks[1], (DEPTH, BATCH, SEQ, PLE_DIM))
    col_scale = jnp.concatenate([
        jnp.ones((2 * RET_WIDTH,), f32), jnp.full((RET_WIDTH,), beta, f32),
        jnp.ones((RET_WIDTH,), f32), jnp.ones((2 * NA_WIDTH,), f32),
        jnp.full((NA_WIDTH,), beta, f32)])
    w_in = nrm(ks[2], (DEPTH, D_MODEL, IN_PROJ_WIDTH)) * (D_MODEL ** -0.5) * col_scale
    gamma0 = 1.0 - 2.0 ** (-5.0 - np.arange(RET_HEADS))
    logit0 = jnp.asarray(np.log(gamma0 / (1.0 - gamma0)), f32)
    ret_decay_fwd = logit0[None] + 0.01 * nrm(ks[3], (DEPTH, RET_HEADS))
    ret_decay_bwd = logit0[None] + 0.01 * nrm(ks[4], (DEPTH, RET_HEADS))
    ret_gn_gain = 1.0 + 0.02 * nrm(ks[5], (DEPTH, RET_WIDTH))
    na_rpb = 0.02 * nrm(ks[6], (DEPTH, NA_HEADS, 2 * NA_WIN_ROWS_MAX - 1, 2 * NA_WIN_COLS - 1))
    w_out = nrm(ks[7], (DEPTH, MIX_WIDTH, D_MODEL)) * (MIX_WIDTH ** -0.5) * beta
    ln1_gain = 1.0 + 0.02 * nrm(ks[8], (DEPTH, D_MODEL))
    ln1_bias = 0.02 * nrm(ks[9], (DEPTH, D_MODEL))
    w_router = nrm(ks[10], (DEPTH, D_MODEL, N_EXPERTS)) * (D_MODEL ** -0.5)
    router_bias = 0.01 * nrm(ks[11], (DEPTH, N_EXPERTS))
    w_expert_gu = nrm(ks[12], (DEPTH, N_EXPERTS, D_MODEL, 2 * EXPERT_DIM)) * (D_MODEL ** -0.5)
    w_expert_down = nrm(ks[13], (DEPTH, N_EXPERTS, EXPERT_DIM, D_MODEL)) * (EXPERT_DIM ** -0.5) * beta
    w_shared_gu = nrm(ks[14], (DEPTH, D_MODEL, 2 * SHARED_DIM)) * (D_MODEL ** -0.5)
    w_shared_down = nrm(ks[15], (DEPTH, SHARED_DIM, D_MODEL)) * (SHARED_DIM ** -0.5) * beta
    w_ple_proj = nrm(ks[16], (DEPTH, PLE_DIM, D_MODEL)) * (PLE_DIM ** -0.5) * beta
    w_ple_gate = nrm(ks[17], (DEPTH, D_MODEL, D_MODEL)) * (D_MODEL ** -0.5)
    ln2_gain = 1.0 + 0.02 * nrm(ks[18], (DEPTH, D_MODEL))
    ln2_bias = 0.02 * nrm(ks[19], (DEPTH, D_MODEL))
    return {"x": x, "p": p, "w_in": w_in, "ret_decay_fwd": ret_decay_fwd,
            "ret_decay_bwd": ret_decay_bwd, "ret_gn_gain": ret_gn_gain, "na_rpb": na_rpb,
            "w_out": w_out, "ln1_gain": ln1_gain, "ln1_bias": ln1_bias,
            "w_router": w_router, "router_bias": router_bias,
            "w_expert_gu": w_expert_gu, "w_expert_down": w_expert_down,
            "w_shared_gu": w_shared_gu, "w_shared_down": w_shared_down,
            "w_ple_proj": w_ple_proj, "w_ple_gate": w_ple_gate,
            "ln2_gain": ln2_gain, "ln2_bias": ln2_bias}


def reference(x, p, w_in, ret_decay_fwd, ret_decay_bwd, ret_gn_gain, na_rpb, w_out,
              ln1_gain, ln1_bias, w_router, router_bias, w_expert_gu, w_expert_down,
              w_shared_gu, w_shared_down, w_ple_proj, w_ple_gate, ln2_gain, ln2_bias):
    S = x.shape[1]
    alpha = (2.0 * DEPTH) ** 0.25
    pos = jnp.arange(S, dtype=jnp.int32)
    for i in range(DEPTH):
        mix = _hybrid_mixer(x, pos, w_in[i], ret_decay_fwd[i], ret_decay_bwd[i],
                            ret_gn_gain[i], na_rpb[i], w_out[i])
        x = _layer_norm(alpha * x + mix, ln1_gain[i], ln1_bias[i])
        ffn = _moe(x, w_router[i], router_bias[i], w_expert_gu[i], w_expert_down[i],
                   w_shared_gu[i], w_shared_down[i])
        ple = (p[i].astype(x.dtype) @ w_ple_proj[i]) * jax.nn.sigmoid(x @ w_ple_gate[i])
        x = _layer_norm(alpha * x + ffn + ple, ln2_gain[i], ln2_bias[i])
    return x
```

```python
import functools

import numpy as np
import jax
import jax.numpy as jnp
from jax import lax
from jax.experimental import pallas as pl
from jax.experimental.pallas import tpu as pltpu

F32 = jnp.float32
BF16 = jnp.bfloat16
I32 = jnp.int32

D_MODEL = 1024
HEADS = 8
HEAD_DIM = 64
GROUP_W = HEADS * HEAD_DIM
ROPE_BASE = 10000.0
GN_EPS = 1e-6
LN_EPS = 1e-5
GRID_W = 64
NA_WIN_ROWS = 8
NA_WIN_COLS = 16
N_EXPERTS = 64
N_GROUPS = 8
GROUP_SIZE = N_EXPERTS // N_GROUPS
TOPK_GROUPS = 4
TOP_K = 8
EXPERT_DIM = 256
ROUTED_SCALE = 2.5
ALPHA = 2.0 ** 0.25
NEG_BIG = -1e30

LANES = 128
VMEM_LIMIT_BYTES = 56 * 1024 * 1024

PROJ_TM = 512
RET_CHUNK = 128
NA_ROWS = 4
NA_KEY_ROWS = NA_ROWS + NA_WIN_ROWS
MOE_TM = 1024
MOE_CAP = 128
MOE_EB = 2


def _cparams(sem):
    return pltpu.CompilerParams(dimension_semantics=sem, vmem_limit_bytes=VMEM_LIMIT_BYTES)


def _dot(a, b):
    return jnp.dot(a, b, preferred_element_type=F32)


def _dot_nt(a, b):
    return lax.dot_general(a, b, (((1,), (1,)), ((), ())), preferred_element_type=F32)


def _dot_tn(a, b):
    return lax.dot_general(a, b, (((0,), (0,)), ((), ())), preferred_element_type=F32)


def _in_proj_kernel(x_ref, w_ref, cos_ref, sin_ref,
                    rq_ref, rk_ref, rv_ref, rg_ref, nq_ref, nk_ref, nv_ref):
    xb = x_ref[...].astype(BF16)
    cos = cos_ref[...]
    sin = sin_ref[...]
    lane = lax.broadcasted_iota(I32, (1, LANES), 1)
    first_half = (lane % HEAD_DIM) < (HEAD_DIM // 2)

    def proj(g):
        return _dot(xb, w_ref[:, g * GROUP_W:(g + 1) * GROUP_W])

    def rotary(t, scale):
        outs = []
        for j in range(GROUP_W // LANES):
            c = t[:, j * LANES:(j + 1) * LANES]
            swapped = jnp.where(first_half,
                                pltpu.roll(c, LANES - HEAD_DIM // 2, axis=1),
                                pltpu.roll(c, HEAD_DIM // 2, axis=1))
            outs.append((c * cos + swapped * sin) * scale)
        return jnp.concatenate(outs, axis=1)

    rq_ref[...] = rotary(proj(0), 1.0).astype(BF16)
    rk_ref[...] = rotary(proj(1), HEAD_DIM ** -0.5).astype(BF16)
    rv_ref[...] = proj(2).astype(BF16)
    rg_ref[...] = jax.nn.silu(proj(3)).astype(BF16)
    nq_ref[...] = proj(4).astype(BF16)
    nk_ref[...] = proj(5).astype(BF16)
    nv_ref[...] = proj(6).astype(BF16)


def _in_proj(x2, w_in_b, cos_t, sin_t, seq):
    t = x2.shape[0]
    tm = PROJ_TM
    n_pos = seq // tm
    out = jax.ShapeDtypeStruct((t, GROUP_W), BF16)
    tok = lambda i: (i, 0)
    return pl.pallas_call(
        _in_proj_kernel,
        grid=(t // tm,),
        in_specs=[
            pl.BlockSpec((tm, D_MODEL), tok),
            pl.BlockSpec(w_in_b.shape, lambda i: (0, 0)),
            pl.BlockSpec((tm, LANES), lambda i: (i % n_pos, 0)),
            pl.BlockSpec((tm, LANES), lambda i: (i % n_pos, 0)),
        ],
        out_specs=[pl.BlockSpec((tm, GROUP_W), tok)] * 7,
        out_shape=[out] * 7,
        compiler_params=_cparams(("parallel",)),
        name="in_proj",
    )(x2, w_in_b, cos_t, sin_t)


def _log_sigmoid(x):
    return jnp.minimum(x, 0.0) - jnp.log1p(jnp.exp(-jnp.abs(x)))


def _retention_kernel(q_ref, k_ref, v_ref, g_ref, decf_ref, decb_ref, decfd_ref, decbd_ref,
                      gain_ref, o_ref, kv_ref, st_ref, dmat_ref):
    c = RET_CHUNK
    n_chunks = q_ref.shape[0] // c
    pair_w = 2 * HEAD_DIM

    lgf = _log_sigmoid(decf_ref[...])
    lgb = _log_sigmoid(decb_ref[...])
    row = lax.broadcasted_iota(I32, (c, 1), 0).astype(F32)
    k_dec_f = jnp.exp((c - 1.0 - row) * lgf)
    k_dec_b = jnp.exp(row * lgb)
    q_dec_f = jnp.exp((row + 1.0) * lgf)
    q_dec_b = jnp.exp((c - row) * lgb)
    chunk_dec_f = jnp.exp(c * lgf)
    chunk_dec_b = jnp.exp(c * lgb)

    lane = lax.broadcasted_iota(I32, (1, pair_w), 1)
    head0 = lane < HEAD_DIM
    r2 = lax.broadcasted_iota(I32, (pair_w, pair_w), 0) // HEAD_DIM
    c2 = lax.broadcasted_iota(I32, (pair_w, pair_w), 1) // HEAD_DIM
    same_head = r2 == c2
    block_diag = jnp.where(same_head, 1.0, 0.0)
    seg_avg = jnp.where(same_head, 1.0 / HEAD_DIM, 0.0).astype(BF16)

    lgf_d = _log_sigmoid(decfd_ref[0])
    lgb_d = _log_sigmoid(decbd_ref[0])
    di = lax.broadcasted_iota(I32, (c, 2 * c), 0)
    dj = lax.broadcasted_iota(I32, (c, 2 * c), 1) % c
    diff = (di - dj).astype(F32)
    dmat_ref[...] = jnp.where(diff >= 0.0, jnp.exp(diff * lgf_d), jnp.exp(-diff * lgb_d))

    def chunk(ref, n):
        return ref[pl.ds(pl.multiple_of(n * c, c), c), :]

    def summarize(n, carry):
        kf = chunk(k_ref, n).astype(F32)
        kst = jnp.concatenate([kf * k_dec_f, kf * k_dec_b], axis=1).astype(BF16)
        kv_ref[n] = _dot_tn(kst, chunk(v_ref, n))
        return carry

    lax.fori_loop(0, n_chunks, summarize, 0)

    def fwd_scan(n, state):
        st_ref[n, 0:pair_w, :] = state.astype(BF16)
        return state * chunk_dec_f + kv_ref[n, 0:pair_w, :] * block_diag

    lax.fori_loop(0, n_chunks, fwd_scan, jnp.zeros((pair_w, pair_w), F32))

    def bwd_scan(i, state):
        n = n_chunks - 1 - i
        st_ref[n, pair_w:2 * pair_w, :] = state.astype(BF16)
        return state * chunk_dec_b + kv_ref[n, pair_w:2 * pair_w, :] * block_diag

    lax.fori_loop(0, n_chunks, bwd_scan, jnp.zeros((pair_w, pair_w), F32))

    gain = gain_ref[...]

    def seg_mean(z):
        hi = z.astype(BF16)
        lo = (z - hi.astype(F32)).astype(BF16)
        return _dot(hi, seg_avg) + _dot(lo, seg_avg)

    def emit(n, carry):
        q = chunk(q_ref, n)
        k = chunk(k_ref, n)
        v = chunk(v_ref, n)
        zero = jnp.zeros_like(k)
        k_st = jnp.concatenate([jnp.where(head0, k, zero), jnp.where(head0, zero, k)], axis=0)
        v_st = jnp.concatenate([jnp.where(head0, v, zero), jnp.where(head0, zero, v)], axis=0)
        scores = _dot_nt(q, k_st) * dmat_ref[...]
        y = _dot(scores.astype(BF16), v_st)
        qf = q.astype(F32)
        q_st = jnp.concatenate([qf * q_dec_f, qf * q_dec_b], axis=1).astype(BF16)
        y = y + _dot(q_st, st_ref[n])
        mu = seg_mean(y)
        d = y - mu
        var = seg_mean(d * d)
        yn = d * lax.rsqrt(var + GN_EPS) * gain
        o_ref[pl.ds(pl.multiple_of(n * c, c), c), :] = (chunk(g_ref, n).astype(F32) * yn).astype(BF16)
        return carry

    lax.fori_loop(0, n_chunks, emit, 0)


def _retention(rq, rk, rv, rg, dec_f, dec_b, gain, batch, seq):
    t = rq.shape[0]
    c = RET_CHUNK
    n_pairs = HEADS // 2
    pair_w = 2 * HEAD_DIM
    dec_f_lane = jnp.repeat(dec_f.astype(F32), HEAD_DIM)[None, :]
    dec_b_lane = jnp.repeat(dec_b.astype(F32), HEAD_DIM)[None, :]
    dec_f_col = jnp.repeat(dec_f.astype(F32), c).reshape(n_pairs, 1, 2 * c)
    dec_b_col = jnp.repeat(dec_b.astype(F32), c).reshape(n_pairs, 1, 2 * c)
    tok = pl.BlockSpec((seq, pair_w), lambda b, p: (b, p))
    lane_spec = pl.BlockSpec((1, pair_w), lambda b, p: (0, p))
    col_spec = pl.BlockSpec((1, 1, 2 * c), lambda b, p: (p, 0, 0))
    return pl.pallas_call(
        _retention_kernel,
        grid=(batch, n_pairs),
        in_specs=[tok, tok, tok, tok, lane_spec, lane_spec, col_spec, col_spec, lane_spec],
        out_specs=tok,
        out_shape=jax.ShapeDtypeStruct((t, GROUP_W), BF16),
        scratch_shapes=[
            pltpu.VMEM((seq // c, 2 * pair_w, pair_w), F32),
            pltpu.VMEM((seq // c, 2 * pair_w, pair_w), BF16),
            pltpu.VMEM((c, 2 * c), F32),
        ],
        compiler_params=_cparams(("parallel", "parallel")),
        name="retention",
    )(rq, rk, rv, rg, dec_f_lane, dec_b_lane, dec_f_col, dec_b_col, gain.astype(F32)[None, :])


def _natten_bias_index(rows):
    n_blocks = rows // NA_ROWS
    starts = {0: 0, 1: NA_ROWS - NA_WIN_ROWS // 2, 2: rows - NA_KEY_ROWS}
    blocks = {0: 0, 1: 1, 2: n_blocks - 1}
    a = np.arange(NA_ROWS)[:, None, None, None]
    c = np.arange(GRID_W)[None, :, None, None]
    kl = np.arange(NA_KEY_ROWS)[None, None, :, None]
    kc = np.arange(GRID_W)[None, None, None, :]
    cs = np.clip(c - NA_WIN_COLS // 2, 0, GRID_W - NA_WIN_COLS)
    col_ok = (kc >= cs) & (kc < cs + NA_WIN_COLS)
    dc = np.clip(kc - c + NA_WIN_COLS - 1, 0, 2 * NA_WIN_COLS - 2)
    idx, ok = [], []
    for v in range(3):
        r = blocks[v] * NA_ROWS + a
        rs = np.clip(r - NA_WIN_ROWS // 2, 0, rows - NA_WIN_ROWS)
        kr = starts[v] + kl
        row_ok = (kr >= rs) & (kr < rs + NA_WIN_ROWS)
        dr = np.clip(kr - r + NA_WIN_ROWS - 1, 0, 2 * NA_WIN_ROWS - 2)
        flat = dr * (2 * NA_WIN_COLS - 1) + dc
        shape = (NA_ROWS * GRID_W, NA_KEY_ROWS * GRID_W)
        idx.append(np.broadcast_to(flat, (NA_ROWS, GRID_W, NA_KEY_ROWS, GRID_W)).reshape(shape))
        ok.append(np.broadcast_to(row_ok & col_ok, (NA_ROWS, GRID_W, NA_KEY_ROWS, GRID_W)).reshape(shape))
    return np.stack(idx).astype(np.int32), np.stack(ok)


def _natten_kernel(q_ref, k_ref, v_ref, tab_ref, o_ref, *, rows):
    rb = pl.program_id(2)
    nk = NA_KEY_ROWS * GRID_W
    start_row = jnp.clip(rb * NA_ROWS - NA_WIN_ROWS // 2, 0, rows - NA_KEY_ROWS)
    start = pl.multiple_of(start_row * GRID_W, GRID_W)
    q = q_ref[...]
    k = k_ref[pl.ds(start, nk), :]
    v = v_ref[pl.ds(start, nk), :]
    lane = lax.broadcasted_iota(I32, (1, 2 * HEAD_DIM), 1)
    head0 = lane < HEAD_DIM
    zero = jnp.zeros_like(k)
    k_st = jnp.concatenate([jnp.where(head0, k, zero), jnp.where(head0, zero, k)], axis=0)
    v_st = jnp.concatenate([jnp.where(head0, v, zero), jnp.where(head0, zero, v)], axis=0)
    s = _dot_nt(q, k_st) * (HEAD_DIM ** -0.5) + tab_ref[0, 0]
    probs, denoms = [], []
    for h in range(2):
        sh = s[:, h * nk:(h + 1) * nk]
        m = jnp.max(sh, axis=1, keepdims=True)
        p = jnp.exp(sh - m)
        denoms.append(jnp.sum(p, axis=1, keepdims=True))
        probs.append(p.astype(BF16))
    out = _dot(jnp.concatenate(probs, axis=1), v_st)
    denom = jnp.where(head0, denoms[0], denoms[1])
    o_ref[...] = (out / denom).astype(BF16)


def _natten(nq, nk, nv, rpb, batch, seq):
    t = nq.shape[0]
    rows = seq // GRID_W
    n_pairs = HEADS // 2
    n_blocks = rows // NA_ROWS
    nqb = NA_ROWS * GRID_W
    nkb = NA_KEY_ROWS * GRID_W
    idx, ok = _natten_bias_index(rows)
    flat = rpb.astype(F32).reshape(HEADS, -1)
    tab = jnp.where(ok[None], jnp.take(flat, jnp.asarray(idx), axis=1), NEG_BIG)
    tab = tab.reshape(n_pairs, 2, 3, nqb, nkb).transpose(2, 0, 3, 1, 4).reshape(3, n_pairs, nqb, 2 * nkb)

    def variant(rb):
        return jnp.where(rb == 0, 0, jnp.where(rb == n_blocks - 1, 2, 1))

    kv_spec = pl.BlockSpec((seq, 2 * HEAD_DIM), lambda b, p, rb: (b, p))
    q_spec = pl.BlockSpec((nqb, 2 * HEAD_DIM), lambda b, p, rb: (b * n_blocks + rb, p))
    return pl.pallas_call(
        functools.partial(_natten_kernel, rows=rows),
        grid=(batch, n_pairs, n_blocks),
        in_specs=[q_spec, kv_spec, kv_spec,
                  pl.BlockSpec((1, 1, nqb, 2 * nkb), lambda b, p, rb: (variant(rb), p, 0, 0))],
        out_specs=q_spec,
        out_shape=jax.ShapeDtypeStruct((t, GROUP_W), BF16),
        compiler_params=_cparams(("parallel", "parallel", "arbitrary")),
        name="natten",
    )(nq, nk, nv, tab)


def _layer_norm(h, gain, bias):
    mu = jnp.mean(h, axis=-1, keepdims=True)
    d = h - mu
    var = jnp.mean(d * d, axis=-1, keepdims=True)
    return d * lax.rsqrt(var + LN_EPS) * gain + bias


def _out_ln1_kernel(ret_ref, na_ref, x_ref, w_ref, gain_ref, bias_ref, x1_ref, x1b_ref):
    mix = _dot(ret_ref[...], w_ref[0:GROUP_W, :]) + _dot(na_ref[...], w_ref[GROUP_W:2 * GROUP_W, :])
    x1 = _layer_norm(ALPHA * x_ref[...] + mix, gain_ref[...], bias_ref[...])
    x1_ref[...] = x1
    x1b_ref[...] = x1.astype(BF16)


def _out_ln1(ret, na, x2, w_out_b, gain, bias):
    t = x2.shape[0]
    tm = PROJ_TM
    tok = lambda i: (i, 0)
    const = lambda i: (0, 0)
    return pl.pallas_call(
        _out_ln1_kernel,
        grid=(t // tm,),
        in_specs=[pl.BlockSpec((tm, GROUP_W), tok), pl.BlockSpec((tm, GROUP_W), tok),
                  pl.BlockSpec((tm, D_MODEL), tok), pl.BlockSpec(w_out_b.shape, const),
                  pl.BlockSpec((1, D_MODEL), const), pl.BlockSpec((1, D_MODEL), const)],
        out_specs=[pl.BlockSpec((tm, D_MODEL), tok), pl.BlockSpec((tm, D_MODEL), tok)],
        out_shape=[jax.ShapeDtypeStruct((t, D_MODEL), F32), jax.ShapeDtypeStruct((t, D_MODEL), BF16)],
        compiler_params=_cparams(("parallel",)),
        name="out_ln1",
    )(ret, na, x2, w_out_b, gain.astype(F32)[None, :], bias.astype(F32)[None, :])


def _route_kernel(x_ref, wr_ref, rb_ref, w_ref, rank_ref, cnt_ref, tri_ref):
    tm = x_ref.shape[0]

    @pl.when(pl.program_id(0) == 0)
    def _():
        i = lax.broadcasted_iota(I32, (tm, tm), 0)
        j = lax.broadcasted_iota(I32, (tm, tm), 1)
        tri_ref[...] = jnp.where(i < j, 1.0, 0.0).astype(BF16)

    scores = jax.nn.sigmoid(_dot_nt(wr_ref[...], x_ref[...]))
    biased = scores + rb_ref[...]
    sub = lax.broadcasted_iota(I32, (GROUP_SIZE, tm), 0).astype(F32)
    none = float(N_EXPERTS)
    ninf = -jnp.inf

    def first_max(vals, index):
        m = jnp.max(vals, axis=0, keepdims=True)
        return m, jnp.min(jnp.where(vals == m, index, none), axis=0, keepdims=True)

    groups = [biased[g * GROUP_SIZE:(g + 1) * GROUP_SIZE, :] for g in range(N_GROUPS)]
    group_scores = []
    for g in range(N_GROUPS):
        m1, i1 = first_max(groups[g], sub)
        m2 = jnp.max(jnp.where(sub == i1, ninf, groups[g]), axis=0, keepdims=True)
        group_scores.append(m1 + m2)
    cur = jnp.concatenate(group_scores, axis=0)
    group_sel = jnp.zeros(cur.shape, F32)
    for _ in range(TOPK_GROUPS):
        _, i1 = first_max(cur, sub)
        hit = sub == i1
        group_sel = jnp.where(hit, 1.0, group_sel)
        cur = jnp.where(hit, ninf, cur)

    masked = [jnp.where(group_sel[g:g + 1, :] > 0.5, groups[g], ninf) for g in range(N_GROUPS)]
    ids = [sub + float(g * GROUP_SIZE) for g in range(N_GROUPS)]
    chosen = [jnp.zeros((GROUP_SIZE, tm), F32) for _ in range(N_GROUPS)]
    for _ in range(TOP_K):
        m = masked[0]
        for g in range(1, N_GROUPS):
            m = jnp.maximum(m, masked[g])
        m = jnp.max(m, axis=0, keepdims=True)
        cand = jnp.where(masked[0] == m, ids[0], none)
        for g in range(1, N_GROUPS):
            cand = jnp.minimum(cand, jnp.where(masked[g] == m, ids[g], none))
        first = jnp.min(cand, axis=0, keepdims=True)
        for g in range(N_GROUPS):
            hit = ids[g] == first
            chosen[g] = jnp.where(hit, 1.0, chosen[g])
            masked[g] = jnp.where(hit, ninf, masked[g])

    sel = jnp.concatenate(chosen, axis=0) > 0.5
    picked = jnp.where(sel, scores, 0.0)
    total = jnp.sum(picked, axis=0, keepdims=True)
    w_ref[...] = picked / total * ROUTED_SCALE
    sel_b = jnp.where(sel, 1.0, 0.0).astype(BF16)
    before = _dot(sel_b, tri_ref[...])
    rank_ref[...] = jnp.where(sel, before.astype(I32), -1)
    cnt = jnp.sum(jnp.where(sel, 1.0, 0.0), axis=1, keepdims=True)
    cnt_ref[...] = jnp.broadcast_to(cnt, cnt_ref.shape).astype(I32)


def _route(x1b, w_router, router_bias):
    t = x1b.shape[0]
    tm = MOE_TM
    nt = t // tm
    wr_t = w_router.astype(F32).T.astype(BF16)
    const = lambda i: (0, 0)
    col = lambda i: (0, i)
    return pl.pallas_call(
        _route_kernel,
        grid=(nt,),
        in_specs=[pl.BlockSpec((tm, D_MODEL), lambda i: (i, 0)),
                  pl.BlockSpec((N_EXPERTS, D_MODEL), const),
                  pl.BlockSpec((N_EXPERTS, 1), const)],
        out_specs=[pl.BlockSpec((N_EXPERTS, tm), col), pl.BlockSpec((N_EXPERTS, tm), col),
                   pl.BlockSpec((N_EXPERTS, LANES), col)],
        out_shape=[jax.ShapeDtypeStruct((N_EXPERTS, t), F32),
                   jax.ShapeDtypeStruct((N_EXPERTS, t), I32),
                   jax.ShapeDtypeStruct((N_EXPERTS, nt * LANES), I32)],
        scratch_shapes=[pltpu.VMEM((tm, tm), BF16)],
        compiler_params=_cparams(("arbitrary",)),
        name="route",
    )(x1b, wr_t, router_bias.astype(F32)[:, None])


def _moe_kernel(cnt_ref, x_ref, rank_ref, w_ref, wgu_ref, wd_ref, o_ref):
    i = pl.program_id(0)
    eb = pl.program_id(1)
    tm = x_ref.shape[0]

    @pl.when(eb == 0)
    def _():
        o_ref[...] = jnp.zeros_like(o_ref)

    slot = lax.broadcasted_iota(I32, (MOE_CAP, tm), 0)
    for j in range(MOE_EB):
        count = cnt_ref[eb * MOE_EB + j, i]
        n_pass = lax.shift_right_logical(count + (MOE_CAP - 1), int(np.log2(MOE_CAP)))
        rank = rank_ref[j]
        weight = w_ref[j]

        def one_pass(p, carry, rank=rank, weight=weight, j=j):
            match = slot == (rank - p * MOE_CAP)
            onehot = jnp.where(match, 1.0, 0.0).astype(BF16)
            xc = _dot(onehot, x_ref[...]).astype(BF16)
            h = _dot(xc, wgu_ref[j])
            act = (jax.nn.silu(h[:, :EXPERT_DIM]) * h[:, EXPERT_DIM:]).astype(BF16)
            y = _dot(act, wd_ref[j])
            wc = jnp.sum(jnp.where(match, weight, 0.0), axis=1, keepdims=True)
            o_ref[...] += _dot_tn(onehot, (y * wc).astype(BF16))
            return carry

        lax.fori_loop(0, n_pass, one_pass, 0)


def _moe(x1b, rank_t, w_t, counts, wgu_b, wd_b):
    t = x1b.shape[0]
    tm = MOE_TM
    nt = t // tm
    rank3 = rank_t.reshape(N_EXPERTS, 1, t)
    w3 = w_t.reshape(N_EXPERTS, 1, t)
    grid_spec = pltpu.PrefetchScalarGridSpec(
        num_scalar_prefetch=1,
        grid=(nt, N_EXPERTS // MOE_EB),
        in_specs=[
            pl.BlockSpec((tm, D_MODEL), lambda i, e, c: (i, 0)),
            pl.BlockSpec((MOE_EB, 1, tm), lambda i, e, c: (e, 0, i)),
            pl.BlockSpec((MOE_EB, 1, tm), lambda i, e, c: (e, 0, i)),
            pl.BlockSpec((MOE_EB, D_MODEL, 2 * EXPERT_DIM), lambda i, e, c: (e, 0, 0)),
            pl.BlockSpec((MOE_EB, EXPERT_DIM, D_MODEL), lambda i, e, c: (e, 0, 0)),
        ],
        out_specs=pl.BlockSpec((tm, D_MODEL), lambda i, e, c: (i, 0)),
    )
    return pl.pallas_call(
        _moe_kernel,
        grid_spec=grid_spec,
        out_shape=jax.ShapeDtypeStruct((t, D_MODEL), F32),
        compiler_params=_cparams(("parallel", "arbitrary")),
        name="moe",
    )(counts, x1b, rank3, w3, wgu_b, wd_b)


def _final_kernel(x1_ref, ffn_ref, p_ref, wsgu_ref, wsd_ref, wp_ref, wg_ref, gain_ref, bias_ref, o_ref):
    x1 = x1_ref[...]
    xb = x1.astype(BF16)
    h = _dot(xb, wsgu_ref[...])
    act = (jax.nn.silu(h[:, :EXPERT_DIM]) * h[:, EXPERT_DIM:]).astype(BF16)
    shared = _dot(act, wsd_ref[...])
    ple = _dot(p_ref[...].astype(BF16), wp_ref[...]) * jax.nn.sigmoid(_dot(xb, wg_ref[...]))
    o_ref[...] = _layer_norm(ALPHA * x1 + (ffn_ref[...] + shared) + ple, gain_ref[...], bias_ref[...])


def _final(x1, ffn, p2, wsgu_b, wsd_b, wp_b, wg_b, gain, bias):
    t = x1.shape[0]
    tm = PROJ_TM
    tok = lambda i: (i, 0)
    const = lambda i: (0, 0)
    full = lambda a: pl.BlockSpec(a.shape, const)
    return pl.pallas_call(
        _final_kernel,
        grid=(t // tm,),
        in_specs=[pl.BlockSpec((tm, D_MODEL), tok), pl.BlockSpec((tm, D_MODEL), tok),
                  pl.BlockSpec((tm, p2.shape[1]), tok),
                  full(wsgu_b), full(wsd_b), full(wp_b), full(wg_b),
                  pl.BlockSpec((1, D_MODEL), const), pl.BlockSpec((1, D_MODEL), const)],
        out_specs=pl.BlockSpec((tm, D_MODEL), tok),
        out_shape=jax.ShapeDtypeStruct((t, D_MODEL), F32),
        compiler_params=_cparams(("parallel",)),
        name="final",
    )(x1, ffn, p2, wsgu_b, wsd_b, wp_b, wg_b, gain.astype(F32)[None, :], bias.astype(F32)[None, :])


def _rotary_tables(seq):
    half = HEAD_DIM // 2
    inv = ROPE_BASE ** (-jnp.arange(half, dtype=F32) / half)
    ang = jnp.arange(seq, dtype=jnp.int32).astype(F32)[:, None] * inv[None, :]
    cos, sin = jnp.cos(ang), jnp.sin(ang)
    reps = LANES // HEAD_DIM
    cos_t = jnp.tile(jnp.concatenate([cos, cos], axis=1), (1, reps))
    sin_t = jnp.tile(jnp.concatenate([-sin, sin], axis=1), (1, reps))
    return cos_t, sin_t


def kernel(x, p, w_in, ret_decay_fwd, ret_decay_bwd, ret_gn_gain, na_rpb, w_out, ln1_gain, ln1_bias,
           w_router, router_bias, w_expert_gu, w_expert_down, w_shared_gu, w_shared_down,
           w_ple_proj, w_ple_gate, ln2_gain, ln2_bias):
    batch, seq, d = x.shape
    t = batch * seq
    depth = w_in.shape[0]
    assert depth == 1 and d == D_MODEL
    assert seq % PROJ_TM == 0 and seq % MOE_TM == 0 and seq % RET_CHUNK == 0
    assert (seq // GRID_W) % NA_ROWS == 0 and seq // GRID_W >= NA_KEY_ROWS
    cos_t, sin_t = _rotary_tables(seq)
    x2 = x.reshape(t, d)
    for i in range(depth):
        rq, rk, rv, rg, nq, nk, nv = _in_proj(x2, w_in[i].astype(BF16), cos_t, sin_t, seq)
        ret = _retention(rq, rk, rv, rg, ret_decay_fwd[i], ret_decay_bwd[i], ret_gn_gain[i], batch, seq)
        na = _natten(nq, nk, nv, na_rpb[i], batch, seq)
        x1, x1b = _out_ln1(ret, na, x2, w_out[i].astype(BF16), ln1_gain[i], ln1_bias[i])
        w_t, rank_t, cnt = _route(x1b, w_router[i], router_bias[i])
        counts = cnt[:, ::LANES]
        ffn = _moe(x1b, rank_t, w_t, counts, w_expert_gu[i].astype(BF16), w_expert_down[i].astype(BF16))
        x2 = _final(x1, ffn, p[i].reshape(t, -1), w_shared_gu[i].astype(BF16), w_shared_down[i].astype(BF16),
                    w_ple_proj[i].astype(BF16), w_ple_gate[i].astype(BF16), ln2_gain[i], ln2_bias[i])
    return x2.reshape(batch, seq, d)
```

```python
import functools

import numpy as np
import jax
import jax.numpy as jnp
from jax import lax
from jax.experimental import pallas as pl
from jax.experimental.pallas import tpu as pltpu

F32 = jnp.float32
BF16 = jnp.bfloat16
I32 = jnp.int32

D_MODEL = 1024
HEADS = 8
HEAD_DIM = 64
GROUP_W = HEADS * HEAD_DIM
ROPE_BASE = 10000.0
GN_EPS = 1e-6
LN_EPS = 1e-5
GRID_W = 64
NA_WIN_ROWS = 8
NA_WIN_COLS = 16
N_EXPERTS = 64
N_GROUPS = 8
GROUP_SIZE = N_EXPERTS // N_GROUPS
TOPK_GROUPS = 4
TOP_K = 8
EXPERT_DIM = 256
ROUTED_SCALE = 2.5
ALPHA = 2.0 ** 0.25
NEG_BIG = -1e30

LANES = 128
VMEM_LIMIT_BYTES = 56 * 1024 * 1024

PROJ_TM = 512
RET_CHUNK = 128
NA_ROWS = 4
NA_KEY_ROWS = NA_ROWS + NA_WIN_ROWS
MOE_TM = 1024
MOE_CAP = 256
MOE_EB = 2


def _cparams(sem):
    return pltpu.CompilerParams(dimension_semantics=sem, vmem_limit_bytes=VMEM_LIMIT_BYTES)


def _dot(a, b):
    return jnp.dot(a, b, preferred_element_type=F32)


def _dot_nt(a, b):
    return lax.dot_general(a, b, (((1,), (1,)), ((), ())), preferred_element_type=F32)


def _dot_tn(a, b):
    return lax.dot_general(a, b, (((0,), (0,)), ((), ())), preferred_element_type=F32)


def _in_proj_kernel(x_ref, w_ref, cos_ref, sin_ref,
                    rq_ref, rk_ref, rv_ref, rg_ref, nq_ref, nk_ref, nv_ref):
    xb = x_ref[...].astype(BF16)
    cos = cos_ref[...]
    sin = sin_ref[...]
    lane = lax.broadcasted_iota(I32, (1, LANES), 1)
    first_half = (lane % HEAD_DIM) < (HEAD_DIM // 2)

    def proj(g):
        return _dot(xb, w_ref[:, g * GROUP_W:(g + 1) * GROUP_W])

    def rotary(t, scale):
        outs = []
        for j in range(GROUP_W // LANES):
            c = t[:, j * LANES:(j + 1) * LANES]
            swapped = jnp.where(first_half,
                                pltpu.roll(c, LANES - HEAD_DIM // 2, axis=1),
                                pltpu.roll(c, HEAD_DIM // 2, axis=1))
            outs.append((c * cos + swapped * sin) * scale)
        return jnp.concatenate(outs, axis=1)

    rq_ref[...] = rotary(proj(0), 1.0).astype(BF16)
    rk_ref[...] = rotary(proj(1), HEAD_DIM ** -0.5).astype(BF16)
    rv_ref[...] = proj(2).astype(BF16)
    rg_ref[...] = jax.nn.silu(proj(3)).astype(BF16)
    nq_ref[...] = proj(4).astype(BF16)
    nk_ref[...] = proj(5).astype(BF16)
    nv_ref[...] = proj(6).astype(BF16)


def _in_proj(x2, w_in_b, cos_t, sin_t, seq):
    t = x2.shape[0]
    tm = PROJ_TM
    n_pos = seq // tm
    out = jax.ShapeDtypeStruct((t, GROUP_W), BF16)
    tok = lambda i: (i, 0)
    return pl.pallas_call(
        _in_proj_kernel,
        grid=(t // tm,),
        in_specs=[
            pl.BlockSpec((tm, D_MODEL), tok),
            pl.BlockSpec(w_in_b.shape, lambda i: (0, 0)),
            pl.BlockSpec((tm, LANES), lambda i: (i % n_pos, 0)),
            pl.BlockSpec((tm, LANES), lambda i: (i % n_pos, 0)),
        ],
        out_specs=[pl.BlockSpec((tm, GROUP_W), tok)] * 7,
        out_shape=[out] * 7,
        compiler_params=_cparams(("parallel",)),
        name="in_proj",
    )(x2, w_in_b, cos_t, sin_t)


def _log_sigmoid(x):
    return jnp.minimum(x, 0.0) - jnp.log1p(jnp.exp(-jnp.abs(x)))


def _retention_kernel(q_ref, k_ref, v_ref, g_ref, decf_ref, decb_ref, decfd_ref, decbd_ref,
                      gain_ref, o_ref, kv_ref, st_ref, dmat_ref):
    c = RET_CHUNK
    n_chunks = q_ref.shape[0] // c
    pair_w = 2 * HEAD_DIM

    lgf = _log_sigmoid(decf_ref[...])
    lgb = _log_sigmoid(decb_ref[...])
    row = lax.broadcasted_iota(I32, (c, 1), 0).astype(F32)
    k_dec_f = jnp.exp((c - 1.0 - row) * lgf)
    k_dec_b = jnp.exp(row * lgb)
    q_dec_f = jnp.exp((row + 1.0) * lgf)
    q_dec_b = jnp.exp((c - row) * lgb)
    chunk_dec_f = jnp.exp(c * lgf)
    chunk_dec_b = jnp.exp(c * lgb)

    lane = lax.broadcasted_iota(I32, (1, pair_w), 1)
    head0 = lane < HEAD_DIM
    r2 = lax.broadcasted_iota(I32, (pair_w, pair_w), 0) // HEAD_DIM
    c2 = lax.broadcasted_iota(I32, (pair_w, pair_w), 1) // HEAD_DIM
    same_head = r2 == c2
    block_diag = jnp.where(same_head, 1.0, 0.0)
    seg_avg = jnp.where(same_head, 1.0 / HEAD_DIM, 0.0).astype(BF16)

    lgf_d = _log_sigmoid(decfd_ref[0])
    lgb_d = _log_sigmoid(decbd_ref[0])
    di = lax.broadcasted_iota(I32, (c, 2 * c), 0)
    dj = lax.broadcasted_iota(I32, (c, 2 * c), 1) % c
    diff = (di - dj).astype(F32)
    dmat_ref[...] = jnp.where(diff >= 0.0, jnp.exp(diff * lgf_d), jnp.exp(-diff * lgb_d))

    def chunk(ref, n):
        return ref[pl.ds(pl.multiple_of(n * c, c), c), :]

    def summarize(n, carry):
        kf = chunk(k_ref, n).astype(F32)
        kst = jnp.concatenate([kf * k_dec_f, kf * k_dec_b], axis=1).astype(BF16)
        kv_ref[n] = _dot_tn(kst, chunk(v_ref, n))
        return carry

    lax.fori_loop(0, n_chunks, summarize, 0)

    def fwd_scan(n, state):
        st_ref[n, 0:pair_w, :] = state.astype(BF16)
        return state * chunk_dec_f + kv_ref[n, 0:pair_w, :] * block_diag

    lax.fori_loop(0, n_chunks, fwd_scan, jnp.zeros((pair_w, pair_w), F32))

    def bwd_scan(i, state):
        n = n_chunks - 1 - i
        st_ref[n, pair_w:2 * pair_w, :] = state.astype(BF16)
        return state * chunk_dec_b + kv_ref[n, pair_w:2 * pair_w, :] * block_diag

    lax.fori_loop(0, n_chunks, bwd_scan, jnp.zeros((pair_w, pair_w), F32))

    gain = gain_ref[...]

    def seg_mean(z):
        hi = z.astype(BF16)
        lo = (z - hi.astype(F32)).astype(BF16)
        return _dot(hi, seg_avg) + _dot(lo, seg_avg)

    def emit(n, carry):
        q = chunk(q_ref, n)
        k = chunk(k_ref, n)
        v = chunk(v_ref, n)
        zero = jnp.zeros_like(k)
        k_st = jnp.concatenate([jnp.where(head0, k, zero), jnp.where(head0, zero, k)], axis=0)
        v_st = jnp.concatenate([jnp.where(head0, v, zero), jnp.where(head0, zero, v)], axis=0)
        scores = _dot_nt(q, k_st) * dmat_ref[...]
        y = _dot(scores.astype(BF16), v_st)
        qf = q.astype(F32)
        q_st = jnp.concatenate([qf * q_dec_f, qf * q_dec_b], axis=1).astype(BF16)
        y = y + _dot(q_st, st_ref[n])
        mu = seg_mean(y)
        d = y - mu
        var = seg_mean(d * d)
        yn = d * lax.rsqrt(var + GN_EPS) * gain
        o_ref[pl.ds(pl.multiple_of(n * c, c), c), :] = (chunk(g_ref, n).astype(F32) * yn).astype(BF16)
        return carry

    lax.fori_loop(0, n_chunks, emit, 0)


def _retention(rq, rk, rv, rg, dec_f, dec_b, gain, batch, seq):
    t = rq.shape[0]
    c = RET_CHUNK
    n_pairs = HEADS // 2
    pair_w = 2 * HEAD_DIM
    dec_f_lane = jnp.repeat(dec_f.astype(F32), HEAD_DIM)[None, :]
    dec_b_lane = jnp.repeat(dec_b.astype(F32), HEAD_DIM)[None, :]
    dec_f_col = jnp.repeat(dec_f.astype(F32), c).reshape(n_pairs, 1, 2 * c)
    dec_b_col = jnp.repeat(dec_b.astype(F32), c).reshape(n_pairs, 1, 2 * c)
    tok = pl.BlockSpec((seq, pair_w), lambda b, p: (b, p))
    lane_spec = pl.BlockSpec((1, pair_w), lambda b, p: (0, p))
    col_spec = pl.BlockSpec((1, 1, 2 * c), lambda b, p: (p, 0, 0))
    return pl.pallas_call(
        _retention_kernel,
        grid=(batch, n_pairs),
        in_specs=[tok, tok, tok, tok, lane_spec, lane_spec, col_spec, col_spec, lane_spec],
        out_specs=tok,
        out_shape=jax.ShapeDtypeStruct((t, GROUP_W), BF16),
        scratch_shapes=[
            pltpu.VMEM((seq // c, 2 * pair_w, pair_w), F32),
            pltpu.VMEM((seq // c, 2 * pair_w, pair_w), BF16),
            pltpu.VMEM((c, 2 * c), F32),
        ],
        compiler_params=_cparams(("parallel", "parallel")),
        name="retention",
    )(rq, rk, rv, rg, dec_f_lane, dec_b_lane, dec_f_col, dec_b_col, gain.astype(F32)[None, :])


N_ROW_OFFSETS = 2 * NA_WIN_ROWS - 1
N_COL_OFFSETS = 2 * NA_WIN_COLS - 1


def _natten_row_offsets(rows):
    n_blocks = rows // NA_ROWS
    starts = {0: 0, 1: NA_ROWS - NA_WIN_ROWS // 2, 2: rows - NA_KEY_ROWS}
    blocks = {0: 0, 1: 1, 2: n_blocks - 1}
    table = []
    for v in range(3):
        per_a = []
        for a in range(NA_ROWS):
            r = blocks[v] * NA_ROWS + a
            rs = min(max(r - NA_WIN_ROWS // 2, 0), rows - NA_WIN_ROWS)
            per_kl = []
            for kl in range(NA_KEY_ROWS):
                kr = starts[v] + kl
                per_kl.append(kr - r + NA_WIN_ROWS - 1 if rs <= kr < rs + NA_WIN_ROWS else None)
            per_a.append(per_kl)
        table.append(per_a)
    return table


def _natten_bias_kernel(rpb_ref, tab_ref, *, rows):
    offsets = _natten_row_offsets(rows)
    nk = NA_KEY_ROWS * GRID_W
    shape = (GRID_W, LANES)
    lane = lax.broadcasted_iota(I32, shape, 1)
    c = lax.broadcasted_iota(I32, shape, 0)
    second = lane >= GRID_W
    kc = lane % GRID_W
    cs = jnp.clip(c - NA_WIN_COLS // 2, 0, GRID_W - NA_WIN_COLS)
    col_ok = jnp.logical_and(kc >= cs, kc < cs + NA_WIN_COLS)
    neg = jnp.full(shape, NEG_BIG, F32)
    for hh in range(2):
        toeplitz = []
        for dr in range(N_ROW_OFFSETS):
            x = jnp.broadcast_to(rpb_ref[hh, dr:dr + 1, :], shape)
            lo = pltpu.roll(x, LANES - (NA_WIN_COLS - 1), axis=1, stride=1, stride_axis=0)
            hi = pltpu.roll(x, GRID_W - (NA_WIN_COLS - 1), axis=1, stride=1, stride_axis=0)
            toeplitz.append(jnp.where(second, hi, lo))
        for v in range(3):
            for a in range(NA_ROWS):
                for j in range(NA_KEY_ROWS // 2):
                    d0, d1 = offsets[v][a][2 * j], offsets[v][a][2 * j + 1]
                    if d0 is None and d1 is None:
                        piece = neg
                    else:
                        t0 = neg if d0 is None else toeplitz[d0]
                        t1 = neg if d1 is None else toeplitz[d1]
                        piece = jnp.where(col_ok, jnp.where(second, t1, t0), neg)
                    tab_ref[v, 0, a * GRID_W:(a + 1) * GRID_W,
                            hh * nk + j * LANES:hh * nk + (j + 1) * LANES] = piece


def _natten_bias_table(rpb, rows):
    n_pairs = HEADS // 2
    nqb = NA_ROWS * GRID_W
    nkb = NA_KEY_ROWS * GRID_W
    rpb_pad = jnp.pad(rpb.astype(F32), ((0, 0), (0, 0), (0, LANES - N_COL_OFFSETS)))
    return pl.pallas_call(
        functools.partial(_natten_bias_kernel, rows=rows),
        grid=(n_pairs,),
        in_specs=[pl.BlockSpec((2, N_ROW_OFFSETS, LANES), lambda p: (p, 0, 0))],
        out_specs=pl.BlockSpec((3, 1, nqb, 2 * nkb), lambda p: (0, p, 0, 0)),
        out_shape=jax.ShapeDtypeStruct((3, n_pairs, nqb, 2 * nkb), F32),
        compiler_params=_cparams(("parallel",)),
        name="natten_bias",
    )(rpb_pad)


def _natten_kernel(q_ref, k_ref, v_ref, tab_ref, o_ref, *, rows):
    rb = pl.program_id(2)
    nk = NA_KEY_ROWS * GRID_W
    start_row = jnp.clip(rb * NA_ROWS - NA_WIN_ROWS // 2, 0, rows - NA_KEY_ROWS)
    start = pl.multiple_of(start_row * GRID_W, GRID_W)
    q = q_ref[...]
    k = k_ref[pl.ds(start, nk), :]
    v = v_ref[pl.ds(start, nk), :]
    lane = lax.broadcasted_iota(I32, (1, 2 * HEAD_DIM), 1)
    head0 = lane < HEAD_DIM
    zero = jnp.zeros_like(k)
    k_st = jnp.concatenate([jnp.where(head0, k, zero), jnp.where(head0, zero, k)], axis=0)
    v_st = jnp.concatenate([jnp.where(head0, v, zero), jnp.where(head0, zero, v)], axis=0)
    s = _dot_nt(q, k_st) * (HEAD_DIM ** -0.5) + tab_ref[0, 0]
    probs, denoms = [], []
    for h in range(2):
        sh = s[:, h * nk:(h + 1) * nk]
        m = jnp.max(sh, axis=1, keepdims=True)
        p = jnp.exp(sh - m)
        denoms.append(jnp.sum(p, axis=1, keepdims=True))
        probs.append(p.astype(BF16))
    out = _dot(jnp.concatenate(probs, axis=1), v_st)
    denom = jnp.where(head0, denoms[0], denoms[1])
    o_ref[...] = (out / denom).astype(BF16)


def _natten(nq, nk, nv, rpb, batch, seq):
    t = nq.shape[0]
    rows = seq // GRID_W
    n_pairs = HEADS // 2
    n_blocks = rows // NA_ROWS
    nqb = NA_ROWS * GRID_W
    nkb = NA_KEY_ROWS * GRID_W
    tab = _natten_bias_table(rpb, rows)

    def variant(rb):
        return jnp.where(rb == 0, 0, jnp.where(rb == n_blocks - 1, 2, 1))

    kv_spec = pl.BlockSpec((seq, 2 * HEAD_DIM), lambda b, p, rb: (b, p))
    q_spec = pl.BlockSpec((nqb, 2 * HEAD_DIM), lambda b, p, rb: (b * n_blocks + rb, p))
    return pl.pallas_call(
        functools.partial(_natten_kernel, rows=rows),
        grid=(batch, n_pairs, n_blocks),
        in_specs=[q_spec, kv_spec, kv_spec,
                  pl.BlockSpec((1, 1, nqb, 2 * nkb), lambda b, p, rb: (variant(rb), p, 0, 0))],
        out_specs=q_spec,
        out_shape=jax.ShapeDtypeStruct((t, GROUP_W), BF16),
        compiler_params=_cparams(("parallel", "parallel", "arbitrary")),
        name="natten",
    )(nq, nk, nv, tab)


def _layer_norm(h, gain, bias):
    mu = jnp.mean(h, axis=-1, keepdims=True)
    d = h - mu
    var = jnp.mean(d * d, axis=-1, keepdims=True)
    return d * lax.rsqrt(var + LN_EPS) * gain + bias


def _out_ln1_kernel(ret_ref, na_ref, x_ref, w_ref, gain_ref, bias_ref, x1_ref, x1b_ref):
    mix = _dot(ret_ref[...], w_ref[0:GROUP_W, :]) + _dot(na_ref[...], w_ref[GROUP_W:2 * GROUP_W, :])
    x1 = _layer_norm(ALPHA * x_ref[...] + mix, gain_ref[...], bias_ref[...])
    x1_ref[...] = x1
    x1b_ref[...] = x1.astype(BF16)


def _out_ln1(ret, na, x2, w_out_b, gain, bias):
    t = x2.shape[0]
    tm = PROJ_TM
    tok = lambda i: (i, 0)
    const = lambda i: (0, 0)
    return pl.pallas_call(
        _out_ln1_kernel,
        grid=(t // tm,),
        in_specs=[pl.BlockSpec((tm, GROUP_W), tok), pl.BlockSpec((tm, GROUP_W), tok),
                  pl.BlockSpec((tm, D_MODEL), tok), pl.BlockSpec(w_out_b.shape, const),
                  pl.BlockSpec((1, D_MODEL), const), pl.BlockSpec((1, D_MODEL), const)],
        out_specs=[pl.BlockSpec((tm, D_MODEL), tok), pl.BlockSpec((tm, D_MODEL), tok)],
        out_shape=[jax.ShapeDtypeStruct((t, D_MODEL), F32), jax.ShapeDtypeStruct((t, D_MODEL), BF16)],
        compiler_params=_cparams(("parallel",)),
        name="out_ln1",
    )(ret, na, x2, w_out_b, gain.astype(F32)[None, :], bias.astype(F32)[None, :])


def _route_kernel(x_ref, wr_ref, rb_ref, w_ref, rank_ref, cnt_ref, tri_ref):
    tm = x_ref.shape[0]

    @pl.when(pl.program_id(0) == 0)
    def _():
        i = lax.broadcasted_iota(I32, (tm, tm), 0)
        j = lax.broadcasted_iota(I32, (tm, tm), 1)
        tri_ref[...] = jnp.where(i < j, 1.0, 0.0).astype(BF16)

    scores = jax.nn.sigmoid(_dot_nt(wr_ref[...], x_ref[...]))
    biased = scores + rb_ref[...]
    sub = lax.broadcasted_iota(I32, (GROUP_SIZE, tm), 0).astype(F32)
    none = float(N_EXPERTS)
    ninf = -jnp.inf

    def first_max(vals, index):
        m = jnp.max(vals, axis=0, keepdims=True)
        return m, jnp.min(jnp.where(vals == m, index, none), axis=0, keepdims=True)

    groups = [biased[g * GROUP_SIZE:(g + 1) * GROUP_SIZE, :] for g in range(N_GROUPS)]
    group_scores = []
    for g in range(N_GROUPS):
        m1, i1 = first_max(groups[g], sub)
        m2 = jnp.max(jnp.where(sub == i1, ninf, groups[g]), axis=0, keepdims=True)
        group_scores.append(m1 + m2)
    cur = jnp.concatenate(group_scores, axis=0)
    group_sel = jnp.zeros(cur.shape, F32)
    for _ in range(TOPK_GROUPS):
        _, i1 = first_max(cur, sub)
        hit = sub == i1
        group_sel = jnp.where(hit, 1.0, group_sel)
        cur = jnp.where(hit, ninf, cur)

    masked = [jnp.where(group_sel[g:g + 1, :] > 0.5, groups[g], ninf) for g in range(N_GROUPS)]
    ids = [sub + float(g * GROUP_SIZE) for g in range(N_GROUPS)]
    chosen = [jnp.zeros((GROUP_SIZE, tm), F32) for _ in range(N_GROUPS)]
    for _ in range(TOP_K):
        m = masked[0]
        for g in range(1, N_GROUPS):
            m = jnp.maximum(m, masked[g])
        m = jnp.max(m, axis=0, keepdims=True)
        cand = jnp.where(masked[0] == m, ids[0], none)
        for g in range(1, N_GROUPS):
            cand = jnp.minimum(cand, jnp.where(masked[g] == m, ids[g], none))
        first = jnp.min(cand, axis=0, keepdims=True)
        for g in range(N_GROUPS):
            hit = ids[g] == first
            chosen[g] = jnp.where(hit, 1.0, chosen[g])
            masked[g] = jnp.where(hit, ninf, masked[g])

    sel = jnp.concatenate(chosen, axis=0) > 0.5
    picked = jnp.where(sel, scores, 0.0)
    total = jnp.sum(picked, axis=0, keepdims=True)
    w_ref[...] = picked / total * ROUTED_SCALE
    sel_b = jnp.where(sel, 1.0, 0.0).astype(BF16)
    before = _dot(sel_b, tri_ref[...])
    rank_ref[...] = jnp.where(sel, before.astype(I32), -1)
    cnt = jnp.sum(jnp.where(sel, 1.0, 0.0), axis=1, keepdims=True)
    cnt_ref[...] = jnp.broadcast_to(cnt, cnt_ref.shape).astype(I32)


def _route(x1b, w_router, router_bias):
    t = x1b.shape[0]
    tm = MOE_TM
    nt = t // tm
    wr_t = w_router.astype(F32).T.astype(BF16)
    const = lambda i: (0, 0)
    col = lambda i: (0, i)
    return pl.pallas_call(
        _route_kernel,
        grid=(nt,),
        in_specs=[pl.BlockSpec((tm, D_MODEL), lambda i: (i, 0)),
                  pl.BlockSpec((N_EXPERTS, D_MODEL), const),
                  pl.BlockSpec((N_EXPERTS, 1), const)],
        out_specs=[pl.BlockSpec((N_EXPERTS, tm), col), pl.BlockSpec((N_EXPERTS, tm), col),
                   pl.BlockSpec((N_EXPERTS, LANES), col)],
        out_shape=[jax.ShapeDtypeStruct((N_EXPERTS, t), F32),
                   jax.ShapeDtypeStruct((N_EXPERTS, t), I32),
                   jax.ShapeDtypeStruct((N_EXPERTS, nt * LANES), I32)],
        scratch_shapes=[pltpu.VMEM((tm, tm), BF16)],
        compiler_params=_cparams(("arbitrary",)),
        name="route",
    )(x1b, wr_t, router_bias.astype(F32)[:, None])


def _moe_kernel(cnt_ref, x_ref, rank_ref, w_ref, wgu_ref, wd_ref, o_ref):
    i = pl.program_id(0)
    eb = pl.program_id(1)
    tm = x_ref.shape[0]

    @pl.when(eb == 0)
    def _():
        o_ref[...] = jnp.zeros_like(o_ref)

    slot = lax.broadcasted_iota(I32, (MOE_CAP, tm), 0)
    for j in range(MOE_EB):
        count = cnt_ref[eb * MOE_EB + j, i]
        n_pass = lax.shift_right_logical(count + (MOE_CAP - 1), int(np.log2(MOE_CAP)))
        rank = rank_ref[j]
        weight = w_ref[j]

        def one_pass(p, carry, rank=rank, weight=weight, j=j):
            match = slot == (rank - p * MOE_CAP)
            onehot = jnp.where(match, 1.0, 0.0).astype(BF16)
            xc = _dot(onehot, x_ref[...]).astype(BF16)
            h = _dot(xc, wgu_ref[j])
            act = (jax.nn.silu(h[:, :EXPERT_DIM]) * h[:, EXPERT_DIM:]).astype(BF16)
            y = _dot(act, wd_ref[j])
            wc = jnp.sum(jnp.where(match, weight, 0.0), axis=1, keepdims=True)
            o_ref[...] += _dot_tn(onehot, (y * wc).astype(BF16))
            return carry

        lax.fori_loop(0, n_pass, one_pass, 0)


def _moe(x1b, rank_t, w_t, counts, wgu_b, wd_b):
    t = x1b.shape[0]
    tm = MOE_TM
    nt = t // tm
    rank3 = rank_t.reshape(N_EXPERTS, 1, t)
    w3 = w_t.reshape(N_EXPERTS, 1, t)
    grid_spec = pltpu.PrefetchScalarGridSpec(
        num_scalar_prefetch=1,
        grid=(nt, N_EXPERTS // MOE_EB),
        in_specs=[
            pl.BlockSpec((tm, D_MODEL), lambda i, e, c: (i, 0)),
            pl.BlockSpec((MOE_EB, 1, tm), lambda i, e, c: (e, 0, i)),
            pl.BlockSpec((MOE_EB, 1, tm), lambda i, e, c: (e, 0, i)),
            pl.BlockSpec((MOE_EB, D_MODEL, 2 * EXPERT_DIM), lambda i, e, c: (e, 0, 0)),
            pl.BlockSpec((MOE_EB, EXPERT_DIM, D_MODEL), lambda i, e, c: (e, 0, 0)),
        ],
        out_specs=pl.BlockSpec((tm, D_MODEL), lambda i, e, c: (i, 0)),
    )
    return pl.pallas_call(
        _moe_kernel,
        grid_spec=grid_spec,
        out_shape=jax.ShapeDtypeStruct((t, D_MODEL), F32),
        compiler_params=_cparams(("parallel", "arbitrary")),
        name="moe",
    )(counts, x1b, rank3, w3, wgu_b, wd_b)


def _final_kernel(x1_ref, ffn_ref, p_ref, wsgu_ref, wsd_ref, wp_ref, wg_ref, gain_ref, bias_ref, o_ref):
    x1 = x1_ref[...]
    xb = x1.astype(BF16)
    h = _dot(xb, wsgu_ref[...])
    act = (jax.nn.silu(h[:, :EXPERT_DIM]) * h[:, EXPERT_DIM:]).astype(BF16)
    shared = _dot(act, wsd_ref[...])
    ple = _dot(p_ref[...].astype(BF16), wp_ref[...]) * jax.nn.sigmoid(_dot(xb, wg_ref[...]))
    o_ref[...] = _layer_norm(ALPHA * x1 + (ffn_ref[...] + shared) + ple, gain_ref[...], bias_ref[...])


def _final(x1, ffn, p2, wsgu_b, wsd_b, wp_b, wg_b, gain, bias):
    t = x1.shape[0]
    tm = PROJ_TM
    tok = lambda i: (i, 0)
    const = lambda i: (0, 0)
    full = lambda a: pl.BlockSpec(a.shape, const)
    return pl.pallas_call(
        _final_kernel,
        grid=(t // tm,),
        in_specs=[pl.BlockSpec((tm, D_MODEL), tok), pl.BlockSpec((tm, D_MODEL), tok),
                  pl.BlockSpec((tm, p2.shape[1]), tok),
                  full(wsgu_b), full(wsd_b), full(wp_b), full(wg_b),
                  pl.BlockSpec((1, D_MODEL), const), pl.BlockSpec((1, D_MODEL), const)],
        out_specs=pl.BlockSpec((tm, D_MODEL), tok),
        out_shape=jax.ShapeDtypeStruct((t, D_MODEL), F32),
        compiler_params=_cparams(("parallel",)),
        name="final",
    )(x1, ffn, p2, wsgu_b, wsd_b, wp_b, wg_b, gain.astype(F32)[None, :], bias.astype(F32)[None, :])


def _rotary_tables(seq):
    half = HEAD_DIM // 2
    inv = ROPE_BASE ** (-jnp.arange(half, dtype=F32) / half)
    ang = jnp.arange(seq, dtype=jnp.int32).astype(F32)[:, None] * inv[None, :]
    cos, sin = jnp.cos(ang), jnp.sin(ang)
    reps = LANES // HEAD_DIM
    cos_t = jnp.tile(jnp.concatenate([cos, cos], axis=1), (1, reps))
    sin_t = jnp.tile(jnp.concatenate([-sin, sin], axis=1), (1, reps))
    return cos_t, sin_t


def kernel(x, p, w_in, ret_decay_fwd, ret_decay_bwd, ret_gn_gain, na_rpb, w_out, ln1_gain, ln1_bias,
           w_router, router_bias, w_expert_gu, w_expert_down, w_shared_gu, w_shared_down,
           w_ple_proj, w_ple_gate, ln2_gain, ln2_bias):
    batch, seq, d = x.shape
    t = batch * seq
    depth = w_in.shape[0]
    assert depth == 1 and d == D_MODEL
    assert seq % PROJ_TM == 0 and seq % MOE_TM == 0 and seq % RET_CHUNK == 0
    assert (seq // GRID_W) % NA_ROWS == 0 and seq // GRID_W >= NA_KEY_ROWS
    cos_t, sin_t = _rotary_tables(seq)
    x2 = x.reshape(t, d)
    for i in range(depth):
        rq, rk, rv, rg, nq, nk, nv = _in_proj(x2, w_in[i].astype(BF16), cos_t, sin_t, seq)
        ret = _retention(rq, rk, rv, rg, ret_decay_fwd[i], ret_decay_bwd[i], ret_gn_gain[i], batch, seq)
        na = _natten(nq, nk, nv, na_rpb[i], batch, seq)
        x1, x1b = _out_ln1(ret, na, x2, w_out[i].astype(BF16), ln1_gain[i], ln1_bias[i])
        w_t, rank_t, cnt = _route(x1b, w_router[i], router_bias[i])
        counts = cnt[:, ::LANES]
        ffn = _moe(x1b, rank_t, w_t, counts, w_expert_gu[i].astype(BF16), w_expert_down[i].astype(BF16))
        x2 = _final(x1, ffn, p[i].reshape(t, -1), w_shared_gu[i].astype(BF16), w_shared_down[i].astype(BF16),
                    w_ple_proj[i].astype(BF16), w_ple_gate[i].astype(BF16), ln2_gain[i], ln2_bias[i])
    return x2.reshape(batch, seq, d)
```

```python
import functools

import numpy as np
import jax
import jax.numpy as jnp
from jax import lax
from jax.experimental import pallas as pl
from jax.experimental.pallas import tpu as pltpu

F32 = jnp.float32
BF16 = jnp.bfloat16
I32 = jnp.int32

D_MODEL = 1024
HEADS = 8
HEAD_DIM = 64
GROUP_W = HEADS * HEAD_DIM
ROPE_BASE = 10000.0
GN_EPS = 1e-6
LN_EPS = 1e-5
GRID_W = 64
NA_WIN_ROWS = 8
NA_WIN_COLS = 16
N_EXPERTS = 64
N_GROUPS = 8
GROUP_SIZE = N_EXPERTS // N_GROUPS
TOPK_GROUPS = 4
TOP_K = 8
EXPERT_DIM = 256
ROUTED_SCALE = 2.5
ALPHA = 2.0 ** 0.25
NEG_BIG = -1e30

LANES = 128
VMEM_LIMIT_BYTES = 56 * 1024 * 1024

PROJ_TM = 512
RET_CHUNK = 128
NA_ROWS = 4
NA_KEY_ROWS = NA_ROWS + NA_WIN_ROWS
MOE_TM = 1024
MOE_SUB = 256
MOE_EB = 4
MOE_SLOTS = MOE_SUB // MOE_EB
MOE_NSUB = MOE_TM // MOE_SUB


def _cparams(sem):
    return pltpu.CompilerParams(dimension_semantics=sem, vmem_limit_bytes=VMEM_LIMIT_BYTES)


def _dot(a, b):
    return jnp.dot(a, b, preferred_element_type=F32)


def _dot_nt(a, b):
    return lax.dot_general(a, b, (((1,), (1,)), ((), ())), preferred_element_type=F32)


def _dot_tn(a, b):
    return lax.dot_general(a, b, (((0,), (0,)), ((), ())), preferred_element_type=F32)


def _in_proj_kernel(x_ref, w_ref, cos_ref, sin_ref,
                    rq_ref, rk_ref, rv_ref, rg_ref, nq_ref, nk_ref, nv_ref):
    xb = x_ref[...].astype(BF16)
    cos = cos_ref[...]
    sin = sin_ref[...]
    lane = lax.broadcasted_iota(I32, (1, LANES), 1)
    first_half = (lane % HEAD_DIM) < (HEAD_DIM // 2)

    def proj(g):
        return _dot(xb, w_ref[:, g * GROUP_W:(g + 1) * GROUP_W])

    def rotary(t, scale):
        outs = []
        for j in range(GROUP_W // LANES):
            c = t[:, j * LANES:(j + 1) * LANES]
            swapped = jnp.where(first_half,
                                pltpu.roll(c, LANES - HEAD_DIM // 2, axis=1),
                                pltpu.roll(c, HEAD_DIM // 2, axis=1))
            outs.append((c * cos + swapped * sin) * scale)
        return jnp.concatenate(outs, axis=1)

    rq_ref[...] = rotary(proj(0), 1.0).astype(BF16)
    rk_ref[...] = rotary(proj(1), HEAD_DIM ** -0.5).astype(BF16)
    rv_ref[...] = proj(2).astype(BF16)
    rg_ref[...] = jax.nn.silu(proj(3)).astype(BF16)
    nq_ref[...] = proj(4).astype(BF16)
    nk_ref[...] = proj(5).astype(BF16)
    nv_ref[...] = proj(6).astype(BF16)


def _in_proj(x2, w_in_b, cos_t, sin_t, seq):
    t = x2.shape[0]
    tm = PROJ_TM
    n_pos = seq // tm
    out = jax.ShapeDtypeStruct((t, GROUP_W), BF16)
    tok = lambda i: (i, 0)
    return pl.pallas_call(
        _in_proj_kernel,
        grid=(t // tm,),
        in_specs=[
            pl.BlockSpec((tm, D_MODEL), tok),
            pl.BlockSpec(w_in_b.shape, lambda i: (0, 0)),
            pl.BlockSpec((tm, LANES), lambda i: (i % n_pos, 0)),
            pl.BlockSpec((tm, LANES), lambda i: (i % n_pos, 0)),
        ],
        out_specs=[pl.BlockSpec((tm, GROUP_W), tok)] * 7,
        out_shape=[out] * 7,
        compiler_params=_cparams(("parallel",)),
        name="in_proj",
    )(x2, w_in_b, cos_t, sin_t)


def _log_sigmoid(x):
    return jnp.minimum(x, 0.0) - jnp.log1p(jnp.exp(-jnp.abs(x)))


def _retention_kernel(q_ref, k_ref, v_ref, g_ref, decf_ref, decb_ref, decfd_ref, decbd_ref,
                      gain_ref, o_ref, kv_ref, st_ref, dmat_ref):
    c = RET_CHUNK
    n_chunks = q_ref.shape[0] // c
    pair_w = 2 * HEAD_DIM

    lgf = _log_sigmoid(decf_ref[...])
    lgb = _log_sigmoid(decb_ref[...])
    row = lax.broadcasted_iota(I32, (c, 1), 0).astype(F32)
    k_dec_f = jnp.exp((c - 1.0 - row) * lgf)
    k_dec_b = jnp.exp(row * lgb)
    q_dec_f = jnp.exp((row + 1.0) * lgf)
    q_dec_b = jnp.exp((c - row) * lgb)
    chunk_dec_f = jnp.exp(c * lgf)
    chunk_dec_b = jnp.exp(c * lgb)

    lane = lax.broadcasted_iota(I32, (1, pair_w), 1)
    head0 = lane < HEAD_DIM
    r2 = lax.broadcasted_iota(I32, (pair_w, pair_w), 0) // HEAD_DIM
    c2 = lax.broadcasted_iota(I32, (pair_w, pair_w), 1) // HEAD_DIM
    same_head = r2 == c2
    block_diag = jnp.where(same_head, 1.0, 0.0)
    seg_avg = jnp.where(same_head, 1.0 / HEAD_DIM, 0.0).astype(BF16)

    lgf_d = _log_sigmoid(decfd_ref[0])
    lgb_d = _log_sigmoid(decbd_ref[0])
    di = lax.broadcasted_iota(I32, (c, 2 * c), 0)
    dj = lax.broadcasted_iota(I32, (c, 2 * c), 1) % c
    diff = (di - dj).astype(F32)
    dmat_ref[...] = jnp.where(diff >= 0.0, jnp.exp(diff * lgf_d), jnp.exp(-diff * lgb_d))

    def chunk(ref, n):
        return ref[pl.ds(pl.multiple_of(n * c, c), c), :]

    def summarize(n, carry):
        kf = chunk(k_ref, n).astype(F32)
        kst = jnp.concatenate([kf * k_dec_f, kf * k_dec_b], axis=1).astype(BF16)
        kv_ref[n] = _dot_tn(kst, chunk(v_ref, n))
        return carry

    lax.fori_loop(0, n_chunks, summarize, 0)

    def fwd_scan(n, state):
        st_ref[n, 0:pair_w, :] = state.astype(BF16)
        return state * chunk_dec_f + kv_ref[n, 0:pair_w, :] * block_diag

    lax.fori_loop(0, n_chunks, fwd_scan, jnp.zeros((pair_w, pair_w), F32))

    def bwd_scan(i, state):
        n = n_chunks - 1 - i
        st_ref[n, pair_w:2 * pair_w, :] = state.astype(BF16)
        return state * chunk_dec_b + kv_ref[n, pair_w:2 * pair_w, :] * block_diag

    lax.fori_loop(0, n_chunks, bwd_scan, jnp.zeros((pair_w, pair_w), F32))

    gain = gain_ref[...]

    def seg_mean(z):
        hi = z.astype(BF16)
        lo = (z - hi.astype(F32)).astype(BF16)
        return _dot(hi, seg_avg) + _dot(lo, seg_avg)

    def emit(n, carry):
        q = chunk(q_ref, n)
        k = chunk(k_ref, n)
        v = chunk(v_ref, n)
        zero = jnp.zeros_like(k)
        k_st = jnp.concatenate([jnp.where(head0, k, zero), jnp.where(head0, zero, k)], axis=0)
        v_st = jnp.concatenate([jnp.where(head0, v, zero), jnp.where(head0, zero, v)], axis=0)
        scores = _dot_nt(q, k_st) * dmat_ref[...]
        y = _dot(scores.astype(BF16), v_st)
        qf = q.astype(F32)
        q_st = jnp.concatenate([qf * q_dec_f, qf * q_dec_b], axis=1).astype(BF16)
        y = y + _dot(q_st, st_ref[n])
        mu = seg_mean(y)
        d = y - mu
        var = seg_mean(d * d)
        yn = d * lax.rsqrt(var + GN_EPS) * gain
        o_ref[pl.ds(pl.multiple_of(n * c, c), c), :] = (chunk(g_ref, n).astype(F32) * yn).astype(BF16)
        return carry

    lax.fori_loop(0, n_chunks, emit, 0)


def _retention(rq, rk, rv, rg, dec_f, dec_b, gain, batch, seq):
    t = rq.shape[0]
    c = RET_CHUNK
    n_pairs = HEADS // 2
    pair_w = 2 * HEAD_DIM
    dec_f_lane = jnp.repeat(dec_f.astype(F32), HEAD_DIM)[None, :]
    dec_b_lane = jnp.repeat(dec_b.astype(F32), HEAD_DIM)[None, :]
    dec_f_col = jnp.repeat(dec_f.astype(F32), c).reshape(n_pairs, 1, 2 * c)
    dec_b_col = jnp.repeat(dec_b.astype(F32), c).reshape(n_pairs, 1, 2 * c)
    tok = pl.BlockSpec((seq, pair_w), lambda b, p: (b, p))
    lane_spec = pl.BlockSpec((1, pair_w), lambda b, p: (0, p))
    col_spec = pl.BlockSpec((1, 1, 2 * c), lambda b, p: (p, 0, 0))
    return pl.pallas_call(
        _retention_kernel,
        grid=(batch, n_pairs),
        in_specs=[tok, tok, tok, tok, lane_spec, lane_spec, col_spec, col_spec, lane_spec],
        out_specs=tok,
        out_shape=jax.ShapeDtypeStruct((t, GROUP_W), BF16),
        scratch_shapes=[
            pltpu.VMEM((seq // c, 2 * pair_w, pair_w), F32),
            pltpu.VMEM((seq // c, 2 * pair_w, pair_w), BF16),
            pltpu.VMEM((c, 2 * c), F32),
        ],
        compiler_params=_cparams(("parallel", "parallel")),
        name="retention",
    )(rq, rk, rv, rg, dec_f_lane, dec_b_lane, dec_f_col, dec_b_col, gain.astype(F32)[None, :])


N_ROW_OFFSETS = 2 * NA_WIN_ROWS - 1
N_COL_OFFSETS = 2 * NA_WIN_COLS - 1


def _natten_row_offsets(rows):
    n_blocks = rows // NA_ROWS
    starts = {0: 0, 1: NA_ROWS - NA_WIN_ROWS // 2, 2: rows - NA_KEY_ROWS}
    blocks = {0: 0, 1: 1, 2: n_blocks - 1}
    table = []
    for v in range(3):
        per_a = []
        for a in range(NA_ROWS):
            r = blocks[v] * NA_ROWS + a
            rs = min(max(r - NA_WIN_ROWS // 2, 0), rows - NA_WIN_ROWS)
            per_kl = []
            for kl in range(NA_KEY_ROWS):
                kr = starts[v] + kl
                per_kl.append(kr - r + NA_WIN_ROWS - 1 if rs <= kr < rs + NA_WIN_ROWS else None)
            per_a.append(per_kl)
        table.append(per_a)
    return table


def _natten_bias_kernel(rpb_ref, tab_ref, *, rows):
    offsets = _natten_row_offsets(rows)
    nk = NA_KEY_ROWS * GRID_W
    shape = (GRID_W, LANES)
    lane = lax.broadcasted_iota(I32, shape, 1)
    c = lax.broadcasted_iota(I32, shape, 0)
    second = lane >= GRID_W
    kc = lane % GRID_W
    cs = jnp.clip(c - NA_WIN_COLS // 2, 0, GRID_W - NA_WIN_COLS)
    col_ok = jnp.logical_and(kc >= cs, kc < cs + NA_WIN_COLS)
    neg = jnp.full(shape, NEG_BIG, F32)
    for hh in range(2):
        toeplitz = []
        for dr in range(N_ROW_OFFSETS):
            x = jnp.broadcast_to(rpb_ref[hh, dr:dr + 1, :], shape)
            lo = pltpu.roll(x, LANES - (NA_WIN_COLS - 1), axis=1, stride=1, stride_axis=0)
            hi = pltpu.roll(x, GRID_W - (NA_WIN_COLS - 1), axis=1, stride=1, stride_axis=0)
            toeplitz.append(jnp.where(second, hi, lo))
        for v in range(3):
            for a in range(NA_ROWS):
                for j in range(NA_KEY_ROWS // 2):
                    d0, d1 = offsets[v][a][2 * j], offsets[v][a][2 * j + 1]
                    if d0 is None and d1 is None:
                        piece = neg
                    else:
                        t0 = neg if d0 is None else toeplitz[d0]
                        t1 = neg if d1 is None else toeplitz[d1]
                        piece = jnp.where(col_ok, jnp.where(second, t1, t0), neg)
                    tab_ref[v, 0, a * GRID_W:(a + 1) * GRID_W,
                            hh * nk + j * LANES:hh * nk + (j + 1) * LANES] = piece


def _natten_bias_table(rpb, rows):
    n_pairs = HEADS // 2
    nqb = NA_ROWS * GRID_W
    nkb = NA_KEY_ROWS * GRID_W
    rpb_pad = jnp.pad(rpb.astype(F32), ((0, 0), (0, 0), (0, LANES - N_COL_OFFSETS)))
    return pl.pallas_call(
        functools.partial(_natten_bias_kernel, rows=rows),
        grid=(n_pairs,),
        in_specs=[pl.BlockSpec((2, N_ROW_OFFSETS, LANES), lambda p: (p, 0, 0))],
        out_specs=pl.BlockSpec((3, 1, nqb, 2 * nkb), lambda p: (0, p, 0, 0)),
        out_shape=jax.ShapeDtypeStruct((3, n_pairs, nqb, 2 * nkb), F32),
        compiler_params=_cparams(("parallel",)),
        name="natten_bias",
    )(rpb_pad)


def _natten_kernel(q_ref, k_ref, v_ref, tab_ref, o_ref, *, rows):
    rb = pl.program_id(2)
    nk = NA_KEY_ROWS * GRID_W
    start_row = jnp.clip(rb * NA_ROWS - NA_WIN_ROWS // 2, 0, rows - NA_KEY_ROWS)
    start = pl.multiple_of(start_row * GRID_W, GRID_W)
    q = q_ref[...]
    k = k_ref[pl.ds(start, nk), :]
    v = v_ref[pl.ds(start, nk), :]
    lane = lax.broadcasted_iota(I32, (1, 2 * HEAD_DIM), 1)
    head0 = lane < HEAD_DIM
    zero = jnp.zeros_like(k)
    k_st = jnp.concatenate([jnp.where(head0, k, zero), jnp.where(head0, zero, k)], axis=0)
    v_st = jnp.concatenate([jnp.where(head0, v, zero), jnp.where(head0, zero, v)], axis=0)
    s = _dot_nt(q, k_st) * (HEAD_DIM ** -0.5) + tab_ref[0, 0]
    probs, denoms = [], []
    for h in range(2):
        sh = s[:, h * nk:(h + 1) * nk]
        m = jnp.max(sh, axis=1, keepdims=True)
        p = jnp.exp(sh - m)
        denoms.append(jnp.sum(p, axis=1, keepdims=True))
        probs.append(p.astype(BF16))
    out = _dot(jnp.concatenate(probs, axis=1), v_st)
    denom = jnp.where(head0, denoms[0], denoms[1])
    o_ref[...] = (out / denom).astype(BF16)


def _natten(nq, nk, nv, rpb, batch, seq):
    t = nq.shape[0]
    rows = seq // GRID_W
    n_pairs = HEADS // 2
    n_blocks = rows // NA_ROWS
    nqb = NA_ROWS * GRID_W
    nkb = NA_KEY_ROWS * GRID_W
    tab = _natten_bias_table(rpb, rows)

    def variant(rb):
        return jnp.where(rb == 0, 0, jnp.where(rb == n_blocks - 1, 2, 1))

    kv_spec = pl.BlockSpec((seq, 2 * HEAD_DIM), lambda b, p, rb: (b, p))
    q_spec = pl.BlockSpec((nqb, 2 * HEAD_DIM), lambda b, p, rb: (b * n_blocks + rb, p))
    return pl.pallas_call(
        functools.partial(_natten_kernel, rows=rows),
        grid=(batch, n_pairs, n_blocks),
        in_specs=[q_spec, kv_spec, kv_spec,
                  pl.BlockSpec((1, 1, nqb, 2 * nkb), lambda b, p, rb: (variant(rb), p, 0, 0))],
        out_specs=q_spec,
        out_shape=jax.ShapeDtypeStruct((t, GROUP_W), BF16),
        compiler_params=_cparams(("parallel", "parallel", "arbitrary")),
        name="natten",
    )(nq, nk, nv, tab)


def _layer_norm(h, gain, bias):
    mu = jnp.mean(h, axis=-1, keepdims=True)
    d = h - mu
    var = jnp.mean(d * d, axis=-1, keepdims=True)
    return d * lax.rsqrt(var + LN_EPS) * gain + bias


def _out_ln1_kernel(ret_ref, na_ref, x_ref, w_ref, gain_ref, bias_ref, x1_ref, x1b_ref):
    mix = _dot(ret_ref[...], w_ref[0:GROUP_W, :]) + _dot(na_ref[...], w_ref[GROUP_W:2 * GROUP_W, :])
    x1 = _layer_norm(ALPHA * x_ref[...] + mix, gain_ref[...], bias_ref[...])
    x1_ref[...] = x1
    x1b_ref[...] = x1.astype(BF16)


def _out_ln1(ret, na, x2, w_out_b, gain, bias):
    t = x2.shape[0]
    tm = PROJ_TM
    tok = lambda i: (i, 0)
    const = lambda i: (0, 0)
    return pl.pallas_call(
        _out_ln1_kernel,
        grid=(t // tm,),
        in_specs=[pl.BlockSpec((tm, GROUP_W), tok), pl.BlockSpec((tm, GROUP_W), tok),
                  pl.BlockSpec((tm, D_MODEL), tok), pl.BlockSpec(w_out_b.shape, const),
                  pl.BlockSpec((1, D_MODEL), const), pl.BlockSpec((1, D_MODEL), const)],
        out_specs=[pl.BlockSpec((tm, D_MODEL), tok), pl.BlockSpec((tm, D_MODEL), tok)],
        out_shape=[jax.ShapeDtypeStruct((t, D_MODEL), F32), jax.ShapeDtypeStruct((t, D_MODEL), BF16)],
        compiler_params=_cparams(("parallel",)),
        name="out_ln1",
    )(ret, na, x2, w_out_b, gain.astype(F32)[None, :], bias.astype(F32)[None, :])


def _route_kernel(x_ref, wr_ref, rb_ref, w_ref, rank_ref, cnt_ref, tri_ref):
    tm = x_ref.shape[0]

    @pl.when(pl.program_id(0) == 0)
    def _():
        i = lax.broadcasted_iota(I32, (MOE_SUB, MOE_SUB), 0)
        j = lax.broadcasted_iota(I32, (MOE_SUB, MOE_SUB), 1)
        tri_ref[...] = jnp.where(i < j, 1.0, 0.0).astype(BF16)

    scores = jax.nn.sigmoid(_dot_nt(wr_ref[...], x_ref[...]))
    biased = scores + rb_ref[...]
    sub = lax.broadcasted_iota(I32, (GROUP_SIZE, tm), 0).astype(F32)
    none = float(N_EXPERTS)
    ninf = -jnp.inf

    def first_max(vals, index):
        m = jnp.max(vals, axis=0, keepdims=True)
        return m, jnp.min(jnp.where(vals == m, index, none), axis=0, keepdims=True)

    groups = [biased[g * GROUP_SIZE:(g + 1) * GROUP_SIZE, :] for g in range(N_GROUPS)]
    group_scores = []
    for g in range(N_GROUPS):
        m1, i1 = first_max(groups[g], sub)
        m2 = jnp.max(jnp.where(sub == i1, ninf, groups[g]), axis=0, keepdims=True)
        group_scores.append(m1 + m2)
    cur = jnp.concatenate(group_scores, axis=0)
    group_sel = jnp.zeros(cur.shape, F32)
    for _ in range(TOPK_GROUPS):
        _, i1 = first_max(cur, sub)
        hit = sub == i1
        group_sel = jnp.where(hit, 1.0, group_sel)
        cur = jnp.where(hit, ninf, cur)

    masked = [jnp.where(group_sel[g:g + 1, :] > 0.5, groups[g], ninf) for g in range(N_GROUPS)]
    ids = [sub + float(g * GROUP_SIZE) for g in range(N_GROUPS)]
    chosen = [jnp.zeros((GROUP_SIZE, tm), F32) for _ in range(N_GROUPS)]
    for _ in range(TOP_K):
        m = masked[0]
        for g in range(1, N_GROUPS):
            m = jnp.maximum(m, masked[g])
        m = jnp.max(m, axis=0, keepdims=True)
        cand = jnp.where(masked[0] == m, ids[0], none)
        for g in range(1, N_GROUPS):
            cand = jnp.minimum(cand, jnp.where(masked[g] == m, ids[g], none))
        first = jnp.min(cand, axis=0, keepdims=True)
        for g in range(N_GROUPS):
            hit = ids[g] == first
            chosen[g] = jnp.where(hit, 1.0, chosen[g])
            masked[g] = jnp.where(hit, ninf, masked[g])

    sel = jnp.concatenate(chosen, axis=0) > 0.5
    picked = jnp.where(sel, scores, 0.0)
    total = jnp.sum(picked, axis=0, keepdims=True)
    w_ref[...] = picked / total * ROUTED_SCALE
    sel_f = jnp.where(sel, 1.0, 0.0)
    sel_b = sel_f.astype(BF16)
    before, cnt_max = [], None
    for k in range(MOE_NSUB):
        cols = slice(k * MOE_SUB, (k + 1) * MOE_SUB)
        before.append(_dot(sel_b[:, cols], tri_ref[...]))
        cnt = jnp.sum(sel_f[:, cols], axis=1, keepdims=True)
        cnt_max = cnt if cnt_max is None else jnp.maximum(cnt_max, cnt)
    rank_ref[...] = jnp.where(sel, jnp.concatenate(before, axis=1).astype(I32), -1)
    cnt_ref[...] = jnp.broadcast_to(cnt_max, cnt_ref.shape).astype(I32)


def _route(x1b, w_router, router_bias):
    t = x1b.shape[0]
    tm = MOE_TM
    nt = t // tm
    wr_t = w_router.astype(F32).T.astype(BF16)
    const = lambda i: (0, 0)
    col = lambda i: (0, i)
    return pl.pallas_call(
        _route_kernel,
        grid=(nt,),
        in_specs=[pl.BlockSpec((tm, D_MODEL), lambda i: (i, 0)),
                  pl.BlockSpec((N_EXPERTS, D_MODEL), const),
                  pl.BlockSpec((N_EXPERTS, 1), const)],
        out_specs=[pl.BlockSpec((N_EXPERTS, tm), col), pl.BlockSpec((N_EXPERTS, tm), col),
                   pl.BlockSpec((N_EXPERTS, LANES), col)],
        out_shape=[jax.ShapeDtypeStruct((N_EXPERTS, t), F32),
                   jax.ShapeDtypeStruct((N_EXPERTS, t), I32),
                   jax.ShapeDtypeStruct((N_EXPERTS, nt * LANES), I32)],
        scratch_shapes=[pltpu.VMEM((MOE_SUB, MOE_SUB), BF16)],
        compiler_params=_cparams(("arbitrary",)),
        name="route",
    )(x1b, wr_t, router_bias.astype(F32)[:, None])


def _moe_kernel(cnt_ref, x_ref, rank_ref, w_ref, wgu_ref, wd_ref, o_ref, oh_ref, g_ref, y_ref):
    i = pl.program_id(0)
    eb = pl.program_id(1)

    @pl.when(eb == 0)
    def _():
        o_ref[...] = jnp.zeros_like(o_ref)

    n_pass = lax.shift_right_logical(cnt_ref[eb, i] + (MOE_SLOTS - 1), int(np.log2(MOE_SLOTS)))
    slot = lax.broadcasted_iota(I32, (MOE_SLOTS, MOE_SUB), 0)

    def one_pass(p, carry):
        base = p * MOE_SLOTS
        slot_w = []
        for k in range(MOE_NSUB):
            tok = slice(k * MOE_SUB, (k + 1) * MOE_SUB)
            blocks, weights = [], []
            for j in range(MOE_EB):
                match = slot == (rank_ref[j, :, tok] - base)
                blocks.append(jnp.where(match, 1.0, 0.0).astype(BF16))
                weights.append(jnp.sum(jnp.where(match, w_ref[j, :, tok], 0.0), axis=1, keepdims=True))
            onehot = jnp.concatenate(blocks, axis=0)
            oh_ref[k] = onehot
            slot_w.append(weights)
            g_ref[k] = _dot(onehot, x_ref[tok, :]).astype(BF16)
        for j in range(MOE_EB):
            rows = slice(j * MOE_SLOTS, (j + 1) * MOE_SLOTS)
            xe = jnp.concatenate([g_ref[k, rows, :] for k in range(MOE_NSUB)], axis=0)
            h = _dot(xe, wgu_ref[j])
            act = (jax.nn.silu(h[:, :EXPERT_DIM]) * h[:, EXPERT_DIM:]).astype(BF16)
            wc = jnp.concatenate([slot_w[k][j] for k in range(MOE_NSUB)], axis=0)
            yw = (_dot(act, wd_ref[j]) * wc).astype(BF16)
            for k in range(MOE_NSUB):
                y_ref[k, rows, :] = yw[k * MOE_SLOTS:(k + 1) * MOE_SLOTS, :]
        for k in range(MOE_NSUB):
            tok = slice(k * MOE_SUB, (k + 1) * MOE_SUB)
            o_ref[tok, :] += _dot_tn(oh_ref[k], y_ref[k])
        return carry

    lax.fori_loop(0, n_pass, one_pass, 0)


def _moe(x1b, rank_t, w_t, counts, wgu_b, wd_b):
    t = x1b.shape[0]
    tm = MOE_TM
    nt = t // tm
    rank3 = rank_t.reshape(N_EXPERTS, 1, t)
    w3 = w_t.reshape(N_EXPERTS, 1, t)
    counts = counts.reshape(N_EXPERTS // MOE_EB, MOE_EB, nt).max(axis=1)
    grid_spec = pltpu.PrefetchScalarGridSpec(
        num_scalar_prefetch=1,
        grid=(nt, N_EXPERTS // MOE_EB),
        in_specs=[
            pl.BlockSpec((tm, D_MODEL), lambda i, e, c: (i, 0)),
            pl.BlockSpec((MOE_EB, 1, tm), lambda i, e, c: (e, 0, i)),
            pl.BlockSpec((MOE_EB, 1, tm), lambda i, e, c: (e, 0, i)),
            pl.BlockSpec((MOE_EB, D_MODEL, 2 * EXPERT_DIM), lambda i, e, c: (e, 0, 0)),
            pl.BlockSpec((MOE_EB, EXPERT_DIM, D_MODEL), lambda i, e, c: (e, 0, 0)),
        ],
        out_specs=pl.BlockSpec((tm, D_MODEL), lambda i, e, c: (i, 0)),
        scratch_shapes=[pltpu.VMEM((MOE_NSUB, MOE_SUB, MOE_SUB), BF16),
                        pltpu.VMEM((MOE_NSUB, MOE_SUB, D_MODEL), BF16),
                        pltpu.VMEM((MOE_NSUB, MOE_SUB, D_MODEL), BF16)],
    )
    return pl.pallas_call(
        _moe_kernel,
        grid_spec=grid_spec,
        out_shape=jax.ShapeDtypeStruct((t, D_MODEL), F32),
        compiler_params=_cparams(("parallel", "arbitrary")),
        name="moe",
    )(counts, x1b, rank3, w3, wgu_b, wd_b)


def _final_kernel(x1_ref, ffn_ref, p_ref, wsgu_ref, wsd_ref, wp_ref, wg_ref, gain_ref, bias_ref, o_ref):
    x1 = x1_ref[...]
    xb = x1.astype(BF16)
    h = _dot(xb, wsgu_ref[...])
    act = (jax.nn.silu(h[:, :EXPERT_DIM]) * h[:, EXPERT_DIM:]).astype(BF16)
    shared = _dot(act, wsd_ref[...])
    ple = _dot(p_ref[...].astype(BF16), wp_ref[...]) * jax.nn.sigmoid(_dot(xb, wg_ref[...]))
    o_ref[...] = _layer_norm(ALPHA * x1 + (ffn_ref[...] + shared) + ple, gain_ref[...], bias_ref[...])


def _final(x1, ffn, p2, wsgu_b, wsd_b, wp_b, wg_b, gain, bias):
    t = x1.shape[0]
    tm = PROJ_TM
    tok = lambda i: (i, 0)
    const = lambda i: (0, 0)
    full = lambda a: pl.BlockSpec(a.shape, const)
    return pl.pallas_call(
        _final_kernel,
        grid=(t // tm,),
        in_specs=[pl.BlockSpec((tm, D_MODEL), tok), pl.BlockSpec((tm, D_MODEL), tok),
                  pl.BlockSpec((tm, p2.shape[1]), tok),
                  full(wsgu_b), full(wsd_b), full(wp_b), full(wg_b),
                  pl.BlockSpec((1, D_MODEL), const), pl.BlockSpec((1, D_MODEL), const)],
        out_specs=pl.BlockSpec((tm, D_MODEL), tok),
        out_shape=jax.ShapeDtypeStruct((t, D_MODEL), F32),
        compiler_params=_cparams(("parallel",)),
        name="final",
    )(x1, ffn, p2, wsgu_b, wsd_b, wp_b, wg_b, gain.astype(F32)[None, :], bias.astype(F32)[None, :])


def _rotary_tables(seq):
    half = HEAD_DIM // 2
    inv = ROPE_BASE ** (-jnp.arange(half, dtype=F32) / half)
    ang = jnp.arange(seq, dtype=jnp.int32).astype(F32)[:, None] * inv[None, :]
    cos, sin = jnp.cos(ang), jnp.sin(ang)
    reps = LANES // HEAD_DIM
    cos_t = jnp.tile(jnp.concatenate([cos, cos], axis=1), (1, reps))
    sin_t = jnp.tile(jnp.concatenate([-sin, sin], axis=1), (1, reps))
    return cos_t, sin_t


def kernel(x, p, w_in, ret_decay_fwd, ret_decay_bwd, ret_gn_gain, na_rpb, w_out, ln1_gain, ln1_bias,
           w_router, router_bias, w_expert_gu, w_expert_down, w_shared_gu, w_shared_down,
           w_ple_proj, w_ple_gate, ln2_gain, ln2_bias):
    batch, seq, d = x.shape
    t = batch * seq
    depth = w_in.shape[0]
    assert depth == 1 and d == D_MODEL
    assert seq % PROJ_TM == 0 and seq % MOE_TM == 0 and seq % RET_CHUNK == 0
    assert (seq // GRID_W) % NA_ROWS == 0 and seq // GRID_W >= NA_KEY_ROWS
    cos_t, sin_t = _rotary_tables(seq)
    x2 = x.reshape(t, d)
    for i in range(depth):
        rq, rk, rv, rg, nq, nk, nv = _in_proj(x2, w_in[i].astype(BF16), cos_t, sin_t, seq)
        ret = _retention(rq, rk, rv, rg, ret_decay_fwd[i], ret_decay_bwd[i], ret_gn_gain[i], batch, seq)
        na = _natten(nq, nk, nv, na_rpb[i], batch, seq)
        x1, x1b = _out_ln1(ret, na, x2, w_out[i].astype(BF16), ln1_gain[i], ln1_bias[i])
        w_t, rank_t, cnt = _route(x1b, w_router[i], router_bias[i])
        counts = cnt[:, ::LANES]
        ffn = _moe(x1b, rank_t, w_t, counts, w_expert_gu[i].astype(BF16), w_expert_down[i].astype(BF16))
        x2 = _final(x1, ffn, p[i].reshape(t, -1), w_shared_gu[i].astype(BF16), w_shared_down[i].astype(BF16),
                    w_ple_proj[i].astype(BF16), w_ple_gate[i].astype(BF16), ln2_gain[i], ln2_bias[i])
    return x2.reshape(batch, seq, d)
```

```python
import functools

import numpy as np
import jax
import jax.numpy as jnp
from jax import lax
from jax.experimental import pallas as pl
from jax.experimental.pallas import tpu as pltpu

F32 = jnp.float32
BF16 = jnp.bfloat16
I32 = jnp.int32

D_MODEL = 1024
HEADS = 8
HEAD_DIM = 64
GROUP_W = HEADS * HEAD_DIM
ROPE_BASE = 10000.0
GN_EPS = 1e-6
LN_EPS = 1e-5
GRID_W = 64
NA_WIN_ROWS = 8
NA_WIN_COLS = 16
N_EXPERTS = 64
N_GROUPS = 8
GROUP_SIZE = N_EXPERTS // N_GROUPS
TOPK_GROUPS = 4
TOP_K = 8
EXPERT_DIM = 256
ROUTED_SCALE = 2.5
ALPHA = 2.0 ** 0.25
NEG_BIG = -1e30

LANES = 128
VMEM_LIMIT_BYTES = 56 * 1024 * 1024

PROJ_TM = 512
RET_CHUNK = 128
RET_UNROLL = 8
NA_ROWS = 4
NA_KEY_ROWS = NA_ROWS + NA_WIN_ROWS
MOE_TM = 1024
MOE_SUB = 256
MOE_EB = 4
MOE_SLOTS = MOE_SUB // MOE_EB
MOE_NSUB = MOE_TM // MOE_SUB


def _cparams(sem):
    return pltpu.CompilerParams(dimension_semantics=sem, vmem_limit_bytes=VMEM_LIMIT_BYTES)


def _dot(a, b):
    return jnp.dot(a, b, preferred_element_type=F32)


def _dot_nt(a, b):
    return lax.dot_general(a, b, (((1,), (1,)), ((), ())), preferred_element_type=F32)


def _dot_tn(a, b):
    return lax.dot_general(a, b, (((0,), (0,)), ((), ())), preferred_element_type=F32)


def _in_proj_kernel(x_ref, w_ref, cos_ref, sin_ref,
                    rq_ref, rk_ref, rv_ref, rg_ref, nq_ref, nk_ref, nv_ref):
    xb = x_ref[...].astype(BF16)
    cos = cos_ref[...]
    sin = sin_ref[...]
    lane = lax.broadcasted_iota(I32, (1, LANES), 1)
    first_half = (lane % HEAD_DIM) < (HEAD_DIM // 2)

    def proj(g):
        return _dot(xb, w_ref[:, g * GROUP_W:(g + 1) * GROUP_W])

    def rotary(t, scale):
        outs = []
        for j in range(GROUP_W // LANES):
            c = t[:, j * LANES:(j + 1) * LANES]
            swapped = jnp.where(first_half,
                                pltpu.roll(c, LANES - HEAD_DIM // 2, axis=1),
                                pltpu.roll(c, HEAD_DIM // 2, axis=1))
            outs.append((c * cos + swapped * sin) * scale)
        return jnp.concatenate(outs, axis=1)

    rq_ref[...] = rotary(proj(0), 1.0).astype(BF16)
    rk_ref[...] = rotary(proj(1), HEAD_DIM ** -0.5).astype(BF16)
    rv_ref[...] = proj(2).astype(BF16)
    rg_ref[...] = jax.nn.silu(proj(3)).astype(BF16)
    nq_ref[...] = proj(4).astype(BF16)
    nk_ref[...] = proj(5).astype(BF16)
    nv_ref[...] = proj(6).astype(BF16)


def _in_proj(x2, w_in_b, cos_t, sin_t, seq):
    t = x2.shape[0]
    tm = PROJ_TM
    n_pos = seq // tm
    out = jax.ShapeDtypeStruct((t, GROUP_W), BF16)
    tok = lambda i: (i, 0)
    return pl.pallas_call(
        _in_proj_kernel,
        grid=(t // tm,),
        in_specs=[
            pl.BlockSpec((tm, D_MODEL), tok),
            pl.BlockSpec(w_in_b.shape, lambda i: (0, 0)),
            pl.BlockSpec((tm, LANES), lambda i: (i % n_pos, 0)),
            pl.BlockSpec((tm, LANES), lambda i: (i % n_pos, 0)),
        ],
        out_specs=[pl.BlockSpec((tm, GROUP_W), tok)] * 7,
        out_shape=[out] * 7,
        compiler_params=_cparams(("parallel",)),
        name="in_proj",
    )(x2, w_in_b, cos_t, sin_t)


def _log_sigmoid(x):
    return jnp.minimum(x, 0.0) - jnp.log1p(jnp.exp(-jnp.abs(x)))


def _retention_kernel(q_ref, k_ref, v_ref, g_ref, decf_ref, decb_ref, decfd_ref, decbd_ref,
                      gain_ref, o_ref, kv_ref, st_ref, dmat_ref):
    c = RET_CHUNK
    n_chunks = q_ref.shape[0] // c
    pair_w = 2 * HEAD_DIM

    lgf = _log_sigmoid(decf_ref[...])
    lgb = _log_sigmoid(decb_ref[...])
    row = lax.broadcasted_iota(I32, (c, 1), 0).astype(F32)
    k_dec_f = jnp.exp((c - 1.0 - row) * lgf)
    k_dec_b = jnp.exp(row * lgb)
    q_dec_f = jnp.exp((row + 1.0) * lgf)
    q_dec_b = jnp.exp((c - row) * lgb)
    chunk_dec_f = jnp.exp(c * lgf)
    chunk_dec_b = jnp.exp(c * lgb)

    lane = lax.broadcasted_iota(I32, (1, pair_w), 1)
    head0 = lane < HEAD_DIM
    r2 = lax.broadcasted_iota(I32, (pair_w, pair_w), 0) // HEAD_DIM
    c2 = lax.broadcasted_iota(I32, (pair_w, pair_w), 1) // HEAD_DIM
    same_head = r2 == c2
    block_diag = jnp.where(same_head, 1.0, 0.0)
    seg_avg = jnp.where(same_head, 1.0 / HEAD_DIM, 0.0).astype(BF16)

    lgf_d = _log_sigmoid(decfd_ref[0])
    lgb_d = _log_sigmoid(decbd_ref[0])
    di = lax.broadcasted_iota(I32, (c, 2 * c), 0)
    dj = lax.broadcasted_iota(I32, (c, 2 * c), 1) % c
    diff = (di - dj).astype(F32)
    dmat_ref[...] = jnp.where(diff >= 0.0, jnp.exp(diff * lgf_d), jnp.exp(-diff * lgb_d))

    def chunk(ref, n):
        return ref[pl.ds(pl.multiple_of(n * c, c), c), :]

    unroll = RET_UNROLL

    def summarize(nb, carry):
        for u in range(unroll):
            n = nb * unroll + u
            kf = chunk(k_ref, n).astype(F32)
            kst = jnp.concatenate([kf * k_dec_f, kf * k_dec_b], axis=1).astype(BF16)
            kv_ref[n] = _dot_tn(kst, chunk(v_ref, n))
        return carry

    lax.fori_loop(0, n_chunks // unroll, summarize, 0)

    def fwd_scan(n, state):
        st_ref[n, 0:pair_w, :] = state.astype(BF16)
        return state * chunk_dec_f + kv_ref[n, 0:pair_w, :] * block_diag

    lax.fori_loop(0, n_chunks, fwd_scan, jnp.zeros((pair_w, pair_w), F32))

    def bwd_scan(i, state):
        n = n_chunks - 1 - i
        st_ref[n, pair_w:2 * pair_w, :] = state.astype(BF16)
        return state * chunk_dec_b + kv_ref[n, pair_w:2 * pair_w, :] * block_diag

    lax.fori_loop(0, n_chunks, bwd_scan, jnp.zeros((pair_w, pair_w), F32))

    gain = gain_ref[...]

    def seg_mean(z):
        hi = z.astype(BF16)
        lo = (z - hi.astype(F32)).astype(BF16)
        return _dot(hi, seg_avg) + _dot(lo, seg_avg)

    def emit(nb, carry):
        ys = []
        for u in range(unroll):
            n = nb * unroll + u
            q = chunk(q_ref, n)
            k = chunk(k_ref, n)
            v = chunk(v_ref, n)
            zero = jnp.zeros_like(k)
            k_st = jnp.concatenate([jnp.where(head0, k, zero), jnp.where(head0, zero, k)], axis=0)
            v_st = jnp.concatenate([jnp.where(head0, v, zero), jnp.where(head0, zero, v)], axis=0)
            scores = _dot_nt(q, k_st) * dmat_ref[...]
            qf = q.astype(F32)
            q_st = jnp.concatenate([qf * q_dec_f, qf * q_dec_b], axis=1).astype(BF16)
            ys.append(_dot(scores.astype(BF16), v_st) + _dot(q_st, st_ref[n]))
        y = jnp.concatenate(ys, axis=0)
        mu = seg_mean(y)
        d = y - mu
        var = seg_mean(d * d)
        yn = d * lax.rsqrt(var + GN_EPS) * gain
        rows = pl.ds(pl.multiple_of(nb * (unroll * c), unroll * c), unroll * c)
        o_ref[rows, :] = (g_ref[rows, :].astype(F32) * yn).astype(BF16)
        return carry

    lax.fori_loop(0, n_chunks // unroll, emit, 0)


def _retention(rq, rk, rv, rg, dec_f, dec_b, gain, batch, seq):
    t = rq.shape[0]
    c = RET_CHUNK
    n_pairs = HEADS // 2
    pair_w = 2 * HEAD_DIM
    dec_f_lane = jnp.repeat(dec_f.astype(F32), HEAD_DIM)[None, :]
    dec_b_lane = jnp.repeat(dec_b.astype(F32), HEAD_DIM)[None, :]
    dec_f_col = jnp.repeat(dec_f.astype(F32), c).reshape(n_pairs, 1, 2 * c)
    dec_b_col = jnp.repeat(dec_b.astype(F32), c).reshape(n_pairs, 1, 2 * c)
    tok = pl.BlockSpec((seq, pair_w), lambda b, p: (b, p))
    lane_spec = pl.BlockSpec((1, pair_w), lambda b, p: (0, p))
    col_spec = pl.BlockSpec((1, 1, 2 * c), lambda b, p: (p, 0, 0))
    return pl.pallas_call(
        _retention_kernel,
        grid=(batch, n_pairs),
        in_specs=[tok, tok, tok, tok, lane_spec, lane_spec, col_spec, col_spec, lane_spec],
        out_specs=tok,
        out_shape=jax.ShapeDtypeStruct((t, GROUP_W), BF16),
        scratch_shapes=[
            pltpu.VMEM((seq // c, 2 * pair_w, pair_w), F32),
            pltpu.VMEM((seq // c, 2 * pair_w, pair_w), BF16),
            pltpu.VMEM((c, 2 * c), F32),
        ],
        compiler_params=_cparams(("parallel", "parallel")),
        name="retention",
    )(rq, rk, rv, rg, dec_f_lane, dec_b_lane, dec_f_col, dec_b_col, gain.astype(F32)[None, :])


N_ROW_OFFSETS = 2 * NA_WIN_ROWS - 1
N_COL_OFFSETS = 2 * NA_WIN_COLS - 1


def _natten_row_offsets(rows):
    n_blocks = rows // NA_ROWS
    starts = {0: 0, 1: NA_ROWS - NA_WIN_ROWS // 2, 2: rows - NA_KEY_ROWS}
    blocks = {0: 0, 1: 1, 2: n_blocks - 1}
    table = []
    for v in range(3):
        per_a = []
        for a in range(NA_ROWS):
            r = blocks[v] * NA_ROWS + a
            rs = min(max(r - NA_WIN_ROWS // 2, 0), rows - NA_WIN_ROWS)
            per_kl = []
            for kl in range(NA_KEY_ROWS):
                kr = starts[v] + kl
                per_kl.append(kr - r + NA_WIN_ROWS - 1 if rs <= kr < rs + NA_WIN_ROWS else None)
            per_a.append(per_kl)
        table.append(per_a)
    return table


def _natten_bias_kernel(rpb_ref, tab_ref, *, rows):
    offsets = _natten_row_offsets(rows)
    nk = NA_KEY_ROWS * GRID_W
    shape = (GRID_W, LANES)
    lane = lax.broadcasted_iota(I32, shape, 1)
    c = lax.broadcasted_iota(I32, shape, 0)
    second = lane >= GRID_W
    kc = lane % GRID_W
    cs = jnp.clip(c - NA_WIN_COLS // 2, 0, GRID_W - NA_WIN_COLS)
    col_ok = jnp.logical_and(kc >= cs, kc < cs + NA_WIN_COLS)
    neg = jnp.full(shape, NEG_BIG, F32)
    for hh in range(2):
        toeplitz = []
        for dr in range(N_ROW_OFFSETS):
            x = jnp.broadcast_to(rpb_ref[hh, dr:dr + 1, :], shape)
            lo = pltpu.roll(x, LANES - (NA_WIN_COLS - 1), axis=1, stride=1, stride_axis=0)
            hi = pltpu.roll(x, GRID_W - (NA_WIN_COLS - 1), axis=1, stride=1, stride_axis=0)
            toeplitz.append(jnp.where(second, hi, lo))
        for v in range(3):
            for a in range(NA_ROWS):
                for j in range(NA_KEY_ROWS // 2):
                    d0, d1 = offsets[v][a][2 * j], offsets[v][a][2 * j + 1]
                    if d0 is None and d1 is None:
                        piece = neg
                    else:
                        t0 = neg if d0 is None else toeplitz[d0]
                        t1 = neg if d1 is None else toeplitz[d1]
                        piece = jnp.where(col_ok, jnp.where(second, t1, t0), neg)
                    tab_ref[v, 0, a * GRID_W:(a + 1) * GRID_W,
                            hh * nk + j * LANES:hh * nk + (j + 1) * LANES] = piece


def _natten_bias_table(rpb, rows):
    n_pairs = HEADS // 2
    nqb = NA_ROWS * GRID_W
    nkb = NA_KEY_ROWS * GRID_W
    rpb_pad = jnp.pad(rpb.astype(F32), ((0, 0), (0, 0), (0, LANES - N_COL_OFFSETS)))
    return pl.pallas_call(
        functools.partial(_natten_bias_kernel, rows=rows),
        grid=(n_pairs,),
        in_specs=[pl.BlockSpec((2, N_ROW_OFFSETS, LANES), lambda p: (p, 0, 0))],
        out_specs=pl.BlockSpec((3, 1, nqb, 2 * nkb), lambda p: (0, p, 0, 0)),
        out_shape=jax.ShapeDtypeStruct((3, n_pairs, nqb, 2 * nkb), F32),
        compiler_params=_cparams(("parallel",)),
        name="natten_bias",
    )(rpb_pad)


def _natten_kernel(q_ref, k_ref, v_ref, tab_ref, o_ref, *, rows):
    rb = pl.program_id(2)
    nk = NA_KEY_ROWS * GRID_W
    start_row = jnp.clip(rb * NA_ROWS - NA_WIN_ROWS // 2, 0, rows - NA_KEY_ROWS)
    start = pl.multiple_of(start_row * GRID_W, GRID_W)
    q = q_ref[...]
    k = k_ref[pl.ds(start, nk), :]
    v = v_ref[pl.ds(start, nk), :]
    lane = lax.broadcasted_iota(I32, (1, 2 * HEAD_DIM), 1)
    head0 = lane < HEAD_DIM
    zero = jnp.zeros_like(k)
    k_st = jnp.concatenate([jnp.where(head0, k, zero), jnp.where(head0, zero, k)], axis=0)
    v_st = jnp.concatenate([jnp.where(head0, v, zero), jnp.where(head0, zero, v)], axis=0)
    s = _dot_nt(q, k_st) * (HEAD_DIM ** -0.5) + tab_ref[0, 0]
    probs, denoms = [], []
    for h in range(2):
        sh = s[:, h * nk:(h + 1) * nk]
        m = jnp.max(sh, axis=1, keepdims=True)
        p = jnp.exp(sh - m)
        denoms.append(jnp.sum(p, axis=1, keepdims=True))
        probs.append(p.astype(BF16))
    out = _dot(jnp.concatenate(probs, axis=1), v_st)
    denom = jnp.where(head0, denoms[0], denoms[1])
    o_ref[...] = (out / denom).astype(BF16)


def _natten(nq, nk, nv, rpb, batch, seq):
    t = nq.shape[0]
    rows = seq // GRID_W
    n_pairs = HEADS // 2
    n_blocks = rows // NA_ROWS
    nqb = NA_ROWS * GRID_W
    nkb = NA_KEY_ROWS * GRID_W
    tab = _natten_bias_table(rpb, rows)

    def variant(rb):
        return jnp.where(rb == 0, 0, jnp.where(rb == n_blocks - 1, 2, 1))

    kv_spec = pl.BlockSpec((seq, 2 * HEAD_DIM), lambda b, p, rb: (b, p))
    q_spec = pl.BlockSpec((nqb, 2 * HEAD_DIM), lambda b, p, rb: (b * n_blocks + rb, p))
    return pl.pallas_call(
        functools.partial(_natten_kernel, rows=rows),
        grid=(batch, n_pairs, n_blocks),
        in_specs=[q_spec, kv_spec, kv_spec,
                  pl.BlockSpec((1, 1, nqb, 2 * nkb), lambda b, p, rb: (variant(rb), p, 0, 0))],
        out_specs=q_spec,
        out_shape=jax.ShapeDtypeStruct((t, GROUP_W), BF16),
        compiler_params=_cparams(("parallel", "parallel", "arbitrary")),
        name="natten",
    )(nq, nk, nv, tab)


def _layer_norm(h, gain, bias):
    mu = jnp.mean(h, axis=-1, keepdims=True)
    d = h - mu
    var = jnp.mean(d * d, axis=-1, keepdims=True)
    return d * lax.rsqrt(var + LN_EPS) * gain + bias


def _out_ln1_kernel(ret_ref, na_ref, x_ref, w_ref, gain_ref, bias_ref, x1_ref, x1b_ref):
    mix = _dot(ret_ref[...], w_ref[0:GROUP_W, :]) + _dot(na_ref[...], w_ref[GROUP_W:2 * GROUP_W, :])
    x1 = _layer_norm(ALPHA * x_ref[...] + mix, gain_ref[...], bias_ref[...])
    x1_ref[...] = x1
    x1b_ref[...] = x1.astype(BF16)


def _out_ln1(ret, na, x2, w_out_b, gain, bias):
    t = x2.shape[0]
    tm = PROJ_TM
    tok = lambda i: (i, 0)
    const = lambda i: (0, 0)
    return pl.pallas_call(
        _out_ln1_kernel,
        grid=(t // tm,),
        in_specs=[pl.BlockSpec((tm, GROUP_W), tok), pl.BlockSpec((tm, GROUP_W), tok),
                  pl.BlockSpec((tm, D_MODEL), tok), pl.BlockSpec(w_out_b.shape, const),
                  pl.BlockSpec((1, D_MODEL), const), pl.BlockSpec((1, D_MODEL), const)],
        out_specs=[pl.BlockSpec((tm, D_MODEL), tok), pl.BlockSpec((tm, D_MODEL), tok)],
        out_shape=[jax.ShapeDtypeStruct((t, D_MODEL), F32), jax.ShapeDtypeStruct((t, D_MODEL), BF16)],
        compiler_params=_cparams(("parallel",)),
        name="out_ln1",
    )(ret, na, x2, w_out_b, gain.astype(F32)[None, :], bias.astype(F32)[None, :])


def _route_kernel(x_ref, wr_ref, rb_ref, w_ref, rank_ref, cnt_ref, tri_ref):
    tm = x_ref.shape[0]

    @pl.when(pl.program_id(0) == 0)
    def _():
        i = lax.broadcasted_iota(I32, (MOE_SUB, MOE_SUB), 0)
        j = lax.broadcasted_iota(I32, (MOE_SUB, MOE_SUB), 1)
        tri_ref[...] = jnp.where(i < j, 1.0, 0.0).astype(BF16)

    scores = jax.nn.sigmoid(_dot_nt(wr_ref[...], x_ref[...]))
    biased = scores + rb_ref[...]
    sub = lax.broadcasted_iota(I32, (GROUP_SIZE, tm), 0).astype(F32)
    none = float(N_EXPERTS)
    ninf = -jnp.inf

    def first_max(vals, index):
        m = jnp.max(vals, axis=0, keepdims=True)
        return m, jnp.min(jnp.where(vals == m, index, none), axis=0, keepdims=True)

    groups = [biased[g * GROUP_SIZE:(g + 1) * GROUP_SIZE, :] for g in range(N_GROUPS)]
    group_scores = []
    for g in range(N_GROUPS):
        m1, i1 = first_max(groups[g], sub)
        m2 = jnp.max(jnp.where(sub == i1, ninf, groups[g]), axis=0, keepdims=True)
        group_scores.append(m1 + m2)
    cur = jnp.concatenate(group_scores, axis=0)
    group_sel = jnp.zeros(cur.shape, F32)
    for _ in range(TOPK_GROUPS):
        _, i1 = first_max(cur, sub)
        hit = sub == i1
        group_sel = jnp.where(hit, 1.0, group_sel)
        cur = jnp.where(hit, ninf, cur)

    masked = [jnp.where(group_sel[g:g + 1, :] > 0.5, groups[g], ninf) for g in range(N_GROUPS)]
    ids = [sub + float(g * GROUP_SIZE) for g in range(N_GROUPS)]
    chosen = [jnp.zeros((GROUP_SIZE, tm), F32) for _ in range(N_GROUPS)]
    for _ in range(TOP_K):
        m = masked[0]
        for g in range(1, N_GROUPS):
            m = jnp.maximum(m, masked[g])
        m = jnp.max(m, axis=0, keepdims=True)
        cand = jnp.where(masked[0] == m, ids[0], none)
        for g in range(1, N_GROUPS):
            cand = jnp.minimum(cand, jnp.where(masked[g] == m, ids[g], none))
        first = jnp.min(cand, axis=0, keepdims=True)
        for g in range(N_GROUPS):
            hit = ids[g] == first
            chosen[g] = jnp.where(hit, 1.0, chosen[g])
            masked[g] = jnp.where(hit, ninf, masked[g])

    sel = jnp.concatenate(chosen, axis=0) > 0.5
    picked = jnp.where(sel, scores, 0.0)
    total = jnp.sum(picked, axis=0, keepdims=True)
    w_ref[...] = picked / total * ROUTED_SCALE
    sel_f = jnp.where(sel, 1.0, 0.0)
    sel_b = sel_f.astype(BF16)
    before, cnt_max = [], None
    for k in range(MOE_NSUB):
        cols = slice(k * MOE_SUB, (k + 1) * MOE_SUB)
        before.append(_dot(sel_b[:, cols], tri_ref[...]))
        cnt = jnp.sum(sel_f[:, cols], axis=1, keepdims=True)
        cnt_max = cnt if cnt_max is None else jnp.maximum(cnt_max, cnt)
    rank_ref[...] = jnp.where(sel, jnp.concatenate(before, axis=1).astype(I32), -1)
    cnt_ref[...] = jnp.broadcast_to(cnt_max, cnt_ref.shape).astype(I32)


def _route(x1b, w_router, router_bias):
    t = x1b.shape[0]
    tm = MOE_TM
    nt = t // tm
    wr_t = w_router.astype(F32).T.astype(BF16)
    const = lambda i: (0, 0)
    col = lambda i: (0, i)
    return pl.pallas_call(
        _route_kernel,
        grid=(nt,),
        in_specs=[pl.BlockSpec((tm, D_MODEL), lambda i: (i, 0)),
                  pl.BlockSpec((N_EXPERTS, D_MODEL), const),
                  pl.BlockSpec((N_EXPERTS, 1), const)],
        out_specs=[pl.BlockSpec((N_EXPERTS, tm), col), pl.BlockSpec((N_EXPERTS, tm), col),
                   pl.BlockSpec((N_EXPERTS, LANES), col)],
        out_shape=[jax.ShapeDtypeStruct((N_EXPERTS, t), F32),
                   jax.ShapeDtypeStruct((N_EXPERTS, t), I32),
                   jax.ShapeDtypeStruct((N_EXPERTS, nt * LANES), I32)],
        scratch_shapes=[pltpu.VMEM((MOE_SUB, MOE_SUB), BF16)],
        compiler_params=_cparams(("arbitrary",)),
        name="route",
    )(x1b, wr_t, router_bias.astype(F32)[:, None])


def _moe_kernel(cnt_ref, x_ref, rank_ref, w_ref, wgu_ref, wd_ref, o_ref, oh_ref, g_ref, y_ref):
    i = pl.program_id(0)
    eb = pl.program_id(1)

    @pl.when(eb == 0)
    def _():
        o_ref[...] = jnp.zeros_like(o_ref)

    n_pass = lax.shift_right_logical(cnt_ref[eb, i] + (MOE_SLOTS - 1), int(np.log2(MOE_SLOTS)))
    slot = lax.broadcasted_iota(I32, (MOE_SLOTS, MOE_SUB), 0)

    def one_pass(p, carry):
        base = p * MOE_SLOTS
        slot_w = []
        for k in range(MOE_NSUB):
            tok = slice(k * MOE_SUB, (k + 1) * MOE_SUB)
            blocks, weights = [], []
            for j in range(MOE_EB):
                match = slot == (rank_ref[j, :, tok] - base)
                blocks.append(jnp.where(match, 1.0, 0.0).astype(BF16))
                weights.append(jnp.sum(jnp.where(match, w_ref[j, :, tok], 0.0), axis=1, keepdims=True))
            onehot = jnp.concatenate(blocks, axis=0)
            oh_ref[k] = onehot
            slot_w.append(weights)
            g_ref[k] = _dot(onehot, x_ref[tok, :]).astype(BF16)
        for j in range(MOE_EB):
            rows = slice(j * MOE_SLOTS, (j + 1) * MOE_SLOTS)
            xe = jnp.concatenate([g_ref[k, rows, :] for k in range(MOE_NSUB)], axis=0)
            h = _dot(xe, wgu_ref[j])
            act = (jax.nn.silu(h[:, :EXPERT_DIM]) * h[:, EXPERT_DIM:]).astype(BF16)
            wc = jnp.concatenate([slot_w[k][j] for k in range(MOE_NSUB)], axis=0)
            yw = (_dot(act, wd_ref[j]) * wc).astype(BF16)
            for k in range(MOE_NSUB):
                y_ref[k, rows, :] = yw[k * MOE_SLOTS:(k + 1) * MOE_SLOTS, :]
        for k in range(MOE_NSUB):
            tok = slice(k * MOE_SUB, (k + 1) * MOE_SUB)
            o_ref[tok, :] += _dot_tn(oh_ref[k], y_ref[k])
        return carry

    lax.fori_loop(0, n_pass, one_pass, 0)


def _moe(x1b, rank_t, w_t, counts, wgu_b, wd_b):
    t = x1b.shape[0]
    tm = MOE_TM
    nt = t // tm
    rank3 = rank_t.reshape(N_EXPERTS, 1, t)
    w3 = w_t.reshape(N_EXPERTS, 1, t)
    counts = counts.reshape(N_EXPERTS // MOE_EB, MOE_EB, nt).max(axis=1)
    grid_spec = pltpu.PrefetchScalarGridSpec(
        num_scalar_prefetch=1,
        grid=(nt, N_EXPERTS // MOE_EB),
        in_specs=[
            pl.BlockSpec((tm, D_MODEL), lambda i, e, c: (i, 0)),
            pl.BlockSpec((MOE_EB, 1, tm), lambda i, e, c: (e, 0, i)),
            pl.BlockSpec((MOE_EB, 1, tm), lambda i, e, c: (e, 0, i)),
            pl.BlockSpec((MOE_EB, D_MODEL, 2 * EXPERT_DIM), lambda i, e, c: (e, 0, 0)),
            pl.BlockSpec((MOE_EB, EXPERT_DIM, D_MODEL), lambda i, e, c: (e, 0, 0)),
        ],
        out_specs=pl.BlockSpec((tm, D_MODEL), lambda i, e, c: (i, 0)),
        scratch_shapes=[pltpu.VMEM((MOE_NSUB, MOE_SUB, MOE_SUB), BF16),
                        pltpu.VMEM((MOE_NSUB, MOE_SUB, D_MODEL), BF16),
                        pltpu.VMEM((MOE_NSUB, MOE_SUB, D_MODEL), BF16)],
    )
    return pl.pallas_call(
        _moe_kernel,
        grid_spec=grid_spec,
        out_shape=jax.ShapeDtypeStruct((t, D_MODEL), F32),
        compiler_params=_cparams(("parallel", "arbitrary")),
        name="moe",
    )(counts, x1b, rank3, w3, wgu_b, wd_b)


def _final_kernel(x1_ref, ffn_ref, p_ref, wsgu_ref, wsd_ref, wp_ref, wg_ref, gain_ref, bias_ref, o_ref):
    x1 = x1_ref[...]
    xb = x1.astype(BF16)
    h = _dot(xb, wsgu_ref[...])
    act = (jax.nn.silu(h[:, :EXPERT_DIM]) * h[:, EXPERT_DIM:]).astype(BF16)
    shared = _dot(act, wsd_ref[...])
    ple = _dot(p_ref[...].astype(BF16), wp_ref[...]) * jax.nn.sigmoid(_dot(xb, wg_ref[...]))
    o_ref[...] = _layer_norm(ALPHA * x1 + (ffn_ref[...] + shared) + ple, gain_ref[...], bias_ref[...])


def _final(x1, ffn, p2, wsgu_b, wsd_b, wp_b, wg_b, gain, bias):
    t = x1.shape[0]
    tm = PROJ_TM
    tok = lambda i: (i, 0)
    const = lambda i: (0, 0)
    full = lambda a: pl.BlockSpec(a.shape, const)
    return pl.pallas_call(
        _final_kernel,
        grid=(t // tm,),
        in_specs=[pl.BlockSpec((tm, D_MODEL), tok), pl.BlockSpec((tm, D_MODEL), tok),
                  pl.BlockSpec((tm, p2.shape[1]), tok),
                  full(wsgu_b), full(wsd_b), full(wp_b), full(wg_b),
                  pl.BlockSpec((1, D_MODEL), const), pl.BlockSpec((1, D_MODEL), const)],
        out_specs=pl.BlockSpec((tm, D_MODEL), tok),
        out_shape=jax.ShapeDtypeStruct((t, D_MODEL), F32),
        compiler_params=_cparams(("parallel",)),
        name="final",
    )(x1, ffn, p2, wsgu_b, wsd_b, wp_b, wg_b, gain.astype(F32)[None, :], bias.astype(F32)[None, :])


def _rotary_tables(seq):
    half = HEAD_DIM // 2
    inv = ROPE_BASE ** (-jnp.arange(half, dtype=F32) / half)
    ang = jnp.arange(seq, dtype=jnp.int32).astype(F32)[:, None] * inv[None, :]
    cos, sin = jnp.cos(ang), jnp.sin(ang)
    reps = LANES // HEAD_DIM
    cos_t = jnp.tile(jnp.concatenate([cos, cos], axis=1), (1, reps))
    sin_t = jnp.tile(jnp.concatenate([-sin, sin], axis=1), (1, reps))
    return cos_t, sin_t


def kernel(x, p, w_in, ret_decay_fwd, ret_decay_bwd, ret_gn_gain, na_rpb, w_out, ln1_gain, ln1_bias,
           w_router, router_bias, w_expert_gu, w_expert_down, w_shared_gu, w_shared_down,
           w_ple_proj, w_ple_gate, ln2_gain, ln2_bias):
    batch, seq, d = x.shape
    t = batch * seq
    depth = w_in.shape[0]
    assert depth == 1 and d == D_MODEL
    assert seq % PROJ_TM == 0 and seq % MOE_TM == 0 and seq % RET_CHUNK == 0
    assert (seq // GRID_W) % NA_ROWS == 0 and seq // GRID_W >= NA_KEY_ROWS
    cos_t, sin_t = _rotary_tables(seq)
    x2 = x.reshape(t, d)
    for i in range(depth):
        rq, rk, rv, rg, nq, nk, nv = _in_proj(x2, w_in[i].astype(BF16), cos_t, sin_t, seq)
        ret = _retention(rq, rk, rv, rg, ret_decay_fwd[i], ret_decay_bwd[i], ret_gn_gain[i], batch, seq)
        na = _natten(nq, nk, nv, na_rpb[i], batch, seq)
        x1, x1b = _out_ln1(ret, na, x2, w_out[i].astype(BF16), ln1_gain[i], ln1_bias[i])
        w_t, rank_t, cnt = _route(x1b, w_router[i], router_bias[i])
        counts = cnt[:, ::LANES]
        ffn = _moe(x1b, rank_t, w_t, counts, w_expert_gu[i].astype(BF16), w_expert_down[i].astype(BF16))
        x2 = _final(x1, ffn, p[i].reshape(t, -1), w_shared_gu[i].astype(BF16), w_shared_down[i].astype(BF16),
                    w_ple_proj[i].astype(BF16), w_ple_gate[i].astype(BF16), ln2_gain[i], ln2_bias[i])
    return x2.reshape(batch, seq, d)
```

```python
import functools

import numpy as np
import jax
import jax.numpy as jnp
from jax import lax
from jax.experimental import pallas as pl
from jax.experimental.pallas import tpu as pltpu

F32 = jnp.float32
BF16 = jnp.bfloat16
I32 = jnp.int32

D_MODEL = 1024
HEADS = 8
HEAD_DIM = 64
GROUP_W = HEADS * HEAD_DIM
ROPE_BASE = 10000.0
GN_EPS = 1e-6
LN_EPS = 1e-5
GRID_W = 64
NA_WIN_ROWS = 8
NA_WIN_COLS = 16
N_EXPERTS = 64
N_GROUPS = 8
GROUP_SIZE = N_EXPERTS // N_GROUPS
TOPK_GROUPS = 4
TOP_K = 8
EXPERT_DIM = 256
ROUTED_SCALE = 2.5
ALPHA = 2.0 ** 0.25
NEG_BIG = -1e30

LANES = 128
VMEM_LIMIT_BYTES = 56 * 1024 * 1024

PROJ_TM = 512
RET_CHUNK = 128
RET_UNROLL = 8
NA_ROWS = 4
NA_KEY_ROWS = NA_ROWS + NA_WIN_ROWS
MOE_TM = 2048
MOE_SUB = 256
MOE_EB = 4
MOE_SLOTS = MOE_SUB // MOE_EB
MOE_NSUB = MOE_TM // MOE_SUB


def _cparams(sem):
    return pltpu.CompilerParams(dimension_semantics=sem, vmem_limit_bytes=VMEM_LIMIT_BYTES)


def _dot(a, b):
    return jnp.dot(a, b, preferred_element_type=F32)


def _dot_nt(a, b):
    return lax.dot_general(a, b, (((1,), (1,)), ((), ())), preferred_element_type=F32)


def _dot_tn(a, b):
    return lax.dot_general(a, b, (((0,), (0,)), ((), ())), preferred_element_type=F32)


def _in_proj_kernel(x_ref, w_ref, cos_ref, sin_ref,
                    rq_ref, rk_ref, rv_ref, rg_ref, nq_ref, nk_ref, nv_ref):
    xb = x_ref[...].astype(BF16)
    cos = cos_ref[...]
    sin = sin_ref[...]
    lane = lax.broadcasted_iota(I32, (1, LANES), 1)
    first_half = (lane % HEAD_DIM) < (HEAD_DIM // 2)

    def proj(g):
        return _dot(xb, w_ref[:, g * GROUP_W:(g + 1) * GROUP_W])

    def rotary(t, scale):
        outs = []
        for j in range(GROUP_W // LANES):
            c = t[:, j * LANES:(j + 1) * LANES]
            swapped = jnp.where(first_half,
                                pltpu.roll(c, LANES - HEAD_DIM // 2, axis=1),
                                pltpu.roll(c, HEAD_DIM // 2, axis=1))
            outs.append((c * cos + swapped * sin) * scale)
        return jnp.concatenate(outs, axis=1)

    rq_ref[...] = rotary(proj(0), 1.0).astype(BF16)
    rk_ref[...] = rotary(proj(1), HEAD_DIM ** -0.5).astype(BF16)
    rv_ref[...] = proj(2).astype(BF16)
    rg_ref[...] = jax.nn.silu(proj(3)).astype(BF16)
    nq_ref[...] = proj(4).astype(BF16)
    nk_ref[...] = proj(5).astype(BF16)
    nv_ref[...] = proj(6).astype(BF16)


def _in_proj(x2, w_in_b, cos_t, sin_t, seq):
    t = x2.shape[0]
    tm = PROJ_TM
    n_pos = seq // tm
    out = jax.ShapeDtypeStruct((t, GROUP_W), BF16)
    tok = lambda i: (i, 0)
    return pl.pallas_call(
        _in_proj_kernel,
        grid=(t // tm,),
        in_specs=[
            pl.BlockSpec((tm, D_MODEL), tok),
            pl.BlockSpec(w_in_b.shape, lambda i: (0, 0)),
            pl.BlockSpec((tm, LANES), lambda i: (i % n_pos, 0)),
            pl.BlockSpec((tm, LANES), lambda i: (i % n_pos, 0)),
        ],
        out_specs=[pl.BlockSpec((tm, GROUP_W), tok)] * 7,
        out_shape=[out] * 7,
        compiler_params=_cparams(("parallel",)),
        name="in_proj",
    )(x2, w_in_b, cos_t, sin_t)


def _log_sigmoid(x):
    return jnp.minimum(x, 0.0) - jnp.log1p(jnp.exp(-jnp.abs(x)))


def _retention_kernel(q_ref, k_ref, v_ref, g_ref, decf_ref, decb_ref, decfd_ref, decbd_ref,
                      gain_ref, o_ref, kv_ref, st_ref, dmat_ref):
    c = RET_CHUNK
    n_chunks = q_ref.shape[0] // c
    pair_w = 2 * HEAD_DIM

    lgf = _log_sigmoid(decf_ref[...])
    lgb = _log_sigmoid(decb_ref[...])
    row = lax.broadcasted_iota(I32, (c, 1), 0).astype(F32)
    k_dec_f = jnp.exp((c - 1.0 - row) * lgf)
    k_dec_b = jnp.exp(row * lgb)
    q_dec_f = jnp.exp((row + 1.0) * lgf)
    q_dec_b = jnp.exp((c - row) * lgb)
    chunk_dec_f = jnp.exp(c * lgf)
    chunk_dec_b = jnp.exp(c * lgb)

    lane = lax.broadcasted_iota(I32, (1, pair_w), 1)
    head0 = lane < HEAD_DIM
    r2 = lax.broadcasted_iota(I32, (pair_w, pair_w), 0) // HEAD_DIM
    c2 = lax.broadcasted_iota(I32, (pair_w, pair_w), 1) // HEAD_DIM
    same_head = r2 == c2
    block_diag = jnp.where(same_head, 1.0, 0.0)
    seg_avg = jnp.where(same_head, 1.0 / HEAD_DIM, 0.0).astype(BF16)

    lgf_d = _log_sigmoid(decfd_ref[0])
    lgb_d = _log_sigmoid(decbd_ref[0])
    di = lax.broadcasted_iota(I32, (c, 2 * c), 0)
    dj = lax.broadcasted_iota(I32, (c, 2 * c), 1) % c
    diff = (di - dj).astype(F32)
    dmat_ref[...] = jnp.where(diff >= 0.0, jnp.exp(diff * lgf_d), jnp.exp(-diff * lgb_d))

    def chunk(ref, n):
        return ref[pl.ds(pl.multiple_of(n * c, c), c), :]

    unroll = RET_UNROLL

    def summarize(nb, carry):
        for u in range(unroll):
            n = nb * unroll + u
            kf = chunk(k_ref, n).astype(F32)
            kst = jnp.concatenate([kf * k_dec_f, kf * k_dec_b], axis=1).astype(BF16)
            kv_ref[n] = _dot_tn(kst, chunk(v_ref, n))
        return carry

    lax.fori_loop(0, n_chunks // unroll, summarize, 0)

    def fwd_scan(n, state):
        st_ref[n, 0:pair_w, :] = state.astype(BF16)
        return state * chunk_dec_f + kv_ref[n, 0:pair_w, :] * block_diag

    lax.fori_loop(0, n_chunks, fwd_scan, jnp.zeros((pair_w, pair_w), F32))

    def bwd_scan(i, state):
        n = n_chunks - 1 - i
        st_ref[n, pair_w:2 * pair_w, :] = state.astype(BF16)
        return state * chunk_dec_b + kv_ref[n, pair_w:2 * pair_w, :] * block_diag

    lax.fori_loop(0, n_chunks, bwd_scan, jnp.zeros((pair_w, pair_w), F32))

    gain = gain_ref[...]

    def seg_mean(z):
        hi = z.astype(BF16)
        lo = (z - hi.astype(F32)).astype(BF16)
        return _dot(hi, seg_avg) + _dot(lo, seg_avg)

    def emit(nb, carry):
        ys = []
        for u in range(unroll):
            n = nb * unroll + u
            q = chunk(q_ref, n)
            k = chunk(k_ref, n)
            v = chunk(v_ref, n)
            zero = jnp.zeros_like(k)
            k_st = jnp.concatenate([jnp.where(head0, k, zero), jnp.where(head0, zero, k)], axis=0)
            v_st = jnp.concatenate([jnp.where(head0, v, zero), jnp.where(head0, zero, v)], axis=0)
            scores = _dot_nt(q, k_st) * dmat_ref[...]
            qf = q.astype(F32)
            q_st = jnp.concatenate([qf * q_dec_f, qf * q_dec_b], axis=1).astype(BF16)
            ys.append(_dot(scores.astype(BF16), v_st) + _dot(q_st, st_ref[n]))
        y = jnp.concatenate(ys, axis=0)
        mu = seg_mean(y)
        d = y - mu
        var = seg_mean(d * d)
        yn = d * lax.rsqrt(var + GN_EPS) * gain
        rows = pl.ds(pl.multiple_of(nb * (unroll * c), unroll * c), unroll * c)
        o_ref[rows, :] = (g_ref[rows, :].astype(F32) * yn).astype(BF16)
        return carry

    lax.fori_loop(0, n_chunks // unroll, emit, 0)


def _retention(rq, rk, rv, rg, dec_f, dec_b, gain, batch, seq):
    t = rq.shape[0]
    c = RET_CHUNK
    n_pairs = HEADS // 2
    pair_w = 2 * HEAD_DIM
    dec_f_lane = jnp.repeat(dec_f.astype(F32), HEAD_DIM)[None, :]
    dec_b_lane = jnp.repeat(dec_b.astype(F32), HEAD_DIM)[None, :]
    dec_f_col = jnp.repeat(dec_f.astype(F32), c).reshape(n_pairs, 1, 2 * c)
    dec_b_col = jnp.repeat(dec_b.astype(F32), c).reshape(n_pairs, 1, 2 * c)
    tok = pl.BlockSpec((seq, pair_w), lambda b, p: (b, p))
    lane_spec = pl.BlockSpec((1, pair_w), lambda b, p: (0, p))
    col_spec = pl.BlockSpec((1, 1, 2 * c), lambda b, p: (p, 0, 0))
    return pl.pallas_call(
        _retention_kernel,
        grid=(batch, n_pairs),
        in_specs=[tok, tok, tok, tok, lane_spec, lane_spec, col_spec, col_spec, lane_spec],
        out_specs=tok,
        out_shape=jax.ShapeDtypeStruct((t, GROUP_W), BF16),
        scratch_shapes=[
            pltpu.VMEM((seq // c, 2 * pair_w, pair_w), F32),
            pltpu.VMEM((seq // c, 2 * pair_w, pair_w), BF16),
            pltpu.VMEM((c, 2 * c), F32),
        ],
        compiler_params=_cparams(("parallel", "parallel")),
        name="retention",
    )(rq, rk, rv, rg, dec_f_lane, dec_b_lane, dec_f_col, dec_b_col, gain.astype(F32)[None, :])


N_ROW_OFFSETS = 2 * NA_WIN_ROWS - 1
N_COL_OFFSETS = 2 * NA_WIN_COLS - 1


def _natten_row_offsets(rows):
    n_blocks = rows // NA_ROWS
    starts = {0: 0, 1: NA_ROWS - NA_WIN_ROWS // 2, 2: rows - NA_KEY_ROWS}
    blocks = {0: 0, 1: 1, 2: n_blocks - 1}
    table = []
    for v in range(3):
        per_a = []
        for a in range(NA_ROWS):
            r = blocks[v] * NA_ROWS + a
            rs = min(max(r - NA_WIN_ROWS // 2, 0), rows - NA_WIN_ROWS)
            per_kl = []
            for kl in range(NA_KEY_ROWS):
                kr = starts[v] + kl
                per_kl.append(kr - r + NA_WIN_ROWS - 1 if rs <= kr < rs + NA_WIN_ROWS else None)
            per_a.append(per_kl)
        table.append(per_a)
    return table


def _natten_bias_kernel(rpb_ref, tab_ref, *, rows):
    offsets = _natten_row_offsets(rows)
    nk = NA_KEY_ROWS * GRID_W
    shape = (GRID_W, LANES)
    lane = lax.broadcasted_iota(I32, shape, 1)
    c = lax.broadcasted_iota(I32, shape, 0)
    second = lane >= GRID_W
    kc = lane % GRID_W
    cs = jnp.clip(c - NA_WIN_COLS // 2, 0, GRID_W - NA_WIN_COLS)
    col_ok = jnp.logical_and(kc >= cs, kc < cs + NA_WIN_COLS)
    neg = jnp.full(shape, NEG_BIG, F32)
    for hh in range(2):
        toeplitz = []
        for dr in range(N_ROW_OFFSETS):
            x = jnp.broadcast_to(rpb_ref[hh, dr:dr + 1, :], shape)
            lo = pltpu.roll(x, LANES - (NA_WIN_COLS - 1), axis=1, stride=1, stride_axis=0)
            hi = pltpu.roll(x, GRID_W - (NA_WIN_COLS - 1), axis=1, stride=1, stride_axis=0)
            toeplitz.append(jnp.where(second, hi, lo))
        for v in range(3):
            for a in range(NA_ROWS):
                for j in range(NA_KEY_ROWS // 2):
                    d0, d1 = offsets[v][a][2 * j], offsets[v][a][2 * j + 1]
                    if d0 is None and d1 is None:
                        piece = neg
                    else:
                        t0 = neg if d0 is None else toeplitz[d0]
                        t1 = neg if d1 is None else toeplitz[d1]
                        piece = jnp.where(col_ok, jnp.where(second, t1, t0), neg)
                    tab_ref[v, 0, a * GRID_W:(a + 1) * GRID_W,
                            hh * nk + j * LANES:hh * nk + (j + 1) * LANES] = piece


def _natten_bias_table(rpb, rows):
    n_pairs = HEADS // 2
    nqb = NA_ROWS * GRID_W
    nkb = NA_KEY_ROWS * GRID_W
    rpb_pad = jnp.pad(rpb.astype(F32), ((0, 0), (0, 0), (0, LANES - N_COL_OFFSETS)))
    return pl.pallas_call(
        functools.partial(_natten_bias_kernel, rows=rows),
        grid=(n_pairs,),
        in_specs=[pl.BlockSpec((2, N_ROW_OFFSETS, LANES), lambda p: (p, 0, 0))],
        out_specs=pl.BlockSpec((3, 1, nqb, 2 * nkb), lambda p: (0, p, 0, 0)),
        out_shape=jax.ShapeDtypeStruct((3, n_pairs, nqb, 2 * nkb), F32),
        compiler_params=_cparams(("parallel",)),
        name="natten_bias",
    )(rpb_pad)


def _natten_kernel(q_ref, k_ref, v_ref, tab_ref, o_ref, *, rows):
    rb = pl.program_id(2)
    nk = NA_KEY_ROWS * GRID_W
    start_row = jnp.clip(rb * NA_ROWS - NA_WIN_ROWS // 2, 0, rows - NA_KEY_ROWS)
    start = pl.multiple_of(start_row * GRID_W, GRID_W)
    q = q_ref[...]
    k = k_ref[pl.ds(start, nk), :]
    v = v_ref[pl.ds(start, nk), :]
    lane = lax.broadcasted_iota(I32, (1, 2 * HEAD_DIM), 1)
    head0 = lane < HEAD_DIM
    zero = jnp.zeros_like(k)
    k_st = jnp.concatenate([jnp.where(head0, k, zero), jnp.where(head0, zero, k)], axis=0)
    v_st = jnp.concatenate([jnp.where(head0, v, zero), jnp.where(head0, zero, v)], axis=0)
    s = _dot_nt(q, k_st) * (HEAD_DIM ** -0.5) + tab_ref[0, 0]
    probs, denoms = [], []
    for h in range(2):
        sh = s[:, h * nk:(h + 1) * nk]
        m = jnp.max(sh, axis=1, keepdims=True)
        p = jnp.exp(sh - m)
        denoms.append(jnp.sum(p, axis=1, keepdims=True))
        probs.append(p.astype(BF16))
    out = _dot(jnp.concatenate(probs, axis=1), v_st)
    denom = jnp.where(head0, denoms[0], denoms[1])
    o_ref[...] = (out / denom).astype(BF16)


def _natten(nq, nk, nv, rpb, batch, seq):
    t = nq.shape[0]
    rows = seq // GRID_W
    n_pairs = HEADS // 2
    n_blocks = rows // NA_ROWS
    nqb = NA_ROWS * GRID_W
    nkb = NA_KEY_ROWS * GRID_W
    tab = _natten_bias_table(rpb, rows)

    def variant(rb):
        return jnp.where(rb == 0, 0, jnp.where(rb == n_blocks - 1, 2, 1))

    kv_spec = pl.BlockSpec((seq, 2 * HEAD_DIM), lambda b, p, rb: (b, p))
    q_spec = pl.BlockSpec((nqb, 2 * HEAD_DIM), lambda b, p, rb: (b * n_blocks + rb, p))
    return pl.pallas_call(
        functools.partial(_natten_kernel, rows=rows),
        grid=(batch, n_pairs, n_blocks),
        in_specs=[q_spec, kv_spec, kv_spec,
                  pl.BlockSpec((1, 1, nqb, 2 * nkb), lambda b, p, rb: (variant(rb), p, 0, 0))],
        out_specs=q_spec,
        out_shape=jax.ShapeDtypeStruct((t, GROUP_W), BF16),
        compiler_params=_cparams(("parallel", "parallel", "arbitrary")),
        name="natten",
    )(nq, nk, nv, tab)


def _layer_norm(h, gain, bias):
    mu = jnp.mean(h, axis=-1, keepdims=True)
    d = h - mu
    var = jnp.mean(d * d, axis=-1, keepdims=True)
    return d * lax.rsqrt(var + LN_EPS) * gain + bias


def _out_ln1_kernel(ret_ref, na_ref, x_ref, w_ref, gain_ref, bias_ref, x1_ref, x1b_ref):
    mix = _dot(ret_ref[...], w_ref[0:GROUP_W, :]) + _dot(na_ref[...], w_ref[GROUP_W:2 * GROUP_W, :])
    x1 = _layer_norm(ALPHA * x_ref[...] + mix, gain_ref[...], bias_ref[...])
    x1_ref[...] = x1
    x1b_ref[...] = x1.astype(BF16)


def _out_ln1(ret, na, x2, w_out_b, gain, bias):
    t = x2.shape[0]
    tm = PROJ_TM
    tok = lambda i: (i, 0)
    const = lambda i: (0, 0)
    return pl.pallas_call(
        _out_ln1_kernel,
        grid=(t // tm,),
        in_specs=[pl.BlockSpec((tm, GROUP_W), tok), pl.BlockSpec((tm, GROUP_W), tok),
                  pl.BlockSpec((tm, D_MODEL), tok), pl.BlockSpec(w_out_b.shape, const),
                  pl.BlockSpec((1, D_MODEL), const), pl.BlockSpec((1, D_MODEL), const)],
        out_specs=[pl.BlockSpec((tm, D_MODEL), tok), pl.BlockSpec((tm, D_MODEL), tok)],
        out_shape=[jax.ShapeDtypeStruct((t, D_MODEL), F32), jax.ShapeDtypeStruct((t, D_MODEL), BF16)],
        compiler_params=_cparams(("parallel",)),
        name="out_ln1",
    )(ret, na, x2, w_out_b, gain.astype(F32)[None, :], bias.astype(F32)[None, :])


def _route_kernel(x_ref, wr_ref, rb_ref, w_ref, rank_ref, cnt_ref, tri_ref):
    tm = x_ref.shape[0]

    @pl.when(pl.program_id(0) == 0)
    def _():
        i = lax.broadcasted_iota(I32, (MOE_SUB, MOE_SUB), 0)
        j = lax.broadcasted_iota(I32, (MOE_SUB, MOE_SUB), 1)
        tri_ref[...] = jnp.where(i < j, 1.0, 0.0).astype(BF16)

    scores = jax.nn.sigmoid(_dot_nt(wr_ref[...], x_ref[...]))
    biased = scores + rb_ref[...]
    sub = lax.broadcasted_iota(I32, (GROUP_SIZE, tm), 0).astype(F32)
    none = float(N_EXPERTS)
    ninf = -jnp.inf

    def first_max(vals, index):
        m = jnp.max(vals, axis=0, keepdims=True)
        return m, jnp.min(jnp.where(vals == m, index, none), axis=0, keepdims=True)

    groups = [biased[g * GROUP_SIZE:(g + 1) * GROUP_SIZE, :] for g in range(N_GROUPS)]
    group_scores = []
    for g in range(N_GROUPS):
        m1, i1 = first_max(groups[g], sub)
        m2 = jnp.max(jnp.where(sub == i1, ninf, groups[g]), axis=0, keepdims=True)
        group_scores.append(m1 + m2)
    cur = jnp.concatenate(group_scores, axis=0)
    group_sel = jnp.zeros(cur.shape, F32)
    for _ in range(TOPK_GROUPS):
        _, i1 = first_max(cur, sub)
        hit = sub == i1
        group_sel = jnp.where(hit, 1.0, group_sel)
        cur = jnp.where(hit, ninf, cur)

    masked = [jnp.where(group_sel[g:g + 1, :] > 0.5, groups[g], ninf) for g in range(N_GROUPS)]
    ids = [sub + float(g * GROUP_SIZE) for g in range(N_GROUPS)]
    chosen = [jnp.zeros((GROUP_SIZE, tm), F32) for _ in range(N_GROUPS)]
    for _ in range(TOP_K):
        m = masked[0]
        for g in range(1, N_GROUPS):
            m = jnp.maximum(m, masked[g])
        m = jnp.max(m, axis=0, keepdims=True)
        cand = jnp.where(masked[0] == m, ids[0], none)
        for g in range(1, N_GROUPS):
            cand = jnp.minimum(cand, jnp.where(masked[g] == m, ids[g], none))
        first = jnp.min(cand, axis=0, keepdims=True)
        for g in range(N_GROUPS):
            hit = ids[g] == first
            chosen[g] = jnp.where(hit, 1.0, chosen[g])
            masked[g] = jnp.where(hit, ninf, masked[g])

    sel = jnp.concatenate(chosen, axis=0) > 0.5
    picked = jnp.where(sel, scores, 0.0)
    total = jnp.sum(picked, axis=0, keepdims=True)
    w_ref[...] = picked / total * ROUTED_SCALE
    sel_f = jnp.where(sel, 1.0, 0.0)
    sel_b = sel_f.astype(BF16)
    before, cnt_max = [], None
    for k in range(MOE_NSUB):
        cols = slice(k * MOE_SUB, (k + 1) * MOE_SUB)
        before.append(_dot(sel_b[:, cols], tri_ref[...]))
        cnt = jnp.sum(sel_f[:, cols], axis=1, keepdims=True)
        cnt_max = cnt if cnt_max is None else jnp.maximum(cnt_max, cnt)
    rank_ref[...] = jnp.where(sel, jnp.concatenate(before, axis=1).astype(I32), -1)
    cnt_ref[...] = jnp.broadcast_to(cnt_max, cnt_ref.shape).astype(I32)


def _route(x1b, w_router, router_bias):
    t = x1b.shape[0]
    tm = MOE_TM
    nt = t // tm
    wr_t = w_router.astype(F32).T.astype(BF16)
    const = lambda i: (0, 0)
    col = lambda i: (0, i)
    return pl.pallas_call(
        _route_kernel,
        grid=(nt,),
        in_specs=[pl.BlockSpec((tm, D_MODEL), lambda i: (i, 0)),
                  pl.BlockSpec((N_EXPERTS, D_MODEL), const),
                  pl.BlockSpec((N_EXPERTS, 1), const)],
        out_specs=[pl.BlockSpec((N_EXPERTS, tm), col), pl.BlockSpec((N_EXPERTS, tm), col),
                   pl.BlockSpec((N_EXPERTS, LANES), col)],
        out_shape=[jax.ShapeDtypeStruct((N_EXPERTS, t), F32),
                   jax.ShapeDtypeStruct((N_EXPERTS, t), I32),
                   jax.ShapeDtypeStruct((N_EXPERTS, nt * LANES), I32)],
        scratch_shapes=[pltpu.VMEM((MOE_SUB, MOE_SUB), BF16)],
        compiler_params=_cparams(("arbitrary",)),
        name="route",
    )(x1b, wr_t, router_bias.astype(F32)[:, None])


def _moe_kernel(cnt_ref, x_ref, rank_ref, w_ref, wgu_ref, wd_ref, o_ref, oh_ref, g_ref, y_ref):
    i = pl.program_id(0)
    eb = pl.program_id(1)

    @pl.when(eb == 0)
    def _():
        o_ref[...] = jnp.zeros_like(o_ref)

    n_pass = lax.shift_right_logical(cnt_ref[eb, i] + (MOE_SLOTS - 1), int(np.log2(MOE_SLOTS)))
    slot = lax.broadcasted_iota(I32, (MOE_SLOTS, MOE_SUB), 0)

    def one_pass(p, carry):
        base = p * MOE_SLOTS
        slot_w = []
        for k in range(MOE_NSUB):
            tok = slice(k * MOE_SUB, (k + 1) * MOE_SUB)
            blocks, weights = [], []
            for j in range(MOE_EB):
                match = slot == (rank_ref[j, :, tok] - base)
                blocks.append(jnp.where(match, 1.0, 0.0).astype(BF16))
                weights.append(jnp.sum(jnp.where(match, w_ref[j, :, tok], 0.0), axis=1, keepdims=True))
            onehot = jnp.concatenate(blocks, axis=0)
            oh_ref[k] = onehot
            slot_w.append(weights)
            g_ref[k] = _dot(onehot, x_ref[tok, :]).astype(BF16)
        for j in range(MOE_EB):
            rows = slice(j * MOE_SLOTS, (j + 1) * MOE_SLOTS)
            xe = jnp.concatenate([g_ref[k, rows, :] for k in range(MOE_NSUB)], axis=0)
            h = _dot(xe, wgu_ref[j])
            act = (jax.nn.silu(h[:, :EXPERT_DIM]) * h[:, EXPERT_DIM:]).astype(BF16)
            wc = jnp.concatenate([slot_w[k][j] for k in range(MOE_NSUB)], axis=0)
            yw = (_dot(act, wd_ref[j]) * wc).astype(BF16)
            for k in range(MOE_NSUB):
                y_ref[k, rows, :] = yw[k * MOE_SLOTS:(k + 1) * MOE_SLOTS, :]
        for k in range(MOE_NSUB):
            tok = slice(k * MOE_SUB, (k + 1) * MOE_SUB)
            o_ref[tok, :] += _dot_tn(oh_ref[k], y_ref[k])
        return carry

    lax.fori_loop(0, n_pass, one_pass, 0)


def _moe(x1b, rank_t, w_t, counts, wgu_b, wd_b):
    t = x1b.shape[0]
    tm = MOE_TM
    nt = t // tm
    rank3 = rank_t.reshape(N_EXPERTS, 1, t)
    w3 = w_t.reshape(N_EXPERTS, 1, t)
    counts = counts.reshape(N_EXPERTS // MOE_EB, MOE_EB, nt).max(axis=1)
    grid_spec = pltpu.PrefetchScalarGridSpec(
        num_scalar_prefetch=1,
        grid=(nt, N_EXPERTS // MOE_EB),
        in_specs=[
            pl.BlockSpec((tm, D_MODEL), lambda i, e, c: (i, 0)),
            pl.BlockSpec((MOE_EB, 1, tm), lambda i, e, c: (e, 0, i)),
            pl.BlockSpec((MOE_EB, 1, tm), lambda i, e, c: (e, 0, i)),
            pl.BlockSpec((MOE_EB, D_MODEL, 2 * EXPERT_DIM), lambda i, e, c: (e, 0, 0)),
            pl.BlockSpec((MOE_EB, EXPERT_DIM, D_MODEL), lambda i, e, c: (e, 0, 0)),
        ],
        out_specs=pl.BlockSpec((tm, D_MODEL), lambda i, e, c: (i, 0)),
        scratch_shapes=[pltpu.VMEM((MOE_NSUB, MOE_SUB, MOE_SUB), BF16),
                        pltpu.VMEM((MOE_NSUB, MOE_SUB, D_MODEL), BF16),
                        pltpu.VMEM((MOE_NSUB, MOE_SUB, D_MODEL), BF16)],
    )
    return pl.pallas_call(
        _moe_kernel,
        grid_spec=grid_spec,
        out_shape=jax.ShapeDtypeStruct((t, D_MODEL), F32),
        compiler_params=_cparams(("parallel", "arbitrary")),
        name="moe",
    )(counts, x1b, rank3, w3, wgu_b, wd_b)


def _final_kernel(x1_ref, ffn_ref, p_ref, wsgu_ref, wsd_ref, wp_ref, wg_ref, gain_ref, bias_ref, o_ref):
    x1 = x1_ref[...]
    xb = x1.astype(BF16)
    h = _dot(xb, wsgu_ref[...])
    act = (jax.nn.silu(h[:, :EXPERT_DIM]) * h[:, EXPERT_DIM:]).astype(BF16)
    shared = _dot(act, wsd_ref[...])
    ple = _dot(p_ref[...].astype(BF16), wp_ref[...]) * jax.nn.sigmoid(_dot(xb, wg_ref[...]))
    o_ref[...] = _layer_norm(ALPHA * x1 + (ffn_ref[...] + shared) + ple, gain_ref[...], bias_ref[...])


def _final(x1, ffn, p2, wsgu_b, wsd_b, wp_b, wg_b, gain, bias):
    t = x1.shape[0]
    tm = PROJ_TM
    tok = lambda i: (i, 0)
    const = lambda i: (0, 0)
    full = lambda a: pl.BlockSpec(a.shape, const)
    return pl.pallas_call(
        _final_kernel,
        grid=(t // tm,),
        in_specs=[pl.BlockSpec((tm, D_MODEL), tok), pl.BlockSpec((tm, D_MODEL), tok),
                  pl.BlockSpec((tm, p2.shape[1]), tok),
                  full(wsgu_b), full(wsd_b), full(wp_b), full(wg_b),
                  pl.BlockSpec((1, D_MODEL), const), pl.BlockSpec((1, D_MODEL), const)],
        out_specs=pl.BlockSpec((tm, D_MODEL), tok),
        out_shape=jax.ShapeDtypeStruct((t, D_MODEL), F32),
        compiler_params=_cparams(("parallel",)),
        name="final",
    )(x1, ffn, p2, wsgu_b, wsd_b, wp_b, wg_b, gain.astype(F32)[None, :], bias.astype(F32)[None, :])


def _rotary_tables(seq):
    half = HEAD_DIM // 2
    inv = ROPE_BASE ** (-jnp.arange(half, dtype=F32) / half)
    ang = jnp.arange(seq, dtype=jnp.int32).astype(F32)[:, None] * inv[None, :]
    cos, sin = jnp.cos(ang), jnp.sin(ang)
    reps = LANES // HEAD_DIM
    cos_t = jnp.tile(jnp.concatenate([cos, cos], axis=1), (1, reps))
    sin_t = jnp.tile(jnp.concatenate([-sin, sin], axis=1), (1, reps))
    return cos_t, sin_t


def kernel(x, p, w_in, ret_decay_fwd, ret_decay_bwd, ret_gn_gain, na_rpb, w_out, ln1_gain, ln1_bias,
           w_router, router_bias, w_expert_gu, w_expert_down, w_shared_gu, w_shared_down,
           w_ple_proj, w_ple_gate, ln2_gain, ln2_bias):
    batch, seq, d = x.shape
    t = batch * seq
    depth = w_in.shape[0]
    assert depth == 1 and d == D_MODEL
    assert seq % PROJ_TM == 0 and seq % MOE_TM == 0 and seq % RET_CHUNK == 0
    assert (seq // GRID_W) % NA_ROWS == 0 and seq // GRID_W >= NA_KEY_ROWS
    cos_t, sin_t = _rotary_tables(seq)
    x2 = x.reshape(t, d)
    for i in range(depth):
        rq, rk, rv, rg, nq, nk, nv = _in_proj(x2, w_in[i].astype(BF16), cos_t, sin_t, seq)
        ret = _retention(rq, rk, rv, rg, ret_decay_fwd[i], ret_decay_bwd[i], ret_gn_gain[i], batch, seq)
        na = _natten(nq, nk, nv, na_rpb[i], batch, seq)
        x1, x1b = _out_ln1(ret, na, x2, w_out[i].astype(BF16), ln1_gain[i], ln1_bias[i])
        w_t, rank_t, cnt = _route(x1b, w_router[i], router_bias[i])
        counts = cnt[:, ::LANES]
        ffn = _moe(x1b, rank_t, w_t, counts, w_expert_gu[i].astype(BF16), w_expert_down[i].astype(BF16))
        x2 = _final(x1, ffn, p[i].reshape(t, -1), w_shared_gu[i].astype(BF16), w_shared_down[i].astype(BF16),
                    w_ple_proj[i].astype(BF16), w_ple_gate[i].astype(BF16), ln2_gain[i], ln2_bias[i])
    return x2.reshape(batch, seq, d)
```

```python
import functools

import numpy as np
import jax
import jax.numpy as jnp
from jax import lax
from jax.experimental import pallas as pl
from jax.experimental.pallas import tpu as pltpu

F32 = jnp.float32
BF16 = jnp.bfloat16
I32 = jnp.int32

D_MODEL = 1024
HEADS = 8
HEAD_DIM = 64
GROUP_W = HEADS * HEAD_DIM
ROPE_BASE = 10000.0
GN_EPS = 1e-6
LN_EPS = 1e-5
GRID_W = 64
NA_WIN_ROWS = 8
NA_WIN_COLS = 16
N_EXPERTS = 64
N_GROUPS = 8
GROUP_SIZE = N_EXPERTS // N_GROUPS
TOPK_GROUPS = 4
TOP_K = 8
EXPERT_DIM = 256
ROUTED_SCALE = 2.5
ALPHA = 2.0 ** 0.25
NEG_BIG = -1e30

LANES = 128
VMEM_LIMIT_BYTES = 56 * 1024 * 1024

PROJ_TM = 512
RET_CHUNK = 128
RET_UNROLL = 8
NA_ROWS = 4
NA_KEY_ROWS = NA_ROWS + NA_WIN_ROWS
NA_STEP_BLOCKS = 2
MOE_TM = 2048
MOE_SUB = 256
MOE_EB = 4
MOE_SLOTS = MOE_SUB // MOE_EB
MOE_NSUB = MOE_TM // MOE_SUB


def _cparams(sem):
    return pltpu.CompilerParams(dimension_semantics=sem, vmem_limit_bytes=VMEM_LIMIT_BYTES)


def _dot(a, b):
    return jnp.dot(a, b, preferred_element_type=F32)


def _dot_nt(a, b):
    return lax.dot_general(a, b, (((1,), (1,)), ((), ())), preferred_element_type=F32)


def _dot_tn(a, b):
    return lax.dot_general(a, b, (((0,), (0,)), ((), ())), preferred_element_type=F32)


def _in_proj_kernel(x_ref, w_ref, cos_ref, sin_ref,
                    rq_ref, rk_ref, rv_ref, rg_ref, nq_ref, nk_ref, nv_ref):
    xb = x_ref[...].astype(BF16)
    cos = cos_ref[...]
    sin = sin_ref[...]
    lane = lax.broadcasted_iota(I32, (1, LANES), 1)
    first_half = (lane % HEAD_DIM) < (HEAD_DIM // 2)

    def proj(g):
        return _dot(xb, w_ref[:, g * GROUP_W:(g + 1) * GROUP_W])

    def rotary(t, scale):
        outs = []
        for j in range(GROUP_W // LANES):
            c = t[:, j * LANES:(j + 1) * LANES]
            swapped = jnp.where(first_half,
                                pltpu.roll(c, LANES - HEAD_DIM // 2, axis=1),
                                pltpu.roll(c, HEAD_DIM // 2, axis=1))
            outs.append((c * cos + swapped * sin) * scale)
        return jnp.concatenate(outs, axis=1)

    rq_ref[...] = rotary(proj(0), 1.0).astype(BF16)
    rk_ref[...] = rotary(proj(1), HEAD_DIM ** -0.5).astype(BF16)
    rv_ref[...] = proj(2).astype(BF16)
    rg_ref[...] = jax.nn.silu(proj(3)).astype(BF16)
    nq_ref[...] = (proj(4) * HEAD_DIM ** -0.5).astype(BF16)
    nk_ref[...] = proj(5).astype(BF16)
    nv_ref[...] = proj(6).astype(BF16)


def _in_proj(x2, w_in_b, cos_t, sin_t, seq):
    t = x2.shape[0]
    tm = PROJ_TM
    n_pos = seq // tm
    out = jax.ShapeDtypeStruct((t, GROUP_W), BF16)
    tok = lambda i: (i, 0)
    return pl.pallas_call(
        _in_proj_kernel,
        grid=(t // tm,),
        in_specs=[
            pl.BlockSpec((tm, D_MODEL), tok),
            pl.BlockSpec(w_in_b.shape, lambda i: (0, 0)),
            pl.BlockSpec((tm, LANES), lambda i: (i % n_pos, 0)),
            pl.BlockSpec((tm, LANES), lambda i: (i % n_pos, 0)),
        ],
        out_specs=[pl.BlockSpec((tm, GROUP_W), tok)] * 7,
        out_shape=[out] * 7,
        compiler_params=_cparams(("parallel",)),
        name="in_proj",
    )(x2, w_in_b, cos_t, sin_t)


def _log_sigmoid(x):
    return jnp.minimum(x, 0.0) - jnp.log1p(jnp.exp(-jnp.abs(x)))


def _retention_kernel(q_ref, k_ref, v_ref, g_ref, decf_ref, decb_ref, decfd_ref, decbd_ref,
                      gain_ref, o_ref, kv_ref, st_ref, dmat_ref):
    c = RET_CHUNK
    n_chunks = q_ref.shape[0] // c
    pair_w = 2 * HEAD_DIM

    lgf = _log_sigmoid(decf_ref[...])
    lgb = _log_sigmoid(decb_ref[...])
    row = lax.broadcasted_iota(I32, (c, 1), 0).astype(F32)
    k_dec_f = jnp.exp((c - 1.0 - row) * lgf)
    k_dec_b = jnp.exp(row * lgb)
    q_dec_f = jnp.exp((row + 1.0) * lgf)
    q_dec_b = jnp.exp((c - row) * lgb)
    chunk_dec_f = jnp.exp(c * lgf)
    chunk_dec_b = jnp.exp(c * lgb)

    lane = lax.broadcasted_iota(I32, (1, pair_w), 1)
    head0 = lane < HEAD_DIM
    r2 = lax.broadcasted_iota(I32, (pair_w, pair_w), 0) // HEAD_DIM
    c2 = lax.broadcasted_iota(I32, (pair_w, pair_w), 1) // HEAD_DIM
    same_head = r2 == c2
    block_diag = jnp.where(same_head, 1.0, 0.0)
    seg_avg = jnp.where(same_head, 1.0 / HEAD_DIM, 0.0).astype(BF16)

    lgf_d = _log_sigmoid(decfd_ref[0])
    lgb_d = _log_sigmoid(decbd_ref[0])
    di = lax.broadcasted_iota(I32, (c, 2 * c), 0)
    dj = lax.broadcasted_iota(I32, (c, 2 * c), 1) % c
    diff = (di - dj).astype(F32)
    dmat_ref[...] = jnp.where(diff >= 0.0, jnp.exp(diff * lgf_d), jnp.exp(-diff * lgb_d))

    def chunk(ref, n):
        return ref[pl.ds(pl.multiple_of(n * c, c), c), :]

    unroll = RET_UNROLL

    def summarize(nb, carry):
        for u in range(unroll):
            n = nb * unroll + u
            kf = chunk(k_ref, n).astype(F32)
            kst = jnp.concatenate([kf * k_dec_f, kf * k_dec_b], axis=1).astype(BF16)
            kv_ref[n] = _dot_tn(kst, chunk(v_ref, n))
        return carry

    lax.fori_loop(0, n_chunks // unroll, summarize, 0)

    def fwd_scan(n, state):
        st_ref[n, 0:pair_w, :] = state.astype(BF16)
        return state * chunk_dec_f + kv_ref[n, 0:pair_w, :] * block_diag

    lax.fori_loop(0, n_chunks, fwd_scan, jnp.zeros((pair_w, pair_w), F32))

    def bwd_scan(i, state):
        n = n_chunks - 1 - i
        st_ref[n, pair_w:2 * pair_w, :] = state.astype(BF16)
        return state * chunk_dec_b + kv_ref[n, pair_w:2 * pair_w, :] * block_diag

    lax.fori_loop(0, n_chunks, bwd_scan, jnp.zeros((pair_w, pair_w), F32))

    gain = gain_ref[...]

    def seg_mean(z):
        hi = z.astype(BF16)
        lo = (z - hi.astype(F32)).astype(BF16)
        return _dot(hi, seg_avg) + _dot(lo, seg_avg)

    def emit(nb, carry):
        ys = []
        for u in range(unroll):
            n = nb * unroll + u
            q = chunk(q_ref, n)
            k = chunk(k_ref, n)
            v = chunk(v_ref, n)
            zero = jnp.zeros_like(k)
            k_st = jnp.concatenate([jnp.where(head0, k, zero), jnp.where(head0, zero, k)], axis=0)
            v_st = jnp.concatenate([jnp.where(head0, v, zero), jnp.where(head0, zero, v)], axis=0)
            scores = _dot_nt(q, k_st) * dmat_ref[...]
            qf = q.astype(F32)
            q_st = jnp.concatenate([qf * q_dec_f, qf * q_dec_b], axis=1).astype(BF16)
            ys.append(_dot(scores.astype(BF16), v_st) + _dot(q_st, st_ref[n]))
        y = jnp.concatenate(ys, axis=0)
        mu = seg_mean(y)
        d = y - mu
        var = seg_mean(d * d)
        yn = d * lax.rsqrt(var + GN_EPS) * gain
        rows = pl.ds(pl.multiple_of(nb * (unroll * c), unroll * c), unroll * c)
        o_ref[rows, :] = (g_ref[rows, :].astype(F32) * yn).astype(BF16)
        return carry

    lax.fori_loop(0, n_chunks // unroll, emit, 0)


def _retention(rq, rk, rv, rg, dec_f, dec_b, gain, batch, seq):
    t = rq.shape[0]
    c = RET_CHUNK
    n_pairs = HEADS // 2
    pair_w = 2 * HEAD_DIM
    dec_f_lane = jnp.repeat(dec_f.astype(F32), HEAD_DIM)[None, :]
    dec_b_lane = jnp.repeat(dec_b.astype(F32), HEAD_DIM)[None, :]
    dec_f_col = jnp.repeat(dec_f.astype(F32), c).reshape(n_pairs, 1, 2 * c)
    dec_b_col = jnp.repeat(dec_b.astype(F32), c).reshape(n_pairs, 1, 2 * c)
    tok = pl.BlockSpec((seq, pair_w), lambda b, p: (b, p))
    lane_spec = pl.BlockSpec((1, pair_w), lambda b, p: (0, p))
    col_spec = pl.BlockSpec((1, 1, 2 * c), lambda b, p: (p, 0, 0))
    return pl.pallas_call(
        _retention_kernel,
        grid=(batch, n_pairs),
        in_specs=[tok, tok, tok, tok, lane_spec, lane_spec, col_spec, col_spec, lane_spec],
        out_specs=tok,
        out_shape=jax.ShapeDtypeStruct((t, GROUP_W), BF16),
        scratch_shapes=[
            pltpu.VMEM((seq // c, 2 * pair_w, pair_w), F32),
            pltpu.VMEM((seq // c, 2 * pair_w, pair_w), BF16),
            pltpu.VMEM((c, 2 * c), F32),
        ],
        compiler_params=_cparams(("parallel", "parallel")),
        name="retention",
    )(rq, rk, rv, rg, dec_f_lane, dec_b_lane, dec_f_col, dec_b_col, gain.astype(F32)[None, :])


N_ROW_OFFSETS = 2 * NA_WIN_ROWS - 1
N_COL_OFFSETS = 2 * NA_WIN_COLS - 1


def _natten_row_offsets(rows):
    n_blocks = rows // NA_ROWS
    starts = {0: 0, 1: NA_ROWS - NA_WIN_ROWS // 2, 2: rows - NA_KEY_ROWS}
    blocks = {0: 0, 1: 1, 2: n_blocks - 1}
    table = []
    for v in range(3):
        per_a = []
        for a in range(NA_ROWS):
            r = blocks[v] * NA_ROWS + a
            rs = min(max(r - NA_WIN_ROWS // 2, 0), rows - NA_WIN_ROWS)
            per_kl = []
            for kl in range(NA_KEY_ROWS):
                kr = starts[v] + kl
                per_kl.append(kr - r + NA_WIN_ROWS - 1 if rs <= kr < rs + NA_WIN_ROWS else None)
            per_a.append(per_kl)
        table.append(per_a)
    return table


def _natten_bias_kernel(rpb_ref, tab_ref, *, rows):
    offsets = _natten_row_offsets(rows)
    nk = NA_KEY_ROWS * GRID_W
    shape = (GRID_W, LANES)
    lane = lax.broadcasted_iota(I32, shape, 1)
    c = lax.broadcasted_iota(I32, shape, 0)
    second = lane >= GRID_W
    kc = lane % GRID_W
    cs = jnp.clip(c - NA_WIN_COLS // 2, 0, GRID_W - NA_WIN_COLS)
    col_ok = jnp.logical_and(kc >= cs, kc < cs + NA_WIN_COLS)
    neg = jnp.full(shape, NEG_BIG, F32)
    for hh in range(2):
        toeplitz = []
        for dr in range(N_ROW_OFFSETS):
            x = jnp.broadcast_to(rpb_ref[hh, dr:dr + 1, :], shape)
            lo = pltpu.roll(x, LANES - (NA_WIN_COLS - 1), axis=1, stride=1, stride_axis=0)
            hi = pltpu.roll(x, GRID_W - (NA_WIN_COLS - 1), axis=1, stride=1, stride_axis=0)
            toeplitz.append(jnp.where(second, hi, lo))
        for v in range(3):
            for a in range(NA_ROWS):
                for j in range(NA_KEY_ROWS // 2):
                    d0, d1 = offsets[v][a][2 * j], offsets[v][a][2 * j + 1]
                    if d0 is None and d1 is None:
                        piece = neg
                    else:
                        t0 = neg if d0 is None else toeplitz[d0]
                        t1 = neg if d1 is None else toeplitz[d1]
                        piece = jnp.where(col_ok, jnp.where(second, t1, t0), neg)
                    tab_ref[v, 0, a * GRID_W:(a + 1) * GRID_W,
                            hh * nk + j * LANES:hh * nk + (j + 1) * LANES] = piece


def _natten_bias_table(rpb, rows):
    n_pairs = HEADS // 2
    nqb = NA_ROWS * GRID_W
    nkb = NA_KEY_ROWS * GRID_W
    rpb_pad = jnp.pad(rpb.astype(F32), ((0, 0), (0, 0), (0, LANES - N_COL_OFFSETS)))
    return pl.pallas_call(
        functools.partial(_natten_bias_kernel, rows=rows),
        grid=(n_pairs,),
        in_specs=[pl.BlockSpec((2, N_ROW_OFFSETS, LANES), lambda p: (p, 0, 0))],
        out_specs=pl.BlockSpec((3, 1, nqb, 2 * nkb), lambda p: (0, p, 0, 0)),
        out_shape=jax.ShapeDtypeStruct((3, n_pairs, nqb, 2 * nkb), F32),
        compiler_params=_cparams(("parallel",)),
        name="natten_bias",
    )(rpb_pad)


def _natten_kernel(q_ref, k_ref, v_ref, *rest, rows):
    tab_refs, o_ref = rest[:NA_STEP_BLOCKS], rest[NA_STEP_BLOCKS]
    nq = NA_ROWS * GRID_W
    nk = NA_KEY_ROWS * GRID_W
    lane = lax.broadcasted_iota(I32, (1, 2 * HEAD_DIM), 1)
    head0 = lane < HEAD_DIM
    for blk in range(NA_STEP_BLOCKS):
        rb = pl.program_id(2) * NA_STEP_BLOCKS + blk
        start_row = jnp.clip(rb * NA_ROWS - NA_WIN_ROWS // 2, 0, rows - NA_KEY_ROWS)
        start = pl.multiple_of(start_row * GRID_W, GRID_W)
        q = q_ref[blk * nq:(blk + 1) * nq, :]
        k = k_ref[pl.ds(start, nk), :]
        v = v_ref[pl.ds(start, nk), :]
        zero = jnp.zeros_like(k)
        k_st = jnp.concatenate([jnp.where(head0, k, zero), jnp.where(head0, zero, k)], axis=0)
        v_st = jnp.concatenate([jnp.where(head0, v, zero), jnp.where(head0, zero, v)], axis=0)
        s = _dot_nt(q, k_st) + tab_refs[blk][0, 0]
        probs, denoms = [], []
        for h in range(2):
            sh = s[:, h * nk:(h + 1) * nk]
            m = jnp.max(sh, axis=1, keepdims=True)
            p = jnp.exp(sh - m)
            denoms.append(jnp.sum(p, axis=1, keepdims=True))
            probs.append(p.astype(BF16))
        out = _dot(jnp.concatenate(probs, axis=1), v_st)
        denom = jnp.where(head0, denoms[0], denoms[1])
        o_ref[blk * nq:(blk + 1) * nq, :] = (out / denom).astype(BF16)


def _natten(nq, nk, nv, rpb, batch, seq):
    t = nq.shape[0]
    rows = seq // GRID_W
    n_pairs = HEADS // 2
    n_blocks = rows // NA_ROWS
    n_steps = n_blocks // NA_STEP_BLOCKS
    nqb = NA_ROWS * GRID_W
    nkb = NA_KEY_ROWS * GRID_W
    tab = _natten_bias_table(rpb, rows)

    def tab_spec(blk):
        def index(b, p, s):
            rb = s * NA_STEP_BLOCKS + blk
            return (jnp.where(rb == 0, 0, jnp.where(rb == n_blocks - 1, 2, 1)), p, 0, 0)
        return pl.BlockSpec((1, 1, nqb, 2 * nkb), index)

    kv_spec = pl.BlockSpec((seq, 2 * HEAD_DIM), lambda b, p, s: (b, p))
    q_spec = pl.BlockSpec((NA_STEP_BLOCKS * nqb, 2 * HEAD_DIM), lambda b, p, s: (b * n_steps + s, p))
    return pl.pallas_call(
        functools.partial(_natten_kernel, rows=rows),
        grid=(batch, n_pairs, n_steps),
        in_specs=[q_spec, kv_spec, kv_spec] + [tab_spec(blk) for blk in range(NA_STEP_BLOCKS)],
        out_specs=q_spec,
        out_shape=jax.ShapeDtypeStruct((t, GROUP_W), BF16),
        compiler_params=_cparams(("parallel", "parallel", "arbitrary")),
        name="natten",
    )(nq, nk, nv, *([tab] * NA_STEP_BLOCKS))


def _layer_norm(h, gain, bias):
    mu = jnp.mean(h, axis=-1, keepdims=True)
    d = h - mu
    var = jnp.mean(d * d, axis=-1, keepdims=True)
    return d * lax.rsqrt(var + LN_EPS) * gain + bias


def _out_ln1_kernel(ret_ref, na_ref, x_ref, w_ref, gain_ref, bias_ref, x1_ref, x1b_ref):
    mix = _dot(ret_ref[...], w_ref[0:GROUP_W, :]) + _dot(na_ref[...], w_ref[GROUP_W:2 * GROUP_W, :])
    x1 = _layer_norm(ALPHA * x_ref[...] + mix, gain_ref[...], bias_ref[...])
    x1_ref[...] = x1
    x1b_ref[...] = x1.astype(BF16)


def _out_ln1(ret, na, x2, w_out_b, gain, bias):
    t = x2.shape[0]
    tm = PROJ_TM
    tok = lambda i: (i, 0)
    const = lambda i: (0, 0)
    return pl.pallas_call(
        _out_ln1_kernel,
        grid=(t // tm,),
        in_specs=[pl.BlockSpec((tm, GROUP_W), tok), pl.BlockSpec((tm, GROUP_W), tok),
                  pl.BlockSpec((tm, D_MODEL), tok), pl.BlockSpec(w_out_b.shape, const),
                  pl.BlockSpec((1, D_MODEL), const), pl.BlockSpec((1, D_MODEL), const)],
        out_specs=[pl.BlockSpec((tm, D_MODEL), tok), pl.BlockSpec((tm, D_MODEL), tok)],
        out_shape=[jax.ShapeDtypeStruct((t, D_MODEL), F32), jax.ShapeDtypeStruct((t, D_MODEL), BF16)],
        compiler_params=_cparams(("parallel",)),
        name="out_ln1",
    )(ret, na, x2, w_out_b, gain.astype(F32)[None, :], bias.astype(F32)[None, :])


def _route_kernel(x_ref, wr_ref, rb_ref, w_ref, rank_ref, cnt_ref, tri_ref):
    tm = x_ref.shape[0]

    @pl.when(pl.program_id(0) == 0)
    def _():
        i = lax.broadcasted_iota(I32, (MOE_SUB, MOE_SUB), 0)
        j = lax.broadcasted_iota(I32, (MOE_SUB, MOE_SUB), 1)
        tri_ref[...] = jnp.where(i < j, 1.0, 0.0).astype(BF16)

    scores = jax.nn.sigmoid(_dot_nt(wr_ref[...], x_ref[...]))
    biased = scores + rb_ref[...]
    sub = lax.broadcasted_iota(I32, (GROUP_SIZE, tm), 0).astype(F32)
    none = float(N_EXPERTS)
    ninf = -jnp.inf

    def first_max(vals, index):
        m = jnp.max(vals, axis=0, keepdims=True)
        return m, jnp.min(jnp.where(vals == m, index, none), axis=0, keepdims=True)

    groups = [biased[g * GROUP_SIZE:(g + 1) * GROUP_SIZE, :] for g in range(N_GROUPS)]
    group_scores = []
    for g in range(N_GROUPS):
        m1, i1 = first_max(groups[g], sub)
        m2 = jnp.max(jnp.where(sub == i1, ninf, groups[g]), axis=0, keepdims=True)
        group_scores.append(m1 + m2)
    cur = jnp.concatenate(group_scores, axis=0)
    group_sel = jnp.zeros(cur.shape, F32)
    for _ in range(TOPK_GROUPS):
        _, i1 = first_max(cur, sub)
        hit = sub == i1
        group_sel = jnp.where(hit, 1.0, group_sel)
        cur = jnp.where(hit, ninf, cur)

    masked = [jnp.where(group_sel[g:g + 1, :] > 0.5, groups[g], ninf) for g in range(N_GROUPS)]
    ids = [sub + float(g * GROUP_SIZE) for g in range(N_GROUPS)]
    chosen = [jnp.zeros((GROUP_SIZE, tm), F32) for _ in range(N_GROUPS)]
    for _ in range(TOP_K):
        m = masked[0]
        for g in range(1, N_GROUPS):
            m = jnp.maximum(m, masked[g])
        m = jnp.max(m, axis=0, keepdims=True)
        cand = jnp.where(masked[0] == m, ids[0], none)
        for g in range(1, N_GROUPS):
            cand = jnp.minimum(cand, jnp.where(masked[g] == m, ids[g], none))
        first = jnp.min(cand, axis=0, keepdims=True)
        for g in range(N_GROUPS):
            hit = ids[g] == first
            chosen[g] = jnp.where(hit, 1.0, chosen[g])
            masked[g] = jnp.where(hit, ninf, masked[g])

    sel = jnp.concatenate(chosen, axis=0) > 0.5
    picked = jnp.where(sel, scores, 0.0)
    total = jnp.sum(picked, axis=0, keepdims=True)
    w_ref[...] = picked / total * ROUTED_SCALE
    sel_f = jnp.where(sel, 1.0, 0.0)
    sel_b = sel_f.astype(BF16)
    before, cnt_max = [], None
    for k in range(MOE_NSUB):
        cols = slice(k * MOE_SUB, (k + 1) * MOE_SUB)
        before.append(_dot(sel_b[:, cols], tri_ref[...]))
        cnt = jnp.sum(sel_f[:, cols], axis=1, keepdims=True)
        cnt_max = cnt if cnt_max is None else jnp.maximum(cnt_max, cnt)
    rank_ref[...] = jnp.where(sel, jnp.concatenate(before, axis=1).astype(I32), -1)
    cnt_ref[...] = jnp.broadcast_to(cnt_max, cnt_ref.shape).astype(I32)


def _route(x1b, w_router, router_bias):
    t = x1b.shape[0]
    tm = MOE_TM
    nt = t // tm
    wr_t = w_router.astype(F32).T.astype(BF16)
    const = lambda i: (0, 0)
    col = lambda i: (0, i)
    return pl.pallas_call(
        _route_kernel,
        grid=(nt,),
        in_specs=[pl.BlockSpec((tm, D_MODEL), lambda i: (i, 0)),
                  pl.BlockSpec((N_EXPERTS, D_MODEL), const),
                  pl.BlockSpec((N_EXPERTS, 1), const)],
        out_specs=[pl.BlockSpec((N_EXPERTS, tm), col), pl.BlockSpec((N_EXPERTS, tm), col),
                   pl.BlockSpec((N_EXPERTS, LANES), col)],
        out_shape=[jax.ShapeDtypeStruct((N_EXPERTS, t), F32),
                   jax.ShapeDtypeStruct((N_EXPERTS, t), I32),
                   jax.ShapeDtypeStruct((N_EXPERTS, nt * LANES), I32)],
        scratch_shapes=[pltpu.VMEM((MOE_SUB, MOE_SUB), BF16)],
        compiler_params=_cparams(("arbitrary",)),
        name="route",
    )(x1b, wr_t, router_bias.astype(F32)[:, None])


def _moe_kernel(cnt_ref, x_ref, rank_ref, w_ref, wgu_ref, wd_ref, o_ref, oh_ref, g_ref, y_ref):
    i = pl.program_id(0)
    eb = pl.program_id(1)

    @pl.when(eb == 0)
    def _():
        o_ref[...] = jnp.zeros_like(o_ref)

    n_pass = lax.shift_right_logical(cnt_ref[eb, i] + (MOE_SLOTS - 1), int(np.log2(MOE_SLOTS)))
    slot = lax.broadcasted_iota(I32, (MOE_SLOTS, MOE_SUB), 0)

    def one_pass(p, carry):
        base = p * MOE_SLOTS
        slot_w = []
        for k in range(MOE_NSUB):
            tok = slice(k * MOE_SUB, (k + 1) * MOE_SUB)
            blocks, weights = [], []
            for j in range(MOE_EB):
                match = slot == (rank_ref[j, :, tok] - base)
                blocks.append(jnp.where(match, 1.0, 0.0).astype(BF16))
                weights.append(jnp.sum(jnp.where(match, w_ref[j, :, tok], 0.0), axis=1, keepdims=True))
            onehot = jnp.concatenate(blocks, axis=0)
            oh_ref[k] = onehot
            slot_w.append(weights)
            g_ref[k] = _dot(onehot, x_ref[tok, :]).astype(BF16)
        for j in range(MOE_EB):
            rows = slice(j * MOE_SLOTS, (j + 1) * MOE_SLOTS)
            xe = jnp.concatenate([g_ref[k, rows, :] for k in range(MOE_NSUB)], axis=0)
            h = _dot(xe, wgu_ref[j])
            act = (jax.nn.silu(h[:, :EXPERT_DIM]) * h[:, EXPERT_DIM:]).astype(BF16)
            wc = jnp.concatenate([slot_w[k][j] for k in range(MOE_NSUB)], axis=0)
            yw = (_dot(act, wd_ref[j]) * wc).astype(BF16)
            for k in range(MOE_NSUB):
                y_ref[k, rows, :] = yw[k * MOE_SLOTS:(k + 1) * MOE_SLOTS, :]
        for k in range(MOE_NSUB):
            tok = slice(k * MOE_SUB, (k + 1) * MOE_SUB)
            o_ref[tok, :] += _dot_tn(oh_ref[k], y_ref[k])
        return carry

    lax.fori_loop(0, n_pass, one_pass, 0)


def _moe(x1b, rank_t, w_t, counts, wgu_b, wd_b):
    t = x1b.shape[0]
    tm = MOE_TM
    nt = t // tm
    rank3 = rank_t.reshape(N_EXPERTS, 1, t)
    w3 = w_t.reshape(N_EXPERTS, 1, t)
    counts = counts.reshape(N_EXPERTS // MOE_EB, MOE_EB, nt).max(axis=1)
    grid_spec = pltpu.PrefetchScalarGridSpec(
        num_scalar_prefetch=1,
        grid=(nt, N_EXPERTS // MOE_EB),
        in_specs=[
            pl.BlockSpec((tm, D_MODEL), lambda i, e, c: (i, 0)),
            pl.BlockSpec((MOE_EB, 1, tm), lambda i, e, c: (e, 0, i)),
            pl.BlockSpec((MOE_EB, 1, tm), lambda i, e, c: (e, 0, i)),
            pl.BlockSpec((MOE_EB, D_MODEL, 2 * EXPERT_DIM), lambda i, e, c: (e, 0, 0)),
            pl.BlockSpec((MOE_EB, EXPERT_DIM, D_MODEL), lambda i, e, c: (e, 0, 0)),
        ],
        out_specs=pl.BlockSpec((tm, D_MODEL), lambda i, e, c: (i, 0)),
        scratch_shapes=[pltpu.VMEM((MOE_NSUB, MOE_SUB, MOE_SUB), BF16),
                        pltpu.VMEM((MOE_NSUB, MOE_SUB, D_MODEL), BF16),
                        pltpu.VMEM((MOE_NSUB, MOE_SUB, D_MODEL), BF16)],
    )
    return pl.pallas_call(
        _moe_kernel,
        grid_spec=grid_spec,
        out_shape=jax.ShapeDtypeStruct((t, D_MODEL), F32),
        compiler_params=_cparams(("parallel", "arbitrary")),
        name="moe",
    )(counts, x1b, rank3, w3, wgu_b, wd_b)


def _final_kernel(x1_ref, ffn_ref, p_ref, wsgu_ref, wsd_ref, wp_ref, wg_ref, gain_ref, bias_ref, o_ref):
    x1 = x1_ref[...]
    xb = x1.astype(BF16)
    h = _dot(xb, wsgu_ref[...])
    act = (jax.nn.silu(h[:, :EXPERT_DIM]) * h[:, EXPERT_DIM:]).astype(BF16)
    shared = _dot(act, wsd_ref[...])
    ple = _dot(p_ref[...].astype(BF16), wp_ref[...]) * jax.nn.sigmoid(_dot(xb, wg_ref[...]))
    o_ref[...] = _layer_norm(ALPHA * x1 + (ffn_ref[...] + shared) + ple, gain_ref[...], bias_ref[...])


def _final(x1, ffn, p2, wsgu_b, wsd_b, wp_b, wg_b, gain, bias):
    t = x1.shape[0]
    tm = PROJ_TM
    tok = lambda i: (i, 0)
    const = lambda i: (0, 0)
    full = lambda a: pl.BlockSpec(a.shape, const)
    return pl.pallas_call(
        _final_kernel,
        grid=(t // tm,),
        in_specs=[pl.BlockSpec((tm, D_MODEL), tok), pl.BlockSpec((tm, D_MODEL), tok),
                  pl.BlockSpec((tm, p2.shape[1]), tok),
                  full(wsgu_b), full(wsd_b), full(wp_b), full(wg_b),
                  pl.BlockSpec((1, D_MODEL), const), pl.BlockSpec((1, D_MODEL), const)],
        out_specs=pl.BlockSpec((tm, D_MODEL), tok),
        out_shape=jax.ShapeDtypeStruct((t, D_MODEL), F32),
        compiler_params=_cparams(("parallel",)),
        name="final",
    )(x1, ffn, p2, wsgu_b, wsd_b, wp_b, wg_b, gain.astype(F32)[None, :], bias.astype(F32)[None, :])


def _rotary_tables(seq):
    half = HEAD_DIM // 2
    inv = ROPE_BASE ** (-jnp.arange(half, dtype=F32) / half)
    ang = jnp.arange(seq, dtype=jnp.int32).astype(F32)[:, None] * inv[None, :]
    cos, sin = jnp.cos(ang), jnp.sin(ang)
    reps = LANES // HEAD_DIM
    cos_t = jnp.tile(jnp.concatenate([cos, cos], axis=1), (1, reps))
    sin_t = jnp.tile(jnp.concatenate([-sin, sin], axis=1), (1, reps))
    return cos_t, sin_t


def kernel(x, p, w_in, ret_decay_fwd, ret_decay_bwd, ret_gn_gain, na_rpb, w_out, ln1_gain, ln1_bias,
           w_router, router_bias, w_expert_gu, w_expert_down, w_shared_gu, w_shared_down,
           w_ple_proj, w_ple_gate, ln2_gain, ln2_bias):
    batch, seq, d = x.shape
    t = batch * seq
    depth = w_in.shape[0]
    assert depth == 1 and d == D_MODEL
    assert seq % PROJ_TM == 0 and seq % MOE_TM == 0 and seq % RET_CHUNK == 0
    assert (seq // GRID_W) % (NA_ROWS * NA_STEP_BLOCKS) == 0 and seq // GRID_W >= NA_KEY_ROWS
    cos_t, sin_t = _rotary_tables(seq)
    x2 = x.reshape(t, d)
    for i in range(depth):
        rq, rk, rv, rg, nq, nk, nv = _in_proj(x2, w_in[i].astype(BF16), cos_t, sin_t, seq)
        ret = _retention(rq, rk, rv, rg, ret_decay_fwd[i], ret_decay_bwd[i], ret_gn_gain[i], batch, seq)
        na = _natten(nq, nk, nv, na_rpb[i], batch, seq)
        x1, x1b = _out_ln1(ret, na, x2, w_out[i].astype(BF16), ln1_gain[i], ln1_bias[i])
        w_t, rank_t, cnt = _route(x1b, w_router[i], router_bias[i])
        counts = cnt[:, ::LANES]
        ffn = _moe(x1b, rank_t, w_t, counts, w_expert_gu[i].astype(BF16), w_expert_down[i].astype(BF16))
        x2 = _final(x1, ffn, p[i].reshape(t, -1), w_shared_gu[i].astype(BF16), w_shared_down[i].astype(BF16),
                    w_ple_proj[i].astype(BF16), w_ple_gate[i].astype(BF16), ln2_gain[i], ln2_bias[i])
    return x2.reshape(batch, seq, d)
```

```python
import functools

import numpy as np
import jax
import jax.numpy as jnp
from jax import lax
from jax.experimental import pallas as pl
from jax.experimental.pallas import tpu as pltpu

F32 = jnp.float32
BF16 = jnp.bfloat16
I32 = jnp.int32

D_MODEL = 1024
HEADS = 8
HEAD_DIM = 64
GROUP_W = HEADS * HEAD_DIM
ROPE_BASE = 10000.0
GN_EPS = 1e-6
LN_EPS = 1e-5
GRID_W = 64
NA_WIN_ROWS = 8
NA_WIN_COLS = 16
N_EXPERTS = 64
N_GROUPS = 8
GROUP_SIZE = N_EXPERTS // N_GROUPS
TOPK_GROUPS = 4
TOP_K = 8
EXPERT_DIM = 256
ROUTED_SCALE = 2.5
ALPHA = 2.0 ** 0.25
NEG_BIG = -1e30

LANES = 128
VMEM_LIMIT_BYTES = 56 * 1024 * 1024

PROJ_TM = 512
RET_CHUNK = 128
RET_UNROLL = 8
NA_ROWS = 4
NA_KEY_ROWS = NA_ROWS + NA_WIN_ROWS
NA_STEP_BLOCKS = 2
MOE_TM = 2048
MOE_SUB = 256
MOE_EB = 4
MOE_SLOTS = MOE_SUB // MOE_EB
MOE_SLOTS_SMALL = 48
MOE_NSUB = MOE_TM // MOE_SUB


def _cparams(sem):
    return pltpu.CompilerParams(dimension_semantics=sem, vmem_limit_bytes=VMEM_LIMIT_BYTES)


def _dot(a, b):
    return jnp.dot(a, b, preferred_element_type=F32)


def _dot_nt(a, b):
    return lax.dot_general(a, b, (((1,), (1,)), ((), ())), preferred_element_type=F32)


def _dot_tn(a, b):
    return lax.dot_general(a, b, (((0,), (0,)), ((), ())), preferred_element_type=F32)


def _in_proj_kernel(x_ref, w_ref, cos_ref, sin_ref,
                    rq_ref, rk_ref, rv_ref, rg_ref, nq_ref, nk_ref, nv_ref):
    xb = x_ref[...].astype(BF16)
    cos = cos_ref[...]
    sin = sin_ref[...]
    lane = lax.broadcasted_iota(I32, (1, LANES), 1)
    first_half = (lane % HEAD_DIM) < (HEAD_DIM // 2)

    def proj(g):
        return _dot(xb, w_ref[:, g * GROUP_W:(g + 1) * GROUP_W])

    def rotary(t, scale):
        outs = []
        for j in range(GROUP_W // LANES):
            c = t[:, j * LANES:(j + 1) * LANES]
            swapped = jnp.where(first_half,
                                pltpu.roll(c, LANES - HEAD_DIM // 2, axis=1),
                                pltpu.roll(c, HEAD_DIM // 2, axis=1))
            outs.append((c * cos + swapped * sin) * scale)
        return jnp.concatenate(outs, axis=1)

    rq_ref[...] = rotary(proj(0), 1.0).astype(BF16)
    rk_ref[...] = rotary(proj(1), HEAD_DIM ** -0.5).astype(BF16)
    rv_ref[...] = proj(2).astype(BF16)
    rg_ref[...] = jax.nn.silu(proj(3)).astype(BF16)
    nq_ref[...] = (proj(4) * HEAD_DIM ** -0.5).astype(BF16)
    nk_ref[...] = proj(5).astype(BF16)
    nv_ref[...] = proj(6).astype(BF16)


def _in_proj(x2, w_in_b, cos_t, sin_t, seq):
    t = x2.shape[0]
    tm = PROJ_TM
    n_pos = seq // tm
    out = jax.ShapeDtypeStruct((t, GROUP_W), BF16)
    tok = lambda i: (i, 0)
    return pl.pallas_call(
        _in_proj_kernel,
        grid=(t // tm,),
        in_specs=[
            pl.BlockSpec((tm, D_MODEL), tok),
            pl.BlockSpec(w_in_b.shape, lambda i: (0, 0)),
            pl.BlockSpec((tm, LANES), lambda i: (i % n_pos, 0)),
            pl.BlockSpec((tm, LANES), lambda i: (i % n_pos, 0)),
        ],
        out_specs=[pl.BlockSpec((tm, GROUP_W), tok)] * 7,
        out_shape=[out] * 7,
        compiler_params=_cparams(("parallel",)),
        name="in_proj",
    )(x2, w_in_b, cos_t, sin_t)


def _log_sigmoid(x):
    return jnp.minimum(x, 0.0) - jnp.log1p(jnp.exp(-jnp.abs(x)))


def _retention_kernel(q_ref, k_ref, v_ref, g_ref, decf_ref, decb_ref, decfd_ref, decbd_ref,
                      gain_ref, o_ref, kv_ref, st_ref, dmat_ref):
    c = RET_CHUNK
    n_chunks = q_ref.shape[0] // c
    pair_w = 2 * HEAD_DIM

    lgf = _log_sigmoid(decf_ref[...])
    lgb = _log_sigmoid(decb_ref[...])
    row = lax.broadcasted_iota(I32, (c, 1), 0).astype(F32)
    k_dec_f = jnp.exp((c - 1.0 - row) * lgf)
    k_dec_b = jnp.exp(row * lgb)
    q_dec_f = jnp.exp((row + 1.0) * lgf)
    q_dec_b = jnp.exp((c - row) * lgb)
    chunk_dec_f = jnp.exp(c * lgf)
    chunk_dec_b = jnp.exp(c * lgb)

    lane = lax.broadcasted_iota(I32, (1, pair_w), 1)
    head0 = lane < HEAD_DIM
    r2 = lax.broadcasted_iota(I32, (pair_w, pair_w), 0) // HEAD_DIM
    c2 = lax.broadcasted_iota(I32, (pair_w, pair_w), 1) // HEAD_DIM
    same_head = r2 == c2
    block_diag = jnp.where(same_head, 1.0, 0.0)
    seg_avg = jnp.where(same_head, 1.0 / HEAD_DIM, 0.0).astype(BF16)

    lgf_d = _log_sigmoid(decfd_ref[0])
    lgb_d = _log_sigmoid(decbd_ref[0])
    di = lax.broadcasted_iota(I32, (c, 2 * c), 0)
    dj = lax.broadcasted_iota(I32, (c, 2 * c), 1) % c
    diff = (di - dj).astype(F32)
    dmat_ref[...] = jnp.where(diff >= 0.0, jnp.exp(diff * lgf_d), jnp.exp(-diff * lgb_d))

    def chunk(ref, n):
        return ref[pl.ds(pl.multiple_of(n * c, c), c), :]

    unroll = RET_UNROLL

    def summarize(nb, carry):
        for u in range(unroll):
            n = nb * unroll + u
            kf = chunk(k_ref, n).astype(F32)
            kst = jnp.concatenate([kf * k_dec_f, kf * k_dec_b], axis=1).astype(BF16)
            kv_ref[n] = _dot_tn(kst, chunk(v_ref, n))
        return carry

    lax.fori_loop(0, n_chunks // unroll, summarize, 0)

    def fwd_scan(n, state):
        st_ref[n, 0:pair_w, :] = state.astype(BF16)
        return state * chunk_dec_f + kv_ref[n, 0:pair_w, :] * block_diag

    lax.fori_loop(0, n_chunks, fwd_scan, jnp.zeros((pair_w, pair_w), F32))

    def bwd_scan(i, state):
        n = n_chunks - 1 - i
        st_ref[n, pair_w:2 * pair_w, :] = state.astype(BF16)
        return state * chunk_dec_b + kv_ref[n, pair_w:2 * pair_w, :] * block_diag

    lax.fori_loop(0, n_chunks, bwd_scan, jnp.zeros((pair_w, pair_w), F32))

    gain = gain_ref[...]

    def seg_mean(z):
        hi = z.astype(BF16)
        lo = (z - hi.astype(F32)).astype(BF16)
        return _dot(hi, seg_avg) + _dot(lo, seg_avg)

    def emit(nb, carry):
        ys = []
        for u in range(unroll):
            n = nb * unroll + u
            q = chunk(q_ref, n)
            k = chunk(k_ref, n)
            v = chunk(v_ref, n)
            zero = jnp.zeros_like(k)
            k_st = jnp.concatenate([jnp.where(head0, k, zero), jnp.where(head0, zero, k)], axis=0)
            v_st = jnp.concatenate([jnp.where(head0, v, zero), jnp.where(head0, zero, v)], axis=0)
            scores = _dot_nt(q, k_st) * dmat_ref[...]
            qf = q.astype(F32)
            q_st = jnp.concatenate([qf * q_dec_f, qf * q_dec_b], axis=1).astype(BF16)
            ys.append(_dot(scores.astype(BF16), v_st) + _dot(q_st, st_ref[n]))
        y = jnp.concatenate(ys, axis=0)
        mu = seg_mean(y)
        d = y - mu
        var = seg_mean(d * d)
        yn = d * lax.rsqrt(var + GN_EPS) * gain
        rows = pl.ds(pl.multiple_of(nb * (unroll * c), unroll * c), unroll * c)
        o_ref[rows, :] = (g_ref[rows, :].astype(F32) * yn).astype(BF16)
        return carry

    lax.fori_loop(0, n_chunks // unroll, emit, 0)


def _retention(rq, rk, rv, rg, dec_f, dec_b, gain, batch, seq):
    t = rq.shape[0]
    c = RET_CHUNK
    n_pairs = HEADS // 2
    pair_w = 2 * HEAD_DIM
    dec_f_lane = jnp.repeat(dec_f.astype(F32), HEAD_DIM)[None, :]
    dec_b_lane = jnp.repeat(dec_b.astype(F32), HEAD_DIM)[None, :]
    dec_f_col = jnp.repeat(dec_f.astype(F32), c).reshape(n_pairs, 1, 2 * c)
    dec_b_col = jnp.repeat(dec_b.astype(F32), c).reshape(n_pairs, 1, 2 * c)
    tok = pl.BlockSpec((seq, pair_w), lambda b, p: (b, p))
    lane_spec = pl.BlockSpec((1, pair_w), lambda b, p: (0, p))
    col_spec = pl.BlockSpec((1, 1, 2 * c), lambda b, p: (p, 0, 0))
    return pl.pallas_call(
        _retention_kernel,
        grid=(batch, n_pairs),
        in_specs=[tok, tok, tok, tok, lane_spec, lane_spec, col_spec, col_spec, lane_spec],
        out_specs=tok,
        out_shape=jax.ShapeDtypeStruct((t, GROUP_W), BF16),
        scratch_shapes=[
            pltpu.VMEM((seq // c, 2 * pair_w, pair_w), F32),
            pltpu.VMEM((seq // c, 2 * pair_w, pair_w), BF16),
            pltpu.VMEM((c, 2 * c), F32),
        ],
        compiler_params=_cparams(("parallel", "parallel")),
        name="retention",
    )(rq, rk, rv, rg, dec_f_lane, dec_b_lane, dec_f_col, dec_b_col, gain.astype(F32)[None, :])


N_ROW_OFFSETS = 2 * NA_WIN_ROWS - 1
N_COL_OFFSETS = 2 * NA_WIN_COLS - 1


def _natten_row_offsets(rows):
    n_blocks = rows // NA_ROWS
    starts = {0: 0, 1: NA_ROWS - NA_WIN_ROWS // 2, 2: rows - NA_KEY_ROWS}
    blocks = {0: 0, 1: 1, 2: n_blocks - 1}
    table = []
    for v in range(3):
        per_a = []
        for a in range(NA_ROWS):
            r = blocks[v] * NA_ROWS + a
            rs = min(max(r - NA_WIN_ROWS // 2, 0), rows - NA_WIN_ROWS)
            per_kl = []
            for kl in range(NA_KEY_ROWS):
                kr = starts[v] + kl
                per_kl.append(kr - r + NA_WIN_ROWS - 1 if rs <= kr < rs + NA_WIN_ROWS else None)
            per_a.append(per_kl)
        table.append(per_a)
    return table


def _natten_bias_kernel(rpb_ref, tab_ref, *, rows):
    offsets = _natten_row_offsets(rows)
    nk = NA_KEY_ROWS * GRID_W
    shape = (GRID_W, LANES)
    lane = lax.broadcasted_iota(I32, shape, 1)
    c = lax.broadcasted_iota(I32, shape, 0)
    second = lane >= GRID_W
    kc = lane % GRID_W
    cs = jnp.clip(c - NA_WIN_COLS // 2, 0, GRID_W - NA_WIN_COLS)
    col_ok = jnp.logical_and(kc >= cs, kc < cs + NA_WIN_COLS)
    neg = jnp.full(shape, NEG_BIG, F32)
    for hh in range(2):
        toeplitz = []
        for dr in range(N_ROW_OFFSETS):
            x = jnp.broadcast_to(rpb_ref[hh, dr:dr + 1, :], shape)
            lo = pltpu.roll(x, LANES - (NA_WIN_COLS - 1), axis=1, stride=1, stride_axis=0)
            hi = pltpu.roll(x, GRID_W - (NA_WIN_COLS - 1), axis=1, stride=1, stride_axis=0)
            toeplitz.append(jnp.where(second, hi, lo))
        for v in range(3):
            for a in range(NA_ROWS):
                for j in range(NA_KEY_ROWS // 2):
                    d0, d1 = offsets[v][a][2 * j], offsets[v][a][2 * j + 1]
                    if d0 is None and d1 is None:
                        piece = neg
                    else:
                        t0 = neg if d0 is None else toeplitz[d0]
                        t1 = neg if d1 is None else toeplitz[d1]
                        piece = jnp.where(col_ok, jnp.where(second, t1, t0), neg)
                    tab_ref[v, 0, a * GRID_W:(a + 1) * GRID_W,
                            hh * nk + j * LANES:hh * nk + (j + 1) * LANES] = piece


def _natten_bias_table(rpb, rows):
    n_pairs = HEADS // 2
    nqb = NA_ROWS * GRID_W
    nkb = NA_KEY_ROWS * GRID_W
    rpb_pad = jnp.pad(rpb.astype(F32), ((0, 0), (0, 0), (0, LANES - N_COL_OFFSETS)))
    return pl.pallas_call(
        functools.partial(_natten_bias_kernel, rows=rows),
        grid=(n_pairs,),
        in_specs=[pl.BlockSpec((2, N_ROW_OFFSETS, LANES), lambda p: (p, 0, 0))],
        out_specs=pl.BlockSpec((3, 1, nqb, 2 * nkb), lambda p: (0, p, 0, 0)),
        out_shape=jax.ShapeDtypeStruct((3, n_pairs, nqb, 2 * nkb), F32),
        compiler_params=_cparams(("parallel",)),
        name="natten_bias",
    )(rpb_pad)


def _natten_kernel(q_ref, k_ref, v_ref, *rest, rows):
    tab_refs, o_ref = rest[:NA_STEP_BLOCKS], rest[NA_STEP_BLOCKS]
    nq = NA_ROWS * GRID_W
    nk = NA_KEY_ROWS * GRID_W
    lane = lax.broadcasted_iota(I32, (1, 2 * HEAD_DIM), 1)
    head0 = lane < HEAD_DIM
    for blk in range(NA_STEP_BLOCKS):
        rb = pl.program_id(2) * NA_STEP_BLOCKS + blk
        start_row = jnp.clip(rb * NA_ROWS - NA_WIN_ROWS // 2, 0, rows - NA_KEY_ROWS)
        start = pl.multiple_of(start_row * GRID_W, GRID_W)
        q = q_ref[blk * nq:(blk + 1) * nq, :]
        k = k_ref[pl.ds(start, nk), :]
        v = v_ref[pl.ds(start, nk), :]
        zero = jnp.zeros_like(k)
        k_st = jnp.concatenate([jnp.where(head0, k, zero), jnp.where(head0, zero, k)], axis=0)
        v_st = jnp.concatenate([jnp.where(head0, v, zero), jnp.where(head0, zero, v)], axis=0)
        s = _dot_nt(q, k_st) + tab_refs[blk][0, 0]
        probs, denoms = [], []
        for h in range(2):
            sh = s[:, h * nk:(h + 1) * nk]
            m = jnp.max(sh, axis=1, keepdims=True)
            p = jnp.exp(sh - m)
            denoms.append(jnp.sum(p, axis=1, keepdims=True))
            probs.append(p.astype(BF16))
        out = _dot(jnp.concatenate(probs, axis=1), v_st)
        denom = jnp.where(head0, denoms[0], denoms[1])
        o_ref[blk * nq:(blk + 1) * nq, :] = (out / denom).astype(BF16)


def _natten(nq, nk, nv, rpb, batch, seq):
    t = nq.shape[0]
    rows = seq // GRID_W
    n_pairs = HEADS // 2
    n_blocks = rows // NA_ROWS
    n_steps = n_blocks // NA_STEP_BLOCKS
    nqb = NA_ROWS * GRID_W
    nkb = NA_KEY_ROWS * GRID_W
    tab = _natten_bias_table(rpb, rows)

    def tab_spec(blk):
        def index(b, p, s):
            rb = s * NA_STEP_BLOCKS + blk
            return (jnp.where(rb == 0, 0, jnp.where(rb == n_blocks - 1, 2, 1)), p, 0, 0)
        return pl.BlockSpec((1, 1, nqb, 2 * nkb), index)

    kv_spec = pl.BlockSpec((seq, 2 * HEAD_DIM), lambda b, p, s: (b, p))
    q_spec = pl.BlockSpec((NA_STEP_BLOCKS * nqb, 2 * HEAD_DIM), lambda b, p, s: (b * n_steps + s, p))
    return pl.pallas_call(
        functools.partial(_natten_kernel, rows=rows),
        grid=(batch, n_pairs, n_steps),
        in_specs=[q_spec, kv_spec, kv_spec] + [tab_spec(blk) for blk in range(NA_STEP_BLOCKS)],
        out_specs=q_spec,
        out_shape=jax.ShapeDtypeStruct((t, GROUP_W), BF16),
        compiler_params=_cparams(("parallel", "parallel", "arbitrary")),
        name="natten",
    )(nq, nk, nv, *([tab] * NA_STEP_BLOCKS))


def _layer_norm(h, gain, bias):
    mu = jnp.mean(h, axis=-1, keepdims=True)
    d = h - mu
    var = jnp.mean(d * d, axis=-1, keepdims=True)
    return d * lax.rsqrt(var + LN_EPS) * gain + bias


def _out_ln1_kernel(ret_ref, na_ref, x_ref, w_ref, gain_ref, bias_ref, x1_ref, x1b_ref):
    mix = _dot(ret_ref[...], w_ref[0:GROUP_W, :]) + _dot(na_ref[...], w_ref[GROUP_W:2 * GROUP_W, :])
    x1 = _layer_norm(ALPHA * x_ref[...] + mix, gain_ref[...], bias_ref[...])
    x1_ref[...] = x1
    x1b_ref[...] = x1.astype(BF16)


def _out_ln1(ret, na, x2, w_out_b, gain, bias):
    t = x2.shape[0]
    tm = PROJ_TM
    tok = lambda i: (i, 0)
    const = lambda i: (0, 0)
    return pl.pallas_call(
        _out_ln1_kernel,
        grid=(t // tm,),
        in_specs=[pl.BlockSpec((tm, GROUP_W), tok), pl.BlockSpec((tm, GROUP_W), tok),
                  pl.BlockSpec((tm, D_MODEL), tok), pl.BlockSpec(w_out_b.shape, const),
                  pl.BlockSpec((1, D_MODEL), const), pl.BlockSpec((1, D_MODEL), const)],
        out_specs=[pl.BlockSpec((tm, D_MODEL), tok), pl.BlockSpec((tm, D_MODEL), tok)],
        out_shape=[jax.ShapeDtypeStruct((t, D_MODEL), F32), jax.ShapeDtypeStruct((t, D_MODEL), BF16)],
        compiler_params=_cparams(("parallel",)),
        name="out_ln1",
    )(ret, na, x2, w_out_b, gain.astype(F32)[None, :], bias.astype(F32)[None, :])


def _route_kernel(x_ref, wr_ref, rb_ref, w_ref, rank_ref, cnt_ref, tri_ref):
    tm = x_ref.shape[0]

    @pl.when(pl.program_id(0) == 0)
    def _():
        i = lax.broadcasted_iota(I32, (MOE_SUB, MOE_SUB), 0)
        j = lax.broadcasted_iota(I32, (MOE_SUB, MOE_SUB), 1)
        tri_ref[...] = jnp.where(i < j, 1.0, 0.0).astype(BF16)

    scores = jax.nn.sigmoid(_dot_nt(wr_ref[...], x_ref[...]))
    biased = scores + rb_ref[...]
    sub = lax.broadcasted_iota(I32, (GROUP_SIZE, tm), 0).astype(F32)
    none = float(N_EXPERTS)
    ninf = -jnp.inf

    def first_max(vals, index):
        m = jnp.max(vals, axis=0, keepdims=True)
        return m, jnp.min(jnp.where(vals == m, index, none), axis=0, keepdims=True)

    groups = [biased[g * GROUP_SIZE:(g + 1) * GROUP_SIZE, :] for g in range(N_GROUPS)]
    group_scores = []
    for g in range(N_GROUPS):
        m1, i1 = first_max(groups[g], sub)
        m2 = jnp.max(jnp.where(sub == i1, ninf, groups[g]), axis=0, keepdims=True)
        group_scores.append(m1 + m2)
    cur = jnp.concatenate(group_scores, axis=0)
    group_sel = jnp.zeros(cur.shape, F32)
    for _ in range(TOPK_GROUPS):
        _, i1 = first_max(cur, sub)
        hit = sub == i1
        group_sel = jnp.where(hit, 1.0, group_sel)
        cur = jnp.where(hit, ninf, cur)

    masked = [jnp.where(group_sel[g:g + 1, :] > 0.5, groups[g], ninf) for g in range(N_GROUPS)]
    ids = [sub + float(g * GROUP_SIZE) for g in range(N_GROUPS)]
    chosen = [jnp.zeros((GROUP_SIZE, tm), F32) for _ in range(N_GROUPS)]
    for _ in range(TOP_K):
        m = masked[0]
        for g in range(1, N_GROUPS):
            m = jnp.maximum(m, masked[g])
        m = jnp.max(m, axis=0, keepdims=True)
        cand = jnp.where(masked[0] == m, ids[0], none)
        for g in range(1, N_GROUPS):
            cand = jnp.minimum(cand, jnp.where(masked[g] == m, ids[g], none))
        first = jnp.min(cand, axis=0, keepdims=True)
        for g in range(N_GROUPS):
            hit = ids[g] == first
            chosen[g] = jnp.where(hit, 1.0, chosen[g])
            masked[g] = jnp.where(hit, ninf, masked[g])

    sel = jnp.concatenate(chosen, axis=0) > 0.5
    picked = jnp.where(sel, scores, 0.0)
    total = jnp.sum(picked, axis=0, keepdims=True)
    w_ref[...] = picked / total * ROUTED_SCALE
    sel_f = jnp.where(sel, 1.0, 0.0)
    sel_b = sel_f.astype(BF16)
    before, cnt_max = [], None
    for k in range(MOE_NSUB):
        cols = slice(k * MOE_SUB, (k + 1) * MOE_SUB)
        before.append(_dot(sel_b[:, cols], tri_ref[...]))
        cnt = jnp.sum(sel_f[:, cols], axis=1, keepdims=True)
        cnt_max = cnt if cnt_max is None else jnp.maximum(cnt_max, cnt)
    rank_ref[...] = jnp.where(sel, jnp.concatenate(before, axis=1).astype(I32), -1)
    cnt_ref[...] = jnp.broadcast_to(cnt_max, cnt_ref.shape).astype(I32)


def _route(x1b, w_router, router_bias):
    t = x1b.shape[0]
    tm = MOE_TM
    nt = t // tm
    wr_t = w_router.astype(F32).T.astype(BF16)
    const = lambda i: (0, 0)
    col = lambda i: (0, i)
    return pl.pallas_call(
        _route_kernel,
        grid=(nt,),
        in_specs=[pl.BlockSpec((tm, D_MODEL), lambda i: (i, 0)),
                  pl.BlockSpec((N_EXPERTS, D_MODEL), const),
                  pl.BlockSpec((N_EXPERTS, 1), const)],
        out_specs=[pl.BlockSpec((N_EXPERTS, tm), col), pl.BlockSpec((N_EXPERTS, tm), col),
                   pl.BlockSpec((N_EXPERTS, LANES), col)],
        out_shape=[jax.ShapeDtypeStruct((N_EXPERTS, t), F32),
                   jax.ShapeDtypeStruct((N_EXPERTS, t), I32),
                   jax.ShapeDtypeStruct((N_EXPERTS, nt * LANES), I32)],
        scratch_shapes=[pltpu.VMEM((MOE_SUB, MOE_SUB), BF16)],
        compiler_params=_cparams(("arbitrary",)),
        name="route",
    )(x1b, wr_t, router_bias.astype(F32)[:, None])


def _moe_kernel(cnt_ref, x_ref, rank_ref, w_ref, wgu_ref, wd_ref, o_ref, oh_ref, g_ref, y_ref):
    i = pl.program_id(0)
    eb = pl.program_id(1)

    @pl.when(eb == 0)
    def _():
        o_ref[...] = jnp.zeros_like(o_ref)

    max_count = cnt_ref[eb, i]

    def one_pass(p, slots):
        n_rows = MOE_EB * slots
        base = p * slots
        slot = lax.broadcasted_iota(I32, (slots, MOE_SUB), 0)
        slot_w = []
        for k in range(MOE_NSUB):
            tok = slice(k * MOE_SUB, (k + 1) * MOE_SUB)
            blocks, weights = [], []
            for j in range(MOE_EB):
                match = slot == (rank_ref[j, :, tok] - base)
                blocks.append(jnp.where(match, 1.0, 0.0).astype(BF16))
                weights.append(jnp.sum(jnp.where(match, w_ref[j, :, tok], 0.0), axis=1, keepdims=True))
            onehot = jnp.concatenate(blocks, axis=0)
            oh_ref[k, 0:n_rows, :] = onehot
            slot_w.append(weights)
            g_ref[k, 0:n_rows, :] = _dot(onehot, x_ref[tok, :]).astype(BF16)
        for j in range(MOE_EB):
            rows = slice(j * slots, (j + 1) * slots)
            xe = jnp.concatenate([g_ref[k, rows, :] for k in range(MOE_NSUB)], axis=0)
            h = _dot(xe, wgu_ref[j])
            act = (jax.nn.silu(h[:, :EXPERT_DIM]) * h[:, EXPERT_DIM:]).astype(BF16)
            wc = jnp.concatenate([slot_w[k][j] for k in range(MOE_NSUB)], axis=0)
            yw = (_dot(act, wd_ref[j]) * wc).astype(BF16)
            for k in range(MOE_NSUB):
                y_ref[k, rows, :] = yw[k * slots:(k + 1) * slots, :]
        for k in range(MOE_NSUB):
            tok = slice(k * MOE_SUB, (k + 1) * MOE_SUB)
            o_ref[tok, :] += _dot_tn(oh_ref[k, 0:n_rows, :], y_ref[k, 0:n_rows, :])

    @pl.when(max_count <= MOE_SLOTS_SMALL)
    def _():
        one_pass(0, MOE_SLOTS_SMALL)

    @pl.when(max_count > MOE_SLOTS_SMALL)
    def _():
        n_pass = lax.shift_right_logical(max_count + (MOE_SLOTS - 1), int(np.log2(MOE_SLOTS)))

        def body(p, carry):
            one_pass(p, MOE_SLOTS)
            return carry

        lax.fori_loop(0, n_pass, body, 0)


def _moe(x1b, rank_t, w_t, counts, wgu_b, wd_b):
    t = x1b.shape[0]
    tm = MOE_TM
    nt = t // tm
    rank3 = rank_t.reshape(N_EXPERTS, 1, t)
    w3 = w_t.reshape(N_EXPERTS, 1, t)
    counts = counts.reshape(N_EXPERTS // MOE_EB, MOE_EB, nt).max(axis=1)
    grid_spec = pltpu.PrefetchScalarGridSpec(
        num_scalar_prefetch=1,
        grid=(nt, N_EXPERTS // MOE_EB),
        in_specs=[
            pl.BlockSpec((tm, D_MODEL), lambda i, e, c: (i, 0)),
            pl.BlockSpec((MOE_EB, 1, tm), lambda i, e, c: (e, 0, i)),
            pl.BlockSpec((MOE_EB, 1, tm), lambda i, e, c: (e, 0, i)),
            pl.BlockSpec((MOE_EB, D_MODEL, 2 * EXPERT_DIM), lambda i, e, c: (e, 0, 0)),
            pl.BlockSpec((MOE_EB, EXPERT_DIM, D_MODEL), lambda i, e, c: (e, 0, 0)),
        ],
        out_specs=pl.BlockSpec((tm, D_MODEL), lambda i, e, c: (i, 0)),
        scratch_shapes=[pltpu.VMEM((MOE_NSUB, MOE_SUB, MOE_SUB), BF16),
                        pltpu.VMEM((MOE_NSUB, MOE_SUB, D_MODEL), BF16),
                        pltpu.VMEM((MOE_NSUB, MOE_SUB, D_MODEL), BF16)],
    )
    return pl.pallas_call(
        _moe_kernel,
        grid_spec=grid_spec,
        out_shape=jax.ShapeDtypeStruct((t, D_MODEL), F32),
        compiler_params=_cparams(("parallel", "arbitrary")),
        name="moe",
    )(counts, x1b, rank3, w3, wgu_b, wd_b)


def _final_kernel(x1_ref, ffn_ref, p_ref, wsgu_ref, wsd_ref, wp_ref, wg_ref, gain_ref, bias_ref, o_ref):
    x1 = x1_ref[...]
    xb = x1.astype(BF16)
    h = _dot(xb, wsgu_ref[...])
    act = (jax.nn.silu(h[:, :EXPERT_DIM]) * h[:, EXPERT_DIM:]).astype(BF16)
    shared = _dot(act, wsd_ref[...])
    ple = _dot(p_ref[...].astype(BF16), wp_ref[...]) * jax.nn.sigmoid(_dot(xb, wg_ref[...]))
    o_ref[...] = _layer_norm(ALPHA * x1 + (ffn_ref[...] + shared) + ple, gain_ref[...], bias_ref[...])


def _final(x1, ffn, p2, wsgu_b, wsd_b, wp_b, wg_b, gain, bias):
    t = x1.shape[0]
    tm = PROJ_TM
    tok = lambda i: (i, 0)
    const = lambda i: (0, 0)
    full = lambda a: pl.BlockSpec(a.shape, const)
    return pl.pallas_call(
        _final_kernel,
        grid=(t // tm,),
        in_specs=[pl.BlockSpec((tm, D_MODEL), tok), pl.BlockSpec((tm, D_MODEL), tok),
                  pl.BlockSpec((tm, p2.shape[1]), tok),
                  full(wsgu_b), full(wsd_b), full(wp_b), full(wg_b),
                  pl.BlockSpec((1, D_MODEL), const), pl.BlockSpec((1, D_MODEL), const)],
        out_specs=pl.BlockSpec((tm, D_MODEL), tok),
        out_shape=jax.ShapeDtypeStruct((t, D_MODEL), F32),
        compiler_params=_cparams(("parallel",)),
        name="final",
    )(x1, ffn, p2, wsgu_b, wsd_b, wp_b, wg_b, gain.astype(F32)[None, :], bias.astype(F32)[None, :])


def _rotary_tables(seq):
    half = HEAD_DIM // 2
    inv = ROPE_BASE ** (-jnp.arange(half, dtype=F32) / half)
    ang = jnp.arange(seq, dtype=jnp.int32).astype(F32)[:, None] * inv[None, :]
    cos, sin = jnp.cos(ang), jnp.sin(ang)
    reps = LANES // HEAD_DIM
    cos_t = jnp.tile(jnp.concatenate([cos, cos], axis=1), (1, reps))
    sin_t = jnp.tile(jnp.concatenate([-sin, sin], axis=1), (1, reps))
    return cos_t, sin_t


def kernel(x, p, w_in, ret_decay_fwd, ret_decay_bwd, ret_gn_gain, na_rpb, w_out, ln1_gain, ln1_bias,
           w_router, router_bias, w_expert_gu, w_expert_down, w_shared_gu, w_shared_down,
           w_ple_proj, w_ple_gate, ln2_gain, ln2_bias):
    batch, seq, d = x.shape
    t = batch * seq
    depth = w_in.shape[0]
    assert depth == 1 and d == D_MODEL
    assert seq % PROJ_TM == 0 and seq % MOE_TM == 0 and seq % RET_CHUNK == 0
    assert (seq // GRID_W) % (NA_ROWS * NA_STEP_BLOCKS) == 0 and seq // GRID_W >= NA_KEY_ROWS
    cos_t, sin_t = _rotary_tables(seq)
    x2 = x.reshape(t, d)
    for i in range(depth):
        rq, rk, rv, rg, nq, nk, nv = _in_proj(x2, w_in[i].astype(BF16), cos_t, sin_t, seq)
        ret = _retention(rq, rk, rv, rg, ret_decay_fwd[i], ret_decay_bwd[i], ret_gn_gain[i], batch, seq)
        na = _natten(nq, nk, nv, na_rpb[i], batch, seq)
        x1, x1b = _out_ln1(ret, na, x2, w_out[i].astype(BF16), ln1_gain[i], ln1_bias[i])
        w_t, rank_t, cnt = _route(x1b, w_router[i], router_bias[i])
        counts = cnt[:, ::LANES]
        ffn = _moe(x1b, rank_t, w_t, counts, w_expert_gu[i].astype(BF16), w_expert_down[i].astype(BF16))
        x2 = _final(x1, ffn, p[i].reshape(t, -1), w_shared_gu[i].astype(BF16), w_shared_down[i].astype(BF16),
                    w_ple_proj[i].astype(BF16), w_ple_gate[i].astype(BF16), ln2_gain[i], ln2_bias[i])
    return x2.reshape(batch, seq, d)
```

```python
import functools

import numpy as np
import jax
import jax.numpy as jnp
from jax import lax
from jax.experimental import pallas as pl
from jax.experimental.pallas import tpu as pltpu

F32 = jnp.float32
BF16 = jnp.bfloat16
I32 = jnp.int32

D_MODEL = 1024
HEADS = 8
HEAD_DIM = 64
GROUP_W = HEADS * HEAD_DIM
ROPE_BASE = 10000.0
GN_EPS = 1e-6
LN_EPS = 1e-5
GRID_W = 64
NA_WIN_ROWS = 8
NA_WIN_COLS = 16
N_EXPERTS = 64
N_GROUPS = 8
GROUP_SIZE = N_EXPERTS // N_GROUPS
TOPK_GROUPS = 4
TOP_K = 8
EXPERT_DIM = 256
ROUTED_SCALE = 2.5
ALPHA = 2.0 ** 0.25
NEG_BIG = -1e30
LOG2E = 1.4426950408889634
NA_Q_SCALE = HEAD_DIM ** -0.5 * LOG2E

LANES = 128
VMEM_LIMIT_BYTES = 56 * 1024 * 1024

PROJ_TM = 512
RET_CHUNK = 128
RET_UNROLL = 8
NA_ROWS = 4
NA_KEY_ROWS = NA_ROWS + NA_WIN_ROWS
NA_STEP_BLOCKS = 4
MOE_TM = 2048
MOE_SUB = 256
MOE_EB = 4
MOE_SLOTS = MOE_SUB // MOE_EB
MOE_SLOTS_SMALL = 48
MOE_NSUB = MOE_TM // MOE_SUB


def _cparams(sem):
    return pltpu.CompilerParams(dimension_semantics=sem, vmem_limit_bytes=VMEM_LIMIT_BYTES)


def _dot(a, b):
    return jnp.dot(a, b, preferred_element_type=F32)


def _dot_nt(a, b):
    return lax.dot_general(a, b, (((1,), (1,)), ((), ())), preferred_element_type=F32)


def _dot_tn(a, b):
    return lax.dot_general(a, b, (((0,), (0,)), ((), ())), preferred_element_type=F32)


def _cast_once(src_ref, dst_ref):
    @pl.when(pl.program_id(0) == 0)
    def _():
        dst_ref[...] = src_ref[...].astype(BF16)


def _resident(shape):
    return pl.BlockSpec(shape, lambda i: (0,) * len(shape), pipeline_mode=pl.Buffered(1))


def _in_proj_kernel(x_ref, wf_ref, cos_ref, sin_ref,
                    rq_ref, rk_ref, rv_ref, rg_ref, nq_ref, nk_ref, nv_ref, w_ref):
    _cast_once(wf_ref, w_ref)
    xb = x_ref[...].astype(BF16)
    cos = cos_ref[...]
    sin = sin_ref[...]
    lane = lax.broadcasted_iota(I32, (1, LANES), 1)
    first_half = (lane % HEAD_DIM) < (HEAD_DIM // 2)

    def proj(g):
        return _dot(xb, w_ref[:, g * GROUP_W:(g + 1) * GROUP_W])

    def rotary(t, scale):
        outs = []
        for j in range(GROUP_W // LANES):
            c = t[:, j * LANES:(j + 1) * LANES]
            swapped = jnp.where(first_half,
                                pltpu.roll(c, LANES - HEAD_DIM // 2, axis=1),
                                pltpu.roll(c, HEAD_DIM // 2, axis=1))
            outs.append((c * cos + swapped * sin) * scale)
        return jnp.concatenate(outs, axis=1)

    rq_ref[...] = rotary(proj(0), 1.0).astype(BF16)
    rk_ref[...] = rotary(proj(1), HEAD_DIM ** -0.5).astype(BF16)
    rv_ref[...] = proj(2).astype(BF16)
    rg_ref[...] = jax.nn.silu(proj(3)).astype(BF16)
    nq_ref[...] = (proj(4) * NA_Q_SCALE).astype(BF16)
    nk_ref[...] = proj(5).astype(BF16)
    nv_ref[...] = proj(6).astype(BF16)


def _in_proj(x2, w_in, cos_t, sin_t, seq):
    t = x2.shape[0]
    tm = PROJ_TM
    n_pos = seq // tm
    out = jax.ShapeDtypeStruct((t, GROUP_W), BF16)
    tok = lambda i: (i, 0)
    return pl.pallas_call(
        _in_proj_kernel,
        grid=(t // tm,),
        in_specs=[
            pl.BlockSpec((tm, D_MODEL), tok),
            _resident(w_in.shape),
            pl.BlockSpec((tm, LANES), lambda i: (i % n_pos, 0)),
            pl.BlockSpec((tm, LANES), lambda i: (i % n_pos, 0)),
        ],
        out_specs=[pl.BlockSpec((tm, GROUP_W), tok)] * 7,
        out_shape=[out] * 7,
        scratch_shapes=[pltpu.VMEM(w_in.shape, BF16)],
        compiler_params=_cparams(("arbitrary",)),
        name="in_proj",
    )(x2, w_in.astype(F32), cos_t, sin_t)


def _log_sigmoid(x):
    return jnp.minimum(x, 0.0) - jnp.log1p(jnp.exp(-jnp.abs(x)))


def _retention_kernel(q_ref, k_ref, v_ref, g_ref, decf_ref, decb_ref, decfd_ref, decbd_ref,
                      gain_ref, o_ref, kv_ref, st_ref, dmat_ref):
    c = RET_CHUNK
    n_chunks = q_ref.shape[0] // c
    pair_w = 2 * HEAD_DIM

    lgf = _log_sigmoid(decf_ref[...])
    lgb = _log_sigmoid(decb_ref[...])
    row = lax.broadcasted_iota(I32, (c, 1), 0).astype(F32)
    k_dec_f = jnp.exp((c - 1.0 - row) * lgf)
    k_dec_b = jnp.exp(row * lgb)
    q_dec_f = jnp.exp((row + 1.0) * lgf)
    q_dec_b = jnp.exp((c - row) * lgb)
    chunk_dec_f = jnp.exp(c * lgf)
    chunk_dec_b = jnp.exp(c * lgb)

    lane = lax.broadcasted_iota(I32, (1, pair_w), 1)
    head0 = lane < HEAD_DIM
    r2 = lax.broadcasted_iota(I32, (pair_w, pair_w), 0) // HEAD_DIM
    c2 = lax.broadcasted_iota(I32, (pair_w, pair_w), 1) // HEAD_DIM
    same_head = r2 == c2
    block_diag = jnp.where(same_head, 1.0, 0.0)
    seg_avg = jnp.where(same_head, 1.0 / HEAD_DIM, 0.0).astype(BF16)

    lgf_d = _log_sigmoid(decfd_ref[0])
    lgb_d = _log_sigmoid(decbd_ref[0])
    di = lax.broadcasted_iota(I32, (c, 2 * c), 0)
    dj = lax.broadcasted_iota(I32, (c, 2 * c), 1) % c
    diff = (di - dj).astype(F32)
    dmat_ref[...] = jnp.where(diff >= 0.0, jnp.exp(diff * lgf_d), jnp.exp(-diff * lgb_d))

    def chunk(ref, n):
        return ref[pl.ds(pl.multiple_of(n * c, c), c), :]

    unroll = RET_UNROLL

    def summarize(nb, carry):
        for u in range(unroll):
            n = nb * unroll + u
            kf = chunk(k_ref, n).astype(F32)
            kst = jnp.concatenate([kf * k_dec_f, kf * k_dec_b], axis=1).astype(BF16)
            kv_ref[n] = _dot_tn(kst, chunk(v_ref, n))
        return carry

    lax.fori_loop(0, n_chunks // unroll, summarize, 0)

    def fwd_scan(n, state):
        st_ref[n, 0:pair_w, :] = state.astype(BF16)
        return state * chunk_dec_f + kv_ref[n, 0:pair_w, :] * block_diag

    lax.fori_loop(0, n_chunks, fwd_scan, jnp.zeros((pair_w, pair_w), F32))

    def bwd_scan(i, state):
        n = n_chunks - 1 - i
        st_ref[n, pair_w:2 * pair_w, :] = state.astype(BF16)
        return state * chunk_dec_b + kv_ref[n, pair_w:2 * pair_w, :] * block_diag

    lax.fori_loop(0, n_chunks, bwd_scan, jnp.zeros((pair_w, pair_w), F32))

    gain = gain_ref[...]

    def seg_mean(z):
        hi = z.astype(BF16)
        lo = (z - hi.astype(F32)).astype(BF16)
        return _dot(hi, seg_avg) + _dot(lo, seg_avg)

    def emit(nb, carry):
        ys = []
        for u in range(unroll):
            n = nb * unroll + u
            q = chunk(q_ref, n)
            k = chunk(k_ref, n)
            v = chunk(v_ref, n)
            zero = jnp.zeros_like(k)
            k_st = jnp.concatenate([jnp.where(head0, k, zero), jnp.where(head0, zero, k)], axis=0)
            v_st = jnp.concatenate([jnp.where(head0, v, zero), jnp.where(head0, zero, v)], axis=0)
            scores = _dot_nt(q, k_st) * dmat_ref[...]
            qf = q.astype(F32)
            q_st = jnp.concatenate([qf * q_dec_f, qf * q_dec_b], axis=1).astype(BF16)
            ys.append(_dot(scores.astype(BF16), v_st) + _dot(q_st, st_ref[n]))
        y = jnp.concatenate(ys, axis=0)
        mu = seg_mean(y)
        d = y - mu
        var = seg_mean(d * d)
        yn = d * lax.rsqrt(var + GN_EPS) * gain
        rows = pl.ds(pl.multiple_of(nb * (unroll * c), unroll * c), unroll * c)
        o_ref[rows, :] = (g_ref[rows, :].astype(F32) * yn).astype(BF16)
        return carry

    lax.fori_loop(0, n_chunks // unroll, emit, 0)


def _retention(rq, rk, rv, rg, dec_f, dec_b, gain, batch, seq):
    t = rq.shape[0]
    c = RET_CHUNK
    n_pairs = HEADS // 2
    pair_w = 2 * HEAD_DIM
    dec_f_lane = jnp.repeat(dec_f.astype(F32), HEAD_DIM)[None, :]
    dec_b_lane = jnp.repeat(dec_b.astype(F32), HEAD_DIM)[None, :]
    dec_f_col = jnp.repeat(dec_f.astype(F32), c).reshape(n_pairs, 1, 2 * c)
    dec_b_col = jnp.repeat(dec_b.astype(F32), c).reshape(n_pairs, 1, 2 * c)
    tok = pl.BlockSpec((seq, pair_w), lambda b, p: (b, p))
    lane_spec = pl.BlockSpec((1, pair_w), lambda b, p: (0, p))
    col_spec = pl.BlockSpec((1, 1, 2 * c), lambda b, p: (p, 0, 0))
    return pl.pallas_call(
        _retention_kernel,
        grid=(batch, n_pairs),
        in_specs=[tok, tok, tok, tok, lane_spec, lane_spec, col_spec, col_spec, lane_spec],
        out_specs=tok,
        out_shape=jax.ShapeDtypeStruct((t, GROUP_W), BF16),
        scratch_shapes=[
            pltpu.VMEM((seq // c, 2 * pair_w, pair_w), F32),
            pltpu.VMEM((seq // c, 2 * pair_w, pair_w), BF16),
            pltpu.VMEM((c, 2 * c), F32),
        ],
        compiler_params=_cparams(("parallel", "parallel")),
        name="retention",
    )(rq, rk, rv, rg, dec_f_lane, dec_b_lane, dec_f_col, dec_b_col, gain.astype(F32)[None, :])


N_ROW_OFFSETS = 2 * NA_WIN_ROWS - 1
N_COL_OFFSETS = 2 * NA_WIN_COLS - 1


def _natten_row_offsets(rows):
    n_blocks = rows // NA_ROWS
    starts = {0: 0, 1: NA_ROWS - NA_WIN_ROWS // 2, 2: rows - NA_KEY_ROWS}
    blocks = {0: 0, 1: 1, 2: n_blocks - 1}
    table = []
    for v in range(3):
        per_a = []
        for a in range(NA_ROWS):
            r = blocks[v] * NA_ROWS + a
            rs = min(max(r - NA_WIN_ROWS // 2, 0), rows - NA_WIN_ROWS)
            per_kl = []
            for kl in range(NA_KEY_ROWS):
                kr = starts[v] + kl
                per_kl.append(kr - r + NA_WIN_ROWS - 1 if rs <= kr < rs + NA_WIN_ROWS else None)
            per_a.append(per_kl)
        table.append(per_a)
    return table


def _natten_bias_kernel(rpb_ref, tab_ref, *, rows):
    offsets = _natten_row_offsets(rows)
    nk = NA_KEY_ROWS * GRID_W
    shape = (GRID_W, LANES)
    lane = lax.broadcasted_iota(I32, shape, 1)
    c = lax.broadcasted_iota(I32, shape, 0)
    second = lane >= GRID_W
    kc = lane % GRID_W
    cs = jnp.clip(c - NA_WIN_COLS // 2, 0, GRID_W - NA_WIN_COLS)
    col_ok = jnp.logical_and(kc >= cs, kc < cs + NA_WIN_COLS)
    neg = jnp.full(shape, NEG_BIG, F32)
    for hh in range(2):
        toeplitz = []
        for dr in range(N_ROW_OFFSETS):
            x = jnp.broadcast_to(rpb_ref[hh, dr:dr + 1, :] * LOG2E, shape)
            lo = pltpu.roll(x, LANES - (NA_WIN_COLS - 1), axis=1, stride=1, stride_axis=0)
            hi = pltpu.roll(x, GRID_W - (NA_WIN_COLS - 1), axis=1, stride=1, stride_axis=0)
            toeplitz.append(jnp.where(second, hi, lo))
        for v in range(3):
            for a in range(NA_ROWS):
                for j in range(NA_KEY_ROWS // 2):
                    d0, d1 = offsets[v][a][2 * j], offsets[v][a][2 * j + 1]
                    if d0 is None and d1 is None:
                        piece = neg
                    else:
                        t0 = neg if d0 is None else toeplitz[d0]
                        t1 = neg if d1 is None else toeplitz[d1]
                        piece = jnp.where(col_ok, jnp.where(second, t1, t0), neg)
                    tab_ref[v, 0, a * GRID_W:(a + 1) * GRID_W,
                            hh * nk + j * LANES:hh * nk + (j + 1) * LANES] = piece


def _natten_bias_table(rpb, rows):
    n_pairs = HEADS // 2
    nqb = NA_ROWS * GRID_W
    nkb = NA_KEY_ROWS * GRID_W
    rpb_pad = jnp.pad(rpb.astype(F32), ((0, 0), (0, 0), (0, LANES - N_COL_OFFSETS)))
    return pl.pallas_call(
        functools.partial(_natten_bias_kernel, rows=rows),
        grid=(n_pairs,),
        in_specs=[pl.BlockSpec((2, N_ROW_OFFSETS, LANES), lambda p: (p, 0, 0))],
        out_specs=pl.BlockSpec((3, 1, nqb, 2 * nkb), lambda p: (0, p, 0, 0)),
        out_shape=jax.ShapeDtypeStruct((3, n_pairs, nqb, 2 * nkb), F32),
        compiler_params=_cparams(("parallel",)),
        name="natten_bias",
    )(rpb_pad)


def _natten_kernel(q_ref, k_ref, v_ref, *rest, rows):
    tab_refs, o_ref, ks_ref, vs_ref = rest[:NA_STEP_BLOCKS], *rest[NA_STEP_BLOCKS:]
    nq = NA_ROWS * GRID_W
    nk = NA_KEY_ROWS * GRID_W
    lane = lax.broadcasted_iota(I32, (1, 2 * HEAD_DIM), 1)
    head0 = lane < HEAD_DIM

    @pl.when(pl.program_id(2) == 0)
    def _():
        for src_ref, dst_ref in ((k_ref, ks_ref), (v_ref, vs_ref)):
            full = src_ref[...]
            zero = jnp.zeros_like(full)
            dst_ref[0] = jnp.where(head0, full, zero)
            dst_ref[1] = jnp.where(head0, zero, full)

    for blk in range(NA_STEP_BLOCKS):
        rb = pl.program_id(2) * NA_STEP_BLOCKS + blk
        start_row = jnp.clip(rb * NA_ROWS - NA_WIN_ROWS // 2, 0, rows - NA_KEY_ROWS)
        win = pl.ds(pl.multiple_of(start_row * GRID_W, GRID_W), nk)
        q = q_ref[blk * nq:(blk + 1) * nq, :]
        k_st = jnp.concatenate([ks_ref[0, win, :], ks_ref[1, win, :]], axis=0)
        v_st = jnp.concatenate([vs_ref[0, win, :], vs_ref[1, win, :]], axis=0)
        s = _dot_nt(q, k_st) + tab_refs[blk][0, 0]
        probs, denoms = [], []
        for h in range(2):
            sh = s[:, h * nk:(h + 1) * nk]
            m = jnp.max(sh, axis=1, keepdims=True)
            p = jnp.exp2(sh - m)
            denoms.append(jnp.sum(p, axis=1, keepdims=True))
            probs.append(p.astype(BF16))
        out = _dot(jnp.concatenate(probs, axis=1), v_st)
        denom = jnp.where(head0, denoms[0], denoms[1])
        o_ref[blk * nq:(blk + 1) * nq, :] = (out / denom).astype(BF16)


def _natten(nq, nk, nv, rpb, batch, seq):
    t = nq.shape[0]
    rows = seq // GRID_W
    n_pairs = HEADS // 2
    n_blocks = rows // NA_ROWS
    n_steps = n_blocks // NA_STEP_BLOCKS
    nqb = NA_ROWS * GRID_W
    nkb = NA_KEY_ROWS * GRID_W
    tab = _natten_bias_table(rpb, rows)

    def tab_spec(blk):
        def index(b, p, s):
            rb = s * NA_STEP_BLOCKS + blk
            return (jnp.where(rb == 0, 0, jnp.where(rb == n_blocks - 1, 2, 1)), p, 0, 0)
        return pl.BlockSpec((1, 1, nqb, 2 * nkb), index)

    kv_spec = pl.BlockSpec((seq, 2 * HEAD_DIM), lambda b, p, s: (b, p))
    q_spec = pl.BlockSpec((NA_STEP_BLOCKS * nqb, 2 * HEAD_DIM), lambda b, p, s: (b * n_steps + s, p))
    return pl.pallas_call(
        functools.partial(_natten_kernel, rows=rows),
        grid=(batch, n_pairs, n_steps),
        in_specs=[q_spec, kv_spec, kv_spec] + [tab_spec(blk) for blk in range(NA_STEP_BLOCKS)],
        out_specs=q_spec,
        out_shape=jax.ShapeDtypeStruct((t, GROUP_W), BF16),
        scratch_shapes=[pltpu.VMEM((2, seq, 2 * HEAD_DIM), BF16), pltpu.VMEM((2, seq, 2 * HEAD_DIM), BF16)],
        compiler_params=_cparams(("parallel", "parallel", "arbitrary")),
        name="natten",
    )(nq, nk, nv, *([tab] * NA_STEP_BLOCKS))


def _layer_norm(h, gain, bias):
    mu = jnp.mean(h, axis=-1, keepdims=True)
    d = h - mu
    var = jnp.mean(d * d, axis=-1, keepdims=True)
    return d * lax.rsqrt(var + LN_EPS) * gain + bias


def _out_ln1_kernel(ret_ref, na_ref, x_ref, wf_ref, gain_ref, bias_ref, x1_ref, x1b_ref, w_ref):
    _cast_once(wf_ref, w_ref)
    mix = _dot(ret_ref[...], w_ref[0:GROUP_W, :]) + _dot(na_ref[...], w_ref[GROUP_W:2 * GROUP_W, :])
    x1 = _layer_norm(ALPHA * x_ref[...] + mix, gain_ref[...], bias_ref[...])
    x1_ref[...] = x1
    x1b_ref[...] = x1.astype(BF16)


def _out_ln1(ret, na, x2, w_out, gain, bias):
    t = x2.shape[0]
    tm = PROJ_TM
    tok = lambda i: (i, 0)
    const = lambda i: (0, 0)
    return pl.pallas_call(
        _out_ln1_kernel,
        grid=(t // tm,),
        in_specs=[pl.BlockSpec((tm, GROUP_W), tok), pl.BlockSpec((tm, GROUP_W), tok),
                  pl.BlockSpec((tm, D_MODEL), tok), _resident(w_out.shape),
                  pl.BlockSpec((1, D_MODEL), const), pl.BlockSpec((1, D_MODEL), const)],
        out_specs=[pl.BlockSpec((tm, D_MODEL), tok), pl.BlockSpec((tm, D_MODEL), tok)],
        out_shape=[jax.ShapeDtypeStruct((t, D_MODEL), F32), jax.ShapeDtypeStruct((t, D_MODEL), BF16)],
        scratch_shapes=[pltpu.VMEM(w_out.shape, BF16)],
        compiler_params=_cparams(("arbitrary",)),
        name="out_ln1",
    )(ret, na, x2, w_out.astype(F32), gain.astype(F32)[None, :], bias.astype(F32)[None, :])


def _route_kernel(x_ref, wr_ref, rb_ref, w_ref, rank_ref, cnt_ref, tri_ref):
    tm = x_ref.shape[0]

    @pl.when(pl.program_id(0) == 0)
    def _():
        i = lax.broadcasted_iota(I32, (MOE_SUB, MOE_SUB), 0)
        j = lax.broadcasted_iota(I32, (MOE_SUB, MOE_SUB), 1)
        tri_ref[...] = jnp.where(i < j, 1.0, 0.0).astype(BF16)

    scores = jax.nn.sigmoid(_dot_nt(wr_ref[...], x_ref[...]))
    biased = scores + rb_ref[...]
    sub = lax.broadcasted_iota(I32, (GROUP_SIZE, tm), 0).astype(F32)
    none = float(N_EXPERTS)
    ninf = -jnp.inf

    def first_max(vals, index):
        m = jnp.max(vals, axis=0, keepdims=True)
        return m, jnp.min(jnp.where(vals == m, index, none), axis=0, keepdims=True)

    groups = [biased[g * GROUP_SIZE:(g + 1) * GROUP_SIZE, :] for g in range(N_GROUPS)]
    group_scores = []
    for g in range(N_GROUPS):
        m1, i1 = first_max(groups[g], sub)
        m2 = jnp.max(jnp.where(sub == i1, ninf, groups[g]), axis=0, keepdims=True)
        group_scores.append(m1 + m2)
    cur = jnp.concatenate(group_scores, axis=0)
    group_sel = jnp.zeros(cur.shape, F32)
    for _ in range(TOPK_GROUPS):
        _, i1 = first_max(cur, sub)
        hit = sub == i1
        group_sel = jnp.where(hit, 1.0, group_sel)
        cur = jnp.where(hit, ninf, cur)

    masked = [jnp.where(group_sel[g:g + 1, :] > 0.5, groups[g], ninf) for g in range(N_GROUPS)]
    ids = [sub + float(g * GROUP_SIZE) for g in range(N_GROUPS)]
    chosen = [jnp.zeros((GROUP_SIZE, tm), F32) for _ in range(N_GROUPS)]
    for _ in range(TOP_K):
        m = masked[0]
        for g in range(1, N_GROUPS):
            m = jnp.maximum(m, masked[g])
        m = jnp.max(m, axis=0, keepdims=True)
        cand = jnp.where(masked[0] == m, ids[0], none)
        for g in range(1, N_GROUPS):
            cand = jnp.minimum(cand, jnp.where(masked[g] == m, ids[g], none))
        first = jnp.min(cand, axis=0, keepdims=True)
        for g in range(N_GROUPS):
            hit = ids[g] == first
            chosen[g] = jnp.where(hit, 1.0, chosen[g])
            masked[g] = jnp.where(hit, ninf, masked[g])

    sel = jnp.concatenate(chosen, axis=0) > 0.5
    picked = jnp.where(sel, scores, 0.0)
    total = jnp.sum(picked, axis=0, keepdims=True)
    w_ref[...] = picked / total * ROUTED_SCALE
    sel_f = jnp.where(sel, 1.0, 0.0)
    sel_b = sel_f.astype(BF16)
    before, cnt_max = [], None
    for k in range(MOE_NSUB):
        cols = slice(k * MOE_SUB, (k + 1) * MOE_SUB)
        before.append(_dot(sel_b[:, cols], tri_ref[...]))
        cnt = jnp.sum(sel_f[:, cols], axis=1, keepdims=True)
        cnt_max = cnt if cnt_max is None else jnp.maximum(cnt_max, cnt)
    rank_ref[...] = jnp.where(sel, jnp.concatenate(before, axis=1).astype(I32), -1)
    cnt_ref[...] = jnp.broadcast_to(cnt_max, cnt_ref.shape).astype(I32)


def _route(x1b, w_router, router_bias):
    t = x1b.shape[0]
    tm = MOE_TM
    nt = t // tm
    wr_t = w_router.astype(F32).T.astype(BF16)
    const = lambda i: (0, 0)
    col = lambda i: (0, i)
    return pl.pallas_call(
        _route_kernel,
        grid=(nt,),
        in_specs=[pl.BlockSpec((tm, D_MODEL), lambda i: (i, 0)),
                  pl.BlockSpec((N_EXPERTS, D_MODEL), const),
                  pl.BlockSpec((N_EXPERTS, 1), const)],
        out_specs=[pl.BlockSpec((N_EXPERTS, tm), col), pl.BlockSpec((N_EXPERTS, tm), col),
                   pl.BlockSpec((N_EXPERTS, LANES), col)],
        out_shape=[jax.ShapeDtypeStruct((N_EXPERTS, t), F32),
                   jax.ShapeDtypeStruct((N_EXPERTS, t), I32),
                   jax.ShapeDtypeStruct((N_EXPERTS, nt * LANES), I32)],
        scratch_shapes=[pltpu.VMEM((MOE_SUB, MOE_SUB), BF16)],
        compiler_params=_cparams(("arbitrary",)),
        name="route",
    )(x1b, wr_t, router_bias.astype(F32)[:, None])


def _moe_kernel(cnt_ref, x_ref, rank_ref, w_ref, wgu_ref, wd_ref, o_ref, oh_ref, g_ref, y_ref):
    i = pl.program_id(0)
    eb = pl.program_id(1)

    @pl.when(eb == 0)
    def _():
        o_ref[...] = jnp.zeros_like(o_ref)

    max_count = cnt_ref[eb, i]

    def one_pass(p, slots):
        n_rows = MOE_EB * slots
        base = p * slots
        slot = lax.broadcasted_iota(I32, (slots, MOE_SUB), 0)
        slot_w = []
        for k in range(MOE_NSUB):
            tok = slice(k * MOE_SUB, (k + 1) * MOE_SUB)
            blocks, weights = [], []
            for j in range(MOE_EB):
                match = slot == (rank_ref[j, :, tok] - base)
                blocks.append(jnp.where(match, 1.0, 0.0).astype(BF16))
                weights.append(jnp.sum(jnp.where(match, w_ref[j, :, tok], 0.0), axis=1, keepdims=True))
            onehot = jnp.concatenate(blocks, axis=0)
            oh_ref[k, 0:n_rows, :] = onehot
            slot_w.append(weights)
            g_ref[k, 0:n_rows, :] = _dot(onehot, x_ref[tok, :]).astype(BF16)
        for j in range(MOE_EB):
            rows = slice(j * slots, (j + 1) * slots)
            xe = jnp.concatenate([g_ref[k, rows, :] for k in range(MOE_NSUB)], axis=0)
            h = _dot(xe, wgu_ref[j])
            act = (jax.nn.silu(h[:, :EXPERT_DIM]) * h[:, EXPERT_DIM:]).astype(BF16)
            wc = jnp.concatenate([slot_w[k][j] for k in range(MOE_NSUB)], axis=0)
            yw = (_dot(act, wd_ref[j]) * wc).astype(BF16)
            for k in range(MOE_NSUB):
                y_ref[k, rows, :] = yw[k * slots:(k + 1) * slots, :]
        for k in range(MOE_NSUB):
            tok = slice(k * MOE_SUB, (k + 1) * MOE_SUB)
            o_ref[tok, :] += _dot_tn(oh_ref[k, 0:n_rows, :], y_ref[k, 0:n_rows, :])

    @pl.when(max_count <= MOE_SLOTS_SMALL)
    def _():
        one_pass(0, MOE_SLOTS_SMALL)

    @pl.when(max_count > MOE_SLOTS_SMALL)
    def _():
        n_pass = lax.shift_right_logical(max_count + (MOE_SLOTS - 1), int(np.log2(MOE_SLOTS)))

        def body(p, carry):
            one_pass(p, MOE_SLOTS)
            return carry

        lax.fori_loop(0, n_pass, body, 0)


def _moe(x1b, rank_t, w_t, counts, wgu_b, wd_b):
    t = x1b.shape[0]
    tm = MOE_TM
    nt = t // tm
    rank3 = rank_t.reshape(N_EXPERTS, 1, t)
    w3 = w_t.reshape(N_EXPERTS, 1, t)
    counts = counts.reshape(N_EXPERTS // MOE_EB, MOE_EB, nt).max(axis=1)
    grid_spec = pltpu.PrefetchScalarGridSpec(
        num_scalar_prefetch=1,
        grid=(nt, N_EXPERTS // MOE_EB),
        in_specs=[
            pl.BlockSpec((tm, D_MODEL), lambda i, e, c: (i, 0)),
            pl.BlockSpec((MOE_EB, 1, tm), lambda i, e, c: (e, 0, i)),
            pl.BlockSpec((MOE_EB, 1, tm), lambda i, e, c: (e, 0, i)),
            pl.BlockSpec((MOE_EB, D_MODEL, 2 * EXPERT_DIM), lambda i, e, c: (e, 0, 0)),
            pl.BlockSpec((MOE_EB, EXPERT_DIM, D_MODEL), lambda i, e, c: (e, 0, 0)),
        ],
        out_specs=pl.BlockSpec((tm, D_MODEL), lambda i, e, c: (i, 0)),
        scratch_shapes=[pltpu.VMEM((MOE_NSUB, MOE_SUB, MOE_SUB), BF16),
                        pltpu.VMEM((MOE_NSUB, MOE_SUB, D_MODEL), BF16),
                        pltpu.VMEM((MOE_NSUB, MOE_SUB, D_MODEL), BF16)],
    )
    return pl.pallas_call(
        _moe_kernel,
        grid_spec=grid_spec,
        out_shape=jax.ShapeDtypeStruct((t, D_MODEL), F32),
        compiler_params=_cparams(("parallel", "arbitrary")),
        name="moe",
    )(counts, x1b, rank3, w3, wgu_b, wd_b)


def _final_kernel(x1_ref, ffn_ref, p_ref, wsgu_f, wsd_f, wp_f, wg_f, gain_ref, bias_ref, o_ref,
                  wsgu_ref, wsd_ref, wp_ref, wg_ref):
    for src_ref, dst_ref in ((wsgu_f, wsgu_ref), (wsd_f, wsd_ref), (wp_f, wp_ref), (wg_f, wg_ref)):
        _cast_once(src_ref, dst_ref)
    x1 = x1_ref[...]
    xb = x1.astype(BF16)
    h = _dot(xb, wsgu_ref[...])
    act = (jax.nn.silu(h[:, :EXPERT_DIM]) * h[:, EXPERT_DIM:]).astype(BF16)
    shared = _dot(act, wsd_ref[...])
    ple = _dot(p_ref[...].astype(BF16), wp_ref[...]) * jax.nn.sigmoid(_dot(xb, wg_ref[...]))
    o_ref[...] = _layer_norm(ALPHA * x1 + (ffn_ref[...] + shared) + ple, gain_ref[...], bias_ref[...])


def _final(x1, ffn, p2, wsgu, wsd, wp, wg, gain, bias):
    t = x1.shape[0]
    tm = PROJ_TM
    tok = lambda i: (i, 0)
    const = lambda i: (0, 0)
    weights = [w.astype(F32) for w in (wsgu, wsd, wp, wg)]
    return pl.pallas_call(
        _final_kernel,
        grid=(t // tm,),
        in_specs=[pl.BlockSpec((tm, D_MODEL), tok), pl.BlockSpec((tm, D_MODEL), tok),
                  pl.BlockSpec((tm, p2.shape[1]), tok)]
                 + [_resident(w.shape) for w in weights]
                 + [pl.BlockSpec((1, D_MODEL), const), pl.BlockSpec((1, D_MODEL), const)],
        out_specs=pl.BlockSpec((tm, D_MODEL), tok),
        out_shape=jax.ShapeDtypeStruct((t, D_MODEL), F32),
        scratch_shapes=[pltpu.VMEM(w.shape, BF16) for w in weights],
        compiler_params=_cparams(("arbitrary",)),
        name="final",
    )(x1, ffn, p2, *weights, gain.astype(F32)[None, :], bias.astype(F32)[None, :])


def _rotary_tables(seq):
    half = HEAD_DIM // 2
    inv = ROPE_BASE ** (-jnp.arange(half, dtype=F32) / half)
    ang = jnp.arange(seq, dtype=jnp.int32).astype(F32)[:, None] * inv[None, :]
    cos, sin = jnp.cos(ang), jnp.sin(ang)
    reps = LANES // HEAD_DIM
    cos_t = jnp.tile(jnp.concatenate([cos, cos], axis=1), (1, reps))
    sin_t = jnp.tile(jnp.concatenate([-sin, sin], axis=1), (1, reps))
    return cos_t, sin_t


def kernel(x, p, w_in, ret_decay_fwd, ret_decay_bwd, ret_gn_gain, na_rpb, w_out, ln1_gain, ln1_bias,
           w_router, router_bias, w_expert_gu, w_expert_down, w_shared_gu, w_shared_down,
           w_ple_proj, w_ple_gate, ln2_gain, ln2_bias):
    batch, seq, d = x.shape
    t = batch * seq
    depth = w_in.shape[0]
    assert depth == 1 and d == D_MODEL
    assert seq % PROJ_TM == 0 and seq % MOE_TM == 0 and seq % RET_CHUNK == 0
    assert (seq // GRID_W) % (NA_ROWS * NA_STEP_BLOCKS) == 0 and seq // GRID_W >= NA_KEY_ROWS
    cos_t, sin_t = _rotary_tables(seq)
    x2 = x.reshape(t, d)
    for i in range(depth):
        rq, rk, rv, rg, nq, nk, nv = _in_proj(x2, w_in[i], cos_t, sin_t, seq)
        ret = _retention(rq, rk, rv, rg, ret_decay_fwd[i], ret_decay_bwd[i], ret_gn_gain[i], batch, seq)
        na = _natten(nq, nk, nv, na_rpb[i], batch, seq)
        x1, x1b = _out_ln1(ret, na, x2, w_out[i], ln1_gain[i], ln1_bias[i])
        w_t, rank_t, cnt = _route(x1b, w_router[i], router_bias[i])
        counts = cnt[:, ::LANES]
        ffn = _moe(x1b, rank_t, w_t, counts, w_expert_gu[i].astype(BF16), w_expert_down[i].astype(BF16))
        x2 = _final(x1, ffn, p[i].reshape(t, -1), w_shared_gu[i], w_shared_down[i],
                    w_ple_proj[i], w_ple_gate[i], ln2_gain[i], ln2_bias[i])
    return x2.reshape(batch, seq, d)
```

```python
import functools

import numpy as np
import jax
import jax.numpy as jnp
from jax import lax
from jax.experimental import pallas as pl
from jax.experimental.pallas import tpu as pltpu

F32 = jnp.float32
BF16 = jnp.bfloat16
I32 = jnp.int32

D_MODEL = 1024
HEADS = 8
HEAD_DIM = 64
GROUP_W = HEADS * HEAD_DIM
ROPE_BASE = 10000.0
GN_EPS = 1e-6
LN_EPS = 1e-5
GRID_W = 64
NA_WIN_ROWS = 8
NA_WIN_COLS = 16
N_EXPERTS = 64
N_GROUPS = 8
GROUP_SIZE = N_EXPERTS // N_GROUPS
TOPK_GROUPS = 4
TOP_K = 8
EXPERT_DIM = 256
ROUTED_SCALE = 2.5
ALPHA = 2.0 ** 0.25
NEG_BIG = -1e30
LOG2E = 1.4426950408889634
NA_Q_SCALE = HEAD_DIM ** -0.5 * LOG2E

LANES = 128
VMEM_LIMIT_BYTES = 56 * 1024 * 1024

PROJ_TM = 512
PROJ_SUB = 256
RET_CHUNK = 128
RET_UNROLL = 8
NA_ROWS = 4
NA_KEY_ROWS = NA_ROWS + NA_WIN_ROWS
NA_STEP_BLOCKS = 4
MOE_TM = 2048
MOE_SUB = 256
MOE_EB = 4
MOE_SLOTS = MOE_SUB // MOE_EB
MOE_SLOTS_SMALL = 48
MOE_NSUB = MOE_TM // MOE_SUB


def _cparams(sem):
    return pltpu.CompilerParams(dimension_semantics=sem, vmem_limit_bytes=VMEM_LIMIT_BYTES)


def _dot(a, b):
    return jnp.dot(a, b, preferred_element_type=F32)


def _dot_nt(a, b):
    return lax.dot_general(a, b, (((1,), (1,)), ((), ())), preferred_element_type=F32)


def _dot_tn(a, b):
    return lax.dot_general(a, b, (((0,), (0,)), ((), ())), preferred_element_type=F32)


def _cast_once(src_ref, dst_ref):
    @pl.when(pl.program_id(0) == 0)
    def _():
        dst_ref[...] = src_ref[...].astype(BF16)


def _resident(shape):
    return pl.BlockSpec(shape, lambda i: (0,) * len(shape), pipeline_mode=pl.Buffered(1))


def _in_proj_kernel(x_ref, wf_ref, cos_ref, sin_ref,
                    rq_ref, rk_ref, rv_ref, rg_ref, nq_ref, nk_ref, nv_ref, w_ref):
    _cast_once(wf_ref, w_ref)
    xb = x_ref[...].astype(BF16)
    cos = cos_ref[...]
    sin = sin_ref[...]
    lane = lax.broadcasted_iota(I32, (1, LANES), 1)
    first_half = (lane % HEAD_DIM) < (HEAD_DIM // 2)

    def proj(g):
        return _dot(xb, w_ref[:, g * GROUP_W:(g + 1) * GROUP_W])

    def rotary(t, scale):
        outs = []
        for j in range(GROUP_W // LANES):
            c = t[:, j * LANES:(j + 1) * LANES]
            swapped = jnp.where(first_half,
                                pltpu.roll(c, LANES - HEAD_DIM // 2, axis=1),
                                pltpu.roll(c, HEAD_DIM // 2, axis=1))
            outs.append((c * cos + swapped * sin) * scale)
        return jnp.concatenate(outs, axis=1)

    rq_ref[...] = rotary(proj(0), 1.0).astype(BF16)
    rk_ref[...] = rotary(proj(1), HEAD_DIM ** -0.5).astype(BF16)
    rv_ref[...] = proj(2).astype(BF16)
    rg_ref[...] = jax.nn.silu(proj(3)).astype(BF16)
    nq_ref[...] = (proj(4) * NA_Q_SCALE).astype(BF16)
    nk_ref[...] = proj(5).astype(BF16)
    nv_ref[...] = proj(6).astype(BF16)


def _in_proj(x2, w_in, cos_t, sin_t, seq):
    t = x2.shape[0]
    tm = PROJ_TM
    n_pos = seq // tm
    out = jax.ShapeDtypeStruct((t, GROUP_W), BF16)
    tok = lambda i: (i, 0)
    return pl.pallas_call(
        _in_proj_kernel,
        grid=(t // tm,),
        in_specs=[
            pl.BlockSpec((tm, D_MODEL), tok),
            _resident(w_in.shape),
            pl.BlockSpec((tm, LANES), lambda i: (i % n_pos, 0)),
            pl.BlockSpec((tm, LANES), lambda i: (i % n_pos, 0)),
        ],
        out_specs=[pl.BlockSpec((tm, GROUP_W), tok)] * 7,
        out_shape=[out] * 7,
        scratch_shapes=[pltpu.VMEM(w_in.shape, BF16)],
        compiler_params=_cparams(("arbitrary",)),
        name="in_proj",
    )(x2, w_in.astype(F32), cos_t, sin_t)


def _log_sigmoid(x):
    return jnp.minimum(x, 0.0) - jnp.log1p(jnp.exp(-jnp.abs(x)))


def _retention_kernel(q_ref, k_ref, v_ref, g_ref, decf_ref, decb_ref, decfd_ref, decbd_ref,
                      gain_ref, o_ref, kv_ref, st_ref, dmat_ref):
    c = RET_CHUNK
    n_chunks = q_ref.shape[0] // c
    pair_w = 2 * HEAD_DIM

    lgf = _log_sigmoid(decf_ref[...])
    lgb = _log_sigmoid(decb_ref[...])
    row = lax.broadcasted_iota(I32, (c, 1), 0).astype(F32)
    k_dec_f = jnp.exp((c - 1.0 - row) * lgf)
    k_dec_b = jnp.exp(row * lgb)
    q_dec_f = jnp.exp((row + 1.0) * lgf)
    q_dec_b = jnp.exp((c - row) * lgb)
    chunk_dec_f = jnp.exp(c * lgf)
    chunk_dec_b = jnp.exp(c * lgb)

    lane = lax.broadcasted_iota(I32, (1, pair_w), 1)
    head0 = lane < HEAD_DIM
    r2 = lax.broadcasted_iota(I32, (pair_w, pair_w), 0) // HEAD_DIM
    c2 = lax.broadcasted_iota(I32, (pair_w, pair_w), 1) // HEAD_DIM
    same_head = r2 == c2
    block_diag = jnp.where(same_head, 1.0, 0.0)
    seg_avg = jnp.where(same_head, 1.0 / HEAD_DIM, 0.0).astype(BF16)

    lgf_d = _log_sigmoid(decfd_ref[0])
    lgb_d = _log_sigmoid(decbd_ref[0])
    di = lax.broadcasted_iota(I32, (c, 2 * c), 0)
    dj = lax.broadcasted_iota(I32, (c, 2 * c), 1) % c
    diff = (di - dj).astype(F32)
    dmat_ref[...] = jnp.where(diff >= 0.0, jnp.exp(diff * lgf_d), jnp.exp(-diff * lgb_d))

    def chunk(ref, n):
        return ref[pl.ds(pl.multiple_of(n * c, c), c), :]

    unroll = RET_UNROLL

    def summarize(nb, carry):
        for u in range(unroll):
            n = nb * unroll + u
            kf = chunk(k_ref, n).astype(F32)
            kst = jnp.concatenate([kf * k_dec_f, kf * k_dec_b], axis=1).astype(BF16)
            kv_ref[n] = _dot_tn(kst, chunk(v_ref, n))
        return carry

    lax.fori_loop(0, n_chunks // unroll, summarize, 0)

    def fwd_scan(n, state):
        st_ref[n, 0:pair_w, :] = state.astype(BF16)
        return state * chunk_dec_f + kv_ref[n, 0:pair_w, :] * block_diag

    lax.fori_loop(0, n_chunks, fwd_scan, jnp.zeros((pair_w, pair_w), F32))

    def bwd_scan(i, state):
        n = n_chunks - 1 - i
        st_ref[n, pair_w:2 * pair_w, :] = state.astype(BF16)
        return state * chunk_dec_b + kv_ref[n, pair_w:2 * pair_w, :] * block_diag

    lax.fori_loop(0, n_chunks, bwd_scan, jnp.zeros((pair_w, pair_w), F32))

    gain = gain_ref[...]

    def seg_mean(z):
        hi = z.astype(BF16)
        lo = (z - hi.astype(F32)).astype(BF16)
        return _dot(hi, seg_avg) + _dot(lo, seg_avg)

    def emit(nb, carry):
        ys = []
        for u in range(unroll):
            n = nb * unroll + u
            q = chunk(q_ref, n)
            k = chunk(k_ref, n)
            v = chunk(v_ref, n)
            zero = jnp.zeros_like(k)
            k_st = jnp.concatenate([jnp.where(head0, k, zero), jnp.where(head0, zero, k)], axis=0)
            v_st = jnp.concatenate([jnp.where(head0, v, zero), jnp.where(head0, zero, v)], axis=0)
            scores = _dot_nt(q, k_st) * dmat_ref[...]
            qf = q.astype(F32)
            q_st = jnp.concatenate([qf * q_dec_f, qf * q_dec_b], axis=1).astype(BF16)
            ys.append(_dot(scores.astype(BF16), v_st) + _dot(q_st, st_ref[n]))
        y = jnp.concatenate(ys, axis=0)
        mu = seg_mean(y)
        d = y - mu
        var = seg_mean(d * d)
        yn = d * lax.rsqrt(var + GN_EPS) * gain
        rows = pl.ds(pl.multiple_of(nb * (unroll * c), unroll * c), unroll * c)
        o_ref[rows, :] = (g_ref[rows, :].astype(F32) * yn).astype(BF16)
        return carry

    lax.fori_loop(0, n_chunks // unroll, emit, 0)


def _retention(rq, rk, rv, rg, dec_f, dec_b, gain, batch, seq):
    t = rq.shape[0]
    c = RET_CHUNK
    n_pairs = HEADS // 2
    pair_w = 2 * HEAD_DIM
    dec_f_lane = jnp.repeat(dec_f.astype(F32), HEAD_DIM)[None, :]
    dec_b_lane = jnp.repeat(dec_b.astype(F32), HEAD_DIM)[None, :]
    dec_f_col = jnp.repeat(dec_f.astype(F32), c).reshape(n_pairs, 1, 2 * c)
    dec_b_col = jnp.repeat(dec_b.astype(F32), c).reshape(n_pairs, 1, 2 * c)
    tok = pl.BlockSpec((seq, pair_w), lambda b, p: (b, p))
    lane_spec = pl.BlockSpec((1, pair_w), lambda b, p: (0, p))
    col_spec = pl.BlockSpec((1, 1, 2 * c), lambda b, p: (p, 0, 0))
    return pl.pallas_call(
        _retention_kernel,
        grid=(batch, n_pairs),
        in_specs=[tok, tok, tok, tok, lane_spec, lane_spec, col_spec, col_spec, lane_spec],
        out_specs=tok,
        out_shape=jax.ShapeDtypeStruct((t, GROUP_W), BF16),
        scratch_shapes=[
            pltpu.VMEM((seq // c, 2 * pair_w, pair_w), F32),
            pltpu.VMEM((seq // c, 2 * pair_w, pair_w), BF16),
            pltpu.VMEM((c, 2 * c), F32),
        ],
        compiler_params=_cparams(("parallel", "parallel")),
        name="retention",
    )(rq, rk, rv, rg, dec_f_lane, dec_b_lane, dec_f_col, dec_b_col, gain.astype(F32)[None, :])


N_ROW_OFFSETS = 2 * NA_WIN_ROWS - 1
N_COL_OFFSETS = 2 * NA_WIN_COLS - 1


def _natten_row_offsets(rows):
    n_blocks = rows // NA_ROWS
    starts = {0: 0, 1: NA_ROWS - NA_WIN_ROWS // 2, 2: rows - NA_KEY_ROWS}
    blocks = {0: 0, 1: 1, 2: n_blocks - 1}
    table = []
    for v in range(3):
        per_a = []
        for a in range(NA_ROWS):
            r = blocks[v] * NA_ROWS + a
            rs = min(max(r - NA_WIN_ROWS // 2, 0), rows - NA_WIN_ROWS)
            per_kl = []
            for kl in range(NA_KEY_ROWS):
                kr = starts[v] + kl
                per_kl.append(kr - r + NA_WIN_ROWS - 1 if rs <= kr < rs + NA_WIN_ROWS else None)
            per_a.append(per_kl)
        table.append(per_a)
    return table


def _natten_bias_kernel(rpb_ref, tab_ref, *, rows):
    offsets = _natten_row_offsets(rows)
    nk = NA_KEY_ROWS * GRID_W
    shape = (GRID_W, LANES)
    lane = lax.broadcasted_iota(I32, shape, 1)
    c = lax.broadcasted_iota(I32, shape, 0)
    second = lane >= GRID_W
    kc = lane % GRID_W
    cs = jnp.clip(c - NA_WIN_COLS // 2, 0, GRID_W - NA_WIN_COLS)
    col_ok = jnp.logical_and(kc >= cs, kc < cs + NA_WIN_COLS)
    neg = jnp.full(shape, NEG_BIG, F32)
    for hh in range(2):
        toeplitz = []
        for dr in range(N_ROW_OFFSETS):
            x = jnp.broadcast_to(rpb_ref[hh, dr:dr + 1, :] * LOG2E, shape)
            lo = pltpu.roll(x, LANES - (NA_WIN_COLS - 1), axis=1, stride=1, stride_axis=0)
            hi = pltpu.roll(x, GRID_W - (NA_WIN_COLS - 1), axis=1, stride=1, stride_axis=0)
            toeplitz.append(jnp.where(second, hi, lo))
        for v in range(3):
            for a in range(NA_ROWS):
                for j in range(NA_KEY_ROWS // 2):
                    d0, d1 = offsets[v][a][2 * j], offsets[v][a][2 * j + 1]
                    if d0 is None and d1 is None:
                        piece = neg
                    else:
                        t0 = neg if d0 is None else toeplitz[d0]
                        t1 = neg if d1 is None else toeplitz[d1]
                        piece = jnp.where(col_ok, jnp.where(second, t1, t0), neg)
                    tab_ref[v, 0, a * GRID_W:(a + 1) * GRID_W,
                            hh * nk + j * LANES:hh * nk + (j + 1) * LANES] = piece


def _natten_bias_table(rpb, rows):
    n_pairs = HEADS // 2
    nqb = NA_ROWS * GRID_W
    nkb = NA_KEY_ROWS * GRID_W
    rpb_pad = jnp.pad(rpb.astype(F32), ((0, 0), (0, 0), (0, LANES - N_COL_OFFSETS)))
    return pl.pallas_call(
        functools.partial(_natten_bias_kernel, rows=rows),
        grid=(n_pairs,),
        in_specs=[pl.BlockSpec((2, N_ROW_OFFSETS, LANES), lambda p: (p, 0, 0))],
        out_specs=pl.BlockSpec((3, 1, nqb, 2 * nkb), lambda p: (0, p, 0, 0)),
        out_shape=jax.ShapeDtypeStruct((3, n_pairs, nqb, 2 * nkb), F32),
        compiler_params=_cparams(("parallel",)),
        name="natten_bias",
    )(rpb_pad)


def _natten_kernel(q_ref, k_ref, v_ref, *rest, rows):
    tab_refs = rest[:NA_STEP_BLOCKS]
    wgu_f, wd_f, o_ref, wgu_b, wd_b, ks_ref, vs_ref = rest[NA_STEP_BLOCKS:]
    wgu_b[...] = wgu_f[...].astype(BF16)
    wd_b[...] = wd_f[...].astype(BF16)
    nq = NA_ROWS * GRID_W
    nk = NA_KEY_ROWS * GRID_W
    lane = lax.broadcasted_iota(I32, (1, 2 * HEAD_DIM), 1)
    head0 = lane < HEAD_DIM

    @pl.when(pl.program_id(2) == 0)
    def _():
        for src_ref, dst_ref in ((k_ref, ks_ref), (v_ref, vs_ref)):
            full = src_ref[...]
            zero = jnp.zeros_like(full)
            dst_ref[0] = jnp.where(head0, full, zero)
            dst_ref[1] = jnp.where(head0, zero, full)

    for blk in range(NA_STEP_BLOCKS):
        rb = pl.program_id(2) * NA_STEP_BLOCKS + blk
        start_row = jnp.clip(rb * NA_ROWS - NA_WIN_ROWS // 2, 0, rows - NA_KEY_ROWS)
        win = pl.ds(pl.multiple_of(start_row * GRID_W, GRID_W), nk)
        q = q_ref[blk * nq:(blk + 1) * nq, :]
        k_st = jnp.concatenate([ks_ref[0, win, :], ks_ref[1, win, :]], axis=0)
        v_st = jnp.concatenate([vs_ref[0, win, :], vs_ref[1, win, :]], axis=0)
        s = _dot_nt(q, k_st) + tab_refs[blk][0, 0]
        probs, denoms = [], []
        for h in range(2):
            sh = s[:, h * nk:(h + 1) * nk]
            m = jnp.max(sh, axis=1, keepdims=True)
            p = jnp.exp2(sh - m)
            denoms.append(jnp.sum(p, axis=1, keepdims=True))
            probs.append(p.astype(BF16))
        out = _dot(jnp.concatenate(probs, axis=1), v_st)
        denom = jnp.where(head0, denoms[0], denoms[1])
        o_ref[blk * nq:(blk + 1) * nq, :] = (out / denom).astype(BF16)


def _natten(nq, nk, nv, rpb, w_gu, w_down, batch, seq):
    t = nq.shape[0]
    rows = seq // GRID_W
    n_pairs = HEADS // 2
    n_blocks = rows // NA_ROWS
    n_steps = n_blocks // NA_STEP_BLOCKS
    nqb = NA_ROWS * GRID_W
    nkb = NA_KEY_ROWS * GRID_W
    tab = _natten_bias_table(rpb, rows)
    total_steps = batch * n_pairs * n_steps
    n_exp = w_gu.shape[0]
    assert n_exp % total_steps == 0
    epb = n_exp // total_steps

    def expert_spec(shape):
        return pl.BlockSpec((epb,) + shape[1:], lambda b, p, s: ((b * n_pairs + p) * n_steps + s, 0, 0))

    def tab_spec(blk):
        def index(b, p, s):
            rb = s * NA_STEP_BLOCKS + blk
            return (jnp.where(rb == 0, 0, jnp.where(rb == n_blocks - 1, 2, 1)), p, 0, 0)
        return pl.BlockSpec((1, 1, nqb, 2 * nkb), index)

    kv_spec = pl.BlockSpec((seq, 2 * HEAD_DIM), lambda b, p, s: (b, p))
    q_spec = pl.BlockSpec((NA_STEP_BLOCKS * nqb, 2 * HEAD_DIM), lambda b, p, s: (b * n_steps + s, p))
    return pl.pallas_call(
        functools.partial(_natten_kernel, rows=rows),
        grid=(batch, n_pairs, n_steps),
        in_specs=([q_spec, kv_spec, kv_spec] + [tab_spec(blk) for blk in range(NA_STEP_BLOCKS)]
                  + [expert_spec(w_gu.shape), expert_spec(w_down.shape)]),
        out_specs=[q_spec, expert_spec(w_gu.shape), expert_spec(w_down.shape)],
        out_shape=[jax.ShapeDtypeStruct((t, GROUP_W), BF16),
                   jax.ShapeDtypeStruct(w_gu.shape, BF16), jax.ShapeDtypeStruct(w_down.shape, BF16)],
        scratch_shapes=[pltpu.VMEM((2, seq, 2 * HEAD_DIM), BF16), pltpu.VMEM((2, seq, 2 * HEAD_DIM), BF16)],
        compiler_params=_cparams(("arbitrary", "arbitrary", "arbitrary")),
        name="natten",
    )(nq, nk, nv, *([tab] * NA_STEP_BLOCKS), w_gu.astype(F32), w_down.astype(F32))


def _layer_norm(h, gain, bias):
    mu = jnp.mean(h, axis=-1, keepdims=True)
    d = h - mu
    var = jnp.mean(d * d, axis=-1, keepdims=True)
    return d * lax.rsqrt(var + LN_EPS) * gain + bias


def _out_ln1_kernel(ret_ref, na_ref, x_ref, wf_ref, gain_ref, bias_ref, x1_ref, x1b_ref, w_ref):
    _cast_once(wf_ref, w_ref)
    for r in range(0, x_ref.shape[0], PROJ_SUB):
        rows = slice(r, r + PROJ_SUB)
        mix = (_dot(ret_ref[rows, :], w_ref[0:GROUP_W, :])
               + _dot(na_ref[rows, :], w_ref[GROUP_W:2 * GROUP_W, :]))
        x1 = _layer_norm(ALPHA * x_ref[rows, :] + mix, gain_ref[...], bias_ref[...])
        x1_ref[rows, :] = x1
        x1b_ref[rows, :] = x1.astype(BF16)


def _out_ln1(ret, na, x2, w_out, gain, bias):
    t = x2.shape[0]
    tm = PROJ_TM
    tok = lambda i: (i, 0)
    const = lambda i: (0, 0)
    return pl.pallas_call(
        _out_ln1_kernel,
        grid=(t // tm,),
        in_specs=[pl.BlockSpec((tm, GROUP_W), tok), pl.BlockSpec((tm, GROUP_W), tok),
                  pl.BlockSpec((tm, D_MODEL), tok), _resident(w_out.shape),
                  pl.BlockSpec((1, D_MODEL), const), pl.BlockSpec((1, D_MODEL), const)],
        out_specs=[pl.BlockSpec((tm, D_MODEL), tok), pl.BlockSpec((tm, D_MODEL), tok)],
        out_shape=[jax.ShapeDtypeStruct((t, D_MODEL), F32), jax.ShapeDtypeStruct((t, D_MODEL), BF16)],
        scratch_shapes=[pltpu.VMEM(w_out.shape, BF16)],
        compiler_params=_cparams(("arbitrary",)),
        name="out_ln1",
    )(ret, na, x2, w_out.astype(F32), gain.astype(F32)[None, :], bias.astype(F32)[None, :])


def _route_kernel(x_ref, wr_ref, rb_ref, w_ref, rank_ref, cnt_ref, tri_ref):
    tm = x_ref.shape[0]

    @pl.when(pl.program_id(0) == 0)
    def _():
        i = lax.broadcasted_iota(I32, (MOE_SUB, MOE_SUB), 0)
        j = lax.broadcasted_iota(I32, (MOE_SUB, MOE_SUB), 1)
        tri_ref[...] = jnp.where(i < j, 1.0, 0.0).astype(BF16)

    scores = jax.nn.sigmoid(_dot_nt(wr_ref[...], x_ref[...]))
    biased = scores + rb_ref[...]
    sub = lax.broadcasted_iota(I32, (GROUP_SIZE, tm), 0).astype(F32)
    none = float(N_EXPERTS)
    ninf = -jnp.inf

    def first_max(vals, index):
        m = jnp.max(vals, axis=0, keepdims=True)
        return m, jnp.min(jnp.where(vals == m, index, none), axis=0, keepdims=True)

    groups = [biased[g * GROUP_SIZE:(g + 1) * GROUP_SIZE, :] for g in range(N_GROUPS)]
    group_scores = []
    for g in range(N_GROUPS):
        m1, i1 = first_max(groups[g], sub)
        m2 = jnp.max(jnp.where(sub == i1, ninf, groups[g]), axis=0, keepdims=True)
        group_scores.append(m1 + m2)
    cur = jnp.concatenate(group_scores, axis=0)
    group_sel = jnp.zeros(cur.shape, F32)
    for _ in range(TOPK_GROUPS):
        _, i1 = first_max(cur, sub)
        hit = sub == i1
        group_sel = jnp.where(hit, 1.0, group_sel)
        cur = jnp.where(hit, ninf, cur)

    masked = [jnp.where(group_sel[g:g + 1, :] > 0.5, groups[g], ninf) for g in range(N_GROUPS)]
    ids = [sub + float(g * GROUP_SIZE) for g in range(N_GROUPS)]
    chosen = [jnp.zeros((GROUP_SIZE, tm), F32) for _ in range(N_GROUPS)]
    for _ in range(TOP_K):
        m = masked[0]
        for g in range(1, N_GROUPS):
            m = jnp.maximum(m, masked[g])
        m = jnp.max(m, axis=0, keepdims=True)
        cand = jnp.where(masked[0] == m, ids[0], none)
        for g in range(1, N_GROUPS):
            cand = jnp.minimum(cand, jnp.where(masked[g] == m, ids[g], none))
        first = jnp.min(cand, axis=0, keepdims=True)
        for g in range(N_GROUPS):
            hit = ids[g] == first
            chosen[g] = jnp.where(hit, 1.0, chosen[g])
            masked[g] = jnp.where(hit, ninf, masked[g])

    sel = jnp.concatenate(chosen, axis=0) > 0.5
    picked = jnp.where(sel, scores, 0.0)
    total = jnp.sum(picked, axis=0, keepdims=True)
    w_ref[...] = picked / total * ROUTED_SCALE
    sel_f = jnp.where(sel, 1.0, 0.0)
    sel_b = sel_f.astype(BF16)
    before, cnt_max = [], None
    for k in range(MOE_NSUB):
        cols = slice(k * MOE_SUB, (k + 1) * MOE_SUB)
        before.append(_dot(sel_b[:, cols], tri_ref[...]))
        cnt = jnp.sum(sel_f[:, cols], axis=1, keepdims=True)
        cnt_max = cnt if cnt_max is None else jnp.maximum(cnt_max, cnt)
    rank_ref[...] = jnp.where(sel, jnp.concatenate(before, axis=1).astype(I32), -1)
    cnt_ref[...] = jnp.broadcast_to(cnt_max, cnt_ref.shape).astype(I32)


def _route(x1b, w_router, router_bias):
    t = x1b.shape[0]
    tm = MOE_TM
    nt = t // tm
    wr_t = w_router.astype(F32).T.astype(BF16)
    const = lambda i: (0, 0)
    col = lambda i: (0, i)
    return pl.pallas_call(
        _route_kernel,
        grid=(nt,),
        in_specs=[pl.BlockSpec((tm, D_MODEL), lambda i: (i, 0)),
                  pl.BlockSpec((N_EXPERTS, D_MODEL), const),
                  pl.BlockSpec((N_EXPERTS, 1), const)],
        out_specs=[pl.BlockSpec((N_EXPERTS, tm), col), pl.BlockSpec((N_EXPERTS, tm), col),
                   pl.BlockSpec((N_EXPERTS, LANES), col)],
        out_shape=[jax.ShapeDtypeStruct((N_EXPERTS, t), F32),
                   jax.ShapeDtypeStruct((N_EXPERTS, t), I32),
                   jax.ShapeDtypeStruct((N_EXPERTS, nt * LANES), I32)],
        scratch_shapes=[pltpu.VMEM((MOE_SUB, MOE_SUB), BF16)],
        compiler_params=_cparams(("arbitrary",)),
        name="route",
    )(x1b, wr_t, router_bias.astype(F32)[:, None])


def _moe_kernel(cnt_ref, x_ref, rank_ref, w_ref, wgu_ref, wd_ref, o_ref, oh_ref, g_ref, y_ref):
    i = pl.program_id(0)
    eb = pl.program_id(1)

    @pl.when(eb == 0)
    def _():
        o_ref[...] = jnp.zeros_like(o_ref)

    max_count = cnt_ref[eb, i]

    def one_pass(p, slots):
        n_rows = MOE_EB * slots
        base = p * slots
        slot = lax.broadcasted_iota(I32, (slots, MOE_SUB), 0)
        slot_w = []
        for k in range(MOE_NSUB):
            tok = slice(k * MOE_SUB, (k + 1) * MOE_SUB)
            blocks, weights = [], []
            for j in range(MOE_EB):
                match = slot == (rank_ref[j, :, tok] - base)
                blocks.append(jnp.where(match, 1.0, 0.0).astype(BF16))
                weights.append(jnp.sum(jnp.where(match, w_ref[j, :, tok], 0.0), axis=1, keepdims=True))
            onehot = jnp.concatenate(blocks, axis=0)
            oh_ref[k, 0:n_rows, :] = onehot
            slot_w.append(weights)
            g_ref[k, 0:n_rows, :] = _dot(onehot, x_ref[tok, :]).astype(BF16)
        for j in range(MOE_EB):
            rows = slice(j * slots, (j + 1) * slots)
            xe = jnp.concatenate([g_ref[k, rows, :] for k in range(MOE_NSUB)], axis=0)
            h = _dot(xe, wgu_ref[j])
            act = (jax.nn.silu(h[:, :EXPERT_DIM]) * h[:, EXPERT_DIM:]).astype(BF16)
            wc = jnp.concatenate([slot_w[k][j] for k in range(MOE_NSUB)], axis=0)
            yw = (_dot(act, wd_ref[j]) * wc).astype(BF16)
            for k in range(MOE_NSUB):
                y_ref[k, rows, :] = yw[k * slots:(k + 1) * slots, :]
        for k in range(MOE_NSUB):
            tok = slice(k * MOE_SUB, (k + 1) * MOE_SUB)
            o_ref[tok, :] += _dot_tn(oh_ref[k, 0:n_rows, :], y_ref[k, 0:n_rows, :])

    @pl.when(max_count <= MOE_SLOTS_SMALL)
    def _():
        one_pass(0, MOE_SLOTS_SMALL)

    @pl.when(max_count > MOE_SLOTS_SMALL)
    def _():
        n_pass = lax.shift_right_logical(max_count + (MOE_SLOTS - 1), int(np.log2(MOE_SLOTS)))

        def body(p, carry):
            one_pass(p, MOE_SLOTS)
            return carry

        lax.fori_loop(0, n_pass, body, 0)


def _moe(x1b, rank_t, w_t, counts, wgu_b, wd_b):
    t = x1b.shape[0]
    tm = MOE_TM
    nt = t // tm
    rank3 = rank_t.reshape(N_EXPERTS, 1, t)
    w3 = w_t.reshape(N_EXPERTS, 1, t)
    counts = counts.reshape(N_EXPERTS // MOE_EB, MOE_EB, nt).max(axis=1)
    grid_spec = pltpu.PrefetchScalarGridSpec(
        num_scalar_prefetch=1,
        grid=(nt, N_EXPERTS // MOE_EB),
        in_specs=[
            pl.BlockSpec((tm, D_MODEL), lambda i, e, c: (i, 0)),
            pl.BlockSpec((MOE_EB, 1, tm), lambda i, e, c: (e, 0, i)),
            pl.BlockSpec((MOE_EB, 1, tm), lambda i, e, c: (e, 0, i)),
            pl.BlockSpec((MOE_EB, D_MODEL, 2 * EXPERT_DIM), lambda i, e, c: (e, 0, 0)),
            pl.BlockSpec((MOE_EB, EXPERT_DIM, D_MODEL), lambda i, e, c: (e, 0, 0)),
        ],
        out_specs=pl.BlockSpec((tm, D_MODEL), lambda i, e, c: (i, 0)),
        scratch_shapes=[pltpu.VMEM((MOE_NSUB, MOE_SUB, MOE_SUB), BF16),
                        pltpu.VMEM((MOE_NSUB, MOE_SUB, D_MODEL), BF16),
                        pltpu.VMEM((MOE_NSUB, MOE_SUB, D_MODEL), BF16)],
    )
    return pl.pallas_call(
        _moe_kernel,
        grid_spec=grid_spec,
        out_shape=jax.ShapeDtypeStruct((t, D_MODEL), F32),
        compiler_params=_cparams(("parallel", "arbitrary")),
        name="moe",
    )(counts, x1b, rank3, w3, wgu_b, wd_b)


def _final_kernel(x1_ref, ffn_ref, p_ref, wsgu_f, wsd_f, wp_f, wg_f, gain_ref, bias_ref, o_ref,
                  wsgu_ref, wsd_ref, wp_ref, wg_ref):
    for src_ref, dst_ref in ((wsgu_f, wsgu_ref), (wsd_f, wsd_ref), (wp_f, wp_ref), (wg_f, wg_ref)):
        _cast_once(src_ref, dst_ref)
    for r in range(0, x1_ref.shape[0], PROJ_SUB):
        rows = slice(r, r + PROJ_SUB)
        x1 = x1_ref[rows, :]
        xb = x1.astype(BF16)
        h = _dot(xb, wsgu_ref[...])
        act = (jax.nn.silu(h[:, :EXPERT_DIM]) * h[:, EXPERT_DIM:]).astype(BF16)
        shared = _dot(act, wsd_ref[...])
        ple = _dot(p_ref[rows, :].astype(BF16), wp_ref[...]) * jax.nn.sigmoid(_dot(xb, wg_ref[...]))
        o_ref[rows, :] = _layer_norm(ALPHA * x1 + (ffn_ref[rows, :] + shared) + ple,
                                     gain_ref[...], bias_ref[...])


def _final(x1, ffn, p2, wsgu, wsd, wp, wg, gain, bias):
    t = x1.shape[0]
    tm = PROJ_TM
    tok = lambda i: (i, 0)
    const = lambda i: (0, 0)
    weights = [w.astype(F32) for w in (wsgu, wsd, wp, wg)]
    return pl.pallas_call(
        _final_kernel,
        grid=(t // tm,),
        in_specs=[pl.BlockSpec((tm, D_MODEL), tok), pl.BlockSpec((tm, D_MODEL), tok),
                  pl.BlockSpec((tm, p2.shape[1]), tok)]
                 + [_resident(w.shape) for w in weights]
                 + [pl.BlockSpec((1, D_MODEL), const), pl.BlockSpec((1, D_MODEL), const)],
        out_specs=pl.BlockSpec((tm, D_MODEL), tok),
        out_shape=jax.ShapeDtypeStruct((t, D_MODEL), F32),
        scratch_shapes=[pltpu.VMEM(w.shape, BF16) for w in weights],
        compiler_params=_cparams(("arbitrary",)),
        name="final",
    )(x1, ffn, p2, *weights, gain.astype(F32)[None, :], bias.astype(F32)[None, :])


def _rotary_tables(seq):
    half = HEAD_DIM // 2
    inv = ROPE_BASE ** (-jnp.arange(half, dtype=F32) / half)
    ang = jnp.arange(seq, dtype=jnp.int32).astype(F32)[:, None] * inv[None, :]
    cos, sin = jnp.cos(ang), jnp.sin(ang)
    reps = LANES // HEAD_DIM
    cos_t = jnp.tile(jnp.concatenate([cos, cos], axis=1), (1, reps))
    sin_t = jnp.tile(jnp.concatenate([-sin, sin], axis=1), (1, reps))
    return cos_t, sin_t


def kernel(x, p, w_in, ret_decay_fwd, ret_decay_bwd, ret_gn_gain, na_rpb, w_out, ln1_gain, ln1_bias,
           w_router, router_bias, w_expert_gu, w_expert_down, w_shared_gu, w_shared_down,
           w_ple_proj, w_ple_gate, ln2_gain, ln2_bias):
    batch, seq, d = x.shape
    t = batch * seq
    depth = w_in.shape[0]
    assert depth == 1 and d == D_MODEL
    assert seq % PROJ_TM == 0 and seq % MOE_TM == 0 and seq % RET_CHUNK == 0
    assert (seq // GRID_W) % (NA_ROWS * NA_STEP_BLOCKS) == 0 and seq // GRID_W >= NA_KEY_ROWS
    cos_t, sin_t = _rotary_tables(seq)
    x2 = x.reshape(t, d)
    for i in range(depth):
        rq, rk, rv, rg, nq, nk, nv = _in_proj(x2, w_in[i], cos_t, sin_t, seq)
        ret = _retention(rq, rk, rv, rg, ret_decay_fwd[i], ret_decay_bwd[i], ret_gn_gain[i], batch, seq)
        na, wgu_b, wd_b = _natten(nq, nk, nv, na_rpb[i], w_expert_gu[i], w_expert_down[i], batch, seq)
        x1, x1b = _out_ln1(ret, na, x2, w_out[i], ln1_gain[i], ln1_bias[i])
        w_t, rank_t, cnt = _route(x1b, w_router[i], router_bias[i])
        counts = cnt[:, ::LANES]
        ffn = _moe(x1b, rank_t, w_t, counts, wgu_b, wd_b)
        x2 = _final(x1, ffn, p[i].reshape(t, -1), w_shared_gu[i], w_shared_down[i],
                    w_ple_proj[i], w_ple_gate[i], ln2_gain[i], ln2_bias[i])
    return x2.reshape(batch, seq, d)
```

```python
import functools

import numpy as np
import jax
import jax.numpy as jnp
from jax import lax
from jax.experimental import pallas as pl
from jax.experimental.pallas import tpu as pltpu

F32 = jnp.float32
BF16 = jnp.bfloat16
I32 = jnp.int32

D_MODEL = 1024
HEADS = 8
HEAD_DIM = 64
GROUP_W = HEADS * HEAD_DIM
ROPE_BASE = 10000.0
GN_EPS = 1e-6
LN_EPS = 1e-5
GRID_W = 64
NA_WIN_ROWS = 8
NA_WIN_COLS = 16
N_EXPERTS = 64
N_GROUPS = 8
GROUP_SIZE = N_EXPERTS // N_GROUPS
TOPK_GROUPS = 4
TOP_K = 8
EXPERT_DIM = 256
ROUTED_SCALE = 2.5
ALPHA = 2.0 ** 0.25
NEG_BIG = -1e30
LOG2E = 1.4426950408889634
NA_Q_SCALE = HEAD_DIM ** -0.5 * LOG2E

LANES = 128
VMEM_LIMIT_BYTES = 56 * 1024 * 1024

PROJ_TM = 512
PROJ_SUB = 256
RET_CHUNK = 128
RET_UNROLL = 16
NA_ROWS = 4
NA_KEY_ROWS = NA_ROWS + NA_WIN_ROWS
NA_STEP_BLOCKS = 4
MOE_TM = 2048
MOE_SUB = 256
MOE_EB = 4
MOE_SLOTS = MOE_SUB // MOE_EB
MOE_SLOTS_SMALL = 48
MOE_NSUB = MOE_TM // MOE_SUB


def _cparams(sem):
    return pltpu.CompilerParams(dimension_semantics=sem, vmem_limit_bytes=VMEM_LIMIT_BYTES)


def _dot(a, b):
    return jnp.dot(a, b, preferred_element_type=F32)


def _dot_nt(a, b):
    return lax.dot_general(a, b, (((1,), (1,)), ((), ())), preferred_element_type=F32)


def _dot_tn(a, b):
    return lax.dot_general(a, b, (((0,), (0,)), ((), ())), preferred_element_type=F32)


def _cast_once(src_ref, dst_ref):
    @pl.when(pl.program_id(0) == 0)
    def _():
        dst_ref[...] = src_ref[...].astype(BF16)


def _resident(shape):
    return pl.BlockSpec(shape, lambda i: (0,) * len(shape), pipeline_mode=pl.Buffered(1))


def _in_proj_kernel(x_ref, wf_ref, cos_ref, sin_ref,
                    rq_ref, rk_ref, rv_ref, rg_ref, nq_ref, nk_ref, nv_ref, w_ref):
    _cast_once(wf_ref, w_ref)
    xb = x_ref[...].astype(BF16)
    cos = cos_ref[...]
    sin = sin_ref[...]
    lane = lax.broadcasted_iota(I32, (1, LANES), 1)
    first_half = (lane % HEAD_DIM) < (HEAD_DIM // 2)

    def proj(g):
        return _dot(xb, w_ref[:, g * GROUP_W:(g + 1) * GROUP_W])

    def rotary(t, scale):
        outs = []
        for j in range(GROUP_W // LANES):
            c = t[:, j * LANES:(j + 1) * LANES]
            swapped = jnp.where(first_half,
                                pltpu.roll(c, LANES - HEAD_DIM // 2, axis=1),
                                pltpu.roll(c, HEAD_DIM // 2, axis=1))
            outs.append((c * cos + swapped * sin) * scale)
        return jnp.concatenate(outs, axis=1)

    rq_ref[...] = rotary(proj(0), 1.0).astype(BF16)
    rk_ref[...] = rotary(proj(1), HEAD_DIM ** -0.5).astype(BF16)
    rv_ref[...] = proj(2).astype(BF16)
    rg_ref[...] = jax.nn.silu(proj(3)).astype(BF16)
    nq_ref[...] = (proj(4) * NA_Q_SCALE).astype(BF16)
    nk_ref[...] = proj(5).astype(BF16)
    nv_ref[...] = proj(6).astype(BF16)


def _in_proj(x2, w_in, cos_t, sin_t, seq):
    t = x2.shape[0]
    tm = PROJ_TM
    n_pos = seq // tm
    out = jax.ShapeDtypeStruct((t, GROUP_W), BF16)
    tok = lambda i: (i, 0)
    return pl.pallas_call(
        _in_proj_kernel,
        grid=(t // tm,),
        in_specs=[
            pl.BlockSpec((tm, D_MODEL), tok),
            _resident(w_in.shape),
            pl.BlockSpec((tm, LANES), lambda i: (i % n_pos, 0)),
            pl.BlockSpec((tm, LANES), lambda i: (i % n_pos, 0)),
        ],
        out_specs=[pl.BlockSpec((tm, GROUP_W), tok)] * 7,
        out_shape=[out] * 7,
        scratch_shapes=[pltpu.VMEM(w_in.shape, BF16)],
        compiler_params=_cparams(("arbitrary",)),
        name="in_proj",
    )(x2, w_in.astype(F32), cos_t, sin_t)


def _log_sigmoid(x):
    return jnp.minimum(x, 0.0) - jnp.log1p(jnp.exp(-jnp.abs(x)))


def _retention_kernel(q_ref, k_ref, v_ref, g_ref, decf_ref, decb_ref, decfd_ref, decbd_ref,
                      gain_ref, o_ref, kv_ref, st_ref, dmat_ref):
    c = RET_CHUNK
    n_chunks = q_ref.shape[0] // c
    pair_w = 2 * HEAD_DIM

    lgf = _log_sigmoid(decf_ref[...])
    lgb = _log_sigmoid(decb_ref[...])
    row = lax.broadcasted_iota(I32, (c, 1), 0).astype(F32)
    k_dec_f = jnp.exp((c - 1.0 - row) * lgf)
    k_dec_b = jnp.exp(row * lgb)
    q_dec_f = jnp.exp((row + 1.0) * lgf)
    q_dec_b = jnp.exp((c - row) * lgb)
    chunk_dec_f = jnp.exp(c * lgf)
    chunk_dec_b = jnp.exp(c * lgb)

    lane = lax.broadcasted_iota(I32, (1, pair_w), 1)
    head0 = lane < HEAD_DIM
    r2 = lax.broadcasted_iota(I32, (pair_w, pair_w), 0) // HEAD_DIM
    c2 = lax.broadcasted_iota(I32, (pair_w, pair_w), 1) // HEAD_DIM
    same_head = r2 == c2
    block_diag = jnp.where(same_head, 1.0, 0.0)
    seg_avg = jnp.where(same_head, 1.0 / HEAD_DIM, 0.0).astype(BF16)

    lgf_d = _log_sigmoid(decfd_ref[0])
    lgb_d = _log_sigmoid(decbd_ref[0])
    di = lax.broadcasted_iota(I32, (c, 2 * c), 0)
    dj = lax.broadcasted_iota(I32, (c, 2 * c), 1) % c
    diff = (di - dj).astype(F32)
    dmat_ref[...] = jnp.where(diff >= 0.0, jnp.exp(diff * lgf_d), jnp.exp(-diff * lgb_d))

    def chunk(ref, n):
        return ref[pl.ds(pl.multiple_of(n * c, c), c), :]

    unroll = RET_UNROLL

    def summarize(nb, carry):
        for u in range(unroll):
            n = nb * unroll + u
            kf = chunk(k_ref, n).astype(F32)
            kst = jnp.concatenate([kf * k_dec_f, kf * k_dec_b], axis=1).astype(BF16)
            kv_ref[n] = _dot_tn(kst, chunk(v_ref, n))
        return carry

    lax.fori_loop(0, n_chunks // unroll, summarize, 0)

    def fwd_scan(n, state):
        st_ref[n, 0:pair_w, :] = state.astype(BF16)
        return state * chunk_dec_f + kv_ref[n, 0:pair_w, :] * block_diag

    lax.fori_loop(0, n_chunks, fwd_scan, jnp.zeros((pair_w, pair_w), F32))

    def bwd_scan(i, state):
        n = n_chunks - 1 - i
        st_ref[n, pair_w:2 * pair_w, :] = state.astype(BF16)
        return state * chunk_dec_b + kv_ref[n, pair_w:2 * pair_w, :] * block_diag

    lax.fori_loop(0, n_chunks, bwd_scan, jnp.zeros((pair_w, pair_w), F32))

    gain = gain_ref[...]

    seg_avg2 = jnp.concatenate([seg_avg, seg_avg], axis=0)

    def seg_mean(z):
        hi = z.astype(BF16)
        lo = (z - hi.astype(F32)).astype(BF16)
        return _dot(jnp.concatenate([hi, lo], axis=1), seg_avg2)

    def emit(nb, carry):
        ys = []
        for u in range(unroll):
            n = nb * unroll + u
            q = chunk(q_ref, n)
            k = chunk(k_ref, n)
            v = chunk(v_ref, n)
            zero = jnp.zeros_like(k)
            k_st = jnp.concatenate([jnp.where(head0, k, zero), jnp.where(head0, zero, k)], axis=0)
            v_st = jnp.concatenate([jnp.where(head0, v, zero), jnp.where(head0, zero, v)], axis=0)
            scores = _dot_nt(q, k_st) * dmat_ref[...]
            qf = q.astype(F32)
            q_st = jnp.concatenate([qf * q_dec_f, qf * q_dec_b], axis=1).astype(BF16)
            ys.append(_dot(scores.astype(BF16), v_st) + _dot(q_st, st_ref[n]))
        y = jnp.concatenate(ys, axis=0)
        mu = seg_mean(y)
        d = y - mu
        var = seg_mean(d * d)
        yn = d * lax.rsqrt(var + GN_EPS) * gain
        rows = pl.ds(pl.multiple_of(nb * (unroll * c), unroll * c), unroll * c)
        o_ref[rows, :] = (g_ref[rows, :].astype(F32) * yn).astype(BF16)
        return carry

    lax.fori_loop(0, n_chunks // unroll, emit, 0)


def _retention(rq, rk, rv, rg, dec_f, dec_b, gain, batch, seq):
    t = rq.shape[0]
    c = RET_CHUNK
    n_pairs = HEADS // 2
    pair_w = 2 * HEAD_DIM
    dec_f_lane = jnp.repeat(dec_f.astype(F32), HEAD_DIM)[None, :]
    dec_b_lane = jnp.repeat(dec_b.astype(F32), HEAD_DIM)[None, :]
    dec_f_col = jnp.repeat(dec_f.astype(F32), c).reshape(n_pairs, 1, 2 * c)
    dec_b_col = jnp.repeat(dec_b.astype(F32), c).reshape(n_pairs, 1, 2 * c)
    tok = pl.BlockSpec((seq, pair_w), lambda b, p: (b, p))
    lane_spec = pl.BlockSpec((1, pair_w), lambda b, p: (0, p))
    col_spec = pl.BlockSpec((1, 1, 2 * c), lambda b, p: (p, 0, 0))
    return pl.pallas_call(
        _retention_kernel,
        grid=(batch, n_pairs),
        in_specs=[tok, tok, tok, tok, lane_spec, lane_spec, col_spec, col_spec, lane_spec],
        out_specs=tok,
        out_shape=jax.ShapeDtypeStruct((t, GROUP_W), BF16),
        scratch_shapes=[
            pltpu.VMEM((seq // c, 2 * pair_w, pair_w), F32),
            pltpu.VMEM((seq // c, 2 * pair_w, pair_w), BF16),
            pltpu.VMEM((c, 2 * c), F32),
        ],
        compiler_params=_cparams(("parallel", "parallel")),
        name="retention",
    )(rq, rk, rv, rg, dec_f_lane, dec_b_lane, dec_f_col, dec_b_col, gain.astype(F32)[None, :])


N_ROW_OFFSETS = 2 * NA_WIN_ROWS - 1
N_COL_OFFSETS = 2 * NA_WIN_COLS - 1


def _natten_row_offsets(rows):
    n_blocks = rows // NA_ROWS
    starts = {0: 0, 1: NA_ROWS - NA_WIN_ROWS // 2, 2: rows - NA_KEY_ROWS}
    blocks = {0: 0, 1: 1, 2: n_blocks - 1}
    table = []
    for v in range(3):
        per_a = []
        for a in range(NA_ROWS):
            r = blocks[v] * NA_ROWS + a
            rs = min(max(r - NA_WIN_ROWS // 2, 0), rows - NA_WIN_ROWS)
            per_kl = []
            for kl in range(NA_KEY_ROWS):
                kr = starts[v] + kl
                per_kl.append(kr - r + NA_WIN_ROWS - 1 if rs <= kr < rs + NA_WIN_ROWS else None)
            per_a.append(per_kl)
        table.append(per_a)
    return table


def _natten_bias_kernel(rpb_ref, tab_ref, *, rows):
    offsets = _natten_row_offsets(rows)
    nk = NA_KEY_ROWS * GRID_W
    shape = (GRID_W, LANES)
    lane = lax.broadcasted_iota(I32, shape, 1)
    c = lax.broadcasted_iota(I32, shape, 0)
    second = lane >= GRID_W
    kc = lane % GRID_W
    cs = jnp.clip(c - NA_WIN_COLS // 2, 0, GRID_W - NA_WIN_COLS)
    col_ok = jnp.logical_and(kc >= cs, kc < cs + NA_WIN_COLS)
    neg = jnp.full(shape, NEG_BIG, F32)
    for hh in range(2):
        toeplitz = []
        for dr in range(N_ROW_OFFSETS):
            x = jnp.broadcast_to(rpb_ref[hh, dr:dr + 1, :] * LOG2E, shape)
            lo = pltpu.roll(x, LANES - (NA_WIN_COLS - 1), axis=1, stride=1, stride_axis=0)
            hi = pltpu.roll(x, GRID_W - (NA_WIN_COLS - 1), axis=1, stride=1, stride_axis=0)
            toeplitz.append(jnp.where(second, hi, lo))
        for v in range(3):
            for a in range(NA_ROWS):
                for j in range(NA_KEY_ROWS // 2):
                    d0, d1 = offsets[v][a][2 * j], offsets[v][a][2 * j + 1]
                    if d0 is None and d1 is None:
                        piece = neg
                    else:
                        t0 = neg if d0 is None else toeplitz[d0]
                        t1 = neg if d1 is None else toeplitz[d1]
                        piece = jnp.where(col_ok, jnp.where(second, t1, t0), neg)
                    tab_ref[v, 0, a * GRID_W:(a + 1) * GRID_W,
                            hh * nk + j * LANES:hh * nk + (j + 1) * LANES] = piece


def _natten_bias_table(rpb, rows):
    n_pairs = HEADS // 2
    nqb = NA_ROWS * GRID_W
    nkb = NA_KEY_ROWS * GRID_W
    rpb_pad = jnp.pad(rpb.astype(F32), ((0, 0), (0, 0), (0, LANES - N_COL_OFFSETS)))
    return pl.pallas_call(
        functools.partial(_natten_bias_kernel, rows=rows),
        grid=(n_pairs,),
        in_specs=[pl.BlockSpec((2, N_ROW_OFFSETS, LANES), lambda p: (p, 0, 0))],
        out_specs=pl.BlockSpec((3, 1, nqb, 2 * nkb), lambda p: (0, p, 0, 0)),
        out_shape=jax.ShapeDtypeStruct((3, n_pairs, nqb, 2 * nkb), F32),
        compiler_params=_cparams(("parallel",)),
        name="natten_bias",
    )(rpb_pad)


def _natten_kernel(q_ref, k_ref, v_ref, tab_ref, wgu_f, wd_f, o_ref, wgu_b, wd_b, ks_ref, vs_ref, *, rows):
    wgu_b[...] = wgu_f[...].astype(BF16)
    wd_b[...] = wd_f[...].astype(BF16)
    nq = NA_ROWS * GRID_W
    nk = NA_KEY_ROWS * GRID_W
    n_blocks = rows // NA_ROWS
    lane = lax.broadcasted_iota(I32, (1, 2 * HEAD_DIM), 1)
    head0 = lane < HEAD_DIM

    @pl.when(pl.program_id(2) == 0)
    def _():
        for src_ref, dst_ref in ((k_ref, ks_ref), (v_ref, vs_ref)):
            full = src_ref[...]
            zero = jnp.zeros_like(full)
            dst_ref[0] = jnp.where(head0, full, zero)
            dst_ref[1] = jnp.where(head0, zero, full)

    for blk in range(NA_STEP_BLOCKS):
        rb = pl.program_id(2) * NA_STEP_BLOCKS + blk
        variant = jnp.where(rb == 0, 0, jnp.where(rb == n_blocks - 1, 2, 1))
        start_row = jnp.clip(rb * NA_ROWS - NA_WIN_ROWS // 2, 0, rows - NA_KEY_ROWS)
        win = pl.ds(pl.multiple_of(start_row * GRID_W, GRID_W), nk)
        q = q_ref[blk * nq:(blk + 1) * nq, :]
        k_st = jnp.concatenate([ks_ref[0, win, :], ks_ref[1, win, :]], axis=0)
        v_st = jnp.concatenate([vs_ref[0, win, :], vs_ref[1, win, :]], axis=0)
        s = _dot_nt(q, k_st) + tab_ref[variant, 0]
        probs, denoms = [], []
        for h in range(2):
            sh = s[:, h * nk:(h + 1) * nk]
            m = jnp.max(sh, axis=1, keepdims=True)
            p = jnp.exp2(sh - m)
            denoms.append(jnp.sum(p, axis=1, keepdims=True))
            probs.append(p.astype(BF16))
        out = _dot(jnp.concatenate(probs, axis=1), v_st)
        denom = jnp.where(head0, denoms[0], denoms[1])
        o_ref[blk * nq:(blk + 1) * nq, :] = (out / denom).astype(BF16)


def _natten(nq, nk, nv, rpb, w_gu, w_down, batch, seq):
    t = nq.shape[0]
    rows = seq // GRID_W
    n_pairs = HEADS // 2
    n_steps = rows // (NA_ROWS * NA_STEP_BLOCKS)
    nqb = NA_ROWS * GRID_W
    nkb = NA_KEY_ROWS * GRID_W
    tab = _natten_bias_table(rpb, rows)
    total_steps = batch * n_pairs * n_steps
    n_exp = w_gu.shape[0]
    assert n_exp % total_steps == 0
    epb = n_exp // total_steps

    def expert_spec(shape):
        return pl.BlockSpec((epb,) + shape[1:], lambda b, p, s: ((b * n_pairs + p) * n_steps + s, 0, 0))

    tab_spec = pl.BlockSpec((3, 1, nqb, 2 * nkb), lambda b, p, s: (0, p, 0, 0))
    kv_spec = pl.BlockSpec((seq, 2 * HEAD_DIM), lambda b, p, s: (b, p))
    q_spec = pl.BlockSpec((NA_STEP_BLOCKS * nqb, 2 * HEAD_DIM), lambda b, p, s: (b * n_steps + s, p))
    return pl.pallas_call(
        functools.partial(_natten_kernel, rows=rows),
        grid=(batch, n_pairs, n_steps),
        in_specs=[q_spec, kv_spec, kv_spec, tab_spec, expert_spec(w_gu.shape), expert_spec(w_down.shape)],
        out_specs=[q_spec, expert_spec(w_gu.shape), expert_spec(w_down.shape)],
        out_shape=[jax.ShapeDtypeStruct((t, GROUP_W), BF16),
                   jax.ShapeDtypeStruct(w_gu.shape, BF16), jax.ShapeDtypeStruct(w_down.shape, BF16)],
        scratch_shapes=[pltpu.VMEM((2, seq, 2 * HEAD_DIM), BF16), pltpu.VMEM((2, seq, 2 * HEAD_DIM), BF16)],
        compiler_params=_cparams(("arbitrary", "arbitrary", "arbitrary")),
        name="natten",
    )(nq, nk, nv, tab, w_gu.astype(F32), w_down.astype(F32))


def _layer_norm(h, gain, bias):
    mu = jnp.mean(h, axis=-1, keepdims=True)
    d = h - mu
    var = jnp.mean(d * d, axis=-1, keepdims=True)
    return d * lax.rsqrt(var + LN_EPS) * gain + bias


def _out_ln1_kernel(ret_ref, na_ref, x_ref, wf_ref, gain_ref, bias_ref, x1_ref, x1b_ref, w_ref):
    _cast_once(wf_ref, w_ref)
    for r in range(0, x_ref.shape[0], PROJ_SUB):
        rows = slice(r, r + PROJ_SUB)
        mix = (_dot(ret_ref[rows, :], w_ref[0:GROUP_W, :])
               + _dot(na_ref[rows, :], w_ref[GROUP_W:2 * GROUP_W, :]))
        x1 = _layer_norm(ALPHA * x_ref[rows, :] + mix, gain_ref[...], bias_ref[...])
        x1_ref[rows, :] = x1
        x1b_ref[rows, :] = x1.astype(BF16)


def _out_ln1(ret, na, x2, w_out, gain, bias):
    t = x2.shape[0]
    tm = PROJ_TM
    tok = lambda i: (i, 0)
    const = lambda i: (0, 0)
    return pl.pallas_call(
        _out_ln1_kernel,
        grid=(t // tm,),
        in_specs=[pl.BlockSpec((tm, GROUP_W), tok), pl.BlockSpec((tm, GROUP_W), tok),
                  pl.BlockSpec((tm, D_MODEL), tok), _resident(w_out.shape),
                  pl.BlockSpec((1, D_MODEL), const), pl.BlockSpec((1, D_MODEL), const)],
        out_specs=[pl.BlockSpec((tm, D_MODEL), tok), pl.BlockSpec((tm, D_MODEL), tok)],
        out_shape=[jax.ShapeDtypeStruct((t, D_MODEL), F32), jax.ShapeDtypeStruct((t, D_MODEL), BF16)],
        scratch_shapes=[pltpu.VMEM(w_out.shape, BF16)],
        compiler_params=_cparams(("arbitrary",)),
        name="out_ln1",
    )(ret, na, x2, w_out.astype(F32), gain.astype(F32)[None, :], bias.astype(F32)[None, :])


def _route_kernel(x_ref, wr_ref, rb_ref, w_ref, rank_ref, cnt_ref, tri_ref):
    tm = x_ref.shape[0]

    @pl.when(pl.program_id(0) == 0)
    def _():
        i = lax.broadcasted_iota(I32, (MOE_SUB, MOE_SUB), 0)
        j = lax.broadcasted_iota(I32, (MOE_SUB, MOE_SUB), 1)
        tri_ref[...] = jnp.where(i < j, 1.0, 0.0).astype(BF16)

    scores = jax.nn.sigmoid(_dot_nt(wr_ref[...], x_ref[...]))
    biased = scores + rb_ref[...]
    sub = lax.broadcasted_iota(I32, (GROUP_SIZE, tm), 0).astype(F32)
    none = float(N_EXPERTS)
    ninf = -jnp.inf

    def first_max(vals, index):
        m = jnp.max(vals, axis=0, keepdims=True)
        return m, jnp.min(jnp.where(vals == m, index, none), axis=0, keepdims=True)

    groups = [biased[g * GROUP_SIZE:(g + 1) * GROUP_SIZE, :] for g in range(N_GROUPS)]
    group_scores = []
    for g in range(N_GROUPS):
        m1, i1 = first_max(groups[g], sub)
        m2 = jnp.max(jnp.where(sub == i1, ninf, groups[g]), axis=0, keepdims=True)
        group_scores.append(m1 + m2)
    cur = jnp.concatenate(group_scores, axis=0)
    group_sel = jnp.zeros(cur.shape, F32)
    for _ in range(TOPK_GROUPS):
        _, i1 = first_max(cur, sub)
        hit = sub == i1
        group_sel = jnp.where(hit, 1.0, group_sel)
        cur = jnp.where(hit, ninf, cur)

    masked = [jnp.where(group_sel[g:g + 1, :] > 0.5, groups[g], ninf) for g in range(N_GROUPS)]
    ids = [sub + float(g * GROUP_SIZE) for g in range(N_GROUPS)]
    chosen = [jnp.zeros((GROUP_SIZE, tm), F32) for _ in range(N_GROUPS)]
    for _ in range(TOP_K):
        m = masked[0]
        for g in range(1, N_GROUPS):
            m = jnp.maximum(m, masked[g])
        m = jnp.max(m, axis=0, keepdims=True)
        cand = jnp.where(masked[0] == m, ids[0], none)
        for g in range(1, N_GROUPS):
            cand = jnp.minimum(cand, jnp.where(masked[g] == m, ids[g], none))
        first = jnp.min(cand, axis=0, keepdims=True)
        for g in range(N_GROUPS):
            hit = ids[g] == first
            chosen[g] = jnp.where(hit, 1.0, chosen[g])
            masked[g] = jnp.where(hit, ninf, masked[g])

    sel = jnp.concatenate(chosen, axis=0) > 0.5
    picked = jnp.where(sel, scores, 0.0)
    total = jnp.sum(picked, axis=0, keepdims=True)
    w_ref[...] = picked / total * ROUTED_SCALE
    sel_f = jnp.where(sel, 1.0, 0.0)
    sel_b = sel_f.astype(BF16)
    before, cnt_max = [], None
    for k in range(MOE_NSUB):
        cols = slice(k * MOE_SUB, (k + 1) * MOE_SUB)
        before.append(_dot(sel_b[:, cols], tri_ref[...]))
        cnt = jnp.sum(sel_f[:, cols], axis=1, keepdims=True)
        cnt_max = cnt if cnt_max is None else jnp.maximum(cnt_max, cnt)
    rank_ref[...] = jnp.where(sel, jnp.concatenate(before, axis=1).astype(I32), -1)
    cnt_ref[...] = jnp.broadcast_to(cnt_max, cnt_ref.shape).astype(I32)


def _route(x1b, w_router, router_bias):
    t = x1b.shape[0]
    tm = MOE_TM
    nt = t // tm
    wr_t = w_router.astype(F32).T.astype(BF16)
    const = lambda i: (0, 0)
    col = lambda i: (0, i)
    return pl.pallas_call(
        _route_kernel,
        grid=(nt,),
        in_specs=[pl.BlockSpec((tm, D_MODEL), lambda i: (i, 0)),
                  pl.BlockSpec((N_EXPERTS, D_MODEL), const),
                  pl.BlockSpec((N_EXPERTS, 1), const)],
        out_specs=[pl.BlockSpec((N_EXPERTS, tm), col), pl.BlockSpec((N_EXPERTS, tm), col),
                   pl.BlockSpec((N_EXPERTS, LANES), col)],
        out_shape=[jax.ShapeDtypeStruct((N_EXPERTS, t), F32),
                   jax.ShapeDtypeStruct((N_EXPERTS, t), I32),
                   jax.ShapeDtypeStruct((N_EXPERTS, nt * LANES), I32)],
        scratch_shapes=[pltpu.VMEM((MOE_SUB, MOE_SUB), BF16)],
        compiler_params=_cparams(("arbitrary",)),
        name="route",
    )(x1b, wr_t, router_bias.astype(F32)[:, None])


def _moe_kernel(cnt_ref, x_ref, rank_ref, w_ref, wgu_ref, wd_ref, o_ref, oh_ref, g_ref, y_ref):
    i = pl.program_id(0)
    eb = pl.program_id(1)

    @pl.when(eb == 0)
    def _():
        o_ref[...] = jnp.zeros_like(o_ref)

    max_count = cnt_ref[eb, i]

    def one_pass(p, slots):
        n_rows = MOE_EB * slots
        base = p * slots
        slot = lax.broadcasted_iota(I32, (slots, MOE_SUB), 0)
        slot_w = []
        for k in range(MOE_NSUB):
            tok = slice(k * MOE_SUB, (k + 1) * MOE_SUB)
            blocks, weights = [], []
            for j in range(MOE_EB):
                expert = pl.ds(eb * MOE_EB + j, 1)
                match = slot == (rank_ref[expert, tok] - base)
                blocks.append(jnp.where(match, 1.0, 0.0).astype(BF16))
                weights.append(jnp.sum(jnp.where(match, w_ref[expert, tok], 0.0), axis=1, keepdims=True))
            onehot = jnp.concatenate(blocks, axis=0)
            oh_ref[k, 0:n_rows, :] = onehot
            slot_w.append(weights)
            g_ref[k, 0:n_rows, :] = _dot(onehot, x_ref[tok, :]).astype(BF16)
        for j in range(MOE_EB):
            rows = slice(j * slots, (j + 1) * slots)
            xe = jnp.concatenate([g_ref[k, rows, :] for k in range(MOE_NSUB)], axis=0)
            h = _dot(xe, wgu_ref[j])
            act = (jax.nn.silu(h[:, :EXPERT_DIM]) * h[:, EXPERT_DIM:]).astype(BF16)
            wc = jnp.concatenate([slot_w[k][j] for k in range(MOE_NSUB)], axis=0)
            yw = (_dot(act, wd_ref[j]) * wc).astype(BF16)
            for k in range(MOE_NSUB):
                y_ref[k, rows, :] = yw[k * slots:(k + 1) * slots, :]
        for k in range(MOE_NSUB):
            tok = slice(k * MOE_SUB, (k + 1) * MOE_SUB)
            o_ref[tok, :] += _dot_tn(oh_ref[k, 0:n_rows, :], y_ref[k, 0:n_rows, :])

    @pl.when(max_count <= MOE_SLOTS_SMALL)
    def _():
        one_pass(0, MOE_SLOTS_SMALL)

    @pl.when(max_count > MOE_SLOTS_SMALL)
    def _():
        n_pass = lax.shift_right_logical(max_count + (MOE_SLOTS - 1), int(np.log2(MOE_SLOTS)))

        def body(p, carry):
            one_pass(p, MOE_SLOTS)
            return carry

        lax.fori_loop(0, n_pass, body, 0)


def _moe(x1b, rank_t, w_t, counts, wgu_b, wd_b):
    t = x1b.shape[0]
    tm = MOE_TM
    nt = t // tm
    counts = counts.reshape(N_EXPERTS // MOE_EB, MOE_EB, nt).max(axis=1)
    grid_spec = pltpu.PrefetchScalarGridSpec(
        num_scalar_prefetch=1,
        grid=(nt, N_EXPERTS // MOE_EB),
        in_specs=[
            pl.BlockSpec((tm, D_MODEL), lambda i, e, c: (i, 0)),
            pl.BlockSpec((N_EXPERTS, tm), lambda i, e, c: (0, i)),
            pl.BlockSpec((N_EXPERTS, tm), lambda i, e, c: (0, i)),
            pl.BlockSpec((MOE_EB, D_MODEL, 2 * EXPERT_DIM), lambda i, e, c: (e, 0, 0)),
            pl.BlockSpec((MOE_EB, EXPERT_DIM, D_MODEL), lambda i, e, c: (e, 0, 0)),
        ],
        out_specs=pl.BlockSpec((tm, D_MODEL), lambda i, e, c: (i, 0)),
        scratch_shapes=[pltpu.VMEM((MOE_NSUB, MOE_SUB, MOE_SUB), BF16),
                        pltpu.VMEM((MOE_NSUB, MOE_SUB, D_MODEL), BF16),
                        pltpu.VMEM((MOE_NSUB, MOE_SUB, D_MODEL), BF16)],
    )
    return pl.pallas_call(
        _moe_kernel,
        grid_spec=grid_spec,
        out_shape=jax.ShapeDtypeStruct((t, D_MODEL), F32),
        compiler_params=_cparams(("parallel", "arbitrary")),
        name="moe",
    )(counts, x1b, rank_t, w_t, wgu_b, wd_b)


def _final_kernel(x1_ref, ffn_ref, p_ref, wsgu_f, wsd_f, wp_f, wg_f, gain_ref, bias_ref, o_ref,
                  wsgu_ref, wsd_ref, wp_ref, wg_ref):
    for src_ref, dst_ref in ((wsgu_f, wsgu_ref), (wsd_f, wsd_ref), (wp_f, wp_ref), (wg_f, wg_ref)):
        _cast_once(src_ref, dst_ref)
    for r in range(0, x1_ref.shape[0], PROJ_SUB):
        rows = slice(r, r + PROJ_SUB)
        x1 = x1_ref[rows, :]
        xb = x1.astype(BF16)
        h = _dot(xb, wsgu_ref[...])
        act = (jax.nn.silu(h[:, :EXPERT_DIM]) * h[:, EXPERT_DIM:]).astype(BF16)
        shared = _dot(act, wsd_ref[...])
        ple = _dot(p_ref[rows, :].astype(BF16), wp_ref[...]) * jax.nn.sigmoid(_dot(xb, wg_ref[...]))
        o_ref[rows, :] = _layer_norm(ALPHA * x1 + (ffn_ref[rows, :] + shared) + ple,
                                     gain_ref[...], bias_ref[...])


def _final(x1, ffn, p2, wsgu, wsd, wp, wg, gain, bias):
    t = x1.shape[0]
    tm = PROJ_TM
    tok = lambda i: (i, 0)
    const = lambda i: (0, 0)
    weights = [w.astype(F32) for w in (wsgu, wsd, wp, wg)]
    return pl.pallas_call(
        _final_kernel,
        grid=(t // tm,),
        in_specs=[pl.BlockSpec((tm, D_MODEL), tok), pl.BlockSpec((tm, D_MODEL), tok),
                  pl.BlockSpec((tm, p2.shape[1]), tok)]
                 + [_resident(w.shape) for w in weights]
                 + [pl.BlockSpec((1, D_MODEL), const), pl.BlockSpec((1, D_MODEL), const)],
        out_specs=pl.BlockSpec((tm, D_MODEL), tok),
        out_shape=jax.ShapeDtypeStruct((t, D_MODEL), F32),
        scratch_shapes=[pltpu.VMEM(w.shape, BF16) for w in weights],
        compiler_params=_cparams(("arbitrary",)),
        name="final",
    )(x1, ffn, p2, *weights, gain.astype(F32)[None, :], bias.astype(F32)[None, :])


def _rotary_tables(seq):
    half = HEAD_DIM // 2
    inv = ROPE_BASE ** (-jnp.arange(half, dtype=F32) / half)
    ang = jnp.arange(seq, dtype=jnp.int32).astype(F32)[:, None] * inv[None, :]
    cos, sin = jnp.cos(ang), jnp.sin(ang)
    reps = LANES // HEAD_DIM
    cos_t = jnp.tile(jnp.concatenate([cos, cos], axis=1), (1, reps))
    sin_t = jnp.tile(jnp.concatenate([-sin, sin], axis=1), (1, reps))
    return cos_t, sin_t


def kernel(x, p, w_in, ret_decay_fwd, ret_decay_bwd, ret_gn_gain, na_rpb, w_out, ln1_gain, ln1_bias,
           w_router, router_bias, w_expert_gu, w_expert_down, w_shared_gu, w_shared_down,
           w_ple_proj, w_ple_gate, ln2_gain, ln2_bias):
    batch, seq, d = x.shape
    t = batch * seq
    depth = w_in.shape[0]
    assert depth == 1 and d == D_MODEL
    assert seq % PROJ_TM == 0 and seq % MOE_TM == 0 and seq % RET_CHUNK == 0
    assert (seq // GRID_W) % (NA_ROWS * NA_STEP_BLOCKS) == 0 and seq // GRID_W >= NA_KEY_ROWS
    cos_t, sin_t = _rotary_tables(seq)
    x2 = x.reshape(t, d)
    for i in range(depth):
        rq, rk, rv, rg, nq, nk, nv = _in_proj(x2, w_in[i], cos_t, sin_t, seq)
        ret = _retention(rq, rk, rv, rg, ret_decay_fwd[i], ret_decay_bwd[i], ret_gn_gain[i], batch, seq)
        na, wgu_b, wd_b = _natten(nq, nk, nv, na_rpb[i], w_expert_gu[i], w_expert_down[i], batch, seq)
        x1, x1b = _out_ln1(ret, na, x2, w_out[i], ln1_gain[i], ln1_bias[i])
        w_t, rank_t, cnt = _route(x1b, w_router[i], router_bias[i])
        counts = cnt[:, ::LANES]
        ffn = _moe(x1b, rank_t, w_t, counts, wgu_b, wd_b)
        x2 = _final(x1, ffn, p[i].reshape(t, -1), w_shared_gu[i], w_shared_down[i],
                    w_ple_proj[i], w_ple_gate[i], ln2_gain[i], ln2_bias[i])
    return x2.reshape(batch, seq, d)
```

```python
import functools

import numpy as np
import jax
import jax.numpy as jnp
from jax import lax
from jax.experimental import pallas as pl
from jax.experimental.pallas import tpu as pltpu

F32 = jnp.float32
BF16 = jnp.bfloat16
I32 = jnp.int32

D_MODEL = 1024
HEADS = 8
HEAD_DIM = 64
GROUP_W = HEADS * HEAD_DIM
ROPE_BASE = 10000.0
GN_EPS = 1e-6
LN_EPS = 1e-5
GRID_W = 64
NA_WIN_ROWS = 8
NA_WIN_COLS = 16
N_EXPERTS = 64
N_GROUPS = 8
GROUP_SIZE = N_EXPERTS // N_GROUPS
TOPK_GROUPS = 4
TOP_K = 8
EXPERT_DIM = 256
ROUTED_SCALE = 2.5
ALPHA = 2.0 ** 0.25
NEG_BIG = -1e30
LOG2E = 1.4426950408889634
NA_Q_SCALE = HEAD_DIM ** -0.5 * LOG2E

LANES = 128
VMEM_LIMIT_BYTES = 56 * 1024 * 1024

PROJ_TM = 512
PROJ_SUB = 256
RET_CHUNK = 128
RET_UNROLL = 16
NA_ROWS = 4
NA_KEY_ROWS = NA_ROWS + NA_WIN_ROWS
NA_STEP_BLOCKS = 4
MOE_TM = 2048
MOE_SUB = 256
MOE_EB = 4
MOE_SLOTS = MOE_SUB // MOE_EB
MOE_SLOTS_SMALL = 48
MOE_NSUB = MOE_TM // MOE_SUB


def _cparams(sem):
    return pltpu.CompilerParams(dimension_semantics=sem, vmem_limit_bytes=VMEM_LIMIT_BYTES)


def _dot(a, b):
    return jnp.dot(a, b, preferred_element_type=F32)


def _dot_nt(a, b):
    return lax.dot_general(a, b, (((1,), (1,)), ((), ())), preferred_element_type=F32)


def _dot_tn(a, b):
    return lax.dot_general(a, b, (((0,), (0,)), ((), ())), preferred_element_type=F32)


def _cast_once(src_ref, dst_ref):
    @pl.when(pl.program_id(0) == 0)
    def _():
        dst_ref[...] = src_ref[...].astype(BF16)


def _resident(shape):
    return pl.BlockSpec(shape, lambda i: (0,) * len(shape), pipeline_mode=pl.Buffered(1))


def _in_proj_kernel(x_ref, wf_ref, cos_ref, sin_ref,
                    rq_ref, rk_ref, rv_ref, rg_ref, nq_ref, nk_ref, nv_ref, w_ref):
    _cast_once(wf_ref, w_ref)
    xb = x_ref[...].astype(BF16)
    cos = cos_ref[...]
    sin = sin_ref[...]
    lane = lax.broadcasted_iota(I32, (1, LANES), 1)
    first_half = (lane % HEAD_DIM) < (HEAD_DIM // 2)

    def proj(g):
        return _dot(xb, w_ref[:, g * GROUP_W:(g + 1) * GROUP_W])

    def rotary(t, scale):
        outs = []
        for j in range(GROUP_W // LANES):
            c = t[:, j * LANES:(j + 1) * LANES]
            swapped = jnp.where(first_half,
                                pltpu.roll(c, LANES - HEAD_DIM // 2, axis=1),
                                pltpu.roll(c, HEAD_DIM // 2, axis=1))
            outs.append((c * cos + swapped * sin) * scale)
        return jnp.concatenate(outs, axis=1)

    rq_ref[...] = rotary(proj(0), 1.0).astype(BF16)
    rk_ref[...] = rotary(proj(1), HEAD_DIM ** -0.5).astype(BF16)
    rv_ref[...] = proj(2).astype(BF16)
    rg_ref[...] = jax.nn.silu(proj(3)).astype(BF16)
    nq_ref[...] = (proj(4) * NA_Q_SCALE).astype(BF16)
    nk_ref[...] = proj(5).astype(BF16)
    nv_ref[...] = proj(6).astype(BF16)


def _in_proj(x2, w_in, cos_t, sin_t, seq):
    t = x2.shape[0]
    tm = PROJ_TM
    n_pos = seq // tm
    out = jax.ShapeDtypeStruct((t, GROUP_W), BF16)
    tok = lambda i: (i, 0)
    return pl.pallas_call(
        _in_proj_kernel,
        grid=(t // tm,),
        in_specs=[
            pl.BlockSpec((tm, D_MODEL), tok),
            _resident(w_in.shape),
            pl.BlockSpec((tm, LANES), lambda i: (i % n_pos, 0)),
            pl.BlockSpec((tm, LANES), lambda i: (i % n_pos, 0)),
        ],
        out_specs=[pl.BlockSpec((tm, GROUP_W), tok)] * 7,
        out_shape=[out] * 7,
        scratch_shapes=[pltpu.VMEM(w_in.shape, BF16)],
        compiler_params=_cparams(("arbitrary",)),
        name="in_proj",
    )(x2, w_in.astype(F32), cos_t, sin_t)


def _log_sigmoid(x):
    return jnp.minimum(x, 0.0) - jnp.log1p(jnp.exp(-jnp.abs(x)))


def _retention_kernel(q_ref, k_ref, v_ref, g_ref, decf_ref, decb_ref, decfd_ref, decbd_ref,
                      gain_ref, o_ref, kv_ref, st_ref, dmat_ref):
    c = RET_CHUNK
    n_chunks = q_ref.shape[0] // c
    pair_w = 2 * HEAD_DIM

    lgf = _log_sigmoid(decf_ref[...])
    lgb = _log_sigmoid(decb_ref[...])
    row = lax.broadcasted_iota(I32, (c, 1), 0).astype(F32)
    k_dec_f = jnp.exp((c - 1.0 - row) * lgf)
    k_dec_b = jnp.exp(row * lgb)
    q_dec_f = jnp.exp((row + 1.0) * lgf)
    q_dec_b = jnp.exp((c - row) * lgb)
    chunk_dec_f = jnp.exp(c * lgf)
    chunk_dec_b = jnp.exp(c * lgb)

    lane = lax.broadcasted_iota(I32, (1, pair_w), 1)
    head0 = lane < HEAD_DIM
    r2 = lax.broadcasted_iota(I32, (pair_w, pair_w), 0) // HEAD_DIM
    c2 = lax.broadcasted_iota(I32, (pair_w, pair_w), 1) // HEAD_DIM
    same_head = r2 == c2
    block_diag = jnp.where(same_head, 1.0, 0.0)
    seg_avg = jnp.where(same_head, 1.0 / HEAD_DIM, 0.0).astype(BF16)

    lgf_d = _log_sigmoid(decfd_ref[0])
    lgb_d = _log_sigmoid(decbd_ref[0])
    di = lax.broadcasted_iota(I32, (c, 2 * c), 0)
    dj = lax.broadcasted_iota(I32, (c, 2 * c), 1) % c
    diff = (di - dj).astype(F32)
    dmat_ref[...] = jnp.where(diff >= 0.0, jnp.exp(diff * lgf_d), jnp.exp(-diff * lgb_d))

    def chunk(ref, n):
        return ref[pl.ds(pl.multiple_of(n * c, c), c), :]

    unroll = RET_UNROLL

    def summarize(nb, carry):
        for u in range(unroll):
            n = nb * unroll + u
            kf = chunk(k_ref, n).astype(F32)
            kst = jnp.concatenate([kf * k_dec_f, kf * k_dec_b], axis=1).astype(BF16)
            kv_ref[n] = _dot_tn(kst, chunk(v_ref, n))
        return carry

    lax.fori_loop(0, n_chunks // unroll, summarize, 0)

    def fwd_scan(n, state):
        st_ref[n, 0:pair_w, :] = state.astype(BF16)
        return state * chunk_dec_f + kv_ref[n, 0:pair_w, :] * block_diag

    lax.fori_loop(0, n_chunks, fwd_scan, jnp.zeros((pair_w, pair_w), F32))

    def bwd_scan(i, state):
        n = n_chunks - 1 - i
        st_ref[n, pair_w:2 * pair_w, :] = state.astype(BF16)
        return state * chunk_dec_b + kv_ref[n, pair_w:2 * pair_w, :] * block_diag

    lax.fori_loop(0, n_chunks, bwd_scan, jnp.zeros((pair_w, pair_w), F32))

    gain = gain_ref[...]

    seg_avg2 = jnp.concatenate([seg_avg, seg_avg], axis=0)

    def seg_mean(z):
        hi = z.astype(BF16)
        lo = (z - hi.astype(F32)).astype(BF16)
        return _dot(jnp.concatenate([hi, lo], axis=1), seg_avg2)

    def emit(nb, carry):
        ys = []
        for u in range(unroll):
            n = nb * unroll + u
            q = chunk(q_ref, n)
            k = chunk(k_ref, n)
            v = chunk(v_ref, n)
            zero = jnp.zeros_like(k)
            k_st = jnp.concatenate([jnp.where(head0, k, zero), jnp.where(head0, zero, k)], axis=0)
            v_st = jnp.concatenate([jnp.where(head0, v, zero), jnp.where(head0, zero, v)], axis=0)
            scores = _dot_nt(q, k_st) * dmat_ref[...]
            qf = q.astype(F32)
            q_st = jnp.concatenate([qf * q_dec_f, qf * q_dec_b], axis=1).astype(BF16)
            ys.append(_dot(scores.astype(BF16), v_st) + _dot(q_st, st_ref[n]))
        y = jnp.concatenate(ys, axis=0)
        mu = seg_mean(y)
        d = y - mu
        var = seg_mean(d * d)
        yn = d * lax.rsqrt(var + GN_EPS) * gain
        rows = pl.ds(pl.multiple_of(nb * (unroll * c), unroll * c), unroll * c)
        o_ref[rows, :] = (g_ref[rows, :].astype(F32) * yn).astype(BF16)
        return carry

    lax.fori_loop(0, n_chunks // unroll, emit, 0)


def _retention(rq, rk, rv, rg, dec_f, dec_b, gain, batch, seq):
    t = rq.shape[0]
    c = RET_CHUNK
    n_pairs = HEADS // 2
    pair_w = 2 * HEAD_DIM
    dec_f_lane = jnp.repeat(dec_f.astype(F32), HEAD_DIM)[None, :]
    dec_b_lane = jnp.repeat(dec_b.astype(F32), HEAD_DIM)[None, :]
    dec_f_col = jnp.repeat(dec_f.astype(F32), c).reshape(n_pairs, 1, 2 * c)
    dec_b_col = jnp.repeat(dec_b.astype(F32), c).reshape(n_pairs, 1, 2 * c)
    tok = pl.BlockSpec((seq, pair_w), lambda b, p: (b, p))
    lane_spec = pl.BlockSpec((1, pair_w), lambda b, p: (0, p))
    col_spec = pl.BlockSpec((1, 1, 2 * c), lambda b, p: (p, 0, 0))
    return pl.pallas_call(
        _retention_kernel,
        grid=(batch, n_pairs),
        in_specs=[tok, tok, tok, tok, lane_spec, lane_spec, col_spec, col_spec, lane_spec],
        out_specs=tok,
        out_shape=jax.ShapeDtypeStruct((t, GROUP_W), BF16),
        scratch_shapes=[
            pltpu.VMEM((seq // c, 2 * pair_w, pair_w), F32),
            pltpu.VMEM((seq // c, 2 * pair_w, pair_w), BF16),
            pltpu.VMEM((c, 2 * c), F32),
        ],
        compiler_params=_cparams(("parallel", "parallel")),
        name="retention",
    )(rq, rk, rv, rg, dec_f_lane, dec_b_lane, dec_f_col, dec_b_col, gain.astype(F32)[None, :])


N_ROW_OFFSETS = 2 * NA_WIN_ROWS - 1
N_COL_OFFSETS = 2 * NA_WIN_COLS - 1


def _natten_row_offsets(rows):
    n_blocks = rows // NA_ROWS
    starts = {0: 0, 1: NA_ROWS - NA_WIN_ROWS // 2, 2: rows - NA_KEY_ROWS}
    blocks = {0: 0, 1: 1, 2: n_blocks - 1}
    table = []
    for v in range(3):
        per_a = []
        for a in range(NA_ROWS):
            r = blocks[v] * NA_ROWS + a
            rs = min(max(r - NA_WIN_ROWS // 2, 0), rows - NA_WIN_ROWS)
            per_kl = []
            for kl in range(NA_KEY_ROWS):
                kr = starts[v] + kl
                per_kl.append(kr - r + NA_WIN_ROWS - 1 if rs <= kr < rs + NA_WIN_ROWS else None)
            per_a.append(per_kl)
        table.append(per_a)
    return table


def _natten_live_columns(rows):
    offsets = _natten_row_offsets(rows)
    live = []
    for blk in range(NA_STEP_BLOCKS):
        variants = [1] + ([0] if blk == 0 else []) + ([2] if blk == NA_STEP_BLOCKS - 1 else [])
        live.append([[j for j in range(NA_KEY_ROWS // 2)
                      if any(offsets[v][a][kl] is not None for v in variants for kl in (2 * j, 2 * j + 1))]
                     for a in range(NA_ROWS)])
    return live


def _natten_bias_kernel(rpb_ref, tab_ref, *, rows):
    offsets = _natten_row_offsets(rows)
    nk = NA_KEY_ROWS * GRID_W
    shape = (GRID_W, LANES)
    lane = lax.broadcasted_iota(I32, shape, 1)
    c = lax.broadcasted_iota(I32, shape, 0)
    second = lane >= GRID_W
    kc = lane % GRID_W
    cs = jnp.clip(c - NA_WIN_COLS // 2, 0, GRID_W - NA_WIN_COLS)
    col_ok = jnp.logical_and(kc >= cs, kc < cs + NA_WIN_COLS)
    neg = jnp.full(shape, NEG_BIG, F32)
    for hh in range(2):
        toeplitz = []
        for dr in range(N_ROW_OFFSETS):
            x = jnp.broadcast_to(rpb_ref[hh, dr:dr + 1, :] * LOG2E, shape)
            lo = pltpu.roll(x, LANES - (NA_WIN_COLS - 1), axis=1, stride=1, stride_axis=0)
            hi = pltpu.roll(x, GRID_W - (NA_WIN_COLS - 1), axis=1, stride=1, stride_axis=0)
            toeplitz.append(jnp.where(second, hi, lo))
        for v in range(3):
            for a in range(NA_ROWS):
                for j in range(NA_KEY_ROWS // 2):
                    d0, d1 = offsets[v][a][2 * j], offsets[v][a][2 * j + 1]
                    if d0 is None and d1 is None:
                        piece = neg
                    else:
                        t0 = neg if d0 is None else toeplitz[d0]
                        t1 = neg if d1 is None else toeplitz[d1]
                        piece = jnp.where(col_ok, jnp.where(second, t1, t0), neg)
                    tab_ref[v, 0, a * GRID_W:(a + 1) * GRID_W,
                            hh * nk + j * LANES:hh * nk + (j + 1) * LANES] = piece


def _natten_bias_table(rpb, rows):
    n_pairs = HEADS // 2
    nqb = NA_ROWS * GRID_W
    nkb = NA_KEY_ROWS * GRID_W
    rpb_pad = jnp.pad(rpb.astype(F32), ((0, 0), (0, 0), (0, LANES - N_COL_OFFSETS)))
    return pl.pallas_call(
        functools.partial(_natten_bias_kernel, rows=rows),
        grid=(n_pairs,),
        in_specs=[pl.BlockSpec((2, N_ROW_OFFSETS, LANES), lambda p: (p, 0, 0))],
        out_specs=pl.BlockSpec((3, 1, nqb, 2 * nkb), lambda p: (0, p, 0, 0)),
        out_shape=jax.ShapeDtypeStruct((3, n_pairs, nqb, 2 * nkb), F32),
        compiler_params=_cparams(("parallel",)),
        name="natten_bias",
    )(rpb_pad)


def _natten_kernel(q_ref, k_ref, v_ref, tab_ref, wgu_f, wd_f, o_ref, wgu_b, wd_b, ks_ref, vs_ref, *, rows):
    wgu_b[...] = wgu_f[...].astype(BF16)
    wd_b[...] = wd_f[...].astype(BF16)
    nq = NA_ROWS * GRID_W
    nk = NA_KEY_ROWS * GRID_W
    n_blocks = rows // NA_ROWS
    live = _natten_live_columns(rows)
    lane = lax.broadcasted_iota(I32, (1, 2 * HEAD_DIM), 1)
    head0 = lane < HEAD_DIM

    @pl.when(pl.program_id(2) == 0)
    def _():
        k = k_ref[...]
        v = v_ref[...]
        zero = jnp.zeros_like(k)
        ks_ref[0] = jnp.where(head0, k, zero)
        ks_ref[1] = jnp.where(head0, zero, k)
        ind0 = jnp.broadcast_to(jnp.where(head0, 1.0, 0.0).astype(BF16), k.shape)
        ind1 = jnp.broadcast_to(jnp.where(head0, 0.0, 1.0).astype(BF16), k.shape)
        vs_ref[0] = jnp.concatenate([jnp.where(head0, v, zero), ind0], axis=1)
        vs_ref[1] = jnp.concatenate([jnp.where(head0, zero, v), ind1], axis=1)

    for blk in range(NA_STEP_BLOCKS):
        rb = pl.program_id(2) * NA_STEP_BLOCKS + blk
        variant = jnp.where(rb == 0, 0, jnp.where(rb == n_blocks - 1, 2, 1))
        start_row = jnp.clip(rb * NA_ROWS - NA_WIN_ROWS // 2, 0, rows - NA_KEY_ROWS)
        win = pl.ds(pl.multiple_of(start_row * GRID_W, GRID_W), nk)
        q = q_ref[blk * nq:(blk + 1) * nq, :]
        k_st = jnp.concatenate([ks_ref[0, win, :], ks_ref[1, win, :]], axis=0)
        v_st = jnp.concatenate([vs_ref[0, win, :], vs_ref[1, win, :]], axis=0)
        s = _dot_nt(q, k_st) + tab_ref[variant, 0]
        dead = jnp.zeros((GRID_W, LANES), BF16)
        slabs = []
        for a in range(NA_ROWS):
            qrows = slice(a * GRID_W, (a + 1) * GRID_W)
            parts = []
            for h in range(2):
                cols = {j: s[qrows, h * nk + j * LANES:h * nk + (j + 1) * LANES] for j in live[blk][a]}
                m = jnp.max(jnp.concatenate(list(cols.values()), axis=1), axis=1, keepdims=True)
                parts += [jnp.exp2(cols[j] - m).astype(BF16) if j in cols else dead
                          for j in range(NA_KEY_ROWS // 2)]
            slabs.append(jnp.concatenate(parts, axis=1))
        out = _dot(jnp.concatenate(slabs, axis=0), v_st)
        o_ref[blk * nq:(blk + 1) * nq, :] = (out[:, :2 * HEAD_DIM] / out[:, 2 * HEAD_DIM:]).astype(BF16)


def _natten(nq, nk, nv, rpb, w_gu, w_down, batch, seq):
    t = nq.shape[0]
    rows = seq // GRID_W
    n_pairs = HEADS // 2
    n_steps = rows // (NA_ROWS * NA_STEP_BLOCKS)
    nqb = NA_ROWS * GRID_W
    nkb = NA_KEY_ROWS * GRID_W
    tab = _natten_bias_table(rpb, rows)
    total_steps = batch * n_pairs * n_steps
    n_exp = w_gu.shape[0]
    assert n_exp % total_steps == 0
    epb = n_exp // total_steps

    def expert_spec(shape):
        return pl.BlockSpec((epb,) + shape[1:], lambda b, p, s: ((b * n_pairs + p) * n_steps + s, 0, 0))

    tab_spec = pl.BlockSpec((3, 1, nqb, 2 * nkb), lambda b, p, s: (0, p, 0, 0))
    kv_spec = pl.BlockSpec((seq, 2 * HEAD_DIM), lambda b, p, s: (b, p))
    q_spec = pl.BlockSpec((NA_STEP_BLOCKS * nqb, 2 * HEAD_DIM), lambda b, p, s: (b * n_steps + s, p))
    return pl.pallas_call(
        functools.partial(_natten_kernel, rows=rows),
        grid=(batch, n_pairs, n_steps),
        in_specs=[q_spec, kv_spec, kv_spec, tab_spec, expert_spec(w_gu.shape), expert_spec(w_down.shape)],
        out_specs=[q_spec, expert_spec(w_gu.shape), expert_spec(w_down.shape)],
        out_shape=[jax.ShapeDtypeStruct((t, GROUP_W), BF16),
                   jax.ShapeDtypeStruct(w_gu.shape, BF16), jax.ShapeDtypeStruct(w_down.shape, BF16)],
        scratch_shapes=[pltpu.VMEM((2, seq, 2 * HEAD_DIM), BF16), pltpu.VMEM((2, seq, 4 * HEAD_DIM), BF16)],
        compiler_params=_cparams(("arbitrary", "arbitrary", "arbitrary")),
        name="natten",
    )(nq, nk, nv, tab, w_gu.astype(F32), w_down.astype(F32))


def _layer_norm(h, gain, bias):
    mu = jnp.mean(h, axis=-1, keepdims=True)
    d = h - mu
    var = jnp.mean(d * d, axis=-1, keepdims=True)
    return d * lax.rsqrt(var + LN_EPS) * gain + bias


def _out_ln1_kernel(ret_ref, na_ref, x_ref, wf_ref, gain_ref, bias_ref, x1_ref, x1b_ref, w_ref):
    _cast_once(wf_ref, w_ref)
    for r in range(0, x_ref.shape[0], PROJ_SUB):
        rows = slice(r, r + PROJ_SUB)
        mix = (_dot(ret_ref[rows, :], w_ref[0:GROUP_W, :])
               + _dot(na_ref[rows, :], w_ref[GROUP_W:2 * GROUP_W, :]))
        x1 = _layer_norm(ALPHA * x_ref[rows, :] + mix, gain_ref[...], bias_ref[...])
        x1_ref[rows, :] = x1
        x1b_ref[rows, :] = x1.astype(BF16)


def _out_ln1(ret, na, x2, w_out, gain, bias):
    t = x2.shape[0]
    tm = PROJ_TM
    tok = lambda i: (i, 0)
    const = lambda i: (0, 0)
    return pl.pallas_call(
        _out_ln1_kernel,
        grid=(t // tm,),
        in_specs=[pl.BlockSpec((tm, GROUP_W), tok), pl.BlockSpec((tm, GROUP_W), tok),
                  pl.BlockSpec((tm, D_MODEL), tok), _resident(w_out.shape),
                  pl.BlockSpec((1, D_MODEL), const), pl.BlockSpec((1, D_MODEL), const)],
        out_specs=[pl.BlockSpec((tm, D_MODEL), tok), pl.BlockSpec((tm, D_MODEL), tok)],
        out_shape=[jax.ShapeDtypeStruct((t, D_MODEL), F32), jax.ShapeDtypeStruct((t, D_MODEL), BF16)],
        scratch_shapes=[pltpu.VMEM(w_out.shape, BF16)],
        compiler_params=_cparams(("arbitrary",)),
        name="out_ln1",
    )(ret, na, x2, w_out.astype(F32), gain.astype(F32)[None, :], bias.astype(F32)[None, :])


def _route_kernel(x_ref, wr_ref, rb_ref, w_ref, rank_ref, cnt_ref, tri_ref):
    tm = x_ref.shape[0]

    @pl.when(pl.program_id(0) == 0)
    def _():
        i = lax.broadcasted_iota(I32, (MOE_SUB, MOE_SUB), 0)
        j = lax.broadcasted_iota(I32, (MOE_SUB, MOE_SUB), 1)
        tri_ref[...] = jnp.where(i < j, 1.0, 0.0).astype(BF16)

    scores = jax.nn.sigmoid(_dot_nt(wr_ref[...], x_ref[...]))
    biased = scores + rb_ref[...]
    sub = lax.broadcasted_iota(I32, (GROUP_SIZE, tm), 0).astype(F32)
    none = float(N_EXPERTS)
    ninf = -jnp.inf

    def first_max(vals, index):
        m = jnp.max(vals, axis=0, keepdims=True)
        return m, jnp.min(jnp.where(vals == m, index, none), axis=0, keepdims=True)

    groups = [biased[g * GROUP_SIZE:(g + 1) * GROUP_SIZE, :] for g in range(N_GROUPS)]
    group_scores = []
    for g in range(N_GROUPS):
        m1, i1 = first_max(groups[g], sub)
        m2 = jnp.max(jnp.where(sub == i1, ninf, groups[g]), axis=0, keepdims=True)
        group_scores.append(m1 + m2)
    cur = jnp.concatenate(group_scores, axis=0)
    group_sel = jnp.zeros(cur.shape, F32)
    for _ in range(TOPK_GROUPS):
        _, i1 = first_max(cur, sub)
        hit = sub == i1
        group_sel = jnp.where(hit, 1.0, group_sel)
        cur = jnp.where(hit, ninf, cur)

    masked = [jnp.where(group_sel[g:g + 1, :] > 0.5, groups[g], ninf) for g in range(N_GROUPS)]
    ids = [sub + float(g * GROUP_SIZE) for g in range(N_GROUPS)]
    chosen = [jnp.zeros((GROUP_SIZE, tm), F32) for _ in range(N_GROUPS)]
    for _ in range(TOP_K):
        m = masked[0]
        for g in range(1, N_GROUPS):
            m = jnp.maximum(m, masked[g])
        m = jnp.max(m, axis=0, keepdims=True)
        cand = jnp.where(masked[0] == m, ids[0], none)
        for g in range(1, N_GROUPS):
            cand = jnp.minimum(cand, jnp.where(masked[g] == m, ids[g], none))
        first = jnp.min(cand, axis=0, keepdims=True)
        for g in range(N_GROUPS):
            hit = ids[g] == first
            chosen[g] = jnp.where(hit, 1.0, chosen[g])
            masked[g] = jnp.where(hit, ninf, masked[g])

    sel = jnp.concatenate(chosen, axis=0) > 0.5
    picked = jnp.where(sel, scores, 0.0)
    total = jnp.sum(picked, axis=0, keepdims=True)
    w_ref[...] = picked / total * ROUTED_SCALE
    sel_f = jnp.where(sel, 1.0, 0.0)
    sel_b = sel_f.astype(BF16)
    before, cnt_max = [], None
    for k in range(MOE_NSUB):
        cols = slice(k * MOE_SUB, (k + 1) * MOE_SUB)
        before.append(_dot(sel_b[:, cols], tri_ref[...]))
        cnt = jnp.sum(sel_f[:, cols], axis=1, keepdims=True)
        cnt_max = cnt if cnt_max is None else jnp.maximum(cnt_max, cnt)
    rank_ref[...] = jnp.where(sel, jnp.concatenate(before, axis=1).astype(I32), -1)
    cnt_ref[...] = jnp.broadcast_to(cnt_max, cnt_ref.shape).astype(I32)


def _route(x1b, w_router, router_bias):
    t = x1b.shape[0]
    tm = MOE_TM
    nt = t // tm
    wr_t = w_router.astype(F32).T.astype(BF16)
    const = lambda i: (0, 0)
    col = lambda i: (0, i)
    return pl.pallas_call(
        _route_kernel,
        grid=(nt,),
        in_specs=[pl.BlockSpec((tm, D_MODEL), lambda i: (i, 0)),
                  pl.BlockSpec((N_EXPERTS, D_MODEL), const),
                  pl.BlockSpec((N_EXPERTS, 1), const)],
        out_specs=[pl.BlockSpec((N_EXPERTS, tm), col), pl.BlockSpec((N_EXPERTS, tm), col),
                   pl.BlockSpec((N_EXPERTS, LANES), col)],
        out_shape=[jax.ShapeDtypeStruct((N_EXPERTS, t), F32),
                   jax.ShapeDtypeStruct((N_EXPERTS, t), I32),
                   jax.ShapeDtypeStruct((N_EXPERTS, nt * LANES), I32)],
        scratch_shapes=[pltpu.VMEM((MOE_SUB, MOE_SUB), BF16)],
        compiler_params=_cparams(("arbitrary",)),
        name="route",
    )(x1b, wr_t, router_bias.astype(F32)[:, None])


def _moe_kernel(cnt_ref, x_ref, rank_ref, w_ref, wgu_ref, wd_ref, o_ref, oh_ref, g_ref, y_ref):
    i = pl.program_id(0)
    eb = pl.program_id(1)

    @pl.when(eb == 0)
    def _():
        o_ref[...] = jnp.zeros_like(o_ref)

    max_count = cnt_ref[eb, i]

    def one_pass(p, slots):
        n_rows = MOE_EB * slots
        base = p * slots
        slot = lax.broadcasted_iota(I32, (slots, MOE_SUB), 0)
        slot_w = []
        for k in range(MOE_NSUB):
            tok = slice(k * MOE_SUB, (k + 1) * MOE_SUB)
            blocks, weights = [], []
            for j in range(MOE_EB):
                expert = pl.ds(eb * MOE_EB + j, 1)
                match = slot == (rank_ref[expert, tok] - base)
                blocks.append(jnp.where(match, 1.0, 0.0).astype(BF16))
                weights.append(jnp.sum(jnp.where(match, w_ref[expert, tok], 0.0), axis=1, keepdims=True))
            onehot = jnp.concatenate(blocks, axis=0)
            oh_ref[k, 0:n_rows, :] = onehot
            slot_w.append(weights)
            g_ref[k, 0:n_rows, :] = _dot(onehot, x_ref[tok, :]).astype(BF16)
        for j in range(MOE_EB):
            rows = slice(j * slots, (j + 1) * slots)
            xe = jnp.concatenate([g_ref[k, rows, :] for k in range(MOE_NSUB)], axis=0)
            h = _dot(xe, wgu_ref[j])
            act = (jax.nn.silu(h[:, :EXPERT_DIM]) * h[:, EXPERT_DIM:]).astype(BF16)
            wc = jnp.concatenate([slot_w[k][j] for k in range(MOE_NSUB)], axis=0)
            yw = (_dot(act, wd_ref[j]) * wc).astype(BF16)
            for k in range(MOE_NSUB):
                y_ref[k, rows, :] = yw[k * slots:(k + 1) * slots, :]
        for k in range(MOE_NSUB):
            tok = slice(k * MOE_SUB, (k + 1) * MOE_SUB)
            o_ref[tok, :] += _dot_tn(oh_ref[k, 0:n_rows, :], y_ref[k, 0:n_rows, :])

    @pl.when(max_count <= MOE_SLOTS_SMALL)
    def _():
        one_pass(0, MOE_SLOTS_SMALL)

    @pl.when(max_count > MOE_SLOTS_SMALL)
    def _():
        n_pass = lax.shift_right_logical(max_count + (MOE_SLOTS - 1), int(np.log2(MOE_SLOTS)))

        def body(p, carry):
            one_pass(p, MOE_SLOTS)
            return carry

        lax.fori_loop(0, n_pass, body, 0)


def _moe(x1b, rank_t, w_t, counts, wgu_b, wd_b):
    t = x1b.shape[0]
    tm = MOE_TM
    nt = t // tm
    counts = counts.reshape(N_EXPERTS // MOE_EB, MOE_EB, nt).max(axis=1)
    grid_spec = pltpu.PrefetchScalarGridSpec(
        num_scalar_prefetch=1,
        grid=(nt, N_EXPERTS // MOE_EB),
        in_specs=[
            pl.BlockSpec((tm, D_MODEL), lambda i, e, c: (i, 0)),
            pl.BlockSpec((N_EXPERTS, tm), lambda i, e, c: (0, i)),
            pl.BlockSpec((N_EXPERTS, tm), lambda i, e, c: (0, i)),
            pl.BlockSpec((MOE_EB, D_MODEL, 2 * EXPERT_DIM), lambda i, e, c: (e, 0, 0)),
            pl.BlockSpec((MOE_EB, EXPERT_DIM, D_MODEL), lambda i, e, c: (e, 0, 0)),
        ],
        out_specs=pl.BlockSpec((tm, D_MODEL), lambda i, e, c: (i, 0)),
        scratch_shapes=[pltpu.VMEM((MOE_NSUB, MOE_SUB, MOE_SUB), BF16),
                        pltpu.VMEM((MOE_NSUB, MOE_SUB, D_MODEL), BF16),
                        pltpu.VMEM((MOE_NSUB, MOE_SUB, D_MODEL), BF16)],
    )
    return pl.pallas_call(
        _moe_kernel,
        grid_spec=grid_spec,
        out_shape=jax.ShapeDtypeStruct((t, D_MODEL), F32),
        compiler_params=_cparams(("parallel", "arbitrary")),
        name="moe",
    )(counts, x1b, rank_t, w_t, wgu_b, wd_b)


def _final_kernel(x1_ref, ffn_ref, p_ref, wsgu_f, wsd_f, wp_f, wg_f, gain_ref, bias_ref, o_ref,
                  wsgu_ref, wsd_ref, wp_ref, wg_ref):
    for src_ref, dst_ref in ((wsgu_f, wsgu_ref), (wsd_f, wsd_ref), (wp_f, wp_ref), (wg_f, wg_ref)):
        _cast_once(src_ref, dst_ref)
    for r in range(0, x1_ref.shape[0], PROJ_SUB):
        rows = slice(r, r + PROJ_SUB)
        x1 = x1_ref[rows, :]
        xb = x1.astype(BF16)
        h = _dot(xb, wsgu_ref[...])
        act = (jax.nn.silu(h[:, :EXPERT_DIM]) * h[:, EXPERT_DIM:]).astype(BF16)
        shared = _dot(act, wsd_ref[...])
        ple = _dot(p_ref[rows, :].astype(BF16), wp_ref[...]) * jax.nn.sigmoid(_dot(xb, wg_ref[...]))
        o_ref[rows, :] = _layer_norm(ALPHA * x1 + (ffn_ref[rows, :] + shared) + ple,
                                     gain_ref[...], bias_ref[...])


def _final(x1, ffn, p2, wsgu, wsd, wp, wg, gain, bias):
    t = x1.shape[0]
    tm = PROJ_TM
    tok = lambda i: (i, 0)
    const = lambda i: (0, 0)
    weights = [w.astype(F32) for w in (wsgu, wsd, wp, wg)]
    return pl.pallas_call(
        _final_kernel,
        grid=(t // tm,),
        in_specs=[pl.BlockSpec((tm, D_MODEL), tok), pl.BlockSpec((tm, D_MODEL), tok),
                  pl.BlockSpec((tm, p2.shape[1]), tok)]
                 + [_resident(w.shape) for w in weights]
                 + [pl.BlockSpec((1, D_MODEL), const), pl.BlockSpec((1, D_MODEL), const)],
        out_specs=pl.BlockSpec((tm, D_MODEL), tok),
        out_shape=jax.ShapeDtypeStruct((t, D_MODEL), F32),
        scratch_shapes=[pltpu.VMEM(w.shape, BF16) for w in weights],
        compiler_params=_cparams(("arbitrary",)),
        name="final",
    )(x1, ffn, p2, *weights, gain.astype(F32)[None, :], bias.astype(F32)[None, :])


def _rotary_tables(seq):
    half = HEAD_DIM // 2
    inv = ROPE_BASE ** (-jnp.arange(half, dtype=F32) / half)
    ang = jnp.arange(seq, dtype=jnp.int32).astype(F32)[:, None] * inv[None, :]
    cos, sin = jnp.cos(ang), jnp.sin(ang)
    reps = LANES // HEAD_DIM
    cos_t = jnp.tile(jnp.concatenate([cos, cos], axis=1), (1, reps))
    sin_t = jnp.tile(jnp.concatenate([-sin, sin], axis=1), (1, reps))
    return cos_t, sin_t


def kernel(x, p, w_in, ret_decay_fwd, ret_decay_bwd, ret_gn_gain, na_rpb, w_out, ln1_gain, ln1_bias,
           w_router, router_bias, w_expert_gu, w_expert_down, w_shared_gu, w_shared_down,
           w_ple_proj, w_ple_gate, ln2_gain, ln2_bias):
    batch, seq, d = x.shape
    t = batch * seq
    depth = w_in.shape[0]
    assert depth == 1 and d == D_MODEL
    assert seq % PROJ_TM == 0 and seq % MOE_TM == 0 and seq % RET_CHUNK == 0
    assert (seq // GRID_W) % (NA_ROWS * NA_STEP_BLOCKS) == 0 and seq // GRID_W >= NA_KEY_ROWS
    cos_t, sin_t = _rotary_tables(seq)
    x2 = x.reshape(t, d)
    for i in range(depth):
        rq, rk, rv, rg, nq, nk, nv = _in_proj(x2, w_in[i], cos_t, sin_t, seq)
        ret = _retention(rq, rk, rv, rg, ret_decay_fwd[i], ret_decay_bwd[i], ret_gn_gain[i], batch, seq)
        na, wgu_b, wd_b = _natten(nq, nk, nv, na_rpb[i], w_expert_gu[i], w_expert_down[i], batch, seq)
        x1, x1b = _out_ln1(ret, na, x2, w_out[i], ln1_gain[i], ln1_bias[i])
        w_t, rank_t, cnt = _route(x1b, w_router[i], router_bias[i])
        counts = cnt[:, ::LANES]
        ffn = _moe(x1b, rank_t, w_t, counts, wgu_b, wd_b)
        x2 = _final(x1, ffn, p[i].reshape(t, -1), w_shared_gu[i], w_shared_down[i],
                    w_ple_proj[i], w_ple_gate[i], ln2_gain[i], ln2_bias[i])
    return x2.reshape(batch, seq, d)
```

```python
import functools

import numpy as np
import jax
import jax.numpy as jnp
from jax import lax
from jax.experimental import pallas as pl
from jax.experimental.pallas import tpu as pltpu

F32 = jnp.float32
BF16 = jnp.bfloat16
I32 = jnp.int32

D_MODEL = 1024
HEADS = 8
HEAD_DIM = 64
GROUP_W = HEADS * HEAD_DIM
ROPE_BASE = 10000.0
GN_EPS = 1e-6
LN_EPS = 1e-5
GRID_W = 64
NA_WIN_ROWS = 8
NA_WIN_COLS = 16
N_EXPERTS = 64
N_GROUPS = 8
GROUP_SIZE = N_EXPERTS // N_GROUPS
TOPK_GROUPS = 4
TOP_K = 8
EXPERT_DIM = 256
ROUTED_SCALE = 2.5
ALPHA = 2.0 ** 0.25
NEG_BIG = -1e30
LOG2E = 1.4426950408889634
NA_Q_SCALE = HEAD_DIM ** -0.5 * LOG2E

LANES = 128
VMEM_LIMIT_BYTES = 56 * 1024 * 1024

PROJ_TM = 512
MID_TM = 1024
RET_CHUNK = 128
RET_UNROLL = 16
NA_ROWS = 4
NA_KEY_ROWS = NA_ROWS + NA_WIN_ROWS
NA_STEP_BLOCKS = 4
MOE_TM = 2048
MOE_SUB = 256
MOE_EB = 4
MOE_SLOTS = MOE_SUB // MOE_EB
MOE_SLOTS_SMALL = 48
MOE_NSUB = MOE_TM // MOE_SUB


def _cparams(sem):
    return pltpu.CompilerParams(dimension_semantics=sem, vmem_limit_bytes=VMEM_LIMIT_BYTES)


def _dot(a, b):
    return jnp.dot(a, b, preferred_element_type=F32)


def _dot_nt(a, b):
    return lax.dot_general(a, b, (((1,), (1,)), ((), ())), preferred_element_type=F32)


def _dot_tn(a, b):
    return lax.dot_general(a, b, (((0,), (0,)), ((), ())), preferred_element_type=F32)


def _cast_once(src_ref, dst_ref):
    @pl.when(pl.program_id(0) == 0)
    def _():
        dst_ref[...] = src_ref[...].astype(BF16)


def _resident(shape):
    return pl.BlockSpec(shape, lambda i: (0,) * len(shape), pipeline_mode=pl.Buffered(1))


def _in_proj_kernel(x_ref, wf_ref, cos_ref, sin_ref,
                    rq_ref, rk_ref, rv_ref, rg_ref, nq_ref, nk_ref, nv_ref, w_ref):
    _cast_once(wf_ref, w_ref)
    xb = x_ref[...].astype(BF16)
    cos = cos_ref[...]
    sin = sin_ref[...]
    lane = lax.broadcasted_iota(I32, (1, LANES), 1)
    first_half = (lane % HEAD_DIM) < (HEAD_DIM // 2)

    def proj(g):
        return _dot(xb, w_ref[:, g * GROUP_W:(g + 1) * GROUP_W])

    def rotary(t, scale):
        outs = []
        for j in range(GROUP_W // LANES):
            c = t[:, j * LANES:(j + 1) * LANES]
            swapped = jnp.where(first_half,
                                pltpu.roll(c, LANES - HEAD_DIM // 2, axis=1),
                                pltpu.roll(c, HEAD_DIM // 2, axis=1))
            outs.append((c * cos + swapped * sin) * scale)
        return jnp.concatenate(outs, axis=1)

    rq_ref[...] = rotary(proj(0), 1.0).astype(BF16)
    rk_ref[...] = rotary(proj(1), HEAD_DIM ** -0.5).astype(BF16)
    rv_ref[...] = proj(2).astype(BF16)
    rg_ref[...] = jax.nn.silu(proj(3)).astype(BF16)
    nq_ref[...] = (proj(4) * NA_Q_SCALE).astype(BF16)
    nk_ref[...] = proj(5).astype(BF16)
    nv_ref[...] = proj(6).astype(BF16)


def _in_proj(x2, w_in, cos_t, sin_t, seq):
    t = x2.shape[0]
    tm = PROJ_TM
    n_pos = seq // tm
    out = jax.ShapeDtypeStruct((t, GROUP_W), BF16)
    tok = lambda i: (i, 0)
    return pl.pallas_call(
        _in_proj_kernel,
        grid=(t // tm,),
        in_specs=[
            pl.BlockSpec((tm, D_MODEL), tok),
            _resident(w_in.shape),
            pl.BlockSpec((tm, LANES), lambda i: (i % n_pos, 0)),
            pl.BlockSpec((tm, LANES), lambda i: (i % n_pos, 0)),
        ],
        out_specs=[pl.BlockSpec((tm, GROUP_W), tok)] * 7,
        out_shape=[out] * 7,
        scratch_shapes=[pltpu.VMEM(w_in.shape, BF16)],
        compiler_params=_cparams(("arbitrary",)),
        name="in_proj",
    )(x2, w_in.astype(F32), cos_t, sin_t)


def _log_sigmoid(x):
    return jnp.minimum(x, 0.0) - jnp.log1p(jnp.exp(-jnp.abs(x)))


def _retention_kernel(q_ref, k_ref, v_ref, g_ref, decf_ref, decb_ref, decfd_ref, decbd_ref,
                      gain_ref, o_ref, kv_ref, st_ref, dmat_ref):
    c = RET_CHUNK
    n_chunks = q_ref.shape[0] // c
    pair_w = 2 * HEAD_DIM

    lgf = _log_sigmoid(decf_ref[...])
    lgb = _log_sigmoid(decb_ref[...])
    row = lax.broadcasted_iota(I32, (c, 1), 0).astype(F32)
    k_dec_f = jnp.exp((c - 1.0 - row) * lgf)
    k_dec_b = jnp.exp(row * lgb)
    q_dec_f = jnp.exp((row + 1.0) * lgf)
    q_dec_b = jnp.exp((c - row) * lgb)
    chunk_dec_f = jnp.exp(c * lgf)
    chunk_dec_b = jnp.exp(c * lgb)

    lane = lax.broadcasted_iota(I32, (1, pair_w), 1)
    head0 = lane < HEAD_DIM
    r2 = lax.broadcasted_iota(I32, (pair_w, pair_w), 0) // HEAD_DIM
    c2 = lax.broadcasted_iota(I32, (pair_w, pair_w), 1) // HEAD_DIM
    same_head = r2 == c2
    block_diag = jnp.where(same_head, 1.0, 0.0)
    seg_avg = jnp.where(same_head, 1.0 / HEAD_DIM, 0.0).astype(BF16)

    lgf_d = _log_sigmoid(decfd_ref[0])
    lgb_d = _log_sigmoid(decbd_ref[0])
    di = lax.broadcasted_iota(I32, (c, 2 * c), 0)
    dj = lax.broadcasted_iota(I32, (c, 2 * c), 1) % c
    diff = (di - dj).astype(F32)
    dmat_ref[...] = jnp.where(diff >= 0.0, jnp.exp(diff * lgf_d), jnp.exp(-diff * lgb_d))

    def chunk(ref, n):
        return ref[pl.ds(pl.multiple_of(n * c, c), c), :]

    unroll = RET_UNROLL

    def summarize(nb, carry):
        for u in range(unroll):
            n = nb * unroll + u
            kf = chunk(k_ref, n).astype(F32)
            kst = jnp.concatenate([kf * k_dec_f, kf * k_dec_b], axis=1).astype(BF16)
            kv_ref[n] = _dot_tn(kst, chunk(v_ref, n))
        return carry

    lax.fori_loop(0, n_chunks // unroll, summarize, 0)

    def fwd_scan(n, state):
        st_ref[n, 0:pair_w, :] = state.astype(BF16)
        return state * chunk_dec_f + kv_ref[n, 0:pair_w, :] * block_diag

    lax.fori_loop(0, n_chunks, fwd_scan, jnp.zeros((pair_w, pair_w), F32))

    def bwd_scan(i, state):
        n = n_chunks - 1 - i
        st_ref[n, pair_w:2 * pair_w, :] = state.astype(BF16)
        return state * chunk_dec_b + kv_ref[n, pair_w:2 * pair_w, :] * block_diag

    lax.fori_loop(0, n_chunks, bwd_scan, jnp.zeros((pair_w, pair_w), F32))

    gain = gain_ref[...]

    seg_avg2 = jnp.concatenate([seg_avg, seg_avg], axis=0)

    def seg_mean(z):
        hi = z.astype(BF16)
        lo = (z - hi.astype(F32)).astype(BF16)
        return _dot(jnp.concatenate([hi, lo], axis=1), seg_avg2)

    def emit(nb, carry):
        ys = []
        for u in range(unroll):
            n = nb * unroll + u
            q = chunk(q_ref, n)
            k = chunk(k_ref, n)
            v = chunk(v_ref, n)
            zero = jnp.zeros_like(k)
            k_st = jnp.concatenate([jnp.where(head0, k, zero), jnp.where(head0, zero, k)], axis=0)
            v_st = jnp.concatenate([jnp.where(head0, v, zero), jnp.where(head0, zero, v)], axis=0)
            scores = _dot_nt(q, k_st) * dmat_ref[...]
            qf = q.astype(F32)
            q_st = jnp.concatenate([qf * q_dec_f, qf * q_dec_b], axis=1).astype(BF16)
            ys.append(_dot(scores.astype(BF16), v_st) + _dot(q_st, st_ref[n]))
        y = jnp.concatenate(ys, axis=0)
        mu = seg_mean(y)
        d = y - mu
        var = seg_mean(d * d)
        yn = d * lax.rsqrt(var + GN_EPS) * gain
        rows = pl.ds(pl.multiple_of(nb * (unroll * c), unroll * c), unroll * c)
        o_ref[rows, :] = (g_ref[rows, :].astype(F32) * yn).astype(BF16)
        return carry

    lax.fori_loop(0, n_chunks // unroll, emit, 0)


def _retention(rq, rk, rv, rg, dec_f, dec_b, gain, batch, seq):
    t = rq.shape[0]
    c = RET_CHUNK
    n_pairs = HEADS // 2
    pair_w = 2 * HEAD_DIM
    dec_f_lane = jnp.repeat(dec_f.astype(F32), HEAD_DIM)[None, :]
    dec_b_lane = jnp.repeat(dec_b.astype(F32), HEAD_DIM)[None, :]
    dec_f_col = jnp.repeat(dec_f.astype(F32), c).reshape(n_pairs, 1, 2 * c)
    dec_b_col = jnp.repeat(dec_b.astype(F32), c).reshape(n_pairs, 1, 2 * c)
    tok = pl.BlockSpec((seq, pair_w), lambda b, p: (b, p))
    lane_spec = pl.BlockSpec((1, pair_w), lambda b, p: (0, p))
    col_spec = pl.BlockSpec((1, 1, 2 * c), lambda b, p: (p, 0, 0))
    return pl.pallas_call(
        _retention_kernel,
        grid=(batch, n_pairs),
        in_specs=[tok, tok, tok, tok, lane_spec, lane_spec, col_spec, col_spec, lane_spec],
        out_specs=tok,
        out_shape=jax.ShapeDtypeStruct((t, GROUP_W), BF16),
        scratch_shapes=[
            pltpu.VMEM((seq // c, 2 * pair_w, pair_w), F32),
            pltpu.VMEM((seq // c, 2 * pair_w, pair_w), BF16),
            pltpu.VMEM((c, 2 * c), F32),
        ],
        compiler_params=_cparams(("parallel", "parallel")),
        name="retention",
    )(rq, rk, rv, rg, dec_f_lane, dec_b_lane, dec_f_col, dec_b_col, gain.astype(F32)[None, :])


N_ROW_OFFSETS = 2 * NA_WIN_ROWS - 1
N_COL_OFFSETS = 2 * NA_WIN_COLS - 1


def _natten_row_offsets(rows):
    n_blocks = rows // NA_ROWS
    starts = {0: 0, 1: NA_ROWS - NA_WIN_ROWS // 2, 2: rows - NA_KEY_ROWS}
    blocks = {0: 0, 1: 1, 2: n_blocks - 1}
    table = []
    for v in range(3):
        per_a = []
        for a in range(NA_ROWS):
            r = blocks[v] * NA_ROWS + a
            rs = min(max(r - NA_WIN_ROWS // 2, 0), rows - NA_WIN_ROWS)
            per_kl = []
            for kl in range(NA_KEY_ROWS):
                kr = starts[v] + kl
                per_kl.append(kr - r + NA_WIN_ROWS - 1 if rs <= kr < rs + NA_WIN_ROWS else None)
            per_a.append(per_kl)
        table.append(per_a)
    return table


def _natten_live_columns(rows):
    offsets = _natten_row_offsets(rows)
    live = []
    for blk in range(NA_STEP_BLOCKS):
        variants = [1] + ([0] if blk == 0 else []) + ([2] if blk == NA_STEP_BLOCKS - 1 else [])
        live.append([[j for j in range(NA_KEY_ROWS // 2)
                      if any(offsets[v][a][kl] is not None for v in variants for kl in (2 * j, 2 * j + 1))]
                     for a in range(NA_ROWS)])
    return live


def _natten_bias_kernel(rpb_ref, tab_ref, *, rows):
    offsets = _natten_row_offsets(rows)
    nk = NA_KEY_ROWS * GRID_W
    shape = (GRID_W, LANES)
    lane = lax.broadcasted_iota(I32, shape, 1)
    c = lax.broadcasted_iota(I32, shape, 0)
    second = lane >= GRID_W
    kc = lane % GRID_W
    cs = jnp.clip(c - NA_WIN_COLS // 2, 0, GRID_W - NA_WIN_COLS)
    col_ok = jnp.logical_and(kc >= cs, kc < cs + NA_WIN_COLS)
    neg = jnp.full(shape, NEG_BIG, F32)
    for hh in range(2):
        toeplitz = []
        for dr in range(N_ROW_OFFSETS):
            x = jnp.broadcast_to(rpb_ref[hh, dr:dr + 1, :] * LOG2E, shape)
            lo = pltpu.roll(x, LANES - (NA_WIN_COLS - 1), axis=1, stride=1, stride_axis=0)
            hi = pltpu.roll(x, GRID_W - (NA_WIN_COLS - 1), axis=1, stride=1, stride_axis=0)
            toeplitz.append(jnp.where(second, hi, lo))
        for v in range(3):
            for a in range(NA_ROWS):
                for j in range(NA_KEY_ROWS // 2):
                    d0, d1 = offsets[v][a][2 * j], offsets[v][a][2 * j + 1]
                    if d0 is None and d1 is None:
                        piece = neg
                    else:
                        t0 = neg if d0 is None else toeplitz[d0]
                        t1 = neg if d1 is None else toeplitz[d1]
                        piece = jnp.where(col_ok, jnp.where(second, t1, t0), neg)
                    tab_ref[v, 0, a * GRID_W:(a + 1) * GRID_W,
                            hh * nk + j * LANES:hh * nk + (j + 1) * LANES] = piece


def _natten_bias_table(rpb, rows):
    n_pairs = HEADS // 2
    nqb = NA_ROWS * GRID_W
    nkb = NA_KEY_ROWS * GRID_W
    rpb_pad = jnp.pad(rpb.astype(F32), ((0, 0), (0, 0), (0, LANES - N_COL_OFFSETS)))
    return pl.pallas_call(
        functools.partial(_natten_bias_kernel, rows=rows),
        grid=(n_pairs,),
        in_specs=[pl.BlockSpec((2, N_ROW_OFFSETS, LANES), lambda p: (p, 0, 0))],
        out_specs=pl.BlockSpec((3, 1, nqb, 2 * nkb), lambda p: (0, p, 0, 0)),
        out_shape=jax.ShapeDtypeStruct((3, n_pairs, nqb, 2 * nkb), F32),
        compiler_params=_cparams(("parallel",)),
        name="natten_bias",
    )(rpb_pad)


def _natten_kernel(q_ref, k_ref, v_ref, tab_ref, wgu_f, wd_f, o_ref, wgu_b, wd_b, ks_ref, vs_ref, *, rows):
    wgu_b[...] = wgu_f[...].astype(BF16)
    wd_b[...] = wd_f[...].astype(BF16)
    nq = NA_ROWS * GRID_W
    nk = NA_KEY_ROWS * GRID_W
    n_blocks = rows // NA_ROWS
    live = _natten_live_columns(rows)
    lane = lax.broadcasted_iota(I32, (1, 2 * HEAD_DIM), 1)
    head0 = lane < HEAD_DIM

    @pl.when(pl.program_id(2) == 0)
    def _():
        k = k_ref[...]
        v = v_ref[...]
        zero = jnp.zeros_like(k)
        ks_ref[0] = jnp.where(head0, k, zero)
        ks_ref[1] = jnp.where(head0, zero, k)
        ind0 = jnp.broadcast_to(jnp.where(head0, 1.0, 0.0).astype(BF16), k.shape)
        ind1 = jnp.broadcast_to(jnp.where(head0, 0.0, 1.0).astype(BF16), k.shape)
        vs_ref[0] = jnp.concatenate([jnp.where(head0, v, zero), ind0], axis=1)
        vs_ref[1] = jnp.concatenate([jnp.where(head0, zero, v), ind1], axis=1)

    for blk in range(NA_STEP_BLOCKS):
        rb = pl.program_id(2) * NA_STEP_BLOCKS + blk
        variant = jnp.where(rb == 0, 0, jnp.where(rb == n_blocks - 1, 2, 1))
        start_row = jnp.clip(rb * NA_ROWS - NA_WIN_ROWS // 2, 0, rows - NA_KEY_ROWS)
        win = pl.ds(pl.multiple_of(start_row * GRID_W, GRID_W), nk)
        q = q_ref[blk * nq:(blk + 1) * nq, :]
        k_st = jnp.concatenate([ks_ref[0, win, :], ks_ref[1, win, :]], axis=0)
        v_st = jnp.concatenate([vs_ref[0, win, :], vs_ref[1, win, :]], axis=0)
        s = _dot_nt(q, k_st) + tab_ref[variant, 0]
        dead = jnp.zeros((GRID_W, LANES), BF16)
        slabs = []
        for a in range(NA_ROWS):
            qrows = slice(a * GRID_W, (a + 1) * GRID_W)
            parts = []
            for h in range(2):
                cols = {j: s[qrows, h * nk + j * LANES:h * nk + (j + 1) * LANES] for j in live[blk][a]}
                m = jnp.max(jnp.concatenate(list(cols.values()), axis=1), axis=1, keepdims=True)
                parts += [jnp.exp2(cols[j] - m).astype(BF16) if j in cols else dead
                          for j in range(NA_KEY_ROWS // 2)]
            slabs.append(jnp.concatenate(parts, axis=1))
        out = _dot(jnp.concatenate(slabs, axis=0), v_st)
        o_ref[blk * nq:(blk + 1) * nq, :] = (out[:, :2 * HEAD_DIM] / out[:, 2 * HEAD_DIM:]).astype(BF16)


def _natten(nq, nk, nv, rpb, w_gu, w_down, batch, seq):
    t = nq.shape[0]
    rows = seq // GRID_W
    n_pairs = HEADS // 2
    n_steps = rows // (NA_ROWS * NA_STEP_BLOCKS)
    nqb = NA_ROWS * GRID_W
    nkb = NA_KEY_ROWS * GRID_W
    tab = _natten_bias_table(rpb, rows)
    total_steps = batch * n_pairs * n_steps
    n_exp = w_gu.shape[0]
    assert n_exp % total_steps == 0
    epb = n_exp // total_steps

    def expert_spec(shape):
        return pl.BlockSpec((epb,) + shape[1:], lambda b, p, s: ((b * n_pairs + p) * n_steps + s, 0, 0))

    tab_spec = pl.BlockSpec((3, 1, nqb, 2 * nkb), lambda b, p, s: (0, p, 0, 0))
    kv_spec = pl.BlockSpec((seq, 2 * HEAD_DIM), lambda b, p, s: (b, p))
    q_spec = pl.BlockSpec((NA_STEP_BLOCKS * nqb, 2 * HEAD_DIM), lambda b, p, s: (b * n_steps + s, p))
    return pl.pallas_call(
        functools.partial(_natten_kernel, rows=rows),
        grid=(batch, n_pairs, n_steps),
        in_specs=[q_spec, kv_spec, kv_spec, tab_spec, expert_spec(w_gu.shape), expert_spec(w_down.shape)],
        out_specs=[q_spec, expert_spec(w_gu.shape), expert_spec(w_down.shape)],
        out_shape=[jax.ShapeDtypeStruct((t, GROUP_W), BF16),
                   jax.ShapeDtypeStruct(w_gu.shape, BF16), jax.ShapeDtypeStruct(w_down.shape, BF16)],
        scratch_shapes=[pltpu.VMEM((2, seq, 2 * HEAD_DIM), BF16), pltpu.VMEM((2, seq, 4 * HEAD_DIM), BF16)],
        compiler_params=_cparams(("arbitrary", "arbitrary", "arbitrary")),
        name="natten",
    )(nq, nk, nv, tab, w_gu.astype(F32), w_down.astype(F32))


def _layer_norm(h, gain, bias):
    mu = jnp.mean(h, axis=-1, keepdims=True)
    d = h - mu
    var = jnp.mean(d * d, axis=-1, keepdims=True)
    return d * lax.rsqrt(var + LN_EPS) * gain + bias


def _route_tile(xb, wr, rbias, tri):
    tm = xb.shape[0]
    scores = jax.nn.sigmoid(_dot_nt(wr, xb))
    biased = scores + rbias
    sub = lax.broadcasted_iota(I32, (GROUP_SIZE, tm), 0).astype(F32)
    none = float(N_EXPERTS)
    ninf = -jnp.inf

    def first_max(vals, index):
        m = jnp.max(vals, axis=0, keepdims=True)
        return m, jnp.min(jnp.where(vals == m, index, none), axis=0, keepdims=True)

    groups = [biased[g * GROUP_SIZE:(g + 1) * GROUP_SIZE, :] for g in range(N_GROUPS)]
    group_scores = []
    for g in range(N_GROUPS):
        m1, i1 = first_max(groups[g], sub)
        m2 = jnp.max(jnp.where(sub == i1, ninf, groups[g]), axis=0, keepdims=True)
        group_scores.append(m1 + m2)
    cur = jnp.concatenate(group_scores, axis=0)
    group_sel = jnp.zeros(cur.shape, F32)
    for _ in range(TOPK_GROUPS):
        _, i1 = first_max(cur, sub)
        hit = sub == i1
        group_sel = jnp.where(hit, 1.0, group_sel)
        cur = jnp.where(hit, ninf, cur)

    masked = [jnp.where(group_sel[g:g + 1, :] > 0.5, groups[g], ninf) for g in range(N_GROUPS)]
    ids = [sub + float(g * GROUP_SIZE) for g in range(N_GROUPS)]
    chosen = [jnp.zeros((GROUP_SIZE, tm), F32) for _ in range(N_GROUPS)]
    for _ in range(TOP_K):
        m = masked[0]
        for g in range(1, N_GROUPS):
            m = jnp.maximum(m, masked[g])
        m = jnp.max(m, axis=0, keepdims=True)
        cand = jnp.where(masked[0] == m, ids[0], none)
        for g in range(1, N_GROUPS):
            cand = jnp.minimum(cand, jnp.where(masked[g] == m, ids[g], none))
        first = jnp.min(cand, axis=0, keepdims=True)
        for g in range(N_GROUPS):
            hit = ids[g] == first
            chosen[g] = jnp.where(hit, 1.0, chosen[g])
            masked[g] = jnp.where(hit, ninf, masked[g])

    sel = jnp.concatenate(chosen, axis=0) > 0.5
    picked = jnp.where(sel, scores, 0.0)
    total = jnp.sum(picked, axis=0, keepdims=True)
    weight = picked / total * ROUTED_SCALE
    sel_f = jnp.where(sel, 1.0, 0.0)
    sel_b = sel_f.astype(BF16)
    before, cnt_max = [], None
    for k in range(tm // MOE_SUB):
        cols = slice(k * MOE_SUB, (k + 1) * MOE_SUB)
        before.append(_dot(sel_b[:, cols], tri))
        cnt = jnp.sum(sel_f[:, cols], axis=1, keepdims=True)
        cnt_max = cnt if cnt_max is None else jnp.maximum(cnt_max, cnt)
    rank = jnp.where(sel, jnp.concatenate(before, axis=1).astype(I32), -1)
    return weight, rank, cnt_max


def _mid_kernel(ret_ref, na_ref, x_ref, p_ref, wo_f, wsgu_f, wsd_f, wp_f, wg_f, gain_ref, bias_ref,
                wr_ref, rb_ref, pre_ref, x1b_ref, w_ref, rank_ref, cnt_ref,
                wo_ref, wsgu_ref, wsd_ref, wp_ref, wg_ref, tri_ref):
    for src_ref, dst_ref in ((wo_f, wo_ref), (wsgu_f, wsgu_ref), (wsd_f, wsd_ref), (wp_f, wp_ref),
                             (wg_f, wg_ref)):
        _cast_once(src_ref, dst_ref)

    @pl.when(pl.program_id(0) == 0)
    def _():
        i = lax.broadcasted_iota(I32, (MOE_SUB, MOE_SUB), 0)
        j = lax.broadcasted_iota(I32, (MOE_SUB, MOE_SUB), 1)
        tri_ref[...] = jnp.where(i < j, 1.0, 0.0).astype(BF16)

    for r in range(0, x_ref.shape[0], MOE_SUB):
        rows = slice(r, r + MOE_SUB)
        mix = (_dot(ret_ref[rows, :], wo_ref[0:GROUP_W, :])
               + _dot(na_ref[rows, :], wo_ref[GROUP_W:2 * GROUP_W, :]))
        x1 = _layer_norm(ALPHA * x_ref[rows, :] + mix, gain_ref[...], bias_ref[...])
        xb = x1.astype(BF16)
        x1b_ref[rows, :] = xb
        h = _dot(xb, wsgu_ref[...])
        act = (jax.nn.silu(h[:, :EXPERT_DIM]) * h[:, EXPERT_DIM:]).astype(BF16)
        shared = _dot(act, wsd_ref[...])
        ple = _dot(p_ref[rows, :].astype(BF16), wp_ref[...]) * jax.nn.sigmoid(_dot(xb, wg_ref[...]))
        pre_ref[rows, :] = ALPHA * x1 + shared + ple
    weight, rank, cnt_max = _route_tile(x1b_ref[...], wr_ref[...], rb_ref[...], tri_ref[...])
    w_ref[...] = weight
    rank_ref[...] = rank
    cnt_ref[...] = jnp.broadcast_to(cnt_max, cnt_ref.shape).astype(I32)


def _mid(ret, na, x2, p2, w_out, gain, bias, w_router, router_bias, wsgu, wsd, wp, wg):
    t = x2.shape[0]
    tm = MID_TM
    nt = t // tm
    tok = lambda i: (i, 0)
    const = lambda i: (0, 0)
    col = lambda i: (0, i)
    weights = [w.astype(F32) for w in (w_out, wsgu, wsd, wp, wg)]
    wr_t = w_router.astype(F32).T.astype(BF16)
    return pl.pallas_call(
        _mid_kernel,
        grid=(nt,),
        in_specs=[pl.BlockSpec((tm, GROUP_W), tok), pl.BlockSpec((tm, GROUP_W), tok),
                  pl.BlockSpec((tm, D_MODEL), tok), pl.BlockSpec((tm, p2.shape[1]), tok)]
                 + [_resident(w.shape) for w in weights]
                 + [pl.BlockSpec((1, D_MODEL), const), pl.BlockSpec((1, D_MODEL), const),
                    pl.BlockSpec((N_EXPERTS, D_MODEL), const), pl.BlockSpec((N_EXPERTS, 1), const)],
        out_specs=[pl.BlockSpec((tm, D_MODEL), tok), pl.BlockSpec((tm, D_MODEL), tok),
                   pl.BlockSpec((N_EXPERTS, tm), col), pl.BlockSpec((N_EXPERTS, tm), col),
                   pl.BlockSpec((N_EXPERTS, LANES), col)],
        out_shape=[jax.ShapeDtypeStruct((t, D_MODEL), F32), jax.ShapeDtypeStruct((t, D_MODEL), BF16),
                   jax.ShapeDtypeStruct((N_EXPERTS, t), F32), jax.ShapeDtypeStruct((N_EXPERTS, t), I32),
                   jax.ShapeDtypeStruct((N_EXPERTS, nt * LANES), I32)],
        scratch_shapes=[pltpu.VMEM(w.shape, BF16) for w in weights] + [pltpu.VMEM((MOE_SUB, MOE_SUB), BF16)],
        compiler_params=_cparams(("arbitrary",)),
        name="mid",
    )(ret, na, x2, p2, *weights, gain.astype(F32)[None, :], bias.astype(F32)[None, :],
      wr_t, router_bias.astype(F32)[:, None])


def _moe_kernel(cnt_ref, x_ref, rank_ref, w_ref, wgu_ref, wd_ref, pre_ref, gain_ref, bias_ref,
                o_ref, oh_ref, g_ref, y_ref):
    i = pl.program_id(0)
    eb = pl.program_id(1)
    n_eb = pl.num_programs(1)

    @pl.when(eb == 0)
    def _():
        o_ref[...] = jnp.zeros_like(o_ref)

    pre_rows = pre_ref.shape[0]
    o_ref[pl.ds(pl.multiple_of(eb * pre_rows, pre_rows), pre_rows), :] += pre_ref[...]

    max_count = cnt_ref[eb, i]

    def one_pass(p, slots):
        n_rows = MOE_EB * slots
        base = p * slots
        slot = lax.broadcasted_iota(I32, (slots, MOE_SUB), 0)
        slot_w = []
        for k in range(MOE_NSUB):
            tok = slice(k * MOE_SUB, (k + 1) * MOE_SUB)
            blocks, weights = [], []
            for j in range(MOE_EB):
                expert = pl.ds(eb * MOE_EB + j, 1)
                match = slot == (rank_ref[expert, tok] - base)
                blocks.append(jnp.where(match, 1.0, 0.0).astype(BF16))
                weights.append(jnp.sum(jnp.where(match, w_ref[expert, tok], 0.0), axis=1, keepdims=True))
            onehot = jnp.concatenate(blocks, axis=0)
            oh_ref[k, 0:n_rows, :] = onehot
            slot_w.append(weights)
            g_ref[k, 0:n_rows, :] = _dot(onehot, x_ref[tok, :]).astype(BF16)
        for j in range(MOE_EB):
            rows = slice(j * slots, (j + 1) * slots)
            xe = jnp.concatenate([g_ref[k, rows, :] for k in range(MOE_NSUB)], axis=0)
            h = _dot(xe, wgu_ref[j])
            act = (jax.nn.silu(h[:, :EXPERT_DIM]) * h[:, EXPERT_DIM:]).astype(BF16)
            wc = jnp.concatenate([slot_w[k][j] for k in range(MOE_NSUB)], axis=0)
            yw = (_dot(act, wd_ref[j]) * wc).astype(BF16)
            for k in range(MOE_NSUB):
                y_ref[k, rows, :] = yw[k * slots:(k + 1) * slots, :]
        for k in range(MOE_NSUB):
            tok = slice(k * MOE_SUB, (k + 1) * MOE_SUB)
            o_ref[tok, :] += _dot_tn(oh_ref[k, 0:n_rows, :], y_ref[k, 0:n_rows, :])

    @pl.when(max_count <= MOE_SLOTS_SMALL)
    def _():
        one_pass(0, MOE_SLOTS_SMALL)

    @pl.when(max_count > MOE_SLOTS_SMALL)
    def _():
        n_pass = lax.shift_right_logical(max_count + (MOE_SLOTS - 1), int(np.log2(MOE_SLOTS)))

        def body(p, carry):
            one_pass(p, MOE_SLOTS)
            return carry

        lax.fori_loop(0, n_pass, body, 0)

    @pl.when(eb == n_eb - 1)
    def _():
        for r in range(0, o_ref.shape[0], MOE_SUB):
            rows = slice(r, r + MOE_SUB)
            o_ref[rows, :] = _layer_norm(o_ref[rows, :], gain_ref[...], bias_ref[...])


def _moe(x1b, rank_t, w_t, counts, wgu_b, wd_b, pre, gain, bias):
    t = x1b.shape[0]
    tm = MOE_TM
    nt = t // tm
    n_eb = N_EXPERTS // MOE_EB
    pre_rows = tm // n_eb
    counts = counts.reshape(N_EXPERTS // MOE_EB, MOE_EB, nt, tm // MID_TM).max(axis=(1, 3))
    grid_spec = pltpu.PrefetchScalarGridSpec(
        num_scalar_prefetch=1,
        grid=(nt, N_EXPERTS // MOE_EB),
        in_specs=[
            pl.BlockSpec((tm, D_MODEL), lambda i, e, c: (i, 0)),
            pl.BlockSpec((N_EXPERTS, tm), lambda i, e, c: (0, i)),
            pl.BlockSpec((N_EXPERTS, tm), lambda i, e, c: (0, i)),
            pl.BlockSpec((MOE_EB, D_MODEL, 2 * EXPERT_DIM), lambda i, e, c: (e, 0, 0)),
            pl.BlockSpec((MOE_EB, EXPERT_DIM, D_MODEL), lambda i, e, c: (e, 0, 0)),
            pl.BlockSpec((pre_rows, D_MODEL), lambda i, e, c: (i * n_eb + e, 0)),
            pl.BlockSpec((1, D_MODEL), lambda i, e, c: (0, 0)),
            pl.BlockSpec((1, D_MODEL), lambda i, e, c: (0, 0)),
        ],
        out_specs=pl.BlockSpec((tm, D_MODEL), lambda i, e, c: (i, 0)),
        scratch_shapes=[pltpu.VMEM((MOE_NSUB, MOE_SUB, MOE_SUB), BF16),
                        pltpu.VMEM((MOE_NSUB, MOE_SUB, D_MODEL), BF16),
                        pltpu.VMEM((MOE_NSUB, MOE_SUB, D_MODEL), BF16)],
    )
    return pl.pallas_call(
        _moe_kernel,
        grid_spec=grid_spec,
        out_shape=jax.ShapeDtypeStruct((t, D_MODEL), F32),
        compiler_params=_cparams(("parallel", "arbitrary")),
        name="moe",
    )(counts, x1b, rank_t, w_t, wgu_b, wd_b, pre, gain.astype(F32)[None, :], bias.astype(F32)[None, :])


def _rotary_tables(seq):
    half = HEAD_DIM // 2
    inv = ROPE_BASE ** (-jnp.arange(half, dtype=F32) / half)
    ang = jnp.arange(seq, dtype=jnp.int32).astype(F32)[:, None] * inv[None, :]
    cos, sin = jnp.cos(ang), jnp.sin(ang)
    reps = LANES // HEAD_DIM
    cos_t = jnp.tile(jnp.concatenate([cos, cos], axis=1), (1, reps))
    sin_t = jnp.tile(jnp.concatenate([-sin, sin], axis=1), (1, reps))
    return cos_t, sin_t


def kernel(x, p, w_in, ret_decay_fwd, ret_decay_bwd, ret_gn_gain, na_rpb, w_out, ln1_gain, ln1_bias,
           w_router, router_bias, w_expert_gu, w_expert_down, w_shared_gu, w_shared_down,
           w_ple_proj, w_ple_gate, ln2_gain, ln2_bias):
    batch, seq, d = x.shape
    t = batch * seq
    depth = w_in.shape[0]
    assert depth == 1 and d == D_MODEL
    assert seq % PROJ_TM == 0 and seq % MOE_TM == 0 and seq % RET_CHUNK == 0
    assert MOE_TM % (N_EXPERTS // MOE_EB) == 0
    assert seq % MID_TM == 0 and MOE_TM % MID_TM == 0 and MID_TM % MOE_SUB == 0
    assert (seq // GRID_W) % (NA_ROWS * NA_STEP_BLOCKS) == 0 and seq // GRID_W >= NA_KEY_ROWS
    cos_t, sin_t = _rotary_tables(seq)
    x2 = x.reshape(t, d)
    for i in range(depth):
        rq, rk, rv, rg, nq, nk, nv = _in_proj(x2, w_in[i], cos_t, sin_t, seq)
        ret = _retention(rq, rk, rv, rg, ret_decay_fwd[i], ret_decay_bwd[i], ret_gn_gain[i], batch, seq)
        na, wgu_b, wd_b = _natten(nq, nk, nv, na_rpb[i], w_expert_gu[i], w_expert_down[i], batch, seq)
        pre, x1b, w_t, rank_t, cnt = _mid(ret, na, x2, p[i].reshape(t, -1), w_out[i], ln1_gain[i], ln1_bias[i],
                                          w_router[i], router_bias[i], w_shared_gu[i], w_shared_down[i],
                                          w_ple_proj[i], w_ple_gate[i])
        x2 = _moe(x1b, rank_t, w_t, cnt[:, ::LANES], wgu_b, wd_b, pre, ln2_gain[i], ln2_bias[i])
    return x2.reshape(batch, seq, d)
```

```python
import functools

import numpy as np
import jax
import jax.numpy as jnp
from jax import lax
from jax.experimental import pallas as pl
from jax.experimental.pallas import tpu as pltpu

F32 = jnp.float32
BF16 = jnp.bfloat16
I32 = jnp.int32

D_MODEL = 1024
HEADS = 8
HEAD_DIM = 64
GROUP_W = HEADS * HEAD_DIM
ROPE_BASE = 10000.0
GN_EPS = 1e-6
LN_EPS = 1e-5
GRID_W = 64
NA_WIN_ROWS = 8
NA_WIN_COLS = 16
N_EXPERTS = 64
N_GROUPS = 8
GROUP_SIZE = N_EXPERTS // N_GROUPS
TOPK_GROUPS = 4
TOP_K = 8
EXPERT_DIM = 256
ROUTED_SCALE = 2.5
ALPHA = 2.0 ** 0.25
NEG_BIG = -1e30
LOG2E = 1.4426950408889634
NA_Q_SCALE = HEAD_DIM ** -0.5 * LOG2E

LANES = 128
VMEM_LIMIT_BYTES = 56 * 1024 * 1024

PROJ_TM = 512
MID_TM = 1024
RET_CHUNK = 128
RET_UNROLL = 16
NA_ROWS = 4
NA_KEY_ROWS = NA_ROWS + NA_WIN_ROWS
NA_STEP_BLOCKS = 8
MOE_TM = 2048
MOE_SUB = 256
MOE_EB = 4
MOE_SLOTS = MOE_SUB // MOE_EB
MOE_SLOTS_SMALL = 48
MOE_NSUB = MOE_TM // MOE_SUB


def _cparams(sem):
    return pltpu.CompilerParams(dimension_semantics=sem, vmem_limit_bytes=VMEM_LIMIT_BYTES)


def _dot(a, b):
    return jnp.dot(a, b, preferred_element_type=F32)


def _dot_nt(a, b):
    return lax.dot_general(a, b, (((1,), (1,)), ((), ())), preferred_element_type=F32)


def _dot_tn(a, b):
    return lax.dot_general(a, b, (((0,), (0,)), ((), ())), preferred_element_type=F32)


def _cast_once(src_ref, dst_ref):
    @pl.when(pl.program_id(0) == 0)
    def _():
        dst_ref[...] = src_ref[...].astype(BF16)


def _resident(shape):
    return pl.BlockSpec(shape, lambda i: (0,) * len(shape), pipeline_mode=pl.Buffered(1))


def _in_proj_kernel(x_ref, wf_ref, cos_ref, sin_ref,
                    rq_ref, rk_ref, rv_ref, rg_ref, nq_ref, nk_ref, nv_ref, w_ref):
    _cast_once(wf_ref, w_ref)
    xb = x_ref[...].astype(BF16)
    cos = cos_ref[...]
    sin = sin_ref[...]
    lane = lax.broadcasted_iota(I32, (1, LANES), 1)
    first_half = (lane % HEAD_DIM) < (HEAD_DIM // 2)

    def proj(g):
        return _dot(xb, w_ref[:, g * GROUP_W:(g + 1) * GROUP_W])

    def rotary(t, scale):
        outs = []
        for j in range(GROUP_W // LANES):
            c = t[:, j * LANES:(j + 1) * LANES]
            swapped = jnp.where(first_half,
                                pltpu.roll(c, LANES - HEAD_DIM // 2, axis=1),
                                pltpu.roll(c, HEAD_DIM // 2, axis=1))
            outs.append((c * cos + swapped * sin) * scale)
        return jnp.concatenate(outs, axis=1)

    rq_ref[...] = rotary(proj(0), 1.0).astype(BF16)
    rk_ref[...] = rotary(proj(1), HEAD_DIM ** -0.5).astype(BF16)
    rv_ref[...] = proj(2).astype(BF16)
    rg_ref[...] = jax.nn.silu(proj(3)).astype(BF16)
    nq_ref[...] = (proj(4) * NA_Q_SCALE).astype(BF16)
    nk_ref[...] = proj(5).astype(BF16)
    nv_ref[...] = proj(6).astype(BF16)


def _in_proj(x2, w_in, cos_t, sin_t, seq):
    t = x2.shape[0]
    tm = PROJ_TM
    n_pos = seq // tm
    out = jax.ShapeDtypeStruct((t, GROUP_W), BF16)
    tok = lambda i: (i, 0)
    return pl.pallas_call(
        _in_proj_kernel,
        grid=(t // tm,),
        in_specs=[
            pl.BlockSpec((tm, D_MODEL), tok),
            _resident(w_in.shape),
            pl.BlockSpec((tm, LANES), lambda i: (i % n_pos, 0)),
            pl.BlockSpec((tm, LANES), lambda i: (i % n_pos, 0)),
        ],
        out_specs=[pl.BlockSpec((tm, GROUP_W), tok)] * 7,
        out_shape=[out] * 7,
        scratch_shapes=[pltpu.VMEM(w_in.shape, BF16)],
        compiler_params=_cparams(("arbitrary",)),
        name="in_proj",
    )(x2, w_in.astype(F32), cos_t, sin_t)


def _log_sigmoid(x):
    return jnp.minimum(x, 0.0) - jnp.log1p(jnp.exp(-jnp.abs(x)))


def _retention_kernel(q_ref, k_ref, v_ref, g_ref, decf_ref, decb_ref, decfd_ref, decbd_ref,
                      gain_ref, o_ref, kv_ref, st_ref, dmat_ref):
    c = RET_CHUNK
    n_chunks = q_ref.shape[0] // c
    pair_w = 2 * HEAD_DIM

    lgf = _log_sigmoid(decf_ref[...])
    lgb = _log_sigmoid(decb_ref[...])
    row = lax.broadcasted_iota(I32, (c, 1), 0).astype(F32)
    k_dec_f = jnp.exp((c - 1.0 - row) * lgf)
    k_dec_b = jnp.exp(row * lgb)
    q_dec_f = jnp.exp((row + 1.0) * lgf)
    q_dec_b = jnp.exp((c - row) * lgb)
    chunk_dec_f = jnp.exp(c * lgf)
    chunk_dec_b = jnp.exp(c * lgb)

    lane = lax.broadcasted_iota(I32, (1, pair_w), 1)
    head0 = lane < HEAD_DIM
    r2 = lax.broadcasted_iota(I32, (pair_w, pair_w), 0) // HEAD_DIM
    c2 = lax.broadcasted_iota(I32, (pair_w, pair_w), 1) // HEAD_DIM
    same_head = r2 == c2
    block_diag = jnp.where(same_head, 1.0, 0.0)
    seg_avg = jnp.where(same_head, 1.0 / HEAD_DIM, 0.0).astype(BF16)

    lgf_d = _log_sigmoid(decfd_ref[0])
    lgb_d = _log_sigmoid(decbd_ref[0])
    di = lax.broadcasted_iota(I32, (c, 2 * c), 0)
    dj = lax.broadcasted_iota(I32, (c, 2 * c), 1) % c
    diff = (di - dj).astype(F32)
    dmat_ref[...] = jnp.where(diff >= 0.0, jnp.exp(diff * lgf_d), jnp.exp(-diff * lgb_d))

    def chunk(ref, n):
        return ref[pl.ds(pl.multiple_of(n * c, c), c), :]

    unroll = RET_UNROLL

    def summarize(nb, carry):
        for u in range(unroll):
            n = nb * unroll + u
            kf = chunk(k_ref, n).astype(F32)
            kst = jnp.concatenate([kf * k_dec_f, kf * k_dec_b], axis=1).astype(BF16)
            kv_ref[n] = _dot_tn(kst, chunk(v_ref, n))
        return carry

    lax.fori_loop(0, n_chunks // unroll, summarize, 0)

    def fwd_scan(n, state):
        st_ref[n, 0:pair_w, :] = state.astype(BF16)
        return state * chunk_dec_f + kv_ref[n, 0:pair_w, :] * block_diag

    lax.fori_loop(0, n_chunks, fwd_scan, jnp.zeros((pair_w, pair_w), F32))

    def bwd_scan(i, state):
        n = n_chunks - 1 - i
        st_ref[n, pair_w:2 * pair_w, :] = state.astype(BF16)
        return state * chunk_dec_b + kv_ref[n, pair_w:2 * pair_w, :] * block_diag

    lax.fori_loop(0, n_chunks, bwd_scan, jnp.zeros((pair_w, pair_w), F32))

    gain = gain_ref[...]

    seg_avg2 = jnp.concatenate([seg_avg, seg_avg], axis=0)

    def seg_mean(z):
        hi = z.astype(BF16)
        lo = (z - hi.astype(F32)).astype(BF16)
        return _dot(jnp.concatenate([hi, lo], axis=1), seg_avg2)

    def emit(nb, carry):
        ys = []
        for u in range(unroll):
            n = nb * unroll + u
            q = chunk(q_ref, n)
            k = chunk(k_ref, n)
            v = chunk(v_ref, n)
            zero = jnp.zeros_like(k)
            k_st = jnp.concatenate([jnp.where(head0, k, zero), jnp.where(head0, zero, k)], axis=0)
            v_st = jnp.concatenate([jnp.where(head0, v, zero), jnp.where(head0, zero, v)], axis=0)
            scores = _dot_nt(q, k_st) * dmat_ref[...]
            qf = q.astype(F32)
            q_st = jnp.concatenate([qf * q_dec_f, qf * q_dec_b], axis=1).astype(BF16)
            ys.append(_dot(scores.astype(BF16), v_st) + _dot(q_st, st_ref[n]))
        y = jnp.concatenate(ys, axis=0)
        mu = seg_mean(y)
        d = y - mu
        var = seg_mean(d * d)
        yn = d * lax.rsqrt(var + GN_EPS) * gain
        rows = pl.ds(pl.multiple_of(nb * (unroll * c), unroll * c), unroll * c)
        o_ref[rows, :] = (g_ref[rows, :].astype(F32) * yn).astype(BF16)
        return carry

    lax.fori_loop(0, n_chunks // unroll, emit, 0)


def _retention(rq, rk, rv, rg, dec_f, dec_b, gain, batch, seq):
    t = rq.shape[0]
    c = RET_CHUNK
    n_pairs = HEADS // 2
    pair_w = 2 * HEAD_DIM
    dec_f_lane = jnp.repeat(dec_f.astype(F32), HEAD_DIM)[None, :]
    dec_b_lane = jnp.repeat(dec_b.astype(F32), HEAD_DIM)[None, :]
    dec_f_col = jnp.repeat(dec_f.astype(F32), c).reshape(n_pairs, 1, 2 * c)
    dec_b_col = jnp.repeat(dec_b.astype(F32), c).reshape(n_pairs, 1, 2 * c)
    tok = pl.BlockSpec((seq, pair_w), lambda b, p: (b, p))
    lane_spec = pl.BlockSpec((1, pair_w), lambda b, p: (0, p))
    col_spec = pl.BlockSpec((1, 1, 2 * c), lambda b, p: (p, 0, 0))
    return pl.pallas_call(
        _retention_kernel,
        grid=(batch, n_pairs),
        in_specs=[tok, tok, tok, tok, lane_spec, lane_spec, col_spec, col_spec, lane_spec],
        out_specs=tok,
        out_shape=jax.ShapeDtypeStruct((t, GROUP_W), BF16),
        scratch_shapes=[
            pltpu.VMEM((seq // c, 2 * pair_w, pair_w), F32),
            pltpu.VMEM((seq // c, 2 * pair_w, pair_w), BF16),
            pltpu.VMEM((c, 2 * c), F32),
        ],
        compiler_params=_cparams(("parallel", "parallel")),
        name="retention",
    )(rq, rk, rv, rg, dec_f_lane, dec_b_lane, dec_f_col, dec_b_col, gain.astype(F32)[None, :])


N_ROW_OFFSETS = 2 * NA_WIN_ROWS - 1
N_COL_OFFSETS = 2 * NA_WIN_COLS - 1


def _natten_row_offsets(rows):
    n_blocks = rows // NA_ROWS
    starts = {0: 0, 1: NA_ROWS - NA_WIN_ROWS // 2, 2: rows - NA_KEY_ROWS}
    blocks = {0: 0, 1: 1, 2: n_blocks - 1}
    table = []
    for v in range(3):
        per_a = []
        for a in range(NA_ROWS):
            r = blocks[v] * NA_ROWS + a
            rs = min(max(r - NA_WIN_ROWS // 2, 0), rows - NA_WIN_ROWS)
            per_kl = []
            for kl in range(NA_KEY_ROWS):
                kr = starts[v] + kl
                per_kl.append(kr - r + NA_WIN_ROWS - 1 if rs <= kr < rs + NA_WIN_ROWS else None)
            per_a.append(per_kl)
        table.append(per_a)
    return table


def _natten_live_columns(rows):
    offsets = _natten_row_offsets(rows)
    live = []
    for blk in range(NA_STEP_BLOCKS):
        variants = [1] + ([0] if blk == 0 else []) + ([2] if blk == NA_STEP_BLOCKS - 1 else [])
        live.append([[j for j in range(NA_KEY_ROWS // 2)
                      if any(offsets[v][a][kl] is not None for v in variants for kl in (2 * j, 2 * j + 1))]
                     for a in range(NA_ROWS)])
    return live


def _natten_bias_kernel(rpb_ref, tab_ref, *, rows):
    offsets = _natten_row_offsets(rows)
    nk = NA_KEY_ROWS * GRID_W
    shape = (GRID_W, LANES)
    lane = lax.broadcasted_iota(I32, shape, 1)
    c = lax.broadcasted_iota(I32, shape, 0)
    second = lane >= GRID_W
    kc = lane % GRID_W
    cs = jnp.clip(c - NA_WIN_COLS // 2, 0, GRID_W - NA_WIN_COLS)
    col_ok = jnp.logical_and(kc >= cs, kc < cs + NA_WIN_COLS)
    neg = jnp.full(shape, NEG_BIG, F32)
    for hh in range(2):
        toeplitz = []
        for dr in range(N_ROW_OFFSETS):
            x = jnp.broadcast_to(rpb_ref[hh, dr:dr + 1, :] * LOG2E, shape)
            lo = pltpu.roll(x, LANES - (NA_WIN_COLS - 1), axis=1, stride=1, stride_axis=0)
            hi = pltpu.roll(x, GRID_W - (NA_WIN_COLS - 1), axis=1, stride=1, stride_axis=0)
            toeplitz.append(jnp.where(second, hi, lo))
        for v in range(3):
            for a in range(NA_ROWS):
                for j in range(NA_KEY_ROWS // 2):
                    d0, d1 = offsets[v][a][2 * j], offsets[v][a][2 * j + 1]
                    if d0 is None and d1 is None:
                        piece = neg
                    else:
                        t0 = neg if d0 is None else toeplitz[d0]
                        t1 = neg if d1 is None else toeplitz[d1]
                        piece = jnp.where(col_ok, jnp.where(second, t1, t0), neg)
                    tab_ref[v, 0, a * GRID_W:(a + 1) * GRID_W,
                            hh * nk + j * LANES:hh * nk + (j + 1) * LANES] = piece


def _natten_bias_table(rpb, rows):
    n_pairs = HEADS // 2
    nqb = NA_ROWS * GRID_W
    nkb = NA_KEY_ROWS * GRID_W
    rpb_pad = jnp.pad(rpb.astype(F32), ((0, 0), (0, 0), (0, LANES - N_COL_OFFSETS)))
    return pl.pallas_call(
        functools.partial(_natten_bias_kernel, rows=rows),
        grid=(n_pairs,),
        in_specs=[pl.BlockSpec((2, N_ROW_OFFSETS, LANES), lambda p: (p, 0, 0))],
        out_specs=pl.BlockSpec((3, 1, nqb, 2 * nkb), lambda p: (0, p, 0, 0)),
        out_shape=jax.ShapeDtypeStruct((3, n_pairs, nqb, 2 * nkb), F32),
        compiler_params=_cparams(("parallel",)),
        name="natten_bias",
    )(rpb_pad)


def _natten_kernel(q_ref, k_ref, v_ref, tab_ref, wgu_f, wd_f, o_ref, wgu_b, wd_b, ks_ref, vs_ref, *, rows):
    wgu_b[...] = wgu_f[...].astype(BF16)
    wd_b[...] = wd_f[...].astype(BF16)
    nq = NA_ROWS * GRID_W
    nk = NA_KEY_ROWS * GRID_W
    n_blocks = rows // NA_ROWS
    live = _natten_live_columns(rows)
    lane = lax.broadcasted_iota(I32, (1, 2 * HEAD_DIM), 1)
    head0 = lane < HEAD_DIM

    @pl.when(pl.program_id(2) == 0)
    def _():
        k = k_ref[...]
        v = v_ref[...]
        zero = jnp.zeros_like(k)
        ks_ref[0] = jnp.where(head0, k, zero)
        ks_ref[1] = jnp.where(head0, zero, k)
        ind0 = jnp.broadcast_to(jnp.where(head0, 1.0, 0.0).astype(BF16), k.shape)
        ind1 = jnp.broadcast_to(jnp.where(head0, 0.0, 1.0).astype(BF16), k.shape)
        vs_ref[0] = jnp.concatenate([jnp.where(head0, v, zero), ind0], axis=1)
        vs_ref[1] = jnp.concatenate([jnp.where(head0, zero, v), ind1], axis=1)

    for blk in range(NA_STEP_BLOCKS):
        rb = pl.program_id(2) * NA_STEP_BLOCKS + blk
        variant = jnp.where(rb == 0, 0, jnp.where(rb == n_blocks - 1, 2, 1))
        start_row = jnp.clip(rb * NA_ROWS - NA_WIN_ROWS // 2, 0, rows - NA_KEY_ROWS)
        win = pl.ds(pl.multiple_of(start_row * GRID_W, GRID_W), nk)
        q = q_ref[blk * nq:(blk + 1) * nq, :]
        k_st = jnp.concatenate([ks_ref[0, win, :], ks_ref[1, win, :]], axis=0)
        v_st = jnp.concatenate([vs_ref[0, win, :], vs_ref[1, win, :]], axis=0)
        s = _dot_nt(q, k_st) + tab_ref[variant, 0]
        dead = jnp.zeros((GRID_W, LANES), BF16)
        slabs = []
        for a in range(NA_ROWS):
            qrows = slice(a * GRID_W, (a + 1) * GRID_W)
            parts = []
            for h in range(2):
                cols = {j: s[qrows, h * nk + j * LANES:h * nk + (j + 1) * LANES] for j in live[blk][a]}
                m = jnp.max(jnp.concatenate(list(cols.values()), axis=1), axis=1, keepdims=True)
                parts += [jnp.exp2(cols[j] - m).astype(BF16) if j in cols else dead
                          for j in range(NA_KEY_ROWS // 2)]
            slabs.append(jnp.concatenate(parts, axis=1))
        out = _dot(jnp.concatenate(slabs, axis=0), v_st)
        o_ref[blk * nq:(blk + 1) * nq, :] = (out[:, :2 * HEAD_DIM] / out[:, 2 * HEAD_DIM:]).astype(BF16)


def _natten(nq, nk, nv, rpb, w_gu, w_down, batch, seq):
    t = nq.shape[0]
    rows = seq // GRID_W
    n_pairs = HEADS // 2
    n_steps = rows // (NA_ROWS * NA_STEP_BLOCKS)
    nqb = NA_ROWS * GRID_W
    nkb = NA_KEY_ROWS * GRID_W
    tab = _natten_bias_table(rpb, rows)
    total_steps = batch * n_pairs * n_steps
    n_exp = w_gu.shape[0]
    assert n_exp % total_steps == 0
    epb = n_exp // total_steps

    def expert_spec(shape):
        return pl.BlockSpec((epb,) + shape[1:], lambda b, p, s: ((b * n_pairs + p) * n_steps + s, 0, 0))

    tab_spec = pl.BlockSpec((3, 1, nqb, 2 * nkb), lambda b, p, s: (0, p, 0, 0))
    kv_spec = pl.BlockSpec((seq, 2 * HEAD_DIM), lambda b, p, s: (b, p))
    q_spec = pl.BlockSpec((NA_STEP_BLOCKS * nqb, 2 * HEAD_DIM), lambda b, p, s: (b * n_steps + s, p))
    return pl.pallas_call(
        functools.partial(_natten_kernel, rows=rows),
        grid=(batch, n_pairs, n_steps),
        in_specs=[q_spec, kv_spec, kv_spec, tab_spec, expert_spec(w_gu.shape), expert_spec(w_down.shape)],
        out_specs=[q_spec, expert_spec(w_gu.shape), expert_spec(w_down.shape)],
        out_shape=[jax.ShapeDtypeStruct((t, GROUP_W), BF16),
                   jax.ShapeDtypeStruct(w_gu.shape, BF16), jax.ShapeDtypeStruct(w_down.shape, BF16)],
        scratch_shapes=[pltpu.VMEM((2, seq, 2 * HEAD_DIM), BF16), pltpu.VMEM((2, seq, 4 * HEAD_DIM), BF16)],
        compiler_params=_cparams(("arbitrary", "arbitrary", "arbitrary")),
        name="natten",
    )(nq, nk, nv, tab, w_gu.astype(F32), w_down.astype(F32))


def _layer_norm(h, gain, bias):
    mu = jnp.mean(h, axis=-1, keepdims=True)
    d = h - mu
    var = jnp.mean(d * d, axis=-1, keepdims=True)
    return d * lax.rsqrt(var + LN_EPS) * gain + bias


def _route_tile(xb, wr, rbias, tri):
    tm = xb.shape[0]
    scores = jax.nn.sigmoid(_dot_nt(wr, xb))
    biased = scores + rbias
    sub = lax.broadcasted_iota(I32, (GROUP_SIZE, tm), 0).astype(F32)
    none = float(N_EXPERTS)
    ninf = -jnp.inf

    def first_max(vals, index):
        m = jnp.max(vals, axis=0, keepdims=True)
        return m, jnp.min(jnp.where(vals == m, index, none), axis=0, keepdims=True)

    groups = [biased[g * GROUP_SIZE:(g + 1) * GROUP_SIZE, :] for g in range(N_GROUPS)]
    group_scores = []
    for g in range(N_GROUPS):
        m1, i1 = first_max(groups[g], sub)
        m2 = jnp.max(jnp.where(sub == i1, ninf, groups[g]), axis=0, keepdims=True)
        group_scores.append(m1 + m2)
    cur = jnp.concatenate(group_scores, axis=0)
    group_sel = jnp.zeros(cur.shape, F32)
    for _ in range(TOPK_GROUPS):
        _, i1 = first_max(cur, sub)
        hit = sub == i1
        group_sel = jnp.where(hit, 1.0, group_sel)
        cur = jnp.where(hit, ninf, cur)

    masked = [jnp.where(group_sel[g:g + 1, :] > 0.5, groups[g], ninf) for g in range(N_GROUPS)]
    ids = [sub + float(g * GROUP_SIZE) for g in range(N_GROUPS)]
    chosen = [jnp.zeros((GROUP_SIZE, tm), F32) for _ in range(N_GROUPS)]
    for _ in range(TOP_K):
        m = masked[0]
        for g in range(1, N_GROUPS):
            m = jnp.maximum(m, masked[g])
        m = jnp.max(m, axis=0, keepdims=True)
        cand = jnp.where(masked[0] == m, ids[0], none)
        for g in range(1, N_GROUPS):
            cand = jnp.minimum(cand, jnp.where(masked[g] == m, ids[g], none))
        first = jnp.min(cand, axis=0, keepdims=True)
        for g in range(N_GROUPS):
            hit = ids[g] == first
            chosen[g] = jnp.where(hit, 1.0, chosen[g])
            masked[g] = jnp.where(hit, ninf, masked[g])

    sel = jnp.concatenate(chosen, axis=0) > 0.5
    picked = jnp.where(sel, scores, 0.0)
    total = jnp.sum(picked, axis=0, keepdims=True)
    weight = picked / total * ROUTED_SCALE
    sel_f = jnp.where(sel, 1.0, 0.0)
    sel_b = sel_f.astype(BF16)
    before, cnt_max = [], None
    for k in range(tm // MOE_SUB):
        cols = slice(k * MOE_SUB, (k + 1) * MOE_SUB)
        before.append(_dot(sel_b[:, cols], tri))
        cnt = jnp.sum(sel_f[:, cols], axis=1, keepdims=True)
        cnt_max = cnt if cnt_max is None else jnp.maximum(cnt_max, cnt)
    rank = jnp.where(sel, jnp.concatenate(before, axis=1).astype(I32), -1)
    return weight, rank, cnt_max


def _mid_kernel(ret_ref, na_ref, x_ref, p_ref, wo_f, wsgu_f, wsd_f, wp_f, wg_f, gain_ref, bias_ref,
                wr_ref, rb_ref, pre_ref, x1b_ref, w_ref, rank_ref, cnt_ref,
                wo_ref, wsgu_ref, wsd_ref, wp_ref, wg_ref, tri_ref):
    for src_ref, dst_ref in ((wo_f, wo_ref), (wsgu_f, wsgu_ref), (wsd_f, wsd_ref), (wp_f, wp_ref),
                             (wg_f, wg_ref)):
        _cast_once(src_ref, dst_ref)

    @pl.when(pl.program_id(0) == 0)
    def _():
        i = lax.broadcasted_iota(I32, (MOE_SUB, MOE_SUB), 0)
        j = lax.broadcasted_iota(I32, (MOE_SUB, MOE_SUB), 1)
        tri_ref[...] = jnp.where(i < j, 1.0, 0.0).astype(BF16)

    for r in range(0, x_ref.shape[0], MOE_SUB):
        rows = slice(r, r + MOE_SUB)
        mix = (_dot(ret_ref[rows, :], wo_ref[0:GROUP_W, :])
               + _dot(na_ref[rows, :], wo_ref[GROUP_W:2 * GROUP_W, :]))
        x1 = _layer_norm(ALPHA * x_ref[rows, :] + mix, gain_ref[...], bias_ref[...])
        xb = x1.astype(BF16)
        x1b_ref[rows, :] = xb
        h = _dot(xb, wsgu_ref[...])
        act = (jax.nn.silu(h[:, :EXPERT_DIM]) * h[:, EXPERT_DIM:]).astype(BF16)
        shared = _dot(act, wsd_ref[...])
        ple = _dot(p_ref[rows, :].astype(BF16), wp_ref[...]) * jax.nn.sigmoid(_dot(xb, wg_ref[...]))
        pre_ref[rows, :] = ALPHA * x1 + shared + ple
    weight, rank, cnt_max = _route_tile(x1b_ref[...], wr_ref[...], rb_ref[...], tri_ref[...])
    w_ref[...] = weight
    rank_ref[...] = rank
    cnt_ref[...] = jnp.broadcast_to(cnt_max, cnt_ref.shape).astype(I32)


def _mid(ret, na, x2, p2, w_out, gain, bias, w_router, router_bias, wsgu, wsd, wp, wg):
    t = x2.shape[0]
    tm = MID_TM
    nt = t // tm
    tok = lambda i: (i, 0)
    const = lambda i: (0, 0)
    col = lambda i: (0, i)
    weights = [w.astype(F32) for w in (w_out, wsgu, wsd, wp, wg)]
    wr_t = w_router.astype(F32).T.astype(BF16)
    return pl.pallas_call(
        _mid_kernel,
        grid=(nt,),
        in_specs=[pl.BlockSpec((tm, GROUP_W), tok), pl.BlockSpec((tm, GROUP_W), tok),
                  pl.BlockSpec((tm, D_MODEL), tok), pl.BlockSpec((tm, p2.shape[1]), tok)]
                 + [_resident(w.shape) for w in weights]
                 + [pl.BlockSpec((1, D_MODEL), const), pl.BlockSpec((1, D_MODEL), const),
                    pl.BlockSpec((N_EXPERTS, D_MODEL), const), pl.BlockSpec((N_EXPERTS, 1), const)],
        out_specs=[pl.BlockSpec((tm, D_MODEL), tok), pl.BlockSpec((tm, D_MODEL), tok),
                   pl.BlockSpec((N_EXPERTS, tm), col), pl.BlockSpec((N_EXPERTS, tm), col),
                   pl.BlockSpec((N_EXPERTS, LANES), col)],
        out_shape=[jax.ShapeDtypeStruct((t, D_MODEL), F32), jax.ShapeDtypeStruct((t, D_MODEL), BF16),
                   jax.ShapeDtypeStruct((N_EXPERTS, t), F32), jax.ShapeDtypeStruct((N_EXPERTS, t), I32),
                   jax.ShapeDtypeStruct((N_EXPERTS, nt * LANES), I32)],
        scratch_shapes=[pltpu.VMEM(w.shape, BF16) for w in weights] + [pltpu.VMEM((MOE_SUB, MOE_SUB), BF16)],
        compiler_params=_cparams(("arbitrary",)),
        name="mid",
    )(ret, na, x2, p2, *weights, gain.astype(F32)[None, :], bias.astype(F32)[None, :],
      wr_t, router_bias.astype(F32)[:, None])


def _moe_kernel(cnt_ref, x_ref, rank_ref, w_ref, wgu_ref, wd_ref, pre_ref, gain_ref, bias_ref,
                o_ref, oh_ref, g_ref, y_ref):
    i = pl.program_id(0)
    eb = pl.program_id(1)
    n_eb = pl.num_programs(1)

    @pl.when(eb == 0)
    def _():
        o_ref[...] = jnp.zeros_like(o_ref)

    pre_rows = pre_ref.shape[0]
    o_ref[pl.ds(pl.multiple_of(eb * pre_rows, pre_rows), pre_rows), :] += pre_ref[...]

    max_count = cnt_ref[eb, i]

    def one_pass(p, slots):
        n_rows = MOE_EB * slots
        base = p * slots
        slot = lax.broadcasted_iota(I32, (slots, MOE_SUB), 0)
        slot_w = []
        for k in range(MOE_NSUB):
            tok = slice(k * MOE_SUB, (k + 1) * MOE_SUB)
            blocks, weights = [], []
            for j in range(MOE_EB):
                expert = pl.ds(eb * MOE_EB + j, 1)
                match = slot == (rank_ref[expert, tok] - base)
                blocks.append(jnp.where(match, 1.0, 0.0).astype(BF16))
                weights.append(jnp.sum(jnp.where(match, w_ref[expert, tok], 0.0), axis=1, keepdims=True))
            onehot = jnp.concatenate(blocks, axis=0)
            oh_ref[k, 0:n_rows, :] = onehot
            slot_w.append(weights)
            g_ref[k, 0:n_rows, :] = _dot(onehot, x_ref[tok, :]).astype(BF16)
        for j in range(MOE_EB):
            rows = slice(j * slots, (j + 1) * slots)
            xe = jnp.concatenate([g_ref[k, rows, :] for k in range(MOE_NSUB)], axis=0)
            h = _dot(xe, wgu_ref[j])
            act = (jax.nn.silu(h[:, :EXPERT_DIM]) * h[:, EXPERT_DIM:]).astype(BF16)
            wc = jnp.concatenate([slot_w[k][j] for k in range(MOE_NSUB)], axis=0)
            yw = (_dot(act, wd_ref[j]) * wc).astype(BF16)
            for k in range(MOE_NSUB):
                y_ref[k, rows, :] = yw[k * slots:(k + 1) * slots, :]
        for k in range(MOE_NSUB):
            tok = slice(k * MOE_SUB, (k + 1) * MOE_SUB)
            o_ref[tok, :] += _dot_tn(oh_ref[k, 0:n_rows, :], y_ref[k, 0:n_rows, :])

    @pl.when(max_count <= MOE_SLOTS_SMALL)
    def _():
        one_pass(0, MOE_SLOTS_SMALL)

    @pl.when(max_count > MOE_SLOTS_SMALL)
    def _():
        n_pass = lax.shift_right_logical(max_count + (MOE_SLOTS - 1), int(np.log2(MOE_SLOTS)))

        def body(p, carry):
            one_pass(p, MOE_SLOTS)
            return carry

        lax.fori_loop(0, n_pass, body, 0)

    @pl.when(eb == n_eb - 1)
    def _():
        for r in range(0, o_ref.shape[0], MOE_SUB):
            rows = slice(r, r + MOE_SUB)
            o_ref[rows, :] = _layer_norm(o_ref[rows, :], gain_ref[...], bias_ref[...])


def _moe(x1b, rank_t, w_t, counts, wgu_b, wd_b, pre, gain, bias):
    t = x1b.shape[0]
    tm = MOE_TM
    nt = t // tm
    n_eb = N_EXPERTS // MOE_EB
    pre_rows = tm // n_eb
    counts = counts.reshape(N_EXPERTS // MOE_EB, MOE_EB, nt, tm // MID_TM).max(axis=(1, 3))
    grid_spec = pltpu.PrefetchScalarGridSpec(
        num_scalar_prefetch=1,
        grid=(nt, N_EXPERTS // MOE_EB),
        in_specs=[
            pl.BlockSpec((tm, D_MODEL), lambda i, e, c: (i, 0)),
            pl.BlockSpec((N_EXPERTS, tm), lambda i, e, c: (0, i)),
            pl.BlockSpec((N_EXPERTS, tm), lambda i, e, c: (0, i)),
            pl.BlockSpec((MOE_EB, D_MODEL, 2 * EXPERT_DIM), lambda i, e, c: (e, 0, 0)),
            pl.BlockSpec((MOE_EB, EXPERT_DIM, D_MODEL), lambda i, e, c: (e, 0, 0)),
            pl.BlockSpec((pre_rows, D_MODEL), lambda i, e, c: (i * n_eb + e, 0)),
            pl.BlockSpec((1, D_MODEL), lambda i, e, c: (0, 0)),
            pl.BlockSpec((1, D_MODEL), lambda i, e, c: (0, 0)),
        ],
        out_specs=pl.BlockSpec((tm, D_MODEL), lambda i, e, c: (i, 0)),
        scratch_shapes=[pltpu.VMEM((MOE_NSUB, MOE_SUB, MOE_SUB), BF16),
                        pltpu.VMEM((MOE_NSUB, MOE_SUB, D_MODEL), BF16),
                        pltpu.VMEM((MOE_NSUB, MOE_SUB, D_MODEL), BF16)],
    )
    return pl.pallas_call(
        _moe_kernel,
        grid_spec=grid_spec,
        out_shape=jax.ShapeDtypeStruct((t, D_MODEL), F32),
        compiler_params=_cparams(("parallel", "arbitrary")),
        name="moe",
    )(counts, x1b, rank_t, w_t, wgu_b, wd_b, pre, gain.astype(F32)[None, :], bias.astype(F32)[None, :])


def _rotary_tables(seq):
    half = HEAD_DIM // 2
    inv = ROPE_BASE ** (-jnp.arange(half, dtype=F32) / half)
    ang = jnp.arange(seq, dtype=jnp.int32).astype(F32)[:, None] * inv[None, :]
    cos, sin = jnp.cos(ang), jnp.sin(ang)
    reps = LANES // HEAD_DIM
    cos_t = jnp.tile(jnp.concatenate([cos, cos], axis=1), (1, reps))
    sin_t = jnp.tile(jnp.concatenate([-sin, sin], axis=1), (1, reps))
    return cos_t, sin_t


def kernel(x, p, w_in, ret_decay_fwd, ret_decay_bwd, ret_gn_gain, na_rpb, w_out, ln1_gain, ln1_bias,
           w_router, router_bias, w_expert_gu, w_expert_down, w_shared_gu, w_shared_down,
           w_ple_proj, w_ple_gate, ln2_gain, ln2_bias):
    batch, seq, d = x.shape
    t = batch * seq
    depth = w_in.shape[0]
    assert depth == 1 and d == D_MODEL
    assert seq % PROJ_TM == 0 and seq % MOE_TM == 0 and seq % RET_CHUNK == 0
    assert MOE_TM % (N_EXPERTS // MOE_EB) == 0
    assert seq % MID_TM == 0 and MOE_TM % MID_TM == 0 and MID_TM % MOE_SUB == 0
    assert (seq // GRID_W) % (NA_ROWS * NA_STEP_BLOCKS) == 0 and seq // GRID_W >= NA_KEY_ROWS
    cos_t, sin_t = _rotary_tables(seq)
    x2 = x.reshape(t, d)
    for i in range(depth):
        rq, rk, rv, rg, nq, nk, nv = _in_proj(x2, w_in[i], cos_t, sin_t, seq)
        ret = _retention(rq, rk, rv, rg, ret_decay_fwd[i], ret_decay_bwd[i], ret_gn_gain[i], batch, seq)
        na, wgu_b, wd_b = _natten(nq, nk, nv, na_rpb[i], w_expert_gu[i], w_expert_down[i], batch, seq)
        pre, x1b, w_t, rank_t, cnt = _mid(ret, na, x2, p[i].reshape(t, -1), w_out[i], ln1_gain[i], ln1_bias[i],
                                          w_router[i], router_bias[i], w_shared_gu[i], w_shared_down[i],
                                          w_ple_proj[i], w_ple_gate[i])
        x2 = _moe(x1b, rank_t, w_t, cnt[:, ::LANES], wgu_b, wd_b, pre, ln2_gain[i], ln2_bias[i])
    return x2.reshape(batch, seq, d)
```

```python
import functools

import numpy as np
import jax
import jax.numpy as jnp
from jax import lax
from jax.experimental import pallas as pl
from jax.experimental.pallas import tpu as pltpu

F32 = jnp.float32
BF16 = jnp.bfloat16
I32 = jnp.int32

D_MODEL = 1024
HEADS = 8
HEAD_DIM = 64
GROUP_W = HEADS * HEAD_DIM
ROPE_BASE = 10000.0
GN_EPS = 1e-6
LN_EPS = 1e-5
GRID_W = 64
NA_WIN_ROWS = 8
NA_WIN_COLS = 16
N_EXPERTS = 64
N_GROUPS = 8
GROUP_SIZE = N_EXPERTS // N_GROUPS
TOPK_GROUPS = 4
TOP_K = 8
EXPERT_DIM = 256
ROUTED_SCALE = 2.5
ALPHA = 2.0 ** 0.25
NEG_BIG = -1e30
LOG2E = 1.4426950408889634
NA_Q_SCALE = HEAD_DIM ** -0.5 * LOG2E

LANES = 128
VMEM_LIMIT_BYTES = 56 * 1024 * 1024

PROJ_TM = 1024
MID_TM = 1024
RET_CHUNK = 128
RET_UNROLL = 16
NA_ROWS = 4
NA_KEY_ROWS = NA_ROWS + NA_WIN_ROWS
NA_STEP_BLOCKS = 8
MOE_TM = 2048
MOE_SUB = 256
MOE_EB = 4
MOE_SLOTS = MOE_SUB // MOE_EB
MOE_SLOTS_SMALL = 48
MOE_NSUB = MOE_TM // MOE_SUB


def _cparams(sem):
    return pltpu.CompilerParams(dimension_semantics=sem, vmem_limit_bytes=VMEM_LIMIT_BYTES)


def _dot(a, b):
    return jnp.dot(a, b, preferred_element_type=F32)


def _dot_nt(a, b):
    return lax.dot_general(a, b, (((1,), (1,)), ((), ())), preferred_element_type=F32)


def _dot_tn(a, b):
    return lax.dot_general(a, b, (((0,), (0,)), ((), ())), preferred_element_type=F32)


def _cast_once(src_ref, dst_ref):
    @pl.when(pl.program_id(0) == 0)
    def _():
        dst_ref[...] = src_ref[...].astype(BF16)


def _resident(shape):
    return pl.BlockSpec(shape, lambda i: (0,) * len(shape), pipeline_mode=pl.Buffered(1))


def _in_proj_kernel(x_ref, wf_ref, cos_ref, sin_ref,
                    rq_ref, rk_ref, rv_ref, rg_ref, nq_ref, nk_ref, nv_ref, w_ref):
    _cast_once(wf_ref, w_ref)
    xb = x_ref[...].astype(BF16)
    cos = cos_ref[...]
    sin = sin_ref[...]
    lane = lax.broadcasted_iota(I32, (1, LANES), 1)
    first_half = (lane % HEAD_DIM) < (HEAD_DIM // 2)

    def proj(g):
        return _dot(xb, w_ref[:, g * GROUP_W:(g + 1) * GROUP_W])

    def rotary(t, scale):
        outs = []
        for j in range(GROUP_W // LANES):
            c = t[:, j * LANES:(j + 1) * LANES]
            swapped = jnp.where(first_half,
                                pltpu.roll(c, LANES - HEAD_DIM // 2, axis=1),
                                pltpu.roll(c, HEAD_DIM // 2, axis=1))
            outs.append((c * cos + swapped * sin) * scale)
        return jnp.concatenate(outs, axis=1)

    rq_ref[...] = rotary(proj(0), 1.0).astype(BF16)
    rk_ref[...] = rotary(proj(1), HEAD_DIM ** -0.5).astype(BF16)
    rv_ref[...] = proj(2).astype(BF16)
    rg_ref[...] = jax.nn.silu(proj(3)).astype(BF16)
    nq_ref[...] = (proj(4) * NA_Q_SCALE).astype(BF16)
    nk_ref[...] = proj(5).astype(BF16)
    nv_ref[...] = proj(6).astype(BF16)


def _in_proj(x2, w_in, cos_t, sin_t, seq):
    t = x2.shape[0]
    tm = PROJ_TM
    n_pos = seq // tm
    out = jax.ShapeDtypeStruct((t, GROUP_W), BF16)
    tok = lambda i: (i, 0)
    return pl.pallas_call(
        _in_proj_kernel,
        grid=(t // tm,),
        in_specs=[
            pl.BlockSpec((tm, D_MODEL), tok),
            _resident(w_in.shape),
            pl.BlockSpec((tm, LANES), lambda i: (i % n_pos, 0)),
            pl.BlockSpec((tm, LANES), lambda i: (i % n_pos, 0)),
        ],
        out_specs=[pl.BlockSpec((tm, GROUP_W), tok)] * 7,
        out_shape=[out] * 7,
        scratch_shapes=[pltpu.VMEM(w_in.shape, BF16)],
        compiler_params=_cparams(("arbitrary",)),
        name="in_proj",
    )(x2, w_in.astype(F32), cos_t, sin_t)


def _log_sigmoid(x):
    return jnp.minimum(x, 0.0) - jnp.log1p(jnp.exp(-jnp.abs(x)))


def _retention_kernel(q_ref, k_ref, v_ref, g_ref, decf_ref, decb_ref, decfd_ref, decbd_ref,
                      gain_ref, o_ref, kv_ref, st_ref, dmat_ref):
    c = RET_CHUNK
    n_chunks = q_ref.shape[0] // c
    pair_w = 2 * HEAD_DIM

    lgf = _log_sigmoid(decf_ref[...])
    lgb = _log_sigmoid(decb_ref[...])
    row = lax.broadcasted_iota(I32, (c, 1), 0).astype(F32)
    k_dec_f = jnp.exp((c - 1.0 - row) * lgf)
    k_dec_b = jnp.exp(row * lgb)
    q_dec_f = jnp.exp((row + 1.0) * lgf)
    q_dec_b = jnp.exp((c - row) * lgb)
    chunk_dec_f = jnp.exp(c * lgf)
    chunk_dec_b = jnp.exp(c * lgb)

    lane = lax.broadcasted_iota(I32, (1, pair_w), 1)
    head0 = lane < HEAD_DIM
    r2 = lax.broadcasted_iota(I32, (pair_w, pair_w), 0) // HEAD_DIM
    c2 = lax.broadcasted_iota(I32, (pair_w, pair_w), 1) // HEAD_DIM
    same_head = r2 == c2
    block_diag = jnp.where(same_head, 1.0, 0.0)
    seg_avg = jnp.where(same_head, 1.0 / HEAD_DIM, 0.0).astype(BF16)

    lgf_d = _log_sigmoid(decfd_ref[0])
    lgb_d = _log_sigmoid(decbd_ref[0])
    di = lax.broadcasted_iota(I32, (c, 2 * c), 0)
    dj = lax.broadcasted_iota(I32, (c, 2 * c), 1) % c
    diff = (di - dj).astype(F32)
    dmat_ref[...] = jnp.where(diff >= 0.0, jnp.exp(diff * lgf_d), jnp.exp(-diff * lgb_d))

    def chunk(ref, n):
        return ref[pl.ds(pl.multiple_of(n * c, c), c), :]

    unroll = RET_UNROLL

    def summarize(nb, carry):
        for u in range(unroll):
            n = nb * unroll + u
            kf = chunk(k_ref, n).astype(F32)
            kst = jnp.concatenate([kf * k_dec_f, kf * k_dec_b], axis=1).astype(BF16)
            kv_ref[n] = _dot_tn(kst, chunk(v_ref, n))
        return carry

    lax.fori_loop(0, n_chunks // unroll, summarize, 0)

    def scan(i, states):
        fwd, bwd = states
        nb = n_chunks - 1 - i
        st_ref[i, 0:pair_w, :] = fwd.astype(BF16)
        st_ref[nb, pair_w:2 * pair_w, :] = bwd.astype(BF16)
        return (fwd * chunk_dec_f + kv_ref[i, 0:pair_w, :] * block_diag,
                bwd * chunk_dec_b + kv_ref[nb, pair_w:2 * pair_w, :] * block_diag)

    zero_state = jnp.zeros((pair_w, pair_w), F32)
    lax.fori_loop(0, n_chunks, scan, (zero_state, zero_state), unroll=True)

    gain = gain_ref[...]

    seg_avg2 = jnp.concatenate([seg_avg, seg_avg], axis=0)

    def seg_mean(z):
        hi = z.astype(BF16)
        lo = (z - hi.astype(F32)).astype(BF16)
        return _dot(jnp.concatenate([hi, lo], axis=1), seg_avg2)

    def emit(nb, carry):
        ys = []
        for u in range(unroll):
            n = nb * unroll + u
            q = chunk(q_ref, n)
            k = chunk(k_ref, n)
            v = chunk(v_ref, n)
            zero = jnp.zeros_like(k)
            k_st = jnp.concatenate([jnp.where(head0, k, zero), jnp.where(head0, zero, k)], axis=0)
            v_st = jnp.concatenate([jnp.where(head0, v, zero), jnp.where(head0, zero, v)], axis=0)
            scores = _dot_nt(q, k_st) * dmat_ref[...]
            qf = q.astype(F32)
            q_st = jnp.concatenate([qf * q_dec_f, qf * q_dec_b], axis=1).astype(BF16)
            ys.append(_dot(scores.astype(BF16), v_st) + _dot(q_st, st_ref[n]))
        y = jnp.concatenate(ys, axis=0)
        mu = seg_mean(y)
        d = y - mu
        var = seg_mean(d * d)
        yn = d * lax.rsqrt(var + GN_EPS) * gain
        rows = pl.ds(pl.multiple_of(nb * (unroll * c), unroll * c), unroll * c)
        o_ref[rows, :] = (g_ref[rows, :].astype(F32) * yn).astype(BF16)
        return carry

    lax.fori_loop(0, n_chunks // unroll, emit, 0)


def _retention(rq, rk, rv, rg, dec_f, dec_b, gain, batch, seq):
    t = rq.shape[0]
    c = RET_CHUNK
    n_pairs = HEADS // 2
    pair_w = 2 * HEAD_DIM
    dec_f_lane = jnp.repeat(dec_f.astype(F32), HEAD_DIM)[None, :]
    dec_b_lane = jnp.repeat(dec_b.astype(F32), HEAD_DIM)[None, :]
    dec_f_col = jnp.repeat(dec_f.astype(F32), c).reshape(n_pairs, 1, 2 * c)
    dec_b_col = jnp.repeat(dec_b.astype(F32), c).reshape(n_pairs, 1, 2 * c)
    tok = pl.BlockSpec((seq, pair_w), lambda b, p: (b, p))
    lane_spec = pl.BlockSpec((1, pair_w), lambda b, p: (0, p))
    col_spec = pl.BlockSpec((1, 1, 2 * c), lambda b, p: (p, 0, 0))
    return pl.pallas_call(
        _retention_kernel,
        grid=(batch, n_pairs),
        in_specs=[tok, tok, tok, tok, lane_spec, lane_spec, col_spec, col_spec, lane_spec],
        out_specs=tok,
        out_shape=jax.ShapeDtypeStruct((t, GROUP_W), BF16),
        scratch_shapes=[
            pltpu.VMEM((seq // c, 2 * pair_w, pair_w), F32),
            pltpu.VMEM((seq // c, 2 * pair_w, pair_w), BF16),
            pltpu.VMEM((c, 2 * c), F32),
        ],
        compiler_params=_cparams(("parallel", "parallel")),
        name="retention",
    )(rq, rk, rv, rg, dec_f_lane, dec_b_lane, dec_f_col, dec_b_col, gain.astype(F32)[None, :])


N_ROW_OFFSETS = 2 * NA_WIN_ROWS - 1
N_COL_OFFSETS = 2 * NA_WIN_COLS - 1


def _natten_row_offsets(rows):
    n_blocks = rows // NA_ROWS
    starts = {0: 0, 1: NA_ROWS - NA_WIN_ROWS // 2, 2: rows - NA_KEY_ROWS}
    blocks = {0: 0, 1: 1, 2: n_blocks - 1}
    table = []
    for v in range(3):
        per_a = []
        for a in range(NA_ROWS):
            r = blocks[v] * NA_ROWS + a
            rs = min(max(r - NA_WIN_ROWS // 2, 0), rows - NA_WIN_ROWS)
            per_kl = []
            for kl in range(NA_KEY_ROWS):
                kr = starts[v] + kl
                per_kl.append(kr - r + NA_WIN_ROWS - 1 if rs <= kr < rs + NA_WIN_ROWS else None)
            per_a.append(per_kl)
        table.append(per_a)
    return table


def _natten_live_columns(rows):
    offsets = _natten_row_offsets(rows)
    live = []
    for blk in range(NA_STEP_BLOCKS):
        variants = [1] + ([0] if blk == 0 else []) + ([2] if blk == NA_STEP_BLOCKS - 1 else [])
        live.append([[j for j in range(NA_KEY_ROWS // 2)
                      if any(offsets[v][a][kl] is not None for v in variants for kl in (2 * j, 2 * j + 1))]
                     for a in range(NA_ROWS)])
    return live


def _natten_bias_kernel(rpb_ref, tab_ref, *, rows):
    offsets = _natten_row_offsets(rows)
    nk = NA_KEY_ROWS * GRID_W
    shape = (GRID_W, LANES)
    lane = lax.broadcasted_iota(I32, shape, 1)
    c = lax.broadcasted_iota(I32, shape, 0)
    second = lane >= GRID_W
    kc = lane % GRID_W
    cs = jnp.clip(c - NA_WIN_COLS // 2, 0, GRID_W - NA_WIN_COLS)
    col_ok = jnp.logical_and(kc >= cs, kc < cs + NA_WIN_COLS)
    neg = jnp.full(shape, NEG_BIG, F32)
    for hh in range(2):
        toeplitz = []
        for dr in range(N_ROW_OFFSETS):
            x = jnp.broadcast_to(rpb_ref[hh, dr:dr + 1, :] * LOG2E, shape)
            lo = pltpu.roll(x, LANES - (NA_WIN_COLS - 1), axis=1, stride=1, stride_axis=0)
            hi = pltpu.roll(x, GRID_W - (NA_WIN_COLS - 1), axis=1, stride=1, stride_axis=0)
            toeplitz.append(jnp.where(second, hi, lo))
        for v in range(3):
            for a in range(NA_ROWS):
                for j in range(NA_KEY_ROWS // 2):
                    d0, d1 = offsets[v][a][2 * j], offsets[v][a][2 * j + 1]
                    if d0 is None and d1 is None:
                        piece = neg
                    else:
                        t0 = neg if d0 is None else toeplitz[d0]
                        t1 = neg if d1 is None else toeplitz[d1]
                        piece = jnp.where(col_ok, jnp.where(second, t1, t0), neg)
                    tab_ref[v, 0, a * GRID_W:(a + 1) * GRID_W,
                            hh * nk + j * LANES:hh * nk + (j + 1) * LANES] = piece


def _natten_bias_table(rpb, rows):
    n_pairs = HEADS // 2
    nqb = NA_ROWS * GRID_W
    nkb = NA_KEY_ROWS * GRID_W
    rpb_pad = jnp.pad(rpb.astype(F32), ((0, 0), (0, 0), (0, LANES - N_COL_OFFSETS)))
    return pl.pallas_call(
        functools.partial(_natten_bias_kernel, rows=rows),
        grid=(n_pairs,),
        in_specs=[pl.BlockSpec((2, N_ROW_OFFSETS, LANES), lambda p: (p, 0, 0))],
        out_specs=pl.BlockSpec((3, 1, nqb, 2 * nkb), lambda p: (0, p, 0, 0)),
        out_shape=jax.ShapeDtypeStruct((3, n_pairs, nqb, 2 * nkb), F32),
        compiler_params=_cparams(("parallel",)),
        name="natten_bias",
    )(rpb_pad)


def _natten_kernel(q_ref, k_ref, v_ref, tab_ref, wgu_f, wd_f, o_ref, wgu_b, wd_b, ks_ref, vs_ref, *, rows):
    wgu_b[...] = wgu_f[...].astype(BF16)
    wd_b[...] = wd_f[...].astype(BF16)
    nq = NA_ROWS * GRID_W
    nk = NA_KEY_ROWS * GRID_W
    n_blocks = rows // NA_ROWS
    live = _natten_live_columns(rows)
    lane = lax.broadcasted_iota(I32, (1, 2 * HEAD_DIM), 1)
    head0 = lane < HEAD_DIM

    @pl.when(pl.program_id(2) == 0)
    def _():
        k = k_ref[...]
        v = v_ref[...]
        zero = jnp.zeros_like(k)
        ks_ref[0] = jnp.where(head0, k, zero)
        ks_ref[1] = jnp.where(head0, zero, k)
        ind0 = jnp.broadcast_to(jnp.where(head0, 1.0, 0.0).astype(BF16), k.shape)
        ind1 = jnp.broadcast_to(jnp.where(head0, 0.0, 1.0).astype(BF16), k.shape)
        vs_ref[0] = jnp.concatenate([jnp.where(head0, v, zero), ind0], axis=1)
        vs_ref[1] = jnp.concatenate([jnp.where(head0, zero, v), ind1], axis=1)

    for blk in range(NA_STEP_BLOCKS):
        rb = pl.program_id(2) * NA_STEP_BLOCKS + blk
        variant = jnp.where(rb == 0, 0, jnp.where(rb == n_blocks - 1, 2, 1))
        start_row = jnp.clip(rb * NA_ROWS - NA_WIN_ROWS // 2, 0, rows - NA_KEY_ROWS)
        win = pl.ds(pl.multiple_of(start_row * GRID_W, GRID_W), nk)
        q = q_ref[blk * nq:(blk + 1) * nq, :]
        k_st = jnp.concatenate([ks_ref[0, win, :], ks_ref[1, win, :]], axis=0)
        v_st = jnp.concatenate([vs_ref[0, win, :], vs_ref[1, win, :]], axis=0)
        s = _dot_nt(q, k_st) + tab_ref[variant, 0]
        dead = jnp.zeros((GRID_W, LANES), BF16)
        slabs = []
        for a in range(NA_ROWS):
            qrows = slice(a * GRID_W, (a + 1) * GRID_W)
            parts = []
            for h in range(2):
                cols = {j: s[qrows, h * nk + j * LANES:h * nk + (j + 1) * LANES] for j in live[blk][a]}
                m = jnp.max(jnp.concatenate(list(cols.values()), axis=1), axis=1, keepdims=True)
                parts += [jnp.exp2(cols[j] - m).astype(BF16) if j in cols else dead
                          for j in range(NA_KEY_ROWS // 2)]
            slabs.append(jnp.concatenate(parts, axis=1))
        out = _dot(jnp.concatenate(slabs, axis=0), v_st)
        o_ref[blk * nq:(blk + 1) * nq, :] = (out[:, :2 * HEAD_DIM] / out[:, 2 * HEAD_DIM:]).astype(BF16)


def _natten(nq, nk, nv, rpb, w_gu, w_down, batch, seq):
    t = nq.shape[0]
    rows = seq // GRID_W
    n_pairs = HEADS // 2
    n_steps = rows // (NA_ROWS * NA_STEP_BLOCKS)
    nqb = NA_ROWS * GRID_W
    nkb = NA_KEY_ROWS * GRID_W
    tab = _natten_bias_table(rpb, rows)
    total_steps = batch * n_pairs * n_steps
    n_exp = w_gu.shape[0]
    assert n_exp % total_steps == 0
    epb = n_exp // total_steps

    def expert_spec(shape):
        return pl.BlockSpec((epb,) + shape[1:], lambda b, p, s: ((b * n_pairs + p) * n_steps + s, 0, 0))

    tab_spec = pl.BlockSpec((3, 1, nqb, 2 * nkb), lambda b, p, s: (0, p, 0, 0))
    kv_spec = pl.BlockSpec((seq, 2 * HEAD_DIM), lambda b, p, s: (b, p))
    q_spec = pl.BlockSpec((NA_STEP_BLOCKS * nqb, 2 * HEAD_DIM), lambda b, p, s: (b * n_steps + s, p))
    return pl.pallas_call(
        functools.partial(_natten_kernel, rows=rows),
        grid=(batch, n_pairs, n_steps),
        in_specs=[q_spec, kv_spec, kv_spec, tab_spec, expert_spec(w_gu.shape), expert_spec(w_down.shape)],
        out_specs=[q_spec, expert_spec(w_gu.shape), expert_spec(w_down.shape)],
        out_shape=[jax.ShapeDtypeStruct((t, GROUP_W), BF16),
                   jax.ShapeDtypeStruct(w_gu.shape, BF16), jax.ShapeDtypeStruct(w_down.shape, BF16)],
        scratch_shapes=[pltpu.VMEM((2, seq, 2 * HEAD_DIM), BF16), pltpu.VMEM((2, seq, 4 * HEAD_DIM), BF16)],
        compiler_params=_cparams(("arbitrary", "arbitrary", "arbitrary")),
        name="natten",
    )(nq, nk, nv, tab, w_gu.astype(F32), w_down.astype(F32))


def _layer_norm(h, gain, bias):
    mu = jnp.mean(h, axis=-1, keepdims=True)
    d = h - mu
    var = jnp.mean(d * d, axis=-1, keepdims=True)
    return d * lax.rsqrt(var + LN_EPS) * gain + bias


def _route_tile(xb, wr, rbias, tri):
    tm = xb.shape[0]
    scores = jax.nn.sigmoid(_dot_nt(wr, xb))
    biased = scores + rbias
    sub = lax.broadcasted_iota(I32, (GROUP_SIZE, tm), 0).astype(F32)
    none = float(N_EXPERTS)
    ninf = -jnp.inf

    def first_max(vals, index):
        m = jnp.max(vals, axis=0, keepdims=True)
        return m, jnp.min(jnp.where(vals == m, index, none), axis=0, keepdims=True)

    groups = [biased[g * GROUP_SIZE:(g + 1) * GROUP_SIZE, :] for g in range(N_GROUPS)]
    group_scores = []
    for g in range(N_GROUPS):
        m1, i1 = first_max(groups[g], sub)
        m2 = jnp.max(jnp.where(sub == i1, ninf, groups[g]), axis=0, keepdims=True)
        group_scores.append(m1 + m2)
    cur = jnp.concatenate(group_scores, axis=0)
    group_sel = jnp.zeros(cur.shape, F32)
    for _ in range(TOPK_GROUPS):
        _, i1 = first_max(cur, sub)
        hit = sub == i1
        group_sel = jnp.where(hit, 1.0, group_sel)
        cur = jnp.where(hit, ninf, cur)

    masked = [jnp.where(group_sel[g:g + 1, :] > 0.5, groups[g], ninf) for g in range(N_GROUPS)]
    ids = [sub + float(g * GROUP_SIZE) for g in range(N_GROUPS)]
    chosen = [jnp.zeros((GROUP_SIZE, tm), F32) for _ in range(N_GROUPS)]
    for _ in range(TOP_K):
        m = masked[0]
        for g in range(1, N_GROUPS):
            m = jnp.maximum(m, masked[g])
        m = jnp.max(m, axis=0, keepdims=True)
        cand = jnp.where(masked[0] == m, ids[0], none)
        for g in range(1, N_GROUPS):
            cand = jnp.minimum(cand, jnp.where(masked[g] == m, ids[g], none))
        first = jnp.min(cand, axis=0, keepdims=True)
        for g in range(N_GROUPS):
            hit = ids[g] == first
            chosen[g] = jnp.where(hit, 1.0, chosen[g])
            masked[g] = jnp.where(hit, ninf, masked[g])

    sel = jnp.concatenate(chosen, axis=0) > 0.5
    picked = jnp.where(sel, scores, 0.0)
    total = jnp.sum(picked, axis=0, keepdims=True)
    weight = picked / total * ROUTED_SCALE
    sel_f = jnp.where(sel, 1.0, 0.0)
    sel_b = sel_f.astype(BF16)
    before, cnt_max = [], None
    for k in range(tm // MOE_SUB):
        cols = slice(k * MOE_SUB, (k + 1) * MOE_SUB)
        before.append(_dot(sel_b[:, cols], tri))
        cnt = jnp.sum(sel_f[:, cols], axis=1, keepdims=True)
        cnt_max = cnt if cnt_max is None else jnp.maximum(cnt_max, cnt)
    rank = jnp.where(sel, jnp.concatenate(before, axis=1).astype(I32), -1)
    return weight, rank, cnt_max


def _mid_kernel(ret_ref, na_ref, x_ref, p_ref, wo_f, wsgu_f, wsd_f, wp_f, wg_f, gain_ref, bias_ref,
                wr_ref, rb_ref, pre_ref, x1b_ref, w_ref, rank_ref, cnt_ref,
                wo_ref, wsgu_ref, wsd_ref, wp_ref, wg_ref, tri_ref):
    for src_ref, dst_ref in ((wo_f, wo_ref), (wsgu_f, wsgu_ref), (wsd_f, wsd_ref), (wp_f, wp_ref),
                             (wg_f, wg_ref)):
        _cast_once(src_ref, dst_ref)

    @pl.when(pl.program_id(0) == 0)
    def _():
        i = lax.broadcasted_iota(I32, (MOE_SUB, MOE_SUB), 0)
        j = lax.broadcasted_iota(I32, (MOE_SUB, MOE_SUB), 1)
        tri_ref[...] = jnp.where(i < j, 1.0, 0.0).astype(BF16)

    sub_blocks = [slice(r, r + MOE_SUB) for r in range(0, x_ref.shape[0], MOE_SUB)]
    for rows in sub_blocks:
        mix = (_dot(ret_ref[rows, :], wo_ref[0:GROUP_W, :])
               + _dot(na_ref[rows, :], wo_ref[GROUP_W:2 * GROUP_W, :]))
        x1 = _layer_norm(ALPHA * x_ref[rows, :] + mix, gain_ref[...], bias_ref[...])
        x1b_ref[rows, :] = x1.astype(BF16)
        pre_ref[rows, :] = ALPHA * x1
    weight, rank, cnt_max = _route_tile(x1b_ref[...], wr_ref[...], rb_ref[...], tri_ref[...])
    w_ref[...] = weight
    rank_ref[...] = rank
    cnt_ref[...] = jnp.broadcast_to(cnt_max, cnt_ref.shape).astype(I32)
    for rows in sub_blocks:
        xb = x1b_ref[rows, :]
        h = _dot(xb, wsgu_ref[...])
        act = (jax.nn.silu(h[:, :EXPERT_DIM]) * h[:, EXPERT_DIM:]).astype(BF16)
        shared = _dot(act, wsd_ref[...])
        ple = _dot(p_ref[rows, :].astype(BF16), wp_ref[...]) * jax.nn.sigmoid(_dot(xb, wg_ref[...]))
        pre_ref[rows, :] += shared + ple


def _mid(ret, na, x2, p2, w_out, gain, bias, w_router, router_bias, wsgu, wsd, wp, wg):
    t = x2.shape[0]
    tm = MID_TM
    nt = t // tm
    tok = lambda i: (i, 0)
    const = lambda i: (0, 0)
    col = lambda i: (0, i)
    weights = [w.astype(F32) for w in (w_out, wsgu, wsd, wp, wg)]
    wr_t = w_router.astype(F32).T.astype(BF16)
    return pl.pallas_call(
        _mid_kernel,
        grid=(nt,),
        in_specs=[pl.BlockSpec((tm, GROUP_W), tok), pl.BlockSpec((tm, GROUP_W), tok),
                  pl.BlockSpec((tm, D_MODEL), tok), pl.BlockSpec((tm, p2.shape[1]), tok)]
                 + [_resident(w.shape) for w in weights]
                 + [pl.BlockSpec((1, D_MODEL), const), pl.BlockSpec((1, D_MODEL), const),
                    pl.BlockSpec((N_EXPERTS, D_MODEL), const), pl.BlockSpec((N_EXPERTS, 1), const)],
        out_specs=[pl.BlockSpec((tm, D_MODEL), tok), pl.BlockSpec((tm, D_MODEL), tok),
                   pl.BlockSpec((N_EXPERTS, tm), col), pl.BlockSpec((N_EXPERTS, tm), col),
                   pl.BlockSpec((N_EXPERTS, LANES), col)],
        out_shape=[jax.ShapeDtypeStruct((t, D_MODEL), F32), jax.ShapeDtypeStruct((t, D_MODEL), BF16),
                   jax.ShapeDtypeStruct((N_EXPERTS, t), F32), jax.ShapeDtypeStruct((N_EXPERTS, t), I32),
                   jax.ShapeDtypeStruct((N_EXPERTS, nt * LANES), I32)],
        scratch_shapes=[pltpu.VMEM(w.shape, BF16) for w in weights] + [pltpu.VMEM((MOE_SUB, MOE_SUB), BF16)],
        compiler_params=_cparams(("arbitrary",)),
        name="mid",
    )(ret, na, x2, p2, *weights, gain.astype(F32)[None, :], bias.astype(F32)[None, :],
      wr_t, router_bias.astype(F32)[:, None])


def _moe_kernel(cnt_ref, x_ref, rank_ref, w_ref, wgu_ref, wd_ref, pre_ref, gain_ref, bias_ref,
                o_ref, oh_ref, g_ref, y_ref):
    i = pl.program_id(0)
    eb = pl.program_id(1)
    n_eb = pl.num_programs(1)

    @pl.when(eb == 0)
    def _():
        o_ref[...] = jnp.zeros_like(o_ref)

    pre_rows = pre_ref.shape[0]
    o_ref[pl.ds(pl.multiple_of(eb * pre_rows, pre_rows), pre_rows), :] += pre_ref[...]

    max_count = cnt_ref[eb, i]

    def one_pass(p, slots):
        n_rows = MOE_EB * slots
        base = p * slots
        slot = lax.broadcasted_iota(I32, (slots, MOE_SUB), 0)
        slot_w = []
        for k in range(MOE_NSUB):
            tok = slice(k * MOE_SUB, (k + 1) * MOE_SUB)
            blocks, weights = [], []
            for j in range(MOE_EB):
                expert = pl.ds(eb * MOE_EB + j, 1)
                match = slot == (rank_ref[expert, tok] - base)
                blocks.append(jnp.where(match, 1.0, 0.0).astype(BF16))
                weights.append(jnp.sum(jnp.where(match, w_ref[expert, tok], 0.0), axis=1, keepdims=True))
            onehot = jnp.concatenate(blocks, axis=0)
            oh_ref[k, 0:n_rows, :] = onehot
            slot_w.append(weights)
            g_ref[k, 0:n_rows, :] = _dot(onehot, x_ref[tok, :]).astype(BF16)
        for j in range(MOE_EB):
            rows = slice(j * slots, (j + 1) * slots)
            xe = jnp.concatenate([g_ref[k, rows, :] for k in range(MOE_NSUB)], axis=0)
            h = _dot(xe, wgu_ref[j])
            act = (jax.nn.silu(h[:, :EXPERT_DIM]) * h[:, EXPERT_DIM:]).astype(BF16)
            wc = jnp.concatenate([slot_w[k][j] for k in range(MOE_NSUB)], axis=0)
            yw = (_dot(act, wd_ref[j]) * wc).astype(BF16)
            for k in range(MOE_NSUB):
                y_ref[k, rows, :] = yw[k * slots:(k + 1) * slots, :]
        for k in range(MOE_NSUB):
            tok = slice(k * MOE_SUB, (k + 1) * MOE_SUB)
            o_ref[tok, :] += _dot_tn(oh_ref[k, 0:n_rows, :], y_ref[k, 0:n_rows, :])

    @pl.when(max_count <= MOE_SLOTS_SMALL)
    def _():
        one_pass(0, MOE_SLOTS_SMALL)

    @pl.when(max_count > MOE_SLOTS_SMALL)
    def _():
        n_pass = lax.shift_right_logical(max_count + (MOE_SLOTS - 1), int(np.log2(MOE_SLOTS)))

        def body(p, carry):
            one_pass(p, MOE_SLOTS)
            return carry

        lax.fori_loop(0, n_pass, body, 0)

    @pl.when(eb == n_eb - 1)
    def _():
        for r in range(0, o_ref.shape[0], MOE_SUB):
            rows = slice(r, r + MOE_SUB)
            o_ref[rows, :] = _layer_norm(o_ref[rows, :], gain_ref[...], bias_ref[...])


def _moe(x1b, rank_t, w_t, counts, wgu_b, wd_b, pre, gain, bias):
    t = x1b.shape[0]
    tm = MOE_TM
    nt = t // tm
    n_eb = N_EXPERTS // MOE_EB
    pre_rows = tm // n_eb
    counts = counts.reshape(N_EXPERTS // MOE_EB, MOE_EB, nt, tm // MID_TM).max(axis=(1, 3))
    grid_spec = pltpu.PrefetchScalarGridSpec(
        num_scalar_prefetch=1,
        grid=(nt, N_EXPERTS // MOE_EB),
        in_specs=[
            pl.BlockSpec((tm, D_MODEL), lambda i, e, c: (i, 0)),
            pl.BlockSpec((N_EXPERTS, tm), lambda i, e, c: (0, i)),
            pl.BlockSpec((N_EXPERTS, tm), lambda i, e, c: (0, i)),
            pl.BlockSpec((MOE_EB, D_MODEL, 2 * EXPERT_DIM), lambda i, e, c: (e, 0, 0)),
            pl.BlockSpec((MOE_EB, EXPERT_DIM, D_MODEL), lambda i, e, c: (e, 0, 0)),
            pl.BlockSpec((pre_rows, D_MODEL), lambda i, e, c: (i * n_eb + e, 0)),
            pl.BlockSpec((1, D_MODEL), lambda i, e, c: (0, 0)),
            pl.BlockSpec((1, D_MODEL), lambda i, e, c: (0, 0)),
        ],
        out_specs=pl.BlockSpec((tm, D_MODEL), lambda i, e, c: (i, 0)),
        scratch_shapes=[pltpu.VMEM((MOE_NSUB, MOE_SUB, MOE_SUB), BF16),
                        pltpu.VMEM((MOE_NSUB, MOE_SUB, D_MODEL), BF16),
                        pltpu.VMEM((MOE_NSUB, MOE_SUB, D_MODEL), BF16)],
    )
    return pl.pallas_call(
        _moe_kernel,
        grid_spec=grid_spec,
        out_shape=jax.ShapeDtypeStruct((t, D_MODEL), F32),
        compiler_params=_cparams(("parallel", "arbitrary")),
        name="moe",
    )(counts, x1b, rank_t, w_t, wgu_b, wd_b, pre, gain.astype(F32)[None, :], bias.astype(F32)[None, :])


def _rotary_tables(seq):
    half = HEAD_DIM // 2
    inv = ROPE_BASE ** (-jnp.arange(half, dtype=F32) / half)
    ang = jnp.arange(seq, dtype=jnp.int32).astype(F32)[:, None] * inv[None, :]
    cos, sin = jnp.cos(ang), jnp.sin(ang)
    reps = LANES // HEAD_DIM
    cos_t = jnp.tile(jnp.concatenate([cos, cos], axis=1), (1, reps))
    sin_t = jnp.tile(jnp.concatenate([-sin, sin], axis=1), (1, reps))
    return cos_t, sin_t


def kernel(x, p, w_in, ret_decay_fwd, ret_decay_bwd, ret_gn_gain, na_rpb, w_out, ln1_gain, ln1_bias,
           w_router, router_bias, w_expert_gu, w_expert_down, w_shared_gu, w_shared_down,
           w_ple_proj, w_ple_gate, ln2_gain, ln2_bias):
    batch, seq, d = x.shape
    t = batch * seq
    depth = w_in.shape[0]
    assert depth == 1 and d == D_MODEL
    assert seq % PROJ_TM == 0 and seq % MOE_TM == 0 and seq % RET_CHUNK == 0
    assert MOE_TM % (N_EXPERTS // MOE_EB) == 0
    assert seq % MID_TM == 0 and MOE_TM % MID_TM == 0 and MID_TM % MOE_SUB == 0
    assert (seq // GRID_W) % (NA_ROWS * NA_STEP_BLOCKS) == 0 and seq // GRID_W >= NA_KEY_ROWS
    cos_t, sin_t = _rotary_tables(seq)
    x2 = x.reshape(t, d)
    for i in range(depth):
        rq, rk, rv, rg, nq, nk, nv = _in_proj(x2, w_in[i], cos_t, sin_t, seq)
        ret = _retention(rq, rk, rv, rg, ret_decay_fwd[i], ret_decay_bwd[i], ret_gn_gain[i], batch, seq)
        na, wgu_b, wd_b = _natten(nq, nk, nv, na_rpb[i], w_expert_gu[i], w_expert_down[i], batch, seq)
        pre, x1b, w_t, rank_t, cnt = _mid(ret, na, x2, p[i].reshape(t, -1), w_out[i], ln1_gain[i], ln1_bias[i],
                                          w_router[i], router_bias[i], w_shared_gu[i], w_shared_down[i],
                                          w_ple_proj[i], w_ple_gate[i])
        x2 = _moe(x1b, rank_t, w_t, cnt[:, ::LANES], wgu_b, wd_b, pre, ln2_gain[i], ln2_bias[i])
    return x2.reshape(batch, seq, d)
```

```python
import functools

import numpy as np
import jax
import jax.numpy as jnp
from jax import lax
from jax.experimental import pallas as pl
from jax.experimental.pallas import tpu as pltpu

F32 = jnp.float32
BF16 = jnp.bfloat16
I32 = jnp.int32

D_MODEL = 1024
HEADS = 8
HEAD_DIM = 64
GROUP_W = HEADS * HEAD_DIM
ROPE_BASE = 10000.0
GN_EPS = 1e-6
LN_EPS = 1e-5
GRID_W = 64
NA_WIN_ROWS = 8
NA_WIN_COLS = 16
N_EXPERTS = 64
N_GROUPS = 8
GROUP_SIZE = N_EXPERTS // N_GROUPS
TOPK_GROUPS = 4
TOP_K = 8
EXPERT_DIM = 256
ROUTED_SCALE = 2.5
ALPHA = 2.0 ** 0.25
NEG_BIG = -1e30
LOG2E = 1.4426950408889634
NA_Q_SCALE = HEAD_DIM ** -0.5 * LOG2E

LANES = 128
VMEM_LIMIT_BYTES = 56 * 1024 * 1024

PROJ_TM = 1024
MID_TM = 1024
RET_CHUNK = 128
RET_UNROLL = 16
NA_ROWS = 4
NA_KEY_ROWS = NA_ROWS + NA_WIN_ROWS
NA_STEP_BLOCKS = 8
MOE_TM = 2048
MOE_SUB = 256
MOE_EB = 4
MOE_SLOTS = MOE_SUB // MOE_EB
MOE_SLOTS_SMALL = 48
MOE_NSUB = MOE_TM // MOE_SUB


def _cparams(sem):
    return pltpu.CompilerParams(dimension_semantics=sem, vmem_limit_bytes=VMEM_LIMIT_BYTES)


def _dot(a, b):
    return jnp.dot(a, b, preferred_element_type=F32)


def _dot_nt(a, b):
    return lax.dot_general(a, b, (((1,), (1,)), ((), ())), preferred_element_type=F32)


def _dot_tn(a, b):
    return lax.dot_general(a, b, (((0,), (0,)), ((), ())), preferred_element_type=F32)


def _cast_once(src_ref, dst_ref):
    @pl.when(pl.program_id(0) == 0)
    def _():
        dst_ref[...] = src_ref[...].astype(BF16)


def _resident(shape):
    return pl.BlockSpec(shape, lambda i: (0,) * len(shape), pipeline_mode=pl.Buffered(1))


def _in_proj_kernel(x_ref, wf_ref, cos_ref, sin_ref,
                    rq_ref, rk_ref, rv_ref, rg_ref, nq_ref, nk_ref, nv_ref, w_ref):
    _cast_once(wf_ref, w_ref)
    xb = x_ref[...].astype(BF16)
    cos = cos_ref[...]
    sin = sin_ref[...]
    lane = lax.broadcasted_iota(I32, (1, LANES), 1)
    first_half = (lane % HEAD_DIM) < (HEAD_DIM // 2)

    def proj(g):
        return _dot(xb, w_ref[:, g * GROUP_W:(g + 1) * GROUP_W])

    def rotary(t, scale):
        outs = []
        for j in range(GROUP_W // LANES):
            c = t[:, j * LANES:(j + 1) * LANES]
            swapped = jnp.where(first_half,
                                pltpu.roll(c, LANES - HEAD_DIM // 2, axis=1),
                                pltpu.roll(c, HEAD_DIM // 2, axis=1))
            outs.append((c * cos + swapped * sin) * scale)
        return jnp.concatenate(outs, axis=1)

    rq_ref[...] = rotary(proj(0), 1.0).astype(BF16)
    rk_ref[...] = rotary(proj(1), HEAD_DIM ** -0.5).astype(BF16)
    rv_ref[...] = proj(2).astype(BF16)
    rg_ref[...] = jax.nn.silu(proj(3)).astype(BF16)
    nq_ref[...] = (proj(4) * NA_Q_SCALE).astype(BF16)
    nk_ref[...] = proj(5).astype(BF16)
    nv_ref[...] = proj(6).astype(BF16)


def _in_proj(x2, w_in, cos_t, sin_t, seq):
    t = x2.shape[0]
    tm = PROJ_TM
    n_pos = seq // tm
    out = jax.ShapeDtypeStruct((t, GROUP_W), BF16)
    tok = lambda i: (i, 0)
    return pl.pallas_call(
        _in_proj_kernel,
        grid=(t // tm,),
        in_specs=[
            pl.BlockSpec((tm, D_MODEL), tok),
            _resident(w_in.shape),
            pl.BlockSpec((tm, LANES), lambda i: (i % n_pos, 0)),
            pl.BlockSpec((tm, LANES), lambda i: (i % n_pos, 0)),
        ],
        out_specs=[pl.BlockSpec((tm, GROUP_W), tok)] * 7,
        out_shape=[out] * 7,
        scratch_shapes=[pltpu.VMEM(w_in.shape, BF16)],
        compiler_params=_cparams(("arbitrary",)),
        name="in_proj",
    )(x2, w_in.astype(F32), cos_t, sin_t)


def _log_sigmoid(x):
    return jnp.minimum(x, 0.0) - jnp.log1p(jnp.exp(-jnp.abs(x)))


def _retention_kernel(q_ref, k_ref, v_ref, g_ref, decf_ref, decb_ref, decfd_ref, decbd_ref,
                      gain_ref, wd_f, o_ref, wd_b, kv_ref, st_ref, dmat_ref):
    wd_b[...] = wd_f[...].astype(BF16)
    c = RET_CHUNK
    n_chunks = q_ref.shape[0] // c
    pair_w = 2 * HEAD_DIM

    lgf = _log_sigmoid(decf_ref[...])
    lgb = _log_sigmoid(decb_ref[...])
    row = lax.broadcasted_iota(I32, (c, 1), 0).astype(F32)
    k_dec_f = jnp.exp((c - 1.0 - row) * lgf)
    k_dec_b = jnp.exp(row * lgb)
    q_dec_f = jnp.exp((row + 1.0) * lgf)
    q_dec_b = jnp.exp((c - row) * lgb)
    chunk_dec_f = jnp.exp(c * lgf)
    chunk_dec_b = jnp.exp(c * lgb)

    lane = lax.broadcasted_iota(I32, (1, pair_w), 1)
    head0 = lane < HEAD_DIM
    r2 = lax.broadcasted_iota(I32, (pair_w, pair_w), 0) // HEAD_DIM
    c2 = lax.broadcasted_iota(I32, (pair_w, pair_w), 1) // HEAD_DIM
    same_head = r2 == c2
    block_diag = jnp.where(same_head, 1.0, 0.0)
    seg_avg = jnp.where(same_head, 1.0 / HEAD_DIM, 0.0).astype(BF16)

    lgf_d = _log_sigmoid(decfd_ref[0])
    lgb_d = _log_sigmoid(decbd_ref[0])
    di = lax.broadcasted_iota(I32, (c, 2 * c), 0)
    dj = lax.broadcasted_iota(I32, (c, 2 * c), 1) % c
    diff = (di - dj).astype(F32)
    dmat_ref[...] = jnp.where(diff >= 0.0, jnp.exp(diff * lgf_d), jnp.exp(-diff * lgb_d))

    def chunk(ref, n):
        return ref[pl.ds(pl.multiple_of(n * c, c), c), :]

    unroll = RET_UNROLL

    def summarize(nb, carry):
        for u in range(unroll):
            n = nb * unroll + u
            kf = chunk(k_ref, n).astype(F32)
            kst = jnp.concatenate([kf * k_dec_f, kf * k_dec_b], axis=1).astype(BF16)
            kv_ref[n] = _dot_tn(kst, chunk(v_ref, n))
        return carry

    lax.fori_loop(0, n_chunks // unroll, summarize, 0)

    def scan(i, states):
        fwd, bwd = states
        nb = n_chunks - 1 - i
        st_ref[i, 0:pair_w, :] = fwd.astype(BF16)
        st_ref[nb, pair_w:2 * pair_w, :] = bwd.astype(BF16)
        return (fwd * chunk_dec_f + kv_ref[i, 0:pair_w, :] * block_diag,
                bwd * chunk_dec_b + kv_ref[nb, pair_w:2 * pair_w, :] * block_diag)

    zero_state = jnp.zeros((pair_w, pair_w), F32)
    lax.fori_loop(0, n_chunks, scan, (zero_state, zero_state), unroll=True)

    gain = gain_ref[...]

    seg_avg2 = jnp.concatenate([seg_avg, seg_avg], axis=0)

    def seg_mean(z):
        hi = z.astype(BF16)
        lo = (z - hi.astype(F32)).astype(BF16)
        return _dot(jnp.concatenate([hi, lo], axis=1), seg_avg2)

    def emit(nb, carry):
        ys = []
        for u in range(unroll):
            n = nb * unroll + u
            q = chunk(q_ref, n)
            k = chunk(k_ref, n)
            v = chunk(v_ref, n)
            zero = jnp.zeros_like(k)
            k_st = jnp.concatenate([jnp.where(head0, k, zero), jnp.where(head0, zero, k)], axis=0)
            v_st = jnp.concatenate([jnp.where(head0, v, zero), jnp.where(head0, zero, v)], axis=0)
            scores = _dot_nt(q, k_st) * dmat_ref[...]
            qf = q.astype(F32)
            q_st = jnp.concatenate([qf * q_dec_f, qf * q_dec_b], axis=1).astype(BF16)
            ys.append(_dot(scores.astype(BF16), v_st) + _dot(q_st, st_ref[n]))
        y = jnp.concatenate(ys, axis=0)
        mu = seg_mean(y)
        d = y - mu
        var = seg_mean(d * d)
        yn = d * lax.rsqrt(var + GN_EPS) * gain
        rows = pl.ds(pl.multiple_of(nb * (unroll * c), unroll * c), unroll * c)
        o_ref[rows, :] = (g_ref[rows, :].astype(F32) * yn).astype(BF16)
        return carry

    lax.fori_loop(0, n_chunks // unroll, emit, 0)


def _retention(rq, rk, rv, rg, dec_f, dec_b, gain, w_down, batch, seq):
    t = rq.shape[0]
    c = RET_CHUNK
    n_pairs = HEADS // 2
    pair_w = 2 * HEAD_DIM
    n_exp = w_down.shape[0]
    assert n_exp % (batch * n_pairs) == 0
    epb = n_exp // (batch * n_pairs)
    wd_spec = pl.BlockSpec((epb,) + w_down.shape[1:], lambda b, p: (b * n_pairs + p, 0, 0))
    dec_f_lane = jnp.repeat(dec_f.astype(F32), HEAD_DIM)[None, :]
    dec_b_lane = jnp.repeat(dec_b.astype(F32), HEAD_DIM)[None, :]
    dec_f_col = jnp.repeat(dec_f.astype(F32), c).reshape(n_pairs, 1, 2 * c)
    dec_b_col = jnp.repeat(dec_b.astype(F32), c).reshape(n_pairs, 1, 2 * c)
    tok = pl.BlockSpec((seq, pair_w), lambda b, p: (b, p))
    lane_spec = pl.BlockSpec((1, pair_w), lambda b, p: (0, p))
    col_spec = pl.BlockSpec((1, 1, 2 * c), lambda b, p: (p, 0, 0))
    return pl.pallas_call(
        _retention_kernel,
        grid=(batch, n_pairs),
        in_specs=[tok, tok, tok, tok, lane_spec, lane_spec, col_spec, col_spec, lane_spec, wd_spec],
        out_specs=[tok, wd_spec],
        out_shape=[jax.ShapeDtypeStruct((t, GROUP_W), BF16), jax.ShapeDtypeStruct(w_down.shape, BF16)],
        scratch_shapes=[
            pltpu.VMEM((seq // c, 2 * pair_w, pair_w), F32),
            pltpu.VMEM((seq // c, 2 * pair_w, pair_w), BF16),
            pltpu.VMEM((c, 2 * c), F32),
        ],
        compiler_params=_cparams(("parallel", "parallel")),
        name="retention",
    )(rq, rk, rv, rg, dec_f_lane, dec_b_lane, dec_f_col, dec_b_col, gain.astype(F32)[None, :],
      w_down.astype(F32))


N_ROW_OFFSETS = 2 * NA_WIN_ROWS - 1
N_COL_OFFSETS = 2 * NA_WIN_COLS - 1


def _natten_row_offsets(rows):
    n_blocks = rows // NA_ROWS
    starts = {0: 0, 1: NA_ROWS - NA_WIN_ROWS // 2, 2: rows - NA_KEY_ROWS}
    blocks = {0: 0, 1: 1, 2: n_blocks - 1}
    table = []
    for v in range(3):
        per_a = []
        for a in range(NA_ROWS):
            r = blocks[v] * NA_ROWS + a
            rs = min(max(r - NA_WIN_ROWS // 2, 0), rows - NA_WIN_ROWS)
            per_kl = []
            for kl in range(NA_KEY_ROWS):
                kr = starts[v] + kl
                per_kl.append(kr - r + NA_WIN_ROWS - 1 if rs <= kr < rs + NA_WIN_ROWS else None)
            per_a.append(per_kl)
        table.append(per_a)
    return table


def _natten_live_columns(rows):
    offsets = _natten_row_offsets(rows)
    live = []
    for blk in range(NA_STEP_BLOCKS):
        variants = [1] + ([0] if blk == 0 else []) + ([2] if blk == NA_STEP_BLOCKS - 1 else [])
        live.append([[j for j in range(NA_KEY_ROWS // 2)
                      if any(offsets[v][a][kl] is not None for v in variants for kl in (2 * j, 2 * j + 1))]
                     for a in range(NA_ROWS)])
    return live


def _natten_bias_kernel(rpb_ref, tab_ref, *, rows):
    offsets = _natten_row_offsets(rows)
    nk = NA_KEY_ROWS * GRID_W
    shape = (GRID_W, LANES)
    lane = lax.broadcasted_iota(I32, shape, 1)
    c = lax.broadcasted_iota(I32, shape, 0)
    second = lane >= GRID_W
    kc = lane % GRID_W
    cs = jnp.clip(c - NA_WIN_COLS // 2, 0, GRID_W - NA_WIN_COLS)
    col_ok = jnp.logical_and(kc >= cs, kc < cs + NA_WIN_COLS)
    neg = jnp.full(shape, NEG_BIG, F32)
    for hh in range(2):
        toeplitz = []
        for dr in range(N_ROW_OFFSETS):
            x = jnp.broadcast_to(rpb_ref[hh, dr:dr + 1, :] * LOG2E, shape)
            lo = pltpu.roll(x, LANES - (NA_WIN_COLS - 1), axis=1, stride=1, stride_axis=0)
            hi = pltpu.roll(x, GRID_W - (NA_WIN_COLS - 1), axis=1, stride=1, stride_axis=0)
            toeplitz.append(jnp.where(second, hi, lo))
        for v in range(3):
            for a in range(NA_ROWS):
                for j in range(NA_KEY_ROWS // 2):
                    d0, d1 = offsets[v][a][2 * j], offsets[v][a][2 * j + 1]
                    if d0 is None and d1 is None:
                        piece = neg
                    else:
                        t0 = neg if d0 is None else toeplitz[d0]
                        t1 = neg if d1 is None else toeplitz[d1]
                        piece = jnp.where(col_ok, jnp.where(second, t1, t0), neg)
                    tab_ref[v, 0, a * GRID_W:(a + 1) * GRID_W,
                            hh * nk + j * LANES:hh * nk + (j + 1) * LANES] = piece


def _natten_bias_table(rpb, rows):
    n_pairs = HEADS // 2
    nqb = NA_ROWS * GRID_W
    nkb = NA_KEY_ROWS * GRID_W
    rpb_pad = jnp.pad(rpb.astype(F32), ((0, 0), (0, 0), (0, LANES - N_COL_OFFSETS)))
    return pl.pallas_call(
        functools.partial(_natten_bias_kernel, rows=rows),
        grid=(n_pairs,),
        in_specs=[pl.BlockSpec((2, N_ROW_OFFSETS, LANES), lambda p: (p, 0, 0))],
        out_specs=pl.BlockSpec((3, 1, nqb, 2 * nkb), lambda p: (0, p, 0, 0)),
        out_shape=jax.ShapeDtypeStruct((3, n_pairs, nqb, 2 * nkb), F32),
        compiler_params=_cparams(("parallel",)),
        name="natten_bias",
    )(rpb_pad)


def _natten_kernel(q_ref, k_ref, v_ref, tab_ref, wgu_f, o_ref, wgu_b, ks_ref, vs_ref, *, rows):
    wgu_b[...] = wgu_f[...].astype(BF16)
    nq = NA_ROWS * GRID_W
    nk = NA_KEY_ROWS * GRID_W
    n_blocks = rows // NA_ROWS
    live = _natten_live_columns(rows)
    lane = lax.broadcasted_iota(I32, (1, 2 * HEAD_DIM), 1)
    head0 = lane < HEAD_DIM

    @pl.when(pl.program_id(2) == 0)
    def _():
        k = k_ref[...]
        v = v_ref[...]
        zero = jnp.zeros_like(k)
        ks_ref[0] = jnp.where(head0, k, zero)
        ks_ref[1] = jnp.where(head0, zero, k)
        ind0 = jnp.broadcast_to(jnp.where(head0, 1.0, 0.0).astype(BF16), k.shape)
        ind1 = jnp.broadcast_to(jnp.where(head0, 0.0, 1.0).astype(BF16), k.shape)
        vs_ref[0] = jnp.concatenate([jnp.where(head0, v, zero), ind0], axis=1)
        vs_ref[1] = jnp.concatenate([jnp.where(head0, zero, v), ind1], axis=1)

    for blk in range(NA_STEP_BLOCKS):
        rb = pl.program_id(2) * NA_STEP_BLOCKS + blk
        variant = jnp.where(rb == 0, 0, jnp.where(rb == n_blocks - 1, 2, 1))
        start_row = jnp.clip(rb * NA_ROWS - NA_WIN_ROWS // 2, 0, rows - NA_KEY_ROWS)
        win = pl.ds(pl.multiple_of(start_row * GRID_W, GRID_W), nk)
        q = q_ref[blk * nq:(blk + 1) * nq, :]
        k_st = jnp.concatenate([ks_ref[0, win, :], ks_ref[1, win, :]], axis=0)
        v_st = jnp.concatenate([vs_ref[0, win, :], vs_ref[1, win, :]], axis=0)
        s = _dot_nt(q, k_st) + tab_ref[variant, 0]
        dead = jnp.zeros((GRID_W, LANES), BF16)
        slabs = []
        for a in range(NA_ROWS):
            qrows = slice(a * GRID_W, (a + 1) * GRID_W)
            parts = []
            for h in range(2):
                cols = {j: s[qrows, h * nk + j * LANES:h * nk + (j + 1) * LANES] for j in live[blk][a]}
                m = jnp.max(jnp.concatenate(list(cols.values()), axis=1), axis=1, keepdims=True)
                parts += [jnp.exp2(cols[j] - m).astype(BF16) if j in cols else dead
                          for j in range(NA_KEY_ROWS // 2)]
            slabs.append(jnp.concatenate(parts, axis=1))
        out = _dot(jnp.concatenate(slabs, axis=0), v_st)
        o_ref[blk * nq:(blk + 1) * nq, :] = (out[:, :2 * HEAD_DIM] / out[:, 2 * HEAD_DIM:]).astype(BF16)


def _natten(nq, nk, nv, rpb, w_gu, batch, seq):
    t = nq.shape[0]
    rows = seq // GRID_W
    n_pairs = HEADS // 2
    n_steps = rows // (NA_ROWS * NA_STEP_BLOCKS)
    nqb = NA_ROWS * GRID_W
    nkb = NA_KEY_ROWS * GRID_W
    tab = _natten_bias_table(rpb, rows)
    total_steps = batch * n_pairs * n_steps
    n_exp = w_gu.shape[0]
    assert n_exp % total_steps == 0
    epb = n_exp // total_steps

    def expert_spec(shape):
        return pl.BlockSpec((epb,) + shape[1:], lambda b, p, s: ((b * n_pairs + p) * n_steps + s, 0, 0))

    tab_spec = pl.BlockSpec((3, 1, nqb, 2 * nkb), lambda b, p, s: (0, p, 0, 0))
    kv_spec = pl.BlockSpec((seq, 2 * HEAD_DIM), lambda b, p, s: (b, p))
    q_spec = pl.BlockSpec((NA_STEP_BLOCKS * nqb, 2 * HEAD_DIM), lambda b, p, s: (b * n_steps + s, p))
    return pl.pallas_call(
        functools.partial(_natten_kernel, rows=rows),
        grid=(batch, n_pairs, n_steps),
        in_specs=[q_spec, kv_spec, kv_spec, tab_spec, expert_spec(w_gu.shape)],
        out_specs=[q_spec, expert_spec(w_gu.shape)],
        out_shape=[jax.ShapeDtypeStruct((t, GROUP_W), BF16), jax.ShapeDtypeStruct(w_gu.shape, BF16)],
        scratch_shapes=[pltpu.VMEM((2, seq, 2 * HEAD_DIM), BF16), pltpu.VMEM((2, seq, 4 * HEAD_DIM), BF16)],
        compiler_params=_cparams(("arbitrary", "arbitrary", "arbitrary")),
        name="natten",
    )(nq, nk, nv, tab, w_gu.astype(F32))


def _layer_norm(h, gain, bias):
    mu = jnp.mean(h, axis=-1, keepdims=True)
    d = h - mu
    var = jnp.mean(d * d, axis=-1, keepdims=True)
    return d * lax.rsqrt(var + LN_EPS) * gain + bias


def _route_tile(xb, wr, rbias, tri):
    tm = xb.shape[0]
    scores = jax.nn.sigmoid(_dot_nt(wr, xb))
    biased = scores + rbias
    sub = lax.broadcasted_iota(I32, (GROUP_SIZE, tm), 0).astype(F32)
    none = float(N_EXPERTS)
    ninf = -jnp.inf

    def first_max(vals, index):
        m = jnp.max(vals, axis=0, keepdims=True)
        return m, jnp.min(jnp.where(vals == m, index, none), axis=0, keepdims=True)

    groups = [biased[g * GROUP_SIZE:(g + 1) * GROUP_SIZE, :] for g in range(N_GROUPS)]
    group_scores = []
    for g in range(N_GROUPS):
        m1, i1 = first_max(groups[g], sub)
        m2 = jnp.max(jnp.where(sub == i1, ninf, groups[g]), axis=0, keepdims=True)
        group_scores.append(m1 + m2)
    cur = jnp.concatenate(group_scores, axis=0)
    group_sel = jnp.zeros(cur.shape, F32)
    for _ in range(TOPK_GROUPS):
        _, i1 = first_max(cur, sub)
        hit = sub == i1
        group_sel = jnp.where(hit, 1.0, group_sel)
        cur = jnp.where(hit, ninf, cur)

    masked = [jnp.where(group_sel[g:g + 1, :] > 0.5, groups[g], ninf) for g in range(N_GROUPS)]
    ids = [sub + float(g * GROUP_SIZE) for g in range(N_GROUPS)]
    chosen = [jnp.zeros((GROUP_SIZE, tm), F32) for _ in range(N_GROUPS)]
    for _ in range(TOP_K):
        m = masked[0]
        for g in range(1, N_GROUPS):
            m = jnp.maximum(m, masked[g])
        m = jnp.max(m, axis=0, keepdims=True)
        cand = jnp.where(masked[0] == m, ids[0], none)
        for g in range(1, N_GROUPS):
            cand = jnp.minimum(cand, jnp.where(masked[g] == m, ids[g], none))
        first = jnp.min(cand, axis=0, keepdims=True)
        for g in range(N_GROUPS):
            hit = ids[g] == first
            chosen[g] = jnp.where(hit, 1.0, chosen[g])
            masked[g] = jnp.where(hit, ninf, masked[g])

    sel = jnp.concatenate(chosen, axis=0) > 0.5
    picked = jnp.where(sel, scores, 0.0)
    total = jnp.sum(picked, axis=0, keepdims=True)
    weight = picked / total * ROUTED_SCALE
    sel_f = jnp.where(sel, 1.0, 0.0)
    sel_b = sel_f.astype(BF16)
    before, cnt_max = [], None
    for k in range(tm // MOE_SUB):
        cols = slice(k * MOE_SUB, (k + 1) * MOE_SUB)
        before.append(_dot(sel_b[:, cols], tri))
        cnt = jnp.sum(sel_f[:, cols], axis=1, keepdims=True)
        cnt_max = cnt if cnt_max is None else jnp.maximum(cnt_max, cnt)
    rank = jnp.where(sel, jnp.concatenate(before, axis=1).astype(I32), -1)
    return weight, rank, cnt_max


def _mid_kernel(ret_ref, na_ref, x_ref, p_ref, wo_f, wsgu_f, wsd_f, wp_f, wg_f, gain_ref, bias_ref,
                wr_ref, rb_ref, pre_ref, x1b_ref, w_ref, rank_ref, cnt_ref,
                wo_ref, wsgu_ref, wsd_ref, wp_ref, wg_ref, tri_ref):
    for src_ref, dst_ref in ((wo_f, wo_ref), (wsgu_f, wsgu_ref), (wsd_f, wsd_ref), (wp_f, wp_ref),
                             (wg_f, wg_ref)):
        _cast_once(src_ref, dst_ref)

    @pl.when(pl.program_id(0) == 0)
    def _():
        i = lax.broadcasted_iota(I32, (MOE_SUB, MOE_SUB), 0)
        j = lax.broadcasted_iota(I32, (MOE_SUB, MOE_SUB), 1)
        tri_ref[...] = jnp.where(i < j, 1.0, 0.0).astype(BF16)

    sub_blocks = [slice(r, r + MOE_SUB) for r in range(0, x_ref.shape[0], MOE_SUB)]
    for rows in sub_blocks:
        mix = (_dot(ret_ref[rows, :], wo_ref[0:GROUP_W, :])
               + _dot(na_ref[rows, :], wo_ref[GROUP_W:2 * GROUP_W, :]))
        x1 = _layer_norm(ALPHA * x_ref[rows, :] + mix, gain_ref[...], bias_ref[...])
        x1b_ref[rows, :] = x1.astype(BF16)
        pre_ref[rows, :] = ALPHA * x1
    weight, rank, cnt_max = _route_tile(x1b_ref[...], wr_ref[...], rb_ref[...], tri_ref[...])
    w_ref[...] = weight
    rank_ref[...] = rank
    cnt_ref[...] = jnp.broadcast_to(cnt_max, cnt_ref.shape).astype(I32)
    for rows in sub_blocks:
        xb = x1b_ref[rows, :]
        h = _dot(xb, wsgu_ref[...])
        act = (jax.nn.silu(h[:, :EXPERT_DIM]) * h[:, EXPERT_DIM:]).astype(BF16)
        shared = _dot(act, wsd_ref[...])
        ple = _dot(p_ref[rows, :].astype(BF16), wp_ref[...]) * jax.nn.sigmoid(_dot(xb, wg_ref[...]))
        pre_ref[rows, :] += shared + ple


def _mid(ret, na, x2, p2, w_out, gain, bias, w_router, router_bias, wsgu, wsd, wp, wg):
    t = x2.shape[0]
    tm = MID_TM
    nt = t // tm
    tok = lambda i: (i, 0)
    const = lambda i: (0, 0)
    col = lambda i: (0, i)
    weights = [w.astype(F32) for w in (w_out, wsgu, wsd, wp, wg)]
    wr_t = w_router.astype(F32).T.astype(BF16)
    return pl.pallas_call(
        _mid_kernel,
        grid=(nt,),
        in_specs=[pl.BlockSpec((tm, GROUP_W), tok), pl.BlockSpec((tm, GROUP_W), tok),
                  pl.BlockSpec((tm, D_MODEL), tok), pl.BlockSpec((tm, p2.shape[1]), tok)]
                 + [_resident(w.shape) for w in weights]
                 + [pl.BlockSpec((1, D_MODEL), const), pl.BlockSpec((1, D_MODEL), const),
                    pl.BlockSpec((N_EXPERTS, D_MODEL), const), pl.BlockSpec((N_EXPERTS, 1), const)],
        out_specs=[pl.BlockSpec((tm, D_MODEL), tok), pl.BlockSpec((tm, D_MODEL), tok),
                   pl.BlockSpec((N_EXPERTS, tm), col), pl.BlockSpec((N_EXPERTS, tm), col),
                   pl.BlockSpec((N_EXPERTS, LANES), col)],
        out_shape=[jax.ShapeDtypeStruct((t, D_MODEL), F32), jax.ShapeDtypeStruct((t, D_MODEL), BF16),
                   jax.ShapeDtypeStruct((N_EXPERTS, t), F32), jax.ShapeDtypeStruct((N_EXPERTS, t), I32),
                   jax.ShapeDtypeStruct((N_EXPERTS, nt * LANES), I32)],
        scratch_shapes=[pltpu.VMEM(w.shape, BF16) for w in weights] + [pltpu.VMEM((MOE_SUB, MOE_SUB), BF16)],
        compiler_params=_cparams(("arbitrary",)),
        name="mid",
    )(ret, na, x2, p2, *weights, gain.astype(F32)[None, :], bias.astype(F32)[None, :],
      wr_t, router_bias.astype(F32)[:, None])


def _moe_kernel(cnt_ref, x_ref, rank_ref, w_ref, wgu_ref, wd_ref, pre_ref, gain_ref, bias_ref,
                o_ref, oh_ref, g_ref, y_ref):
    i = pl.program_id(0)
    eb = pl.program_id(1)
    n_eb = pl.num_programs(1)

    @pl.when(eb == 0)
    def _():
        o_ref[...] = jnp.zeros_like(o_ref)

    pre_rows = pre_ref.shape[0]
    o_ref[pl.ds(pl.multiple_of(eb * pre_rows, pre_rows), pre_rows), :] += pre_ref[...]

    max_count = cnt_ref[eb, i]

    def one_pass(p, slots):
        n_rows = MOE_EB * slots
        base = p * slots
        slot = lax.broadcasted_iota(I32, (slots, MOE_SUB), 0)
        slot_w = []
        for k in range(MOE_NSUB):
            tok = slice(k * MOE_SUB, (k + 1) * MOE_SUB)
            blocks, weights = [], []
            for j in range(MOE_EB):
                expert = pl.ds(eb * MOE_EB + j, 1)
                match = slot == (rank_ref[expert, tok] - base)
                blocks.append(jnp.where(match, 1.0, 0.0).astype(BF16))
                weights.append(jnp.sum(jnp.where(match, w_ref[expert, tok], 0.0), axis=1, keepdims=True))
            onehot = jnp.concatenate(blocks, axis=0)
            oh_ref[k, 0:n_rows, :] = onehot
            slot_w.append(weights)
            g_ref[k, 0:n_rows, :] = _dot(onehot, x_ref[tok, :]).astype(BF16)
        for j in range(MOE_EB):
            rows = slice(j * slots, (j + 1) * slots)
            xe = jnp.concatenate([g_ref[k, rows, :] for k in range(MOE_NSUB)], axis=0)
            h = _dot(xe, wgu_ref[j])
            act = (jax.nn.silu(h[:, :EXPERT_DIM]) * h[:, EXPERT_DIM:]).astype(BF16)
            wc = jnp.concatenate([slot_w[k][j] for k in range(MOE_NSUB)], axis=0)
            yw = (_dot(act, wd_ref[j]) * wc).astype(BF16)
            for k in range(MOE_NSUB):
                y_ref[k, rows, :] = yw[k * slots:(k + 1) * slots, :]
        for k in range(MOE_NSUB):
            tok = slice(k * MOE_SUB, (k + 1) * MOE_SUB)
            o_ref[tok, :] += _dot_tn(oh_ref[k, 0:n_rows, :], y_ref[k, 0:n_rows, :])

    @pl.when(max_count <= MOE_SLOTS_SMALL)
    def _():
        one_pass(0, MOE_SLOTS_SMALL)

    @pl.when(max_count > MOE_SLOTS_SMALL)
    def _():
        n_pass = lax.shift_right_logical(max_count + (MOE_SLOTS - 1), int(np.log2(MOE_SLOTS)))

        def body(p, carry):
            one_pass(p, MOE_SLOTS)
            return carry

        lax.fori_loop(0, n_pass, body, 0)

    @pl.when(eb == n_eb - 1)
    def _():
        for r in range(0, o_ref.shape[0], MOE_SUB):
            rows = slice(r, r + MOE_SUB)
            o_ref[rows, :] = _layer_norm(o_ref[rows, :], gain_ref[...], bias_ref[...])


def _moe(x1b, rank_t, w_t, counts, wgu_b, wd_b, pre, gain, bias):
    t = x1b.shape[0]
    tm = MOE_TM
    nt = t // tm
    n_eb = N_EXPERTS // MOE_EB
    pre_rows = tm // n_eb
    counts = counts.reshape(N_EXPERTS // MOE_EB, MOE_EB, nt, tm // MID_TM).max(axis=(1, 3))
    grid_spec = pltpu.PrefetchScalarGridSpec(
        num_scalar_prefetch=1,
        grid=(nt, N_EXPERTS // MOE_EB),
        in_specs=[
            pl.BlockSpec((tm, D_MODEL), lambda i, e, c: (i, 0)),
            pl.BlockSpec((N_EXPERTS, tm), lambda i, e, c: (0, i)),
            pl.BlockSpec((N_EXPERTS, tm), lambda i, e, c: (0, i)),
            pl.BlockSpec((MOE_EB, D_MODEL, 2 * EXPERT_DIM), lambda i, e, c: (e, 0, 0)),
            pl.BlockSpec((MOE_EB, EXPERT_DIM, D_MODEL), lambda i, e, c: (e, 0, 0)),
            pl.BlockSpec((pre_rows, D_MODEL), lambda i, e, c: (i * n_eb + e, 0)),
            pl.BlockSpec((1, D_MODEL), lambda i, e, c: (0, 0)),
            pl.BlockSpec((1, D_MODEL), lambda i, e, c: (0, 0)),
        ],
        out_specs=pl.BlockSpec((tm, D_MODEL), lambda i, e, c: (i, 0)),
        scratch_shapes=[pltpu.VMEM((MOE_NSUB, MOE_SUB, MOE_SUB), BF16),
                        pltpu.VMEM((MOE_NSUB, MOE_SUB, D_MODEL), BF16),
                        pltpu.VMEM((MOE_NSUB, MOE_SUB, D_MODEL), BF16)],
    )
    return pl.pallas_call(
        _moe_kernel,
        grid_spec=grid_spec,
        out_shape=jax.ShapeDtypeStruct((t, D_MODEL), F32),
        compiler_params=_cparams(("parallel", "arbitrary")),
        name="moe",
    )(counts, x1b, rank_t, w_t, wgu_b, wd_b, pre, gain.astype(F32)[None, :], bias.astype(F32)[None, :])


def _rotary_tables(seq):
    half = HEAD_DIM // 2
    inv = ROPE_BASE ** (-jnp.arange(half, dtype=F32) / half)
    ang = jnp.arange(seq, dtype=jnp.int32).astype(F32)[:, None] * inv[None, :]
    cos, sin = jnp.cos(ang), jnp.sin(ang)
    reps = LANES // HEAD_DIM
    cos_t = jnp.tile(jnp.concatenate([cos, cos], axis=1), (1, reps))
    sin_t = jnp.tile(jnp.concatenate([-sin, sin], axis=1), (1, reps))
    return cos_t, sin_t


def kernel(x, p, w_in, ret_decay_fwd, ret_decay_bwd, ret_gn_gain, na_rpb, w_out, ln1_gain, ln1_bias,
           w_router, router_bias, w_expert_gu, w_expert_down, w_shared_gu, w_shared_down,
           w_ple_proj, w_ple_gate, ln2_gain, ln2_bias):
    batch, seq, d = x.shape
    t = batch * seq
    depth = w_in.shape[0]
    assert depth == 1 and d == D_MODEL
    assert seq % PROJ_TM == 0 and seq % MOE_TM == 0 and seq % RET_CHUNK == 0
    assert MOE_TM % (N_EXPERTS // MOE_EB) == 0
    assert seq % MID_TM == 0 and MOE_TM % MID_TM == 0 and MID_TM % MOE_SUB == 0
    assert (seq // GRID_W) % (NA_ROWS * NA_STEP_BLOCKS) == 0 and seq // GRID_W >= NA_KEY_ROWS
    cos_t, sin_t = _rotary_tables(seq)
    x2 = x.reshape(t, d)
    for i in range(depth):
        rq, rk, rv, rg, nq, nk, nv = _in_proj(x2, w_in[i], cos_t, sin_t, seq)
        ret, wd_b = _retention(rq, rk, rv, rg, ret_decay_fwd[i], ret_decay_bwd[i], ret_gn_gain[i],
                               w_expert_down[i], batch, seq)
        na, wgu_b = _natten(nq, nk, nv, na_rpb[i], w_expert_gu[i], batch, seq)
        pre, x1b, w_t, rank_t, cnt = _mid(ret, na, x2, p[i].reshape(t, -1), w_out[i], ln1_gain[i], ln1_bias[i],
                                          w_router[i], router_bias[i], w_shared_gu[i], w_shared_down[i],
                                          w_ple_proj[i], w_ple_gate[i])
        x2 = _moe(x1b, rank_t, w_t, cnt[:, ::LANES], wgu_b, wd_b, pre, ln2_gain[i], ln2_bias[i])
    return x2.reshape(batch, seq, d)
```

```python
import functools

import numpy as np
import jax
import jax.numpy as jnp
from jax import lax
from jax.experimental import pallas as pl
from jax.experimental.pallas import tpu as pltpu

F32 = jnp.float32
BF16 = jnp.bfloat16
I32 = jnp.int32

D_MODEL = 1024
HEADS = 8
HEAD_DIM = 64
GROUP_W = HEADS * HEAD_DIM
ROPE_BASE = 10000.0
GN_EPS = 1e-6
LN_EPS = 1e-5
GRID_W = 64
NA_WIN_ROWS = 8
NA_WIN_COLS = 16
N_EXPERTS = 64
N_GROUPS = 8
GROUP_SIZE = N_EXPERTS // N_GROUPS
TOPK_GROUPS = 4
TOP_K = 8
EXPERT_DIM = 256
ROUTED_SCALE = 2.5
ALPHA = 2.0 ** 0.25
NEG_BIG = -1e30
LOG2E = 1.4426950408889634
NA_Q_SCALE = HEAD_DIM ** -0.5 * LOG2E

LANES = 128
VMEM_LIMIT_BYTES = 56 * 1024 * 1024

PROJ_TM = 1024
MID_TM = 1024
RET_CHUNK = 128
RET_UNROLL = 16
NA_ROWS = 4
NA_KEY_ROWS = NA_ROWS + NA_WIN_ROWS
NA_STEP_BLOCKS = 8
MOE_TM = 2048
MOE_SUB = 256
MOE_EB = 4
MOE_SLOTS = MOE_SUB // MOE_EB
MOE_SLOTS_SMALL = 48
MOE_NSUB = MOE_TM // MOE_SUB


def _cparams(sem):
    return pltpu.CompilerParams(dimension_semantics=sem, vmem_limit_bytes=VMEM_LIMIT_BYTES)


def _dot(a, b):
    return jnp.dot(a, b, preferred_element_type=F32)


def _dot_nt(a, b):
    return lax.dot_general(a, b, (((1,), (1,)), ((), ())), preferred_element_type=F32)


def _dot_tn(a, b):
    return lax.dot_general(a, b, (((0,), (0,)), ((), ())), preferred_element_type=F32)


def _cast_once(src_ref, dst_ref):
    @pl.when(pl.program_id(0) == 0)
    def _():
        dst_ref[...] = src_ref[...].astype(BF16)


def _resident(shape):
    return pl.BlockSpec(shape, lambda i: (0,) * len(shape), pipeline_mode=pl.Buffered(1))


def _in_proj_kernel(x_ref, wf_ref, cos_ref, sin_ref,
                    rq_ref, rk_ref, rv_ref, rg_ref, nq_ref, nke_ref, nko_ref, nve_ref, nvo_ref, w_ref):
    _cast_once(wf_ref, w_ref)
    xb = x_ref[...].astype(BF16)
    cos = cos_ref[...]
    sin = sin_ref[...]
    lane = lax.broadcasted_iota(I32, (1, LANES), 1)
    first_half = (lane % HEAD_DIM) < (HEAD_DIM // 2)

    def proj(g):
        return _dot(xb, w_ref[:, g * GROUP_W:(g + 1) * GROUP_W])

    def rotary(t, scale):
        outs = []
        for j in range(GROUP_W // LANES):
            c = t[:, j * LANES:(j + 1) * LANES]
            swapped = jnp.where(first_half,
                                pltpu.roll(c, LANES - HEAD_DIM // 2, axis=1),
                                pltpu.roll(c, HEAD_DIM // 2, axis=1))
            outs.append((c * cos + swapped * sin) * scale)
        return jnp.concatenate(outs, axis=1)

    rq_ref[...] = rotary(proj(0), 1.0).astype(BF16)
    rk_ref[...] = rotary(proj(1), HEAD_DIM ** -0.5).astype(BF16)
    rv_ref[...] = proj(2).astype(BF16)
    rg_ref[...] = jax.nn.silu(proj(3)).astype(BF16)
    nq_ref[...] = (proj(4) * NA_Q_SCALE).astype(BF16)
    even_head = (lax.broadcasted_iota(I32, (1, GROUP_W), 1) % (2 * HEAD_DIM)) < HEAD_DIM
    for g, even_ref, odd_ref in ((5, nke_ref, nko_ref), (6, nve_ref, nvo_ref)):
        t = proj(g)
        even_ref[...] = jnp.where(even_head, t, 0.0).astype(BF16)
        odd_ref[...] = jnp.where(even_head, 0.0, t).astype(BF16)


def _in_proj(x2, w_in, cos_t, sin_t, seq):
    t = x2.shape[0]
    tm = PROJ_TM
    n_pos = seq // tm
    out = jax.ShapeDtypeStruct((t, GROUP_W), BF16)
    tok = lambda i: (i, 0)
    return pl.pallas_call(
        _in_proj_kernel,
        grid=(t // tm,),
        in_specs=[
            pl.BlockSpec((tm, D_MODEL), tok),
            _resident(w_in.shape),
            pl.BlockSpec((tm, LANES), lambda i: (i % n_pos, 0)),
            pl.BlockSpec((tm, LANES), lambda i: (i % n_pos, 0)),
        ],
        out_specs=[pl.BlockSpec((tm, GROUP_W), tok)] * 9,
        out_shape=[out] * 9,
        scratch_shapes=[pltpu.VMEM(w_in.shape, BF16)],
        compiler_params=_cparams(("arbitrary",)),
        name="in_proj",
    )(x2, w_in.astype(F32), cos_t, sin_t)


def _log_sigmoid(x):
    return jnp.minimum(x, 0.0) - jnp.log1p(jnp.exp(-jnp.abs(x)))


def _retention_kernel(q_ref, k_ref, v_ref, g_ref, decf_ref, decb_ref, decfd_ref, decbd_ref,
                      gain_ref, wd_f, o_ref, wd_b, kv_ref, st_ref, dmat_ref):
    wd_b[...] = wd_f[...].astype(BF16)
    c = RET_CHUNK
    n_chunks = q_ref.shape[0] // c
    pair_w = 2 * HEAD_DIM

    lgf = _log_sigmoid(decf_ref[...])
    lgb = _log_sigmoid(decb_ref[...])
    row = lax.broadcasted_iota(I32, (c, 1), 0).astype(F32)
    k_dec_f = jnp.exp((c - 1.0 - row) * lgf)
    k_dec_b = jnp.exp(row * lgb)
    q_dec_f = jnp.exp((row + 1.0) * lgf)
    q_dec_b = jnp.exp((c - row) * lgb)
    chunk_dec_f = jnp.exp(c * lgf)
    chunk_dec_b = jnp.exp(c * lgb)

    lane = lax.broadcasted_iota(I32, (1, pair_w), 1)
    head0 = lane < HEAD_DIM
    r2 = lax.broadcasted_iota(I32, (pair_w, pair_w), 0) // HEAD_DIM
    c2 = lax.broadcasted_iota(I32, (pair_w, pair_w), 1) // HEAD_DIM
    same_head = r2 == c2
    block_diag = jnp.where(same_head, 1.0, 0.0)
    seg_avg = jnp.where(same_head, 1.0 / HEAD_DIM, 0.0).astype(BF16)

    lgf_d = _log_sigmoid(decfd_ref[0])
    lgb_d = _log_sigmoid(decbd_ref[0])
    di = lax.broadcasted_iota(I32, (c, 2 * c), 0)
    dj = lax.broadcasted_iota(I32, (c, 2 * c), 1) % c
    diff = (di - dj).astype(F32)
    dmat_ref[...] = jnp.where(diff >= 0.0, jnp.exp(diff * lgf_d), jnp.exp(-diff * lgb_d))

    def chunk(ref, n):
        return ref[pl.ds(pl.multiple_of(n * c, c), c), :]

    unroll = RET_UNROLL

    def summarize(nb, carry):
        for u in range(unroll):
            n = nb * unroll + u
            kf = chunk(k_ref, n).astype(F32)
            kst = jnp.concatenate([kf * k_dec_f, kf * k_dec_b], axis=1).astype(BF16)
            kv_ref[n] = _dot_tn(kst, chunk(v_ref, n))
        return carry

    lax.fori_loop(0, n_chunks // unroll, summarize, 0)

    def scan(i, states):
        fwd, bwd = states
        nb = n_chunks - 1 - i
        st_ref[i, 0:pair_w, :] = fwd.astype(BF16)
        st_ref[nb, pair_w:2 * pair_w, :] = bwd.astype(BF16)
        return (fwd * chunk_dec_f + kv_ref[i, 0:pair_w, :] * block_diag,
                bwd * chunk_dec_b + kv_ref[nb, pair_w:2 * pair_w, :] * block_diag)

    zero_state = jnp.zeros((pair_w, pair_w), F32)
    lax.fori_loop(0, n_chunks, scan, (zero_state, zero_state), unroll=True)

    gain = gain_ref[...]

    seg_avg2 = jnp.concatenate([seg_avg, seg_avg], axis=0)

    def seg_mean(z):
        hi = z.astype(BF16)
        lo = (z - hi.astype(F32)).astype(BF16)
        return _dot(jnp.concatenate([hi, lo], axis=1), seg_avg2)

    def emit(nb, carry):
        ys = []
        for u in range(unroll):
            n = nb * unroll + u
            q = chunk(q_ref, n)
            k = chunk(k_ref, n)
            v = chunk(v_ref, n)
            zero = jnp.zeros_like(k)
            k_st = jnp.concatenate([jnp.where(head0, k, zero), jnp.where(head0, zero, k)], axis=0)
            v_st = jnp.concatenate([jnp.where(head0, v, zero), jnp.where(head0, zero, v)], axis=0)
            scores = _dot_nt(q, k_st) * dmat_ref[...]
            qf = q.astype(F32)
            q_st = jnp.concatenate([qf * q_dec_f, qf * q_dec_b], axis=1).astype(BF16)
            ys.append(_dot(scores.astype(BF16), v_st) + _dot(q_st, st_ref[n]))
        y = jnp.concatenate(ys, axis=0)
        mu = seg_mean(y)
        d = y - mu
        var = seg_mean(d * d)
        yn = d * lax.rsqrt(var + GN_EPS) * gain
        rows = pl.ds(pl.multiple_of(nb * (unroll * c), unroll * c), unroll * c)
        o_ref[rows, :] = (g_ref[rows, :].astype(F32) * yn).astype(BF16)
        return carry

    lax.fori_loop(0, n_chunks // unroll, emit, 0)


def _retention(rq, rk, rv, rg, dec_f, dec_b, gain, w_down, batch, seq):
    t = rq.shape[0]
    c = RET_CHUNK
    n_pairs = HEADS // 2
    pair_w = 2 * HEAD_DIM
    n_exp = w_down.shape[0]
    assert n_exp % (batch * n_pairs) == 0
    epb = n_exp // (batch * n_pairs)
    wd_spec = pl.BlockSpec((epb,) + w_down.shape[1:], lambda b, p: (b * n_pairs + p, 0, 0))
    dec_f_lane = jnp.repeat(dec_f.astype(F32), HEAD_DIM)[None, :]
    dec_b_lane = jnp.repeat(dec_b.astype(F32), HEAD_DIM)[None, :]
    dec_f_col = jnp.repeat(dec_f.astype(F32), c).reshape(n_pairs, 1, 2 * c)
    dec_b_col = jnp.repeat(dec_b.astype(F32), c).reshape(n_pairs, 1, 2 * c)
    tok = pl.BlockSpec((seq, pair_w), lambda b, p: (b, p))
    lane_spec = pl.BlockSpec((1, pair_w), lambda b, p: (0, p))
    col_spec = pl.BlockSpec((1, 1, 2 * c), lambda b, p: (p, 0, 0))
    return pl.pallas_call(
        _retention_kernel,
        grid=(batch, n_pairs),
        in_specs=[tok, tok, tok, tok, lane_spec, lane_spec, col_spec, col_spec, lane_spec, wd_spec],
        out_specs=[tok, wd_spec],
        out_shape=[jax.ShapeDtypeStruct((t, GROUP_W), BF16), jax.ShapeDtypeStruct(w_down.shape, BF16)],
        scratch_shapes=[
            pltpu.VMEM((seq // c, 2 * pair_w, pair_w), F32),
            pltpu.VMEM((seq // c, 2 * pair_w, pair_w), BF16),
            pltpu.VMEM((c, 2 * c), F32),
        ],
        compiler_params=_cparams(("parallel", "parallel")),
        name="retention",
    )(rq, rk, rv, rg, dec_f_lane, dec_b_lane, dec_f_col, dec_b_col, gain.astype(F32)[None, :],
      w_down.astype(F32))


N_ROW_OFFSETS = 2 * NA_WIN_ROWS - 1
N_COL_OFFSETS = 2 * NA_WIN_COLS - 1


def _natten_row_offsets(rows):
    n_blocks = rows // NA_ROWS
    starts = {0: 0, 1: NA_ROWS - NA_WIN_ROWS // 2, 2: rows - NA_KEY_ROWS}
    blocks = {0: 0, 1: 1, 2: n_blocks - 1}
    table = []
    for v in range(3):
        per_a = []
        for a in range(NA_ROWS):
            r = blocks[v] * NA_ROWS + a
            rs = min(max(r - NA_WIN_ROWS // 2, 0), rows - NA_WIN_ROWS)
            per_kl = []
            for kl in range(NA_KEY_ROWS):
                kr = starts[v] + kl
                per_kl.append(kr - r + NA_WIN_ROWS - 1 if rs <= kr < rs + NA_WIN_ROWS else None)
            per_a.append(per_kl)
        table.append(per_a)
    return table


def _natten_live_columns(rows):
    offsets = _natten_row_offsets(rows)
    live = []
    for blk in range(NA_STEP_BLOCKS):
        variants = [1] + ([0] if blk == 0 else []) + ([2] if blk == NA_STEP_BLOCKS - 1 else [])
        live.append([[j for j in range(NA_KEY_ROWS // 2)
                      if any(offsets[v][a][kl] is not None for v in variants for kl in (2 * j, 2 * j + 1))]
                     for a in range(NA_ROWS)])
    return live


def _natten_bias_kernel(rpb_ref, tab_ref, *, rows):
    offsets = _natten_row_offsets(rows)
    nk = NA_KEY_ROWS * GRID_W
    shape = (GRID_W, LANES)
    lane = lax.broadcasted_iota(I32, shape, 1)
    c = lax.broadcasted_iota(I32, shape, 0)
    second = lane >= GRID_W
    kc = lane % GRID_W
    cs = jnp.clip(c - NA_WIN_COLS // 2, 0, GRID_W - NA_WIN_COLS)
    col_ok = jnp.logical_and(kc >= cs, kc < cs + NA_WIN_COLS)
    neg = jnp.full(shape, NEG_BIG, F32)
    for hh in range(2):
        toeplitz = []
        for dr in range(N_ROW_OFFSETS):
            x = jnp.broadcast_to(rpb_ref[hh, dr:dr + 1, :] * LOG2E, shape)
            lo = pltpu.roll(x, LANES - (NA_WIN_COLS - 1), axis=1, stride=1, stride_axis=0)
            hi = pltpu.roll(x, GRID_W - (NA_WIN_COLS - 1), axis=1, stride=1, stride_axis=0)
            toeplitz.append(jnp.where(second, hi, lo))
        for v in range(3):
            for a in range(NA_ROWS):
                for j in range(NA_KEY_ROWS // 2):
                    d0, d1 = offsets[v][a][2 * j], offsets[v][a][2 * j + 1]
                    if d0 is None and d1 is None:
                        piece = neg
                    else:
                        t0 = neg if d0 is None else toeplitz[d0]
                        t1 = neg if d1 is None else toeplitz[d1]
                        piece = jnp.where(col_ok, jnp.where(second, t1, t0), neg)
                    tab_ref[v, 0, a * GRID_W:(a + 1) * GRID_W,
                            hh * nk + j * LANES:hh * nk + (j + 1) * LANES] = piece


def _natten_bias_table(rpb, rows):
    n_pairs = HEADS // 2
    nqb = NA_ROWS * GRID_W
    nkb = NA_KEY_ROWS * GRID_W
    rpb_pad = jnp.pad(rpb.astype(F32), ((0, 0), (0, 0), (0, LANES - N_COL_OFFSETS)))
    return pl.pallas_call(
        functools.partial(_natten_bias_kernel, rows=rows),
        grid=(n_pairs,),
        in_specs=[pl.BlockSpec((2, N_ROW_OFFSETS, LANES), lambda p: (p, 0, 0))],
        out_specs=pl.BlockSpec((3, 1, nqb, 2 * nkb), lambda p: (0, p, 0, 0)),
        out_shape=jax.ShapeDtypeStruct((3, n_pairs, nqb, 2 * nkb), F32),
        compiler_params=_cparams(("parallel",)),
        name="natten_bias",
    )(rpb_pad)


def _natten_kernel(q_ref, ke_ref, ko_ref, ve_ref, vo_ref, tab_ref, wgu_f, o_ref, wgu_b, *, rows):
    wgu_b[...] = wgu_f[...].astype(BF16)
    nq = NA_ROWS * GRID_W
    nk = NA_KEY_ROWS * GRID_W
    n_blocks = rows // NA_ROWS
    live = _natten_live_columns(rows)
    head0 = lax.broadcasted_iota(I32, (1, 2 * HEAD_DIM), 1) < HEAD_DIM
    ind0 = jnp.broadcast_to(jnp.where(head0, 1.0, 0.0).astype(BF16), (nk, 2 * HEAD_DIM))
    ind1 = jnp.broadcast_to(jnp.where(head0, 0.0, 1.0).astype(BF16), (nk, 2 * HEAD_DIM))

    for blk in range(NA_STEP_BLOCKS):
        rb = pl.program_id(2) * NA_STEP_BLOCKS + blk
        variant = jnp.where(rb == 0, 0, jnp.where(rb == n_blocks - 1, 2, 1))
        start_row = jnp.clip(rb * NA_ROWS - NA_WIN_ROWS // 2, 0, rows - NA_KEY_ROWS)
        win = pl.ds(pl.multiple_of(start_row * GRID_W, GRID_W), nk)
        q = q_ref[blk * nq:(blk + 1) * nq, :]
        k_st = jnp.concatenate([ke_ref[win, :], ko_ref[win, :]], axis=0)
        v_st = jnp.concatenate([jnp.concatenate([ve_ref[win, :], ind0], axis=1),
                                jnp.concatenate([vo_ref[win, :], ind1], axis=1)], axis=0)
        s = _dot_nt(q, k_st) + tab_ref[variant, 0]
        dead = jnp.zeros((GRID_W, LANES), BF16)
        slabs = []
        for a in range(NA_ROWS):
            qrows = slice(a * GRID_W, (a + 1) * GRID_W)
            parts = []
            for h in range(2):
                cols = {j: s[qrows, h * nk + j * LANES:h * nk + (j + 1) * LANES] for j in live[blk][a]}
                m = jnp.max(jnp.concatenate(list(cols.values()), axis=1), axis=1, keepdims=True)
                parts += [jnp.exp2(cols[j] - m).astype(BF16) if j in cols else dead
                          for j in range(NA_KEY_ROWS // 2)]
            slabs.append(jnp.concatenate(parts, axis=1))
        out = _dot(jnp.concatenate(slabs, axis=0), v_st)
        o_ref[blk * nq:(blk + 1) * nq, :] = (out[:, :2 * HEAD_DIM] / out[:, 2 * HEAD_DIM:]).astype(BF16)


def _natten(nq, nke, nko, nve, nvo, rpb, w_gu, batch, seq):
    t = nq.shape[0]
    rows = seq // GRID_W
    n_pairs = HEADS // 2
    n_steps = rows // (NA_ROWS * NA_STEP_BLOCKS)
    nqb = NA_ROWS * GRID_W
    nkb = NA_KEY_ROWS * GRID_W
    tab = _natten_bias_table(rpb, rows)
    total_steps = batch * n_pairs * n_steps
    n_exp = w_gu.shape[0]
    assert n_exp % total_steps == 0
    epb = n_exp // total_steps

    def expert_spec(shape):
        return pl.BlockSpec((epb,) + shape[1:], lambda b, p, s: ((b * n_pairs + p) * n_steps + s, 0, 0))

    tab_spec = pl.BlockSpec((3, 1, nqb, 2 * nkb), lambda b, p, s: (0, p, 0, 0))
    kv_spec = pl.BlockSpec((seq, 2 * HEAD_DIM), lambda b, p, s: (b, p))
    q_spec = pl.BlockSpec((NA_STEP_BLOCKS * nqb, 2 * HEAD_DIM), lambda b, p, s: (b * n_steps + s, p))
    return pl.pallas_call(
        functools.partial(_natten_kernel, rows=rows),
        grid=(batch, n_pairs, n_steps),
        in_specs=[q_spec, kv_spec, kv_spec, kv_spec, kv_spec, tab_spec, expert_spec(w_gu.shape)],
        out_specs=[q_spec, expert_spec(w_gu.shape)],
        out_shape=[jax.ShapeDtypeStruct((t, GROUP_W), BF16), jax.ShapeDtypeStruct(w_gu.shape, BF16)],
        compiler_params=_cparams(("parallel", "parallel", "arbitrary")),
        name="natten",
    )(nq, nke, nko, nve, nvo, tab, w_gu.astype(F32))


def _layer_norm(h, gain, bias):
    mu = jnp.mean(h, axis=-1, keepdims=True)
    d = h - mu
    var = jnp.mean(d * d, axis=-1, keepdims=True)
    return d * lax.rsqrt(var + LN_EPS) * gain + bias


def _route_tile(xb, wr, rbias, tri):
    tm = xb.shape[0]
    scores = jax.nn.sigmoid(_dot_nt(wr, xb))
    biased = scores + rbias
    sub = lax.broadcasted_iota(I32, (GROUP_SIZE, tm), 0).astype(F32)
    none = float(N_EXPERTS)
    ninf = -jnp.inf

    def first_max(vals, index):
        m = jnp.max(vals, axis=0, keepdims=True)
        return m, jnp.min(jnp.where(vals == m, index, none), axis=0, keepdims=True)

    groups = [biased[g * GROUP_SIZE:(g + 1) * GROUP_SIZE, :] for g in range(N_GROUPS)]
    group_scores = []
    for g in range(N_GROUPS):
        m1, i1 = first_max(groups[g], sub)
        m2 = jnp.max(jnp.where(sub == i1, ninf, groups[g]), axis=0, keepdims=True)
        group_scores.append(m1 + m2)
    cur = jnp.concatenate(group_scores, axis=0)
    group_sel = jnp.zeros(cur.shape, F32)
    for _ in range(TOPK_GROUPS):
        _, i1 = first_max(cur, sub)
        hit = sub == i1
        group_sel = jnp.where(hit, 1.0, group_sel)
        cur = jnp.where(hit, ninf, cur)

    masked = [jnp.where(group_sel[g:g + 1, :] > 0.5, groups[g], ninf) for g in range(N_GROUPS)]
    ids = [sub + float(g * GROUP_SIZE) for g in range(N_GROUPS)]
    chosen = [jnp.zeros((GROUP_SIZE, tm), F32) for _ in range(N_GROUPS)]
    for _ in range(TOP_K):
        m = masked[0]
        for g in range(1, N_GROUPS):
            m = jnp.maximum(m, masked[g])
        m = jnp.max(m, axis=0, keepdims=True)
        cand = jnp.where(masked[0] == m, ids[0], none)
        for g in range(1, N_GROUPS):
            cand = jnp.minimum(cand, jnp.where(masked[g] == m, ids[g], none))
        first = jnp.min(cand, axis=0, keepdims=True)
        for g in range(N_GROUPS):
            hit = ids[g] == first
            chosen[g] = jnp.where(hit, 1.0, chosen[g])
            masked[g] = jnp.where(hit, ninf, masked[g])

    sel = jnp.concatenate(chosen, axis=0) > 0.5
    picked = jnp.where(sel, scores, 0.0)
    total = jnp.sum(picked, axis=0, keepdims=True)
    weight = picked / total * ROUTED_SCALE
    sel_f = jnp.where(sel, 1.0, 0.0)
    sel_b = sel_f.astype(BF16)
    before, cnt_max = [], None
    for k in range(tm // MOE_SUB):
        cols = slice(k * MOE_SUB, (k + 1) * MOE_SUB)
        before.append(_dot(sel_b[:, cols], tri))
        cnt = jnp.sum(sel_f[:, cols], axis=1, keepdims=True)
        cnt_max = cnt if cnt_max is None else jnp.maximum(cnt_max, cnt)
    rank = jnp.where(sel, jnp.concatenate(before, axis=1).astype(I32), -1)
    return weight, rank, cnt_max


def _mid_kernel(ret_ref, na_ref, x_ref, p_ref, wo_f, wsgu_f, wsd_f, wp_f, wg_f, gain_ref, bias_ref,
                wr_ref, rb_ref, pre_ref, x1b_ref, w_ref, rank_ref, cnt_ref,
                wo_ref, wsgu_ref, wsd_ref, wp_ref, wg_ref, tri_ref):
    for src_ref, dst_ref in ((wo_f, wo_ref), (wsgu_f, wsgu_ref), (wsd_f, wsd_ref), (wp_f, wp_ref),
                             (wg_f, wg_ref)):
        _cast_once(src_ref, dst_ref)

    @pl.when(pl.program_id(0) == 0)
    def _():
        i = lax.broadcasted_iota(I32, (MOE_SUB, MOE_SUB), 0)
        j = lax.broadcasted_iota(I32, (MOE_SUB, MOE_SUB), 1)
        tri_ref[...] = jnp.where(i < j, 1.0, 0.0).astype(BF16)

    sub_blocks = [slice(r, r + MOE_SUB) for r in range(0, x_ref.shape[0], MOE_SUB)]
    for rows in sub_blocks:
        mix = (_dot(ret_ref[rows, :], wo_ref[0:GROUP_W, :])
               + _dot(na_ref[rows, :], wo_ref[GROUP_W:2 * GROUP_W, :]))
        x1 = _layer_norm(ALPHA * x_ref[rows, :] + mix, gain_ref[...], bias_ref[...])
        x1b_ref[rows, :] = x1.astype(BF16)
        pre_ref[rows, :] = ALPHA * x1
    weight, rank, cnt_max = _route_tile(x1b_ref[...], wr_ref[...], rb_ref[...], tri_ref[...])
    w_ref[...] = weight
    rank_ref[...] = rank
    cnt_ref[...] = jnp.broadcast_to(cnt_max, cnt_ref.shape).astype(I32)
    for rows in sub_blocks:
        xb = x1b_ref[rows, :]
        h = _dot(xb, wsgu_ref[...])
        act = (jax.nn.silu(h[:, :EXPERT_DIM]) * h[:, EXPERT_DIM:]).astype(BF16)
        shared = _dot(act, wsd_ref[...])
        ple = _dot(p_ref[rows, :].astype(BF16), wp_ref[...]) * jax.nn.sigmoid(_dot(xb, wg_ref[...]))
        pre_ref[rows, :] += shared + ple


def _mid(ret, na, x2, p2, w_out, gain, bias, w_router, router_bias, wsgu, wsd, wp, wg):
    t = x2.shape[0]
    tm = MID_TM
    nt = t // tm
    tok = lambda i: (i, 0)
    const = lambda i: (0, 0)
    col = lambda i: (0, i)
    weights = [w.astype(F32) for w in (w_out, wsgu, wsd, wp, wg)]
    wr_t = w_router.astype(F32).T.astype(BF16)
    return pl.pallas_call(
        _mid_kernel,
        grid=(nt,),
        in_specs=[pl.BlockSpec((tm, GROUP_W), tok), pl.BlockSpec((tm, GROUP_W), tok),
                  pl.BlockSpec((tm, D_MODEL), tok), pl.BlockSpec((tm, p2.shape[1]), tok)]
                 + [_resident(w.shape) for w in weights]
                 + [pl.BlockSpec((1, D_MODEL), const), pl.BlockSpec((1, D_MODEL), const),
                    pl.BlockSpec((N_EXPERTS, D_MODEL), const), pl.BlockSpec((N_EXPERTS, 1), const)],
        out_specs=[pl.BlockSpec((tm, D_MODEL), tok), pl.BlockSpec((tm, D_MODEL), tok),
                   pl.BlockSpec((N_EXPERTS, tm), col), pl.BlockSpec((N_EXPERTS, tm), col),
                   pl.BlockSpec((N_EXPERTS, LANES), col)],
        out_shape=[jax.ShapeDtypeStruct((t, D_MODEL), F32), jax.ShapeDtypeStruct((t, D_MODEL), BF16),
                   jax.ShapeDtypeStruct((N_EXPERTS, t), F32), jax.ShapeDtypeStruct((N_EXPERTS, t), I32),
                   jax.ShapeDtypeStruct((N_EXPERTS, nt * LANES), I32)],
        scratch_shapes=[pltpu.VMEM(w.shape, BF16) for w in weights] + [pltpu.VMEM((MOE_SUB, MOE_SUB), BF16)],
        compiler_params=_cparams(("arbitrary",)),
        name="mid",
    )(ret, na, x2, p2, *weights, gain.astype(F32)[None, :], bias.astype(F32)[None, :],
      wr_t, router_bias.astype(F32)[:, None])


def _moe_kernel(cnt_ref, x_ref, rank_ref, w_ref, wgu_ref, wd_ref, pre_ref, gain_ref, bias_ref,
                o_ref, oh_ref, g_ref, y_ref):
    i = pl.program_id(0)
    eb = pl.program_id(1)
    n_eb = pl.num_programs(1)

    @pl.when(eb == 0)
    def _():
        o_ref[...] = jnp.zeros_like(o_ref)

    pre_rows = pre_ref.shape[0]
    o_ref[pl.ds(pl.multiple_of(eb * pre_rows, pre_rows), pre_rows), :] += pre_ref[...]

    max_count = cnt_ref[eb, i]

    def one_pass(p, slots):
        n_rows = MOE_EB * slots
        base = p * slots
        slot = lax.broadcasted_iota(I32, (slots, MOE_SUB), 0)
        slot_w = []
        for k in range(MOE_NSUB):
            tok = slice(k * MOE_SUB, (k + 1) * MOE_SUB)
            blocks, weights = [], []
            for j in range(MOE_EB):
                expert = pl.ds(eb * MOE_EB + j, 1)
                match = slot == (rank_ref[expert, tok] - base)
                blocks.append(jnp.where(match, 1.0, 0.0).astype(BF16))
                weights.append(jnp.sum(jnp.where(match, w_ref[expert, tok], 0.0), axis=1, keepdims=True))
            onehot = jnp.concatenate(blocks, axis=0)
            oh_ref[k, 0:n_rows, :] = onehot
            slot_w.append(weights)
            g_ref[k, 0:n_rows, :] = _dot(onehot, x_ref[tok, :]).astype(BF16)
        for j in range(MOE_EB):
            rows = slice(j * slots, (j + 1) * slots)
            xe = jnp.concatenate([g_ref[k, rows, :] for k in range(MOE_NSUB)], axis=0)
            h = _dot(xe, wgu_ref[j])
            act = (jax.nn.silu(h[:, :EXPERT_DIM]) * h[:, EXPERT_DIM:]).astype(BF16)
            wc = jnp.concatenate([slot_w[k][j] for k in range(MOE_NSUB)], axis=0)
            yw = (_dot(act, wd_ref[j]) * wc).astype(BF16)
            for k in range(MOE_NSUB):
                y_ref[k, rows, :] = yw[k * slots:(k + 1) * slots, :]
        for k in range(MOE_NSUB):
            tok = slice(k * MOE_SUB, (k + 1) * MOE_SUB)
            o_ref[tok, :] += _dot_tn(oh_ref[k, 0:n_rows, :], y_ref[k, 0:n_rows, :])

    @pl.when(max_count <= MOE_SLOTS_SMALL)
    def _():
        one_pass(0, MOE_SLOTS_SMALL)

    @pl.when(max_count > MOE_SLOTS_SMALL)
    def _():
        n_pass = lax.shift_right_logical(max_count + (MOE_SLOTS - 1), int(np.log2(MOE_SLOTS)))

        def body(p, carry):
            one_pass(p, MOE_SLOTS)
            return carry

        lax.fori_loop(0, n_pass, body, 0)

    @pl.when(eb == n_eb - 1)
    def _():
        for r in range(0, o_ref.shape[0], MOE_SUB):
            rows = slice(r, r + MOE_SUB)
            o_ref[rows, :] = _layer_norm(o_ref[rows, :], gain_ref[...], bias_ref[...])


def _moe(x1b, rank_t, w_t, counts, wgu_b, wd_b, pre, gain, bias):
    t = x1b.shape[0]
    tm = MOE_TM
    nt = t // tm
    n_eb = N_EXPERTS // MOE_EB
    pre_rows = tm // n_eb
    counts = counts.reshape(N_EXPERTS // MOE_EB, MOE_EB, nt, tm // MID_TM).max(axis=(1, 3))
    grid_spec = pltpu.PrefetchScalarGridSpec(
        num_scalar_prefetch=1,
        grid=(nt, N_EXPERTS // MOE_EB),
        in_specs=[
            pl.BlockSpec((tm, D_MODEL), lambda i, e, c: (i, 0)),
            pl.BlockSpec((N_EXPERTS, tm), lambda i, e, c: (0, i)),
            pl.BlockSpec((N_EXPERTS, tm), lambda i, e, c: (0, i)),
            pl.BlockSpec((MOE_EB, D_MODEL, 2 * EXPERT_DIM), lambda i, e, c: (e, 0, 0)),
            pl.BlockSpec((MOE_EB, EXPERT_DIM, D_MODEL), lambda i, e, c: (e, 0, 0)),
            pl.BlockSpec((pre_rows, D_MODEL), lambda i, e, c: (i * n_eb + e, 0)),
            pl.BlockSpec((1, D_MODEL), lambda i, e, c: (0, 0)),
            pl.BlockSpec((1, D_MODEL), lambda i, e, c: (0, 0)),
        ],
        out_specs=pl.BlockSpec((tm, D_MODEL), lambda i, e, c: (i, 0)),
        scratch_shapes=[pltpu.VMEM((MOE_NSUB, MOE_SUB, MOE_SUB), BF16),
                        pltpu.VMEM((MOE_NSUB, MOE_SUB, D_MODEL), BF16),
                        pltpu.VMEM((MOE_NSUB, MOE_SUB, D_MODEL), BF16)],
    )
    return pl.pallas_call(
        _moe_kernel,
        grid_spec=grid_spec,
        out_shape=jax.ShapeDtypeStruct((t, D_MODEL), F32),
        compiler_params=_cparams(("parallel", "arbitrary")),
        name="moe",
    )(counts, x1b, rank_t, w_t, wgu_b, wd_b, pre, gain.astype(F32)[None, :], bias.astype(F32)[None, :])


def _rotary_tables(seq):
    half = HEAD_DIM // 2
    inv = ROPE_BASE ** (-jnp.arange(half, dtype=F32) / half)
    ang = jnp.arange(seq, dtype=jnp.int32).astype(F32)[:, None] * inv[None, :]
    cos, sin = jnp.cos(ang), jnp.sin(ang)
    reps = LANES // HEAD_DIM
    cos_t = jnp.tile(jnp.concatenate([cos, cos], axis=1), (1, reps))
    sin_t = jnp.tile(jnp.concatenate([-sin, sin], axis=1), (1, reps))
    return cos_t, sin_t


def kernel(x, p, w_in, ret_decay_fwd, ret_decay_bwd, ret_gn_gain, na_rpb, w_out, ln1_gain, ln1_bias,
           w_router, router_bias, w_expert_gu, w_expert_down, w_shared_gu, w_shared_down,
           w_ple_proj, w_ple_gate, ln2_gain, ln2_bias):
    batch, seq, d = x.shape
    t = batch * seq
    depth = w_in.shape[0]
    assert depth == 1 and d == D_MODEL
    assert seq % PROJ_TM == 0 and seq % MOE_TM == 0 and seq % RET_CHUNK == 0
    assert MOE_TM % (N_EXPERTS // MOE_EB) == 0
    assert seq % MID_TM == 0 and MOE_TM % MID_TM == 0 and MID_TM % MOE_SUB == 0
    assert (seq // GRID_W) % (NA_ROWS * NA_STEP_BLOCKS) == 0 and seq // GRID_W >= NA_KEY_ROWS
    cos_t, sin_t = _rotary_tables(seq)
    x2 = x.reshape(t, d)
    for i in range(depth):
        rq, rk, rv, rg, nq, nke, nko, nve, nvo = _in_proj(x2, w_in[i], cos_t, sin_t, seq)
        ret, wd_b = _retention(rq, rk, rv, rg, ret_decay_fwd[i], ret_decay_bwd[i], ret_gn_gain[i],
                               w_expert_down[i], batch, seq)
        na, wgu_b = _natten(nq, nke, nko, nve, nvo, na_rpb[i], w_expert_gu[i], batch, seq)
        pre, x1b, w_t, rank_t, cnt = _mid(ret, na, x2, p[i].reshape(t, -1), w_out[i], ln1_gain[i], ln1_bias[i],
                                          w_router[i], router_bias[i], w_shared_gu[i], w_shared_down[i],
                                          w_ple_proj[i], w_ple_gate[i])
        x2 = _moe(x1b, rank_t, w_t, cnt[:, ::LANES], wgu_b, wd_b, pre, ln2_gain[i], ln2_bias[i])
    return x2.reshape(batch, seq, d)
```

```python
import functools

import numpy as np
import jax
import jax.numpy as jnp
from jax import lax
from jax.experimental import pallas as pl
from jax.experimental.pallas import tpu as pltpu

F32 = jnp.float32
BF16 = jnp.bfloat16
I32 = jnp.int32

D_MODEL = 1024
HEADS = 8
HEAD_DIM = 64
GROUP_W = HEADS * HEAD_DIM
ROPE_BASE = 10000.0
GN_EPS = 1e-6
LN_EPS = 1e-5
GRID_W = 64
NA_WIN_ROWS = 8
NA_WIN_COLS = 16
N_EXPERTS = 64
N_GROUPS = 8
GROUP_SIZE = N_EXPERTS // N_GROUPS
TOPK_GROUPS = 4
TOP_K = 8
EXPERT_DIM = 256
ROUTED_SCALE = 2.5
ALPHA = 2.0 ** 0.25
NEG_BIG = -1e30
LOG2E = 1.4426950408889634
NA_Q_SCALE = HEAD_DIM ** -0.5 * LOG2E

LANES = 128
VMEM_LIMIT_BYTES = 56 * 1024 * 1024

PROJ_TM = 1024
MID_TM = 1024
RET_CHUNK = 128
RET_UNROLL = 16
NA_ROWS = 4
NA_KEY_ROWS = NA_ROWS + NA_WIN_ROWS
NA_STEP_BLOCKS = 8
MOE_TM = 2048
MOE_SUB = 256
MOE_EB = 4
MOE_SLOTS = MOE_SUB // MOE_EB
MOE_SLOTS_SMALL = 48
MOE_NSUB = MOE_TM // MOE_SUB


def _cparams(sem):
    return pltpu.CompilerParams(dimension_semantics=sem, vmem_limit_bytes=VMEM_LIMIT_BYTES)


def _dot(a, b):
    return jnp.dot(a, b, preferred_element_type=F32)


def _dot_nt(a, b):
    return lax.dot_general(a, b, (((1,), (1,)), ((), ())), preferred_element_type=F32)


def _dot_tn(a, b):
    return lax.dot_general(a, b, (((0,), (0,)), ((), ())), preferred_element_type=F32)


def _cast_once(src_ref, dst_ref):
    @pl.when(pl.program_id(0) == 0)
    def _():
        dst_ref[...] = src_ref[...].astype(BF16)


def _resident(shape):
    return pl.BlockSpec(shape, lambda i: (0,) * len(shape), pipeline_mode=pl.Buffered(1))


def _in_proj_kernel(x_ref, wf_ref, cos_ref, sin_ref,
                    rq_ref, rk_ref, rv_ref, rg_ref, nq_ref, nk_ref, nv_ref, w_ref):
    _cast_once(wf_ref, w_ref)
    xb = x_ref[...].astype(BF16)
    cos = cos_ref[...]
    sin = sin_ref[...]
    lane = lax.broadcasted_iota(I32, (1, LANES), 1)
    first_half = (lane % HEAD_DIM) < (HEAD_DIM // 2)

    def proj(g):
        return _dot(xb, w_ref[:, g * GROUP_W:(g + 1) * GROUP_W])

    def rotary(t, scale):
        outs = []
        for j in range(GROUP_W // LANES):
            c = t[:, j * LANES:(j + 1) * LANES]
            swapped = jnp.where(first_half,
                                pltpu.roll(c, LANES - HEAD_DIM // 2, axis=1),
                                pltpu.roll(c, HEAD_DIM // 2, axis=1))
            outs.append((c * cos + swapped * sin) * scale)
        return jnp.concatenate(outs, axis=1)

    rq_ref[...] = rotary(proj(0), 1.0).astype(BF16)
    rk_ref[...] = rotary(proj(1), HEAD_DIM ** -0.5).astype(BF16)
    rv_ref[...] = proj(2).astype(BF16)
    rg_ref[...] = jax.nn.silu(proj(3)).astype(BF16)
    nq_ref[...] = (proj(4) * NA_Q_SCALE).astype(BF16)
    nk_ref[...] = proj(5).astype(BF16)
    nv_ref[...] = proj(6).astype(BF16)


def _in_proj(x2, w_in, cos_t, sin_t, seq):
    t = x2.shape[0]
    tm = PROJ_TM
    n_pos = seq // tm
    out = jax.ShapeDtypeStruct((t, GROUP_W), BF16)
    tok = lambda i: (i, 0)
    return pl.pallas_call(
        _in_proj_kernel,
        grid=(t // tm,),
        in_specs=[
            pl.BlockSpec((tm, D_MODEL), tok),
            _resident(w_in.shape),
            pl.BlockSpec((tm, LANES), lambda i: (i % n_pos, 0)),
            pl.BlockSpec((tm, LANES), lambda i: (i % n_pos, 0)),
        ],
        out_specs=[pl.BlockSpec((tm, GROUP_W), tok)] * 7,
        out_shape=[out] * 7,
        scratch_shapes=[pltpu.VMEM(w_in.shape, BF16)],
        compiler_params=_cparams(("arbitrary",)),
        name="in_proj",
    )(x2, w_in.astype(F32), cos_t, sin_t)


def _log_sigmoid(x):
    return jnp.minimum(x, 0.0) - jnp.log1p(jnp.exp(-jnp.abs(x)))


def _retention_kernel(q_ref, k_ref, v_ref, g_ref, decf_ref, decb_ref, decfd_ref, decbd_ref,
                      gain_ref, wd_f, o_ref, wd_b, kv_ref, st_ref, dmat_ref):
    wd_b[...] = wd_f[...].astype(BF16)
    c = RET_CHUNK
    n_chunks = q_ref.shape[0] // c
    pair_w = 2 * HEAD_DIM

    lgf = _log_sigmoid(decf_ref[...])
    lgb = _log_sigmoid(decb_ref[...])
    row = lax.broadcasted_iota(I32, (c, 1), 0).astype(F32)
    k_dec_f = jnp.exp((c - 1.0 - row) * lgf)
    k_dec_b = jnp.exp(row * lgb)
    q_dec_f = jnp.exp((row + 1.0) * lgf)
    q_dec_b = jnp.exp((c - row) * lgb)
    chunk_dec_f = jnp.exp(c * lgf)
    chunk_dec_b = jnp.exp(c * lgb)

    lane = lax.broadcasted_iota(I32, (1, pair_w), 1)
    head0 = lane < HEAD_DIM
    r2 = lax.broadcasted_iota(I32, (pair_w, pair_w), 0) // HEAD_DIM
    c2 = lax.broadcasted_iota(I32, (pair_w, pair_w), 1) // HEAD_DIM
    same_head = r2 == c2
    block_diag = jnp.where(same_head, 1.0, 0.0)
    seg_avg = jnp.where(same_head, 1.0 / HEAD_DIM, 0.0).astype(BF16)

    lgf_d = _log_sigmoid(decfd_ref[0])
    lgb_d = _log_sigmoid(decbd_ref[0])
    di = lax.broadcasted_iota(I32, (c, 2 * c), 0)
    dj = lax.broadcasted_iota(I32, (c, 2 * c), 1) % c
    diff = (di - dj).astype(F32)
    dmat_ref[...] = jnp.where(diff >= 0.0, jnp.exp(diff * lgf_d), jnp.exp(-diff * lgb_d))

    def chunk(ref, n):
        return ref[pl.ds(pl.multiple_of(n * c, c), c), :]

    unroll = RET_UNROLL

    def summarize(nb, carry):
        for u in range(unroll):
            n = nb * unroll + u
            kf = chunk(k_ref, n).astype(F32)
            kst = jnp.concatenate([kf * k_dec_f, kf * k_dec_b], axis=1).astype(BF16)
            kv_ref[n] = _dot_tn(kst, chunk(v_ref, n))
        return carry

    lax.fori_loop(0, n_chunks // unroll, summarize, 0)

    def scan(i, states):
        fwd, bwd = states
        nb = n_chunks - 1 - i
        st_ref[i, 0:pair_w, :] = fwd.astype(BF16)
        st_ref[nb, pair_w:2 * pair_w, :] = bwd.astype(BF16)
        return (fwd * chunk_dec_f + kv_ref[i, 0:pair_w, :] * block_diag,
                bwd * chunk_dec_b + kv_ref[nb, pair_w:2 * pair_w, :] * block_diag)

    zero_state = jnp.zeros((pair_w, pair_w), F32)
    lax.fori_loop(0, n_chunks, scan, (zero_state, zero_state), unroll=True)

    gain = gain_ref[...]

    seg_avg2 = jnp.concatenate([seg_avg, seg_avg], axis=0)

    def seg_mean(z):
        hi = z.astype(BF16)
        lo = (z - hi.astype(F32)).astype(BF16)
        return _dot(jnp.concatenate([hi, lo], axis=1), seg_avg2)

    def emit(nb, carry):
        ys = []
        for u in range(unroll):
            n = nb * unroll + u
            q = chunk(q_ref, n)
            k = chunk(k_ref, n)
            v = chunk(v_ref, n)
            zero = jnp.zeros_like(k)
            k_st = jnp.concatenate([jnp.where(head0, k, zero), jnp.where(head0, zero, k)], axis=0)
            v_st = jnp.concatenate([jnp.where(head0, v, zero), jnp.where(head0, zero, v)], axis=0)
            scores = _dot_nt(q, k_st) * dmat_ref[...]
            qf = q.astype(F32)
            q_st = jnp.concatenate([qf * q_dec_f, qf * q_dec_b], axis=1).astype(BF16)
            ys.append(_dot(scores.astype(BF16), v_st) + _dot(q_st, st_ref[n]))
        y = jnp.concatenate(ys, axis=0)
        mu = seg_mean(y)
        d = y - mu
        var = seg_mean(d * d)
        yn = d * lax.rsqrt(var + GN_EPS) * gain
        rows = pl.ds(pl.multiple_of(nb * (unroll * c), unroll * c), unroll * c)
        o_ref[rows, :] = (g_ref[rows, :].astype(F32) * yn).astype(BF16)
        return carry

    lax.fori_loop(0, n_chunks // unroll, emit, 0)


def _retention(rq, rk, rv, rg, dec_f, dec_b, gain, w_down, batch, seq):
    t = rq.shape[0]
    c = RET_CHUNK
    n_pairs = HEADS // 2
    pair_w = 2 * HEAD_DIM
    n_exp = w_down.shape[0]
    assert n_exp % (batch * n_pairs) == 0
    epb = n_exp // (batch * n_pairs)
    wd_spec = pl.BlockSpec((epb,) + w_down.shape[1:], lambda b, p: (b * n_pairs + p, 0, 0))
    dec_f_lane = jnp.repeat(dec_f.astype(F32), HEAD_DIM)[None, :]
    dec_b_lane = jnp.repeat(dec_b.astype(F32), HEAD_DIM)[None, :]
    dec_f_col = jnp.repeat(dec_f.astype(F32), c).reshape(n_pairs, 1, 2 * c)
    dec_b_col = jnp.repeat(dec_b.astype(F32), c).reshape(n_pairs, 1, 2 * c)
    tok = pl.BlockSpec((seq, pair_w), lambda b, p: (b, p))
    lane_spec = pl.BlockSpec((1, pair_w), lambda b, p: (0, p))
    col_spec = pl.BlockSpec((1, 1, 2 * c), lambda b, p: (p, 0, 0))
    return pl.pallas_call(
        _retention_kernel,
        grid=(batch, n_pairs),
        in_specs=[tok, tok, tok, tok, lane_spec, lane_spec, col_spec, col_spec, lane_spec, wd_spec],
        out_specs=[tok, wd_spec],
        out_shape=[jax.ShapeDtypeStruct((t, GROUP_W), BF16), jax.ShapeDtypeStruct(w_down.shape, BF16)],
        scratch_shapes=[
            pltpu.VMEM((seq // c, 2 * pair_w, pair_w), F32),
            pltpu.VMEM((seq // c, 2 * pair_w, pair_w), BF16),
            pltpu.VMEM((c, 2 * c), F32),
        ],
        compiler_params=_cparams(("parallel", "parallel")),
        name="retention",
    )(rq, rk, rv, rg, dec_f_lane, dec_b_lane, dec_f_col, dec_b_col, gain.astype(F32)[None, :],
      w_down.astype(F32))


N_ROW_OFFSETS = 2 * NA_WIN_ROWS - 1
N_COL_OFFSETS = 2 * NA_WIN_COLS - 1


def _natten_row_offsets(rows):
    n_blocks = rows // NA_ROWS
    starts = {0: 0, 1: NA_ROWS - NA_WIN_ROWS // 2, 2: rows - NA_KEY_ROWS}
    blocks = {0: 0, 1: 1, 2: n_blocks - 1}
    table = []
    for v in range(3):
        per_a = []
        for a in range(NA_ROWS):
            r = blocks[v] * NA_ROWS + a
            rs = min(max(r - NA_WIN_ROWS // 2, 0), rows - NA_WIN_ROWS)
            per_kl = []
            for kl in range(NA_KEY_ROWS):
                kr = starts[v] + kl
                per_kl.append(kr - r + NA_WIN_ROWS - 1 if rs <= kr < rs + NA_WIN_ROWS else None)
            per_a.append(per_kl)
        table.append(per_a)
    return table


def _natten_live_columns(rows):
    offsets = _natten_row_offsets(rows)
    live = []
    for blk in range(NA_STEP_BLOCKS):
        variants = [1] + ([0] if blk == 0 else []) + ([2] if blk == NA_STEP_BLOCKS - 1 else [])
        live.append([[j for j in range(NA_KEY_ROWS // 2)
                      if any(offsets[v][a][kl] is not None for v in variants for kl in (2 * j, 2 * j + 1))]
                     for a in range(NA_ROWS)])
    return live


def _natten_build_bias(rpb_ref, tab_ref, pair, rows):
    offsets = _natten_row_offsets(rows)
    nk = NA_KEY_ROWS * GRID_W
    shape = (GRID_W, LANES)
    lane = lax.broadcasted_iota(I32, shape, 1)
    c = lax.broadcasted_iota(I32, shape, 0)
    second = lane >= GRID_W
    kc = lane % GRID_W
    cs = jnp.clip(c - NA_WIN_COLS // 2, 0, GRID_W - NA_WIN_COLS)
    col_ok = jnp.logical_and(kc >= cs, kc < cs + NA_WIN_COLS)
    neg = jnp.full(shape, NEG_BIG, F32)
    for hh in range(2):
        toeplitz = []
        for dr in range(N_ROW_OFFSETS):
            x = jnp.broadcast_to(rpb_ref[2 * pair + hh, dr:dr + 1, :] * LOG2E, shape)
            lo = pltpu.roll(x, LANES - (NA_WIN_COLS - 1), axis=1, stride=1, stride_axis=0)
            hi = pltpu.roll(x, GRID_W - (NA_WIN_COLS - 1), axis=1, stride=1, stride_axis=0)
            toeplitz.append(jnp.where(second, hi, lo))
        for v in range(3):
            for a in range(NA_ROWS):
                for j in range(NA_KEY_ROWS // 2):
                    d0, d1 = offsets[v][a][2 * j], offsets[v][a][2 * j + 1]
                    if d0 is None and d1 is None:
                        piece = neg
                    else:
                        t0 = neg if d0 is None else toeplitz[d0]
                        t1 = neg if d1 is None else toeplitz[d1]
                        piece = jnp.where(col_ok, jnp.where(second, t1, t0), neg)
                    tab_ref[v, pair, a * GRID_W:(a + 1) * GRID_W,
                            hh * nk + j * LANES:hh * nk + (j + 1) * LANES] = piece


def _natten_kernel(q_ref, k_ref, v_ref, rpb_ref, wgu_f, o_ref, wgu_b, ks_ref, vs_ref, tab_ref, *, rows):
    pair_id = pl.program_id(1)

    @pl.when(jnp.logical_and(pl.program_id(0) == 0, jnp.logical_and(pair_id == 0, pl.program_id(2) == 0)))
    def _():
        for pair in range(HEADS // 2):
            _natten_build_bias(rpb_ref, tab_ref, pair, rows)

    wgu_b[...] = wgu_f[...].astype(BF16)
    nq = NA_ROWS * GRID_W
    nk = NA_KEY_ROWS * GRID_W
    n_blocks = rows // NA_ROWS
    live = _natten_live_columns(rows)
    lane = lax.broadcasted_iota(I32, (1, 2 * HEAD_DIM), 1)
    head0 = lane < HEAD_DIM

    @pl.when(pl.program_id(2) == 0)
    def _():
        k = k_ref[...]
        v = v_ref[...]
        zero = jnp.zeros_like(k)
        ks_ref[0] = jnp.where(head0, k, zero)
        ks_ref[1] = jnp.where(head0, zero, k)
        ind0 = jnp.broadcast_to(jnp.where(head0, 1.0, 0.0).astype(BF16), k.shape)
        ind1 = jnp.broadcast_to(jnp.where(head0, 0.0, 1.0).astype(BF16), k.shape)
        vs_ref[0] = jnp.concatenate([jnp.where(head0, v, zero), ind0], axis=1)
        vs_ref[1] = jnp.concatenate([jnp.where(head0, zero, v), ind1], axis=1)

    for blk in range(NA_STEP_BLOCKS):
        rb = pl.program_id(2) * NA_STEP_BLOCKS + blk
        variant = jnp.where(rb == 0, 0, jnp.where(rb == n_blocks - 1, 2, 1))
        start_row = jnp.clip(rb * NA_ROWS - NA_WIN_ROWS // 2, 0, rows - NA_KEY_ROWS)
        win = pl.ds(pl.multiple_of(start_row * GRID_W, GRID_W), nk)
        q = q_ref[blk * nq:(blk + 1) * nq, :]
        k_st = jnp.concatenate([ks_ref[0, win, :], ks_ref[1, win, :]], axis=0)
        v_st = jnp.concatenate([vs_ref[0, win, :], vs_ref[1, win, :]], axis=0)
        s = _dot_nt(q, k_st) + tab_ref[variant, pair_id]
        dead = jnp.zeros((GRID_W, LANES), BF16)
        slabs = []
        for a in range(NA_ROWS):
            qrows = slice(a * GRID_W, (a + 1) * GRID_W)
            parts = []
            for h in range(2):
                cols = {j: s[qrows, h * nk + j * LANES:h * nk + (j + 1) * LANES] for j in live[blk][a]}
                m = jnp.max(jnp.concatenate(list(cols.values()), axis=1), axis=1, keepdims=True)
                parts += [jnp.exp2(cols[j] - m).astype(BF16) if j in cols else dead
                          for j in range(NA_KEY_ROWS // 2)]
            slabs.append(jnp.concatenate(parts, axis=1))
        out = _dot(jnp.concatenate(slabs, axis=0), v_st)
        o_ref[blk * nq:(blk + 1) * nq, :] = (out[:, :2 * HEAD_DIM] / out[:, 2 * HEAD_DIM:]).astype(BF16)


def _natten(nq, nk, nv, rpb, w_gu, batch, seq):
    t = nq.shape[0]
    rows = seq // GRID_W
    n_pairs = HEADS // 2
    n_steps = rows // (NA_ROWS * NA_STEP_BLOCKS)
    nqb = NA_ROWS * GRID_W
    nkb = NA_KEY_ROWS * GRID_W
    rpb_pad = jnp.pad(rpb.astype(F32), ((0, 0), (0, 0), (0, LANES - N_COL_OFFSETS)))
    total_steps = batch * n_pairs * n_steps
    n_exp = w_gu.shape[0]
    assert n_exp % total_steps == 0
    epb = n_exp // total_steps

    def expert_spec(shape):
        return pl.BlockSpec((epb,) + shape[1:], lambda b, p, s: ((b * n_pairs + p) * n_steps + s, 0, 0))

    rpb_spec = pl.BlockSpec(rpb_pad.shape, lambda b, p, s: (0, 0, 0))
    kv_spec = pl.BlockSpec((seq, 2 * HEAD_DIM), lambda b, p, s: (b, p))
    q_spec = pl.BlockSpec((NA_STEP_BLOCKS * nqb, 2 * HEAD_DIM), lambda b, p, s: (b * n_steps + s, p))
    return pl.pallas_call(
        functools.partial(_natten_kernel, rows=rows),
        grid=(batch, n_pairs, n_steps),
        in_specs=[q_spec, kv_spec, kv_spec, rpb_spec, expert_spec(w_gu.shape)],
        out_specs=[q_spec, expert_spec(w_gu.shape)],
        out_shape=[jax.ShapeDtypeStruct((t, GROUP_W), BF16), jax.ShapeDtypeStruct(w_gu.shape, BF16)],
        scratch_shapes=[pltpu.VMEM((2, seq, 2 * HEAD_DIM), BF16), pltpu.VMEM((2, seq, 4 * HEAD_DIM), BF16),
                        pltpu.VMEM((3, n_pairs, nqb, 2 * nkb), F32)],
        compiler_params=_cparams(("arbitrary", "arbitrary", "arbitrary")),
        name="natten",
    )(nq, nk, nv, rpb_pad, w_gu.astype(F32))


def _layer_norm(h, gain, bias):
    mu = jnp.mean(h, axis=-1, keepdims=True)
    d = h - mu
    var = jnp.mean(d * d, axis=-1, keepdims=True)
    return d * lax.rsqrt(var + LN_EPS) * gain + bias


def _route_tile(xb, wr, rbias, tri):
    tm = xb.shape[0]
    scores = jax.nn.sigmoid(_dot_nt(wr, xb))
    biased = scores + rbias
    sub = lax.broadcasted_iota(I32, (GROUP_SIZE, tm), 0).astype(F32)
    none = float(N_EXPERTS)
    ninf = -jnp.inf

    def first_max(vals, index):
        m = jnp.max(vals, axis=0, keepdims=True)
        return m, jnp.min(jnp.where(vals == m, index, none), axis=0, keepdims=True)

    groups = [biased[g * GROUP_SIZE:(g + 1) * GROUP_SIZE, :] for g in range(N_GROUPS)]
    group_scores = []
    for g in range(N_GROUPS):
        m1, i1 = first_max(groups[g], sub)
        m2 = jnp.max(jnp.where(sub == i1, ninf, groups[g]), axis=0, keepdims=True)
        group_scores.append(m1 + m2)
    cur = jnp.concatenate(group_scores, axis=0)
    group_sel = jnp.zeros(cur.shape, F32)
    for _ in range(TOPK_GROUPS):
        _, i1 = first_max(cur, sub)
        hit = sub == i1
        group_sel = jnp.where(hit, 1.0, group_sel)
        cur = jnp.where(hit, ninf, cur)

    masked = [jnp.where(group_sel[g:g + 1, :] > 0.5, groups[g], ninf) for g in range(N_GROUPS)]
    ids = [sub + float(g * GROUP_SIZE) for g in range(N_GROUPS)]
    chosen = [jnp.zeros((GROUP_SIZE, tm), F32) for _ in range(N_GROUPS)]
    for _ in range(TOP_K):
        m = masked[0]
        for g in range(1, N_GROUPS):
            m = jnp.maximum(m, masked[g])
        m = jnp.max(m, axis=0, keepdims=True)
        cand = jnp.where(masked[0] == m, ids[0], none)
        for g in range(1, N_GROUPS):
            cand = jnp.minimum(cand, jnp.where(masked[g] == m, ids[g], none))
        first = jnp.min(cand, axis=0, keepdims=True)
        for g in range(N_GROUPS):
            hit = ids[g] == first
            chosen[g] = jnp.where(hit, 1.0, chosen[g])
            masked[g] = jnp.where(hit, ninf, masked[g])

    sel = jnp.concatenate(chosen, axis=0) > 0.5
    picked = jnp.where(sel, scores, 0.0)
    total = jnp.sum(picked, axis=0, keepdims=True)
    weight = picked / total * ROUTED_SCALE
    sel_f = jnp.where(sel, 1.0, 0.0)
    sel_b = sel_f.astype(BF16)
    before, cnt_max = [], None
    for k in range(tm // MOE_SUB):
        cols = slice(k * MOE_SUB, (k + 1) * MOE_SUB)
        before.append(_dot(sel_b[:, cols], tri))
        cnt = jnp.sum(sel_f[:, cols], axis=1, keepdims=True)
        cnt_max = cnt if cnt_max is None else jnp.maximum(cnt_max, cnt)
    rank = jnp.where(sel, jnp.concatenate(before, axis=1).astype(I32), -1)
    return weight, rank, cnt_max


def _mid_kernel(ret_ref, na_ref, x_ref, p_ref, wo_f, wsgu_f, wsd_f, wp_f, wg_f, gain_ref, bias_ref,
                wr_ref, rb_ref, pre_ref, x1b_ref, w_ref, rank_ref, cnt_ref,
                wo_ref, wsgu_ref, wsd_ref, wp_ref, wg_ref, tri_ref):
    for src_ref, dst_ref in ((wo_f, wo_ref), (wsgu_f, wsgu_ref), (wsd_f, wsd_ref), (wp_f, wp_ref),
                             (wg_f, wg_ref)):
        _cast_once(src_ref, dst_ref)

    @pl.when(pl.program_id(0) == 0)
    def _():
        i = lax.broadcasted_iota(I32, (MOE_SUB, MOE_SUB), 0)
        j = lax.broadcasted_iota(I32, (MOE_SUB, MOE_SUB), 1)
        tri_ref[...] = jnp.where(i < j, 1.0, 0.0).astype(BF16)

    sub_blocks = [slice(r, r + MOE_SUB) for r in range(0, x_ref.shape[0], MOE_SUB)]
    for rows in sub_blocks:
        mix = (_dot(ret_ref[rows, :], wo_ref[0:GROUP_W, :])
               + _dot(na_ref[rows, :], wo_ref[GROUP_W:2 * GROUP_W, :]))
        x1 = _layer_norm(ALPHA * x_ref[rows, :] + mix, gain_ref[...], bias_ref[...])
        x1b_ref[rows, :] = x1.astype(BF16)
        pre_ref[rows, :] = ALPHA * x1
    weight, rank, cnt_max = _route_tile(x1b_ref[...], wr_ref[...], rb_ref[...], tri_ref[...])
    w_ref[...] = weight
    rank_ref[...] = rank
    cnt_ref[...] = jnp.broadcast_to(cnt_max, cnt_ref.shape).astype(I32)
    for rows in sub_blocks:
        xb = x1b_ref[rows, :]
        h = _dot(xb, wsgu_ref[...])
        act = (jax.nn.silu(h[:, :EXPERT_DIM]) * h[:, EXPERT_DIM:]).astype(BF16)
        shared = _dot(act, wsd_ref[...])
        ple = _dot(p_ref[rows, :].astype(BF16), wp_ref[...]) * jax.nn.sigmoid(_dot(xb, wg_ref[...]))
        pre_ref[rows, :] += shared + ple


def _mid(ret, na, x2, p2, w_out, gain, bias, w_router, router_bias, wsgu, wsd, wp, wg):
    t = x2.shape[0]
    tm = MID_TM
    nt = t // tm
    tok = lambda i: (i, 0)
    const = lambda i: (0, 0)
    col = lambda i: (0, i)
    weights = [w.astype(F32) for w in (w_out, wsgu, wsd, wp, wg)]
    wr_t = w_router.astype(F32).T.astype(BF16)
    return pl.pallas_call(
        _mid_kernel,
        grid=(nt,),
        in_specs=[pl.BlockSpec((tm, GROUP_W), tok), pl.BlockSpec((tm, GROUP_W), tok),
                  pl.BlockSpec((tm, D_MODEL), tok), pl.BlockSpec((tm, p2.shape[1]), tok)]
                 + [_resident(w.shape) for w in weights]
                 + [pl.BlockSpec((1, D_MODEL), const), pl.BlockSpec((1, D_MODEL), const),
                    pl.BlockSpec((N_EXPERTS, D_MODEL), const), pl.BlockSpec((N_EXPERTS, 1), const)],
        out_specs=[pl.BlockSpec((tm, D_MODEL), tok), pl.BlockSpec((tm, D_MODEL), tok),
                   pl.BlockSpec((N_EXPERTS, tm), col), pl.BlockSpec((N_EXPERTS, tm), col),
                   pl.BlockSpec((N_EXPERTS, LANES), col)],
        out_shape=[jax.ShapeDtypeStruct((t, D_MODEL), F32), jax.ShapeDtypeStruct((t, D_MODEL), BF16),
                   jax.ShapeDtypeStruct((N_EXPERTS, t), F32), jax.ShapeDtypeStruct((N_EXPERTS, t), I32),
                   jax.ShapeDtypeStruct((N_EXPERTS, nt * LANES), I32)],
        scratch_shapes=[pltpu.VMEM(w.shape, BF16) for w in weights] + [pltpu.VMEM((MOE_SUB, MOE_SUB), BF16)],
        compiler_params=_cparams(("arbitrary",)),
        name="mid",
    )(ret, na, x2, p2, *weights, gain.astype(F32)[None, :], bias.astype(F32)[None, :],
      wr_t, router_bias.astype(F32)[:, None])


def _moe_kernel(cnt_ref, x_ref, rank_ref, w_ref, wgu_ref, wd_ref, pre_ref, gain_ref, bias_ref,
                o_ref, oh_ref, g_ref, y_ref):
    i = pl.program_id(0)
    eb = pl.program_id(1)
    n_eb = pl.num_programs(1)

    @pl.when(eb == 0)
    def _():
        o_ref[...] = jnp.zeros_like(o_ref)

    pre_rows = pre_ref.shape[0]
    o_ref[pl.ds(pl.multiple_of(eb * pre_rows, pre_rows), pre_rows), :] += pre_ref[...]

    max_count = cnt_ref[eb, i]

    def one_pass(p, slots):
        n_rows = MOE_EB * slots
        base = p * slots
        slot = lax.broadcasted_iota(I32, (slots, MOE_SUB), 0)
        slot_w = []
        for k in range(MOE_NSUB):
            tok = slice(k * MOE_SUB, (k + 1) * MOE_SUB)
            blocks, weights = [], []
            for j in range(MOE_EB):
                expert = pl.ds(eb * MOE_EB + j, 1)
                match = slot == (rank_ref[expert, tok] - base)
                blocks.append(jnp.where(match, 1.0, 0.0).astype(BF16))
                weights.append(jnp.sum(jnp.where(match, w_ref[expert, tok], 0.0), axis=1, keepdims=True))
            onehot = jnp.concatenate(blocks, axis=0)
            oh_ref[k, 0:n_rows, :] = onehot
            slot_w.append(weights)
            g_ref[k, 0:n_rows, :] = _dot(onehot, x_ref[tok, :]).astype(BF16)
        for j in range(MOE_EB):
            rows = slice(j * slots, (j + 1) * slots)
            xe = jnp.concatenate([g_ref[k, rows, :] for k in range(MOE_NSUB)], axis=0)
            h = _dot(xe, wgu_ref[j])
            act = (jax.nn.silu(h[:, :EXPERT_DIM]) * h[:, EXPERT_DIM:]).astype(BF16)
            wc = jnp.concatenate([slot_w[k][j] for k in range(MOE_NSUB)], axis=0)
            yw = (_dot(act, wd_ref[j]) * wc).astype(BF16)
            for k in range(MOE_NSUB):
                y_ref[k, rows, :] = yw[k * slots:(k + 1) * slots, :]
        for k in range(MOE_NSUB):
            tok = slice(k * MOE_SUB, (k + 1) * MOE_SUB)
            o_ref[tok, :] += _dot_tn(oh_ref[k, 0:n_rows, :], y_ref[k, 0:n_rows, :])

    @pl.when(max_count <= MOE_SLOTS_SMALL)
    def _():
        one_pass(0, MOE_SLOTS_SMALL)

    @pl.when(max_count > MOE_SLOTS_SMALL)
    def _():
        n_pass = lax.shift_right_logical(max_count + (MOE_SLOTS - 1), int(np.log2(MOE_SLOTS)))

        def body(p, carry):
            one_pass(p, MOE_SLOTS)
            return carry

        lax.fori_loop(0, n_pass, body, 0)

    @pl.when(eb == n_eb - 1)
    def _():
        for r in range(0, o_ref.shape[0], MOE_SUB):
            rows = slice(r, r + MOE_SUB)
            o_ref[rows, :] = _layer_norm(o_ref[rows, :], gain_ref[...], bias_ref[...])


def _moe(x1b, rank_t, w_t, counts, wgu_b, wd_b, pre, gain, bias):
    t = x1b.shape[0]
    tm = MOE_TM
    nt = t // tm
    n_eb = N_EXPERTS // MOE_EB
    pre_rows = tm // n_eb
    counts = counts.reshape(N_EXPERTS // MOE_EB, MOE_EB, nt, tm // MID_TM).max(axis=(1, 3))
    grid_spec = pltpu.PrefetchScalarGridSpec(
        num_scalar_prefetch=1,
        grid=(nt, N_EXPERTS // MOE_EB),
        in_specs=[
            pl.BlockSpec((tm, D_MODEL), lambda i, e, c: (i, 0)),
            pl.BlockSpec((N_EXPERTS, tm), lambda i, e, c: (0, i)),
            pl.BlockSpec((N_EXPERTS, tm), lambda i, e, c: (0, i)),
            pl.BlockSpec((MOE_EB, D_MODEL, 2 * EXPERT_DIM), lambda i, e, c: (e, 0, 0)),
            pl.BlockSpec((MOE_EB, EXPERT_DIM, D_MODEL), lambda i, e, c: (e, 0, 0)),
            pl.BlockSpec((pre_rows, D_MODEL), lambda i, e, c: (i * n_eb + e, 0)),
            pl.BlockSpec((1, D_MODEL), lambda i, e, c: (0, 0)),
            pl.BlockSpec((1, D_MODEL), lambda i, e, c: (0, 0)),
        ],
        out_specs=pl.BlockSpec((tm, D_MODEL), lambda i, e, c: (i, 0)),
        scratch_shapes=[pltpu.VMEM((MOE_NSUB, MOE_SUB, MOE_SUB), BF16),
                        pltpu.VMEM((MOE_NSUB, MOE_SUB, D_MODEL), BF16),
                        pltpu.VMEM((MOE_NSUB, MOE_SUB, D_MODEL), BF16)],
    )
    return pl.pallas_call(
        _moe_kernel,
        grid_spec=grid_spec,
        out_shape=jax.ShapeDtypeStruct((t, D_MODEL), F32),
        compiler_params=_cparams(("parallel", "arbitrary")),
        name="moe",
    )(counts, x1b, rank_t, w_t, wgu_b, wd_b, pre, gain.astype(F32)[None, :], bias.astype(F32)[None, :])


def _rotary_tables(seq):
    half = HEAD_DIM // 2
    inv = ROPE_BASE ** (-jnp.arange(half, dtype=F32) / half)
    ang = jnp.arange(seq, dtype=jnp.int32).astype(F32)[:, None] * inv[None, :]
    cos, sin = jnp.cos(ang), jnp.sin(ang)
    reps = LANES // HEAD_DIM
    cos_t = jnp.tile(jnp.concatenate([cos, cos], axis=1), (1, reps))
    sin_t = jnp.tile(jnp.concatenate([-sin, sin], axis=1), (1, reps))
    return cos_t, sin_t


def kernel(x, p, w_in, ret_decay_fwd, ret_decay_bwd, ret_gn_gain, na_rpb, w_out, ln1_gain, ln1_bias,
           w_router, router_bias, w_expert_gu, w_expert_down, w_shared_gu, w_shared_down,
           w_ple_proj, w_ple_gate, ln2_gain, ln2_bias):
    batch, seq, d = x.shape
    t = batch * seq
    depth = w_in.shape[0]
    assert depth == 1 and d == D_MODEL
    assert seq % PROJ_TM == 0 and seq % MOE_TM == 0 and seq % RET_CHUNK == 0
    assert MOE_TM % (N_EXPERTS // MOE_EB) == 0
    assert seq % MID_TM == 0 and MOE_TM % MID_TM == 0 and MID_TM % MOE_SUB == 0
    assert (seq // GRID_W) % (NA_ROWS * NA_STEP_BLOCKS) == 0 and seq // GRID_W >= NA_KEY_ROWS
    cos_t, sin_t = _rotary_tables(seq)
    x2 = x.reshape(t, d)
    for i in range(depth):
        rq, rk, rv, rg, nq, nk, nv = _in_proj(x2, w_in[i], cos_t, sin_t, seq)
        ret, wd_b = _retention(rq, rk, rv, rg, ret_decay_fwd[i], ret_decay_bwd[i], ret_gn_gain[i],
                               w_expert_down[i], batch, seq)
        na, wgu_b = _natten(nq, nk, nv, na_rpb[i], w_expert_gu[i], batch, seq)
        pre, x1b, w_t, rank_t, cnt = _mid(ret, na, x2, p[i].reshape(t, -1), w_out[i], ln1_gain[i], ln1_bias[i],
                                          w_router[i], router_bias[i], w_shared_gu[i], w_shared_down[i],
                                          w_ple_proj[i], w_ple_gate[i])
        x2 = _moe(x1b, rank_t, w_t, cnt[:, ::LANES], wgu_b, wd_b, pre, ln2_gain[i], ln2_bias[i])
    return x2.reshape(batch, seq, d)
```

```python
import functools

import numpy as np
import jax
import jax.numpy as jnp
from jax import lax
from jax.experimental import pallas as pl
from jax.experimental.pallas import tpu as pltpu

F32 = jnp.float32
BF16 = jnp.bfloat16
I32 = jnp.int32

D_MODEL = 1024
HEADS = 8
HEAD_DIM = 64
GROUP_W = HEADS * HEAD_DIM
ROPE_BASE = 10000.0
GN_EPS = 1e-6
LN_EPS = 1e-5
GRID_W = 64
NA_WIN_ROWS = 8
NA_WIN_COLS = 16
N_EXPERTS = 64
N_GROUPS = 8
GROUP_SIZE = N_EXPERTS // N_GROUPS
TOPK_GROUPS = 4
TOP_K = 8
EXPERT_DIM = 256
ROUTED_SCALE = 2.5
ALPHA = 2.0 ** 0.25
NEG_BIG = -1e30
LOG2E = 1.4426950408889634
NA_Q_SCALE = HEAD_DIM ** -0.5 * LOG2E

LANES = 128
VMEM_LIMIT_BYTES = 56 * 1024 * 1024

PROJ_TM = 1024
MID_TM = 1024
RET_CHUNK = 128
RET_UNROLL = 32
NA_ROWS = 4
NA_KEY_ROWS = NA_ROWS + NA_WIN_ROWS
NA_STEP_BLOCKS = 8
MOE_TM = 2048
MOE_SUB = 256
MOE_GROUP = 4
MOE_EB = 8
MOE_SLOTS = MOE_SUB // MOE_GROUP
MOE_SLOTS_SMALL = 48
MOE_NSUB = MOE_TM // MOE_SUB


def _cparams(sem):
    return pltpu.CompilerParams(dimension_semantics=sem, vmem_limit_bytes=VMEM_LIMIT_BYTES)


def _dot(a, b):
    return jnp.dot(a, b, preferred_element_type=F32)


def _dot_nt(a, b):
    return lax.dot_general(a, b, (((1,), (1,)), ((), ())), preferred_element_type=F32)


def _dot_tn(a, b):
    return lax.dot_general(a, b, (((0,), (0,)), ((), ())), preferred_element_type=F32)


def _cast_once(src_ref, dst_ref):
    @pl.when(pl.program_id(0) == 0)
    def _():
        dst_ref[...] = src_ref[...].astype(BF16)


def _resident(shape):
    return pl.BlockSpec(shape, lambda i: (0,) * len(shape), pipeline_mode=pl.Buffered(1))


def _in_proj_kernel(x_ref, wf_ref, cos_ref, sin_ref,
                    rq_ref, rk_ref, rv_ref, rg_ref, nq_ref, nk_ref, nv_ref, w_ref):
    _cast_once(wf_ref, w_ref)
    xb = x_ref[...].astype(BF16)
    cos = cos_ref[...]
    sin = sin_ref[...]
    lane = lax.broadcasted_iota(I32, (1, LANES), 1)
    first_half = (lane % HEAD_DIM) < (HEAD_DIM // 2)

    def proj(g):
        return _dot(xb, w_ref[:, g * GROUP_W:(g + 1) * GROUP_W])

    def rotary(t, scale):
        outs = []
        for j in range(GROUP_W // LANES):
            c = t[:, j * LANES:(j + 1) * LANES]
            swapped = jnp.where(first_half,
                                pltpu.roll(c, LANES - HEAD_DIM // 2, axis=1),
                                pltpu.roll(c, HEAD_DIM // 2, axis=1))
            outs.append((c * cos + swapped * sin) * scale)
        return jnp.concatenate(outs, axis=1)

    rq_ref[...] = rotary(proj(0), 1.0).astype(BF16)
    rk_ref[...] = rotary(proj(1), HEAD_DIM ** -0.5).astype(BF16)
    rv_ref[...] = proj(2).astype(BF16)
    rg_ref[...] = jax.nn.silu(proj(3)).astype(BF16)
    nq_ref[...] = (proj(4) * NA_Q_SCALE).astype(BF16)
    nk_ref[...] = proj(5).astype(BF16)
    nv_ref[...] = proj(6).astype(BF16)


def _in_proj(x2, w_in, cos_t, sin_t, seq):
    t = x2.shape[0]
    tm = PROJ_TM
    n_pos = seq // tm
    out = jax.ShapeDtypeStruct((t, GROUP_W), BF16)
    tok = lambda i: (i, 0)
    return pl.pallas_call(
        _in_proj_kernel,
        grid=(t // tm,),
        in_specs=[
            pl.BlockSpec((tm, D_MODEL), tok),
            _resident(w_in.shape),
            pl.BlockSpec((tm, LANES), lambda i: (i % n_pos, 0)),
            pl.BlockSpec((tm, LANES), lambda i: (i % n_pos, 0)),
        ],
        out_specs=[pl.BlockSpec((tm, GROUP_W), tok)] * 7,
        out_shape=[out] * 7,
        scratch_shapes=[pltpu.VMEM(w_in.shape, BF16)],
        compiler_params=_cparams(("arbitrary",)),
        name="in_proj",
    )(x2, w_in.astype(F32), cos_t, sin_t)


def _log_sigmoid(x):
    return jnp.minimum(x, 0.0) - jnp.log1p(jnp.exp(-jnp.abs(x)))


def _retention_kernel(q_ref, k_ref, v_ref, g_ref, decf_ref, decb_ref, decfd_ref, decbd_ref,
                      gain_ref, wd_f, o_ref, wd_b, kv_ref, st_ref, dmat_ref):
    wd_b[...] = wd_f[...].astype(BF16)
    c = RET_CHUNK
    n_chunks = q_ref.shape[0] // c
    pair_w = 2 * HEAD_DIM

    lgf = _log_sigmoid(decf_ref[...])
    lgb = _log_sigmoid(decb_ref[...])
    row = lax.broadcasted_iota(I32, (c, 1), 0).astype(F32)
    k_dec_f = jnp.exp((c - 1.0 - row) * lgf)
    k_dec_b = jnp.exp(row * lgb)
    q_dec_f = jnp.exp((row + 1.0) * lgf)
    q_dec_b = jnp.exp((c - row) * lgb)
    chunk_dec_f = jnp.exp(c * lgf)
    chunk_dec_b = jnp.exp(c * lgb)

    lane = lax.broadcasted_iota(I32, (1, pair_w), 1)
    head0 = lane < HEAD_DIM
    r2 = lax.broadcasted_iota(I32, (pair_w, pair_w), 0) // HEAD_DIM
    c2 = lax.broadcasted_iota(I32, (pair_w, pair_w), 1) // HEAD_DIM
    same_head = r2 == c2
    block_diag = jnp.where(same_head, 1.0, 0.0)
    seg_avg = jnp.where(same_head, 1.0 / HEAD_DIM, 0.0).astype(BF16)

    lgf_d = _log_sigmoid(decfd_ref[0])
    lgb_d = _log_sigmoid(decbd_ref[0])
    di = lax.broadcasted_iota(I32, (c, 2 * c), 0)
    dj = lax.broadcasted_iota(I32, (c, 2 * c), 1) % c
    diff = (di - dj).astype(F32)
    dmat_ref[...] = jnp.where(diff >= 0.0, jnp.exp(diff * lgf_d), jnp.exp(-diff * lgb_d))

    def chunk(ref, n):
        return ref[pl.ds(pl.multiple_of(n * c, c), c), :]

    unroll = RET_UNROLL

    def summarize(nb, carry):
        for u in range(unroll):
            n = nb * unroll + u
            kf = chunk(k_ref, n).astype(F32)
            kst = jnp.concatenate([kf * k_dec_f, kf * k_dec_b], axis=1).astype(BF16)
            kv_ref[n] = _dot_tn(kst, chunk(v_ref, n))
        return carry

    lax.fori_loop(0, n_chunks // unroll, summarize, 0)

    def scan(i, states):
        fwd, bwd = states
        nb = n_chunks - 1 - i
        st_ref[i, 0:pair_w, :] = fwd.astype(BF16)
        st_ref[nb, pair_w:2 * pair_w, :] = bwd.astype(BF16)
        return (fwd * chunk_dec_f + kv_ref[i, 0:pair_w, :] * block_diag,
                bwd * chunk_dec_b + kv_ref[nb, pair_w:2 * pair_w, :] * block_diag)

    zero_state = jnp.zeros((pair_w, pair_w), F32)
    lax.fori_loop(0, n_chunks, scan, (zero_state, zero_state), unroll=True)

    gain = gain_ref[...]

    seg_avg2 = jnp.concatenate([seg_avg, seg_avg], axis=0)

    def seg_mean(z):
        hi = z.astype(BF16)
        lo = (z - hi.astype(F32)).astype(BF16)
        return _dot(jnp.concatenate([hi, lo], axis=1), seg_avg2)

    def emit(nb, carry):
        ys = []
        for u in range(unroll):
            n = nb * unroll + u
            q = chunk(q_ref, n)
            k = chunk(k_ref, n)
            v = chunk(v_ref, n)
            zero = jnp.zeros_like(k)
            k_st = jnp.concatenate([jnp.where(head0, k, zero), jnp.where(head0, zero, k)], axis=0)
            v_st = jnp.concatenate([jnp.where(head0, v, zero), jnp.where(head0, zero, v)], axis=0)
            scores = _dot_nt(q, k_st) * dmat_ref[...]
            qf = q.astype(F32)
            q_st = jnp.concatenate([qf * q_dec_f, qf * q_dec_b], axis=1).astype(BF16)
            ys.append(_dot(scores.astype(BF16), v_st) + _dot(q_st, st_ref[n]))
        y = jnp.concatenate(ys, axis=0)
        mu = seg_mean(y)
        d = y - mu
        var = seg_mean(d * d)
        yn = d * lax.rsqrt(var + GN_EPS) * gain
        rows = pl.ds(pl.multiple_of(nb * (unroll * c), unroll * c), unroll * c)
        o_ref[rows, :] = (g_ref[rows, :].astype(F32) * yn).astype(BF16)
        return carry

    lax.fori_loop(0, n_chunks // unroll, emit, 0)


def _retention(rq, rk, rv, rg, dec_f, dec_b, gain, w_down, batch, seq):
    t = rq.shape[0]
    c = RET_CHUNK
    n_pairs = HEADS // 2
    pair_w = 2 * HEAD_DIM
    n_exp = w_down.shape[0]
    assert n_exp % (batch * n_pairs) == 0
    epb = n_exp // (batch * n_pairs)
    wd_spec = pl.BlockSpec((epb,) + w_down.shape[1:], lambda b, p: (b * n_pairs + p, 0, 0))
    dec_f_lane = jnp.repeat(dec_f.astype(F32), HEAD_DIM)[None, :]
    dec_b_lane = jnp.repeat(dec_b.astype(F32), HEAD_DIM)[None, :]
    dec_f_col = jnp.repeat(dec_f.astype(F32), c).reshape(n_pairs, 1, 2 * c)
    dec_b_col = jnp.repeat(dec_b.astype(F32), c).reshape(n_pairs, 1, 2 * c)
    tok = pl.BlockSpec((seq, pair_w), lambda b, p: (b, p))
    lane_spec = pl.BlockSpec((1, pair_w), lambda b, p: (0, p))
    col_spec = pl.BlockSpec((1, 1, 2 * c), lambda b, p: (p, 0, 0))
    return pl.pallas_call(
        _retention_kernel,
        grid=(batch, n_pairs),
        in_specs=[tok, tok, tok, tok, lane_spec, lane_spec, col_spec, col_spec, lane_spec, wd_spec],
        out_specs=[tok, wd_spec],
        out_shape=[jax.ShapeDtypeStruct((t, GROUP_W), BF16), jax.ShapeDtypeStruct(w_down.shape, BF16)],
        scratch_shapes=[
            pltpu.VMEM((seq // c, 2 * pair_w, pair_w), F32),
            pltpu.VMEM((seq // c, 2 * pair_w, pair_w), BF16),
            pltpu.VMEM((c, 2 * c), F32),
        ],
        compiler_params=_cparams(("parallel", "parallel")),
        name="retention",
    )(rq, rk, rv, rg, dec_f_lane, dec_b_lane, dec_f_col, dec_b_col, gain.astype(F32)[None, :],
      w_down.astype(F32))


N_ROW_OFFSETS = 2 * NA_WIN_ROWS - 1
N_COL_OFFSETS = 2 * NA_WIN_COLS - 1


def _natten_row_offsets(rows):
    n_blocks = rows // NA_ROWS
    starts = {0: 0, 1: NA_ROWS - NA_WIN_ROWS // 2, 2: rows - NA_KEY_ROWS}
    blocks = {0: 0, 1: 1, 2: n_blocks - 1}
    table = []
    for v in range(3):
        per_a = []
        for a in range(NA_ROWS):
            r = blocks[v] * NA_ROWS + a
            rs = min(max(r - NA_WIN_ROWS // 2, 0), rows - NA_WIN_ROWS)
            per_kl = []
            for kl in range(NA_KEY_ROWS):
                kr = starts[v] + kl
                per_kl.append(kr - r + NA_WIN_ROWS - 1 if rs <= kr < rs + NA_WIN_ROWS else None)
            per_a.append(per_kl)
        table.append(per_a)
    return table


def _natten_live_columns(rows):
    offsets = _natten_row_offsets(rows)
    live = []
    for blk in range(NA_STEP_BLOCKS):
        variants = [1] + ([0] if blk == 0 else []) + ([2] if blk == NA_STEP_BLOCKS - 1 else [])
        live.append([[j for j in range(NA_KEY_ROWS // 2)
                      if any(offsets[v][a][kl] is not None for v in variants for kl in (2 * j, 2 * j + 1))]
                     for a in range(NA_ROWS)])
    return live


def _natten_build_bias(rpb_ref, tab_ref, pair, rows):
    offsets = _natten_row_offsets(rows)
    nk = NA_KEY_ROWS * GRID_W
    shape = (GRID_W, LANES)
    lane = lax.broadcasted_iota(I32, shape, 1)
    c = lax.broadcasted_iota(I32, shape, 0)
    second = lane >= GRID_W
    kc = lane % GRID_W
    cs = jnp.clip(c - NA_WIN_COLS // 2, 0, GRID_W - NA_WIN_COLS)
    col_ok = jnp.logical_and(kc >= cs, kc < cs + NA_WIN_COLS)
    neg = jnp.full(shape, NEG_BIG, F32)
    for hh in range(2):
        toeplitz = []
        for dr in range(N_ROW_OFFSETS):
            x = jnp.broadcast_to(rpb_ref[2 * pair + hh, dr:dr + 1, :] * LOG2E, shape)
            lo = pltpu.roll(x, LANES - (NA_WIN_COLS - 1), axis=1, stride=1, stride_axis=0)
            hi = pltpu.roll(x, GRID_W - (NA_WIN_COLS - 1), axis=1, stride=1, stride_axis=0)
            toeplitz.append(jnp.where(second, hi, lo))
        for v in range(3):
            for a in range(NA_ROWS):
                for j in range(NA_KEY_ROWS // 2):
                    d0, d1 = offsets[v][a][2 * j], offsets[v][a][2 * j + 1]
                    if d0 is None and d1 is None:
                        piece = neg
                    else:
                        t0 = neg if d0 is None else toeplitz[d0]
                        t1 = neg if d1 is None else toeplitz[d1]
                        piece = jnp.where(col_ok, jnp.where(second, t1, t0), neg)
                    tab_ref[v, pair, a * GRID_W:(a + 1) * GRID_W,
                            hh * nk + j * LANES:hh * nk + (j + 1) * LANES] = piece


def _natten_kernel(q_ref, k_ref, v_ref, rpb_ref, wgu_f, o_ref, wgu_b, ks_ref, vs_ref, tab_ref, *, rows):
    pair_id = pl.program_id(1)

    @pl.when(jnp.logical_and(pl.program_id(0) == 0, jnp.logical_and(pair_id == 0, pl.program_id(2) == 0)))
    def _():
        for pair in range(HEADS // 2):
            _natten_build_bias(rpb_ref, tab_ref, pair, rows)

    wgu_b[...] = wgu_f[...].astype(BF16)
    nq = NA_ROWS * GRID_W
    nk = NA_KEY_ROWS * GRID_W
    n_blocks = rows // NA_ROWS
    live = _natten_live_columns(rows)
    lane = lax.broadcasted_iota(I32, (1, 2 * HEAD_DIM), 1)
    head0 = lane < HEAD_DIM

    @pl.when(pl.program_id(2) == 0)
    def _():
        k = k_ref[...]
        v = v_ref[...]
        zero = jnp.zeros_like(k)
        ks_ref[0] = jnp.where(head0, k, zero)
        ks_ref[1] = jnp.where(head0, zero, k)
        ind0 = jnp.broadcast_to(jnp.where(head0, 1.0, 0.0).astype(BF16), k.shape)
        ind1 = jnp.broadcast_to(jnp.where(head0, 0.0, 1.0).astype(BF16), k.shape)
        vs_ref[0] = jnp.concatenate([jnp.where(head0, v, zero), ind0], axis=1)
        vs_ref[1] = jnp.concatenate([jnp.where(head0, zero, v), ind1], axis=1)

    for blk in range(NA_STEP_BLOCKS):
        rb = pl.program_id(2) * NA_STEP_BLOCKS + blk
        variant = jnp.where(rb == 0, 0, jnp.where(rb == n_blocks - 1, 2, 1))
        start_row = jnp.clip(rb * NA_ROWS - NA_WIN_ROWS // 2, 0, rows - NA_KEY_ROWS)
        win = pl.ds(pl.multiple_of(start_row * GRID_W, GRID_W), nk)
        q = q_ref[blk * nq:(blk + 1) * nq, :]
        k_st = jnp.concatenate([ks_ref[0, win, :], ks_ref[1, win, :]], axis=0)
        v_st = jnp.concatenate([vs_ref[0, win, :], vs_ref[1, win, :]], axis=0)
        s = _dot_nt(q, k_st) + tab_ref[variant, pair_id]
        dead = jnp.zeros((GRID_W, LANES), BF16)
        slabs = []
        for a in range(NA_ROWS):
            qrows = slice(a * GRID_W, (a + 1) * GRID_W)
            parts = []
            for h in range(2):
                cols = {j: s[qrows, h * nk + j * LANES:h * nk + (j + 1) * LANES] for j in live[blk][a]}
                m = jnp.max(jnp.concatenate(list(cols.values()), axis=1), axis=1, keepdims=True)
                parts += [jnp.exp2(cols[j] - m).astype(BF16) if j in cols else dead
                          for j in range(NA_KEY_ROWS // 2)]
            slabs.append(jnp.concatenate(parts, axis=1))
        out = _dot(jnp.concatenate(slabs, axis=0), v_st)
        o_ref[blk * nq:(blk + 1) * nq, :] = (out[:, :2 * HEAD_DIM] / out[:, 2 * HEAD_DIM:]).astype(BF16)


def _natten(nq, nk, nv, rpb, w_gu, batch, seq):
    t = nq.shape[0]
    rows = seq // GRID_W
    n_pairs = HEADS // 2
    n_steps = rows // (NA_ROWS * NA_STEP_BLOCKS)
    nqb = NA_ROWS * GRID_W
    nkb = NA_KEY_ROWS * GRID_W
    rpb_pad = jnp.pad(rpb.astype(F32), ((0, 0), (0, 0), (0, LANES - N_COL_OFFSETS)))
    total_steps = batch * n_pairs * n_steps
    n_exp = w_gu.shape[0]
    assert n_exp % total_steps == 0
    epb = n_exp // total_steps

    def expert_spec(shape):
        return pl.BlockSpec((epb,) + shape[1:], lambda b, p, s: ((b * n_pairs + p) * n_steps + s, 0, 0))

    rpb_spec = pl.BlockSpec(rpb_pad.shape, lambda b, p, s: (0, 0, 0))
    kv_spec = pl.BlockSpec((seq, 2 * HEAD_DIM), lambda b, p, s: (b, p))
    q_spec = pl.BlockSpec((NA_STEP_BLOCKS * nqb, 2 * HEAD_DIM), lambda b, p, s: (b * n_steps + s, p))
    return pl.pallas_call(
        functools.partial(_natten_kernel, rows=rows),
        grid=(batch, n_pairs, n_steps),
        in_specs=[q_spec, kv_spec, kv_spec, rpb_spec, expert_spec(w_gu.shape)],
        out_specs=[q_spec, expert_spec(w_gu.shape)],
        out_shape=[jax.ShapeDtypeStruct((t, GROUP_W), BF16), jax.ShapeDtypeStruct(w_gu.shape, BF16)],
        scratch_shapes=[pltpu.VMEM((2, seq, 2 * HEAD_DIM), BF16), pltpu.VMEM((2, seq, 4 * HEAD_DIM), BF16),
                        pltpu.VMEM((3, n_pairs, nqb, 2 * nkb), F32)],
        compiler_params=_cparams(("arbitrary", "arbitrary", "arbitrary")),
        name="natten",
    )(nq, nk, nv, rpb_pad, w_gu.astype(F32))


def _layer_norm(h, gain, bias):
    mu = jnp.mean(h, axis=-1, keepdims=True)
    d = h - mu
    var = jnp.mean(d * d, axis=-1, keepdims=True)
    return d * lax.rsqrt(var + LN_EPS) * gain + bias


def _route_tile(xb, wr, rbias, tri):
    tm = xb.shape[0]
    scores = jax.nn.sigmoid(_dot_nt(wr, xb))
    biased = scores + rbias
    sub = lax.broadcasted_iota(I32, (GROUP_SIZE, tm), 0).astype(F32)
    none = float(N_EXPERTS)
    ninf = -jnp.inf

    def first_max(vals, index):
        m = jnp.max(vals, axis=0, keepdims=True)
        return m, jnp.min(jnp.where(vals == m, index, none), axis=0, keepdims=True)

    groups = [biased[g * GROUP_SIZE:(g + 1) * GROUP_SIZE, :] for g in range(N_GROUPS)]
    group_scores = []
    for g in range(N_GROUPS):
        m1, i1 = first_max(groups[g], sub)
        m2 = jnp.max(jnp.where(sub == i1, ninf, groups[g]), axis=0, keepdims=True)
        group_scores.append(m1 + m2)
    cur = jnp.concatenate(group_scores, axis=0)
    group_sel = jnp.zeros(cur.shape, F32)
    for _ in range(TOPK_GROUPS):
        _, i1 = first_max(cur, sub)
        hit = sub == i1
        group_sel = jnp.where(hit, 1.0, group_sel)
        cur = jnp.where(hit, ninf, cur)

    masked = [jnp.where(group_sel[g:g + 1, :] > 0.5, groups[g], ninf) for g in range(N_GROUPS)]
    ids = [sub + float(g * GROUP_SIZE) for g in range(N_GROUPS)]
    chosen = [jnp.zeros((GROUP_SIZE, tm), F32) for _ in range(N_GROUPS)]
    for _ in range(TOP_K):
        m = masked[0]
        for g in range(1, N_GROUPS):
            m = jnp.maximum(m, masked[g])
        m = jnp.max(m, axis=0, keepdims=True)
        cand = jnp.where(masked[0] == m, ids[0], none)
        for g in range(1, N_GROUPS):
            cand = jnp.minimum(cand, jnp.where(masked[g] == m, ids[g], none))
        first = jnp.min(cand, axis=0, keepdims=True)
        for g in range(N_GROUPS):
            hit = ids[g] == first
            chosen[g] = jnp.where(hit, 1.0, chosen[g])
            masked[g] = jnp.where(hit, ninf, masked[g])

    sel = jnp.concatenate(chosen, axis=0) > 0.5
    picked = jnp.where(sel, scores, 0.0)
    total = jnp.sum(picked, axis=0, keepdims=True)
    weight = picked / total * ROUTED_SCALE
    sel_f = jnp.where(sel, 1.0, 0.0)
    sel_b = sel_f.astype(BF16)
    before, cnt_max = [], None
    for k in range(tm // MOE_SUB):
        cols = slice(k * MOE_SUB, (k + 1) * MOE_SUB)
        before.append(_dot(sel_b[:, cols], tri))
        cnt = jnp.sum(sel_f[:, cols], axis=1, keepdims=True)
        cnt_max = cnt if cnt_max is None else jnp.maximum(cnt_max, cnt)
    rank = jnp.where(sel, jnp.concatenate(before, axis=1).astype(I32), -1)
    return weight, rank, cnt_max


def _mid_kernel(ret_ref, na_ref, x_ref, p_ref, wo_f, wsgu_f, wsd_f, wp_f, wg_f, gain_ref, bias_ref,
                wr_ref, rb_ref, pre_ref, x1b_ref, w_ref, rank_ref, cnt_ref,
                wo_ref, wsgu_ref, wsd_ref, wp_ref, wg_ref, tri_ref):
    for src_ref, dst_ref in ((wo_f, wo_ref), (wsgu_f, wsgu_ref), (wsd_f, wsd_ref), (wp_f, wp_ref),
                             (wg_f, wg_ref)):
        _cast_once(src_ref, dst_ref)

    @pl.when(pl.program_id(0) == 0)
    def _():
        i = lax.broadcasted_iota(I32, (MOE_SUB, MOE_SUB), 0)
        j = lax.broadcasted_iota(I32, (MOE_SUB, MOE_SUB), 1)
        tri_ref[...] = jnp.where(i < j, 1.0, 0.0).astype(BF16)

    sub_blocks = [slice(r, r + MOE_SUB) for r in range(0, x_ref.shape[0], MOE_SUB)]
    for rows in sub_blocks:
        mix = (_dot(ret_ref[rows, :], wo_ref[0:GROUP_W, :])
               + _dot(na_ref[rows, :], wo_ref[GROUP_W:2 * GROUP_W, :]))
        x1 = _layer_norm(ALPHA * x_ref[rows, :] + mix, gain_ref[...], bias_ref[...])
        x1b_ref[rows, :] = x1.astype(BF16)
        pre_ref[rows, :] = ALPHA * x1
    weight, rank, cnt_max = _route_tile(x1b_ref[...], wr_ref[...], rb_ref[...], tri_ref[...])
    w_ref[...] = weight
    rank_ref[...] = rank
    cnt_ref[...] = jnp.broadcast_to(cnt_max, cnt_ref.shape).astype(I32)
    for rows in sub_blocks:
        xb = x1b_ref[rows, :]
        h = _dot(xb, wsgu_ref[...])
        act = (jax.nn.silu(h[:, :EXPERT_DIM]) * h[:, EXPERT_DIM:]).astype(BF16)
        shared = _dot(act, wsd_ref[...])
        ple = _dot(p_ref[rows, :].astype(BF16), wp_ref[...]) * jax.nn.sigmoid(_dot(xb, wg_ref[...]))
        pre_ref[rows, :] += shared + ple


def _mid(ret, na, x2, p2, w_out, gain, bias, w_router, router_bias, wsgu, wsd, wp, wg):
    t = x2.shape[0]
    tm = MID_TM
    nt = t // tm
    tok = lambda i: (i, 0)
    const = lambda i: (0, 0)
    col = lambda i: (0, i)
    weights = [w.astype(F32) for w in (w_out, wsgu, wsd, wp, wg)]
    wr_t = w_router.astype(F32).T.astype(BF16)
    return pl.pallas_call(
        _mid_kernel,
        grid=(nt,),
        in_specs=[pl.BlockSpec((tm, GROUP_W), tok), pl.BlockSpec((tm, GROUP_W), tok),
                  pl.BlockSpec((tm, D_MODEL), tok), pl.BlockSpec((tm, p2.shape[1]), tok)]
                 + [_resident(w.shape) for w in weights]
                 + [pl.BlockSpec((1, D_MODEL), const), pl.BlockSpec((1, D_MODEL), const),
                    pl.BlockSpec((N_EXPERTS, D_MODEL), const), pl.BlockSpec((N_EXPERTS, 1), const)],
        out_specs=[pl.BlockSpec((tm, D_MODEL), tok), pl.BlockSpec((tm, D_MODEL), tok),
                   pl.BlockSpec((N_EXPERTS, tm), col), pl.BlockSpec((N_EXPERTS, tm), col),
                   pl.BlockSpec((N_EXPERTS, LANES), col)],
        out_shape=[jax.ShapeDtypeStruct((t, D_MODEL), F32), jax.ShapeDtypeStruct((t, D_MODEL), BF16),
                   jax.ShapeDtypeStruct((N_EXPERTS, t), F32), jax.ShapeDtypeStruct((N_EXPERTS, t), I32),
                   jax.ShapeDtypeStruct((N_EXPERTS, nt * LANES), I32)],
        scratch_shapes=[pltpu.VMEM(w.shape, BF16) for w in weights] + [pltpu.VMEM((MOE_SUB, MOE_SUB), BF16)],
        compiler_params=_cparams(("arbitrary",)),
        name="mid",
    )(ret, na, x2, p2, *weights, gain.astype(F32)[None, :], bias.astype(F32)[None, :],
      wr_t, router_bias.astype(F32)[:, None])


def _moe_kernel(cnt_ref, x_ref, rank_ref, w_ref, wgu_ref, wd_ref, pre_ref, gain_ref, bias_ref,
                o_ref, oh_ref, g_ref):
    i = pl.program_id(0)
    eb = pl.program_id(1)
    n_eb = pl.num_programs(1)

    @pl.when(eb == 0)
    def _():
        o_ref[...] = jnp.zeros_like(o_ref)

    pre_rows = pre_ref.shape[0]
    o_ref[pl.ds(pl.multiple_of(eb * pre_rows, pre_rows), pre_rows), :] += pre_ref[...]

    def one_pass(grp, p, slots):
        n_rows = MOE_GROUP * slots
        base = p * slots
        slot = lax.broadcasted_iota(I32, (slots, MOE_SUB), 0)
        slot_w = []
        for k in range(MOE_NSUB):
            tok = slice(k * MOE_SUB, (k + 1) * MOE_SUB)
            blocks, weights = [], []
            for j in range(MOE_GROUP):
                expert = pl.ds(eb * MOE_EB + grp * MOE_GROUP + j, 1)
                match = slot == (rank_ref[expert, tok] - base)
                blocks.append(jnp.where(match, 1.0, 0.0).astype(BF16))
                weights.append(jnp.sum(jnp.where(match, w_ref[expert, tok], 0.0), axis=1, keepdims=True))
            onehot = jnp.concatenate(blocks, axis=0)
            oh_ref[k, 0:n_rows, :] = onehot
            slot_w.append(weights)
            g_ref[k, 0:n_rows, :] = _dot(onehot, x_ref[tok, :]).astype(BF16)
        for j in range(MOE_GROUP):
            rows = slice(j * slots, (j + 1) * slots)
            xe = jnp.concatenate([g_ref[k, rows, :] for k in range(MOE_NSUB)], axis=0)
            h = _dot(xe, wgu_ref[grp * MOE_GROUP + j])
            act = (jax.nn.silu(h[:, :EXPERT_DIM]) * h[:, EXPERT_DIM:]).astype(BF16)
            wc = jnp.concatenate([slot_w[k][j] for k in range(MOE_NSUB)], axis=0)
            yw = (_dot(act, wd_ref[grp * MOE_GROUP + j]) * wc).astype(BF16)
            for k in range(MOE_NSUB):
                g_ref[k, rows, :] = yw[k * slots:(k + 1) * slots, :]
        for k in range(MOE_NSUB):
            tok = slice(k * MOE_SUB, (k + 1) * MOE_SUB)
            o_ref[tok, :] += _dot_tn(oh_ref[k, 0:n_rows, :], g_ref[k, 0:n_rows, :])

    for grp in range(MOE_EB // MOE_GROUP):
        max_count = cnt_ref[eb * (MOE_EB // MOE_GROUP) + grp, i]

        @pl.when(max_count <= MOE_SLOTS_SMALL)
        def _(grp=grp):
            one_pass(grp, 0, MOE_SLOTS_SMALL)

        @pl.when(max_count > MOE_SLOTS_SMALL)
        def _(grp=grp, max_count=max_count):
            n_pass = lax.shift_right_logical(max_count + (MOE_SLOTS - 1), int(np.log2(MOE_SLOTS)))

            def body(p, carry):
                one_pass(grp, p, MOE_SLOTS)
                return carry

            lax.fori_loop(0, n_pass, body, 0)

    @pl.when(eb == n_eb - 1)
    def _():
        for r in range(0, o_ref.shape[0], MOE_SUB):
            rows = slice(r, r + MOE_SUB)
            o_ref[rows, :] = _layer_norm(o_ref[rows, :], gain_ref[...], bias_ref[...])


def _moe(x1b, rank_t, w_t, counts, wgu_b, wd_b, pre, gain, bias):
    t = x1b.shape[0]
    tm = MOE_TM
    nt = t // tm
    n_eb = N_EXPERTS // MOE_EB
    pre_rows = tm // n_eb
    counts = counts.reshape(N_EXPERTS // MOE_GROUP, MOE_GROUP, nt, tm // MID_TM).max(axis=(1, 3))
    grid_spec = pltpu.PrefetchScalarGridSpec(
        num_scalar_prefetch=1,
        grid=(nt, N_EXPERTS // MOE_EB),
        in_specs=[
            pl.BlockSpec((tm, D_MODEL), lambda i, e, c: (i, 0), pipeline_mode=pl.Buffered(1)),
            pl.BlockSpec((N_EXPERTS, tm), lambda i, e, c: (0, i)),
            pl.BlockSpec((N_EXPERTS, tm), lambda i, e, c: (0, i)),
            pl.BlockSpec((MOE_EB, D_MODEL, 2 * EXPERT_DIM), lambda i, e, c: (e, 0, 0)),
            pl.BlockSpec((MOE_EB, EXPERT_DIM, D_MODEL), lambda i, e, c: (e, 0, 0)),
            pl.BlockSpec((pre_rows, D_MODEL), lambda i, e, c: (i * n_eb + e, 0)),
            pl.BlockSpec((1, D_MODEL), lambda i, e, c: (0, 0)),
            pl.BlockSpec((1, D_MODEL), lambda i, e, c: (0, 0)),
        ],
        out_specs=pl.BlockSpec((tm, D_MODEL), lambda i, e, c: (i, 0)),
        scratch_shapes=[pltpu.VMEM((MOE_NSUB, MOE_SUB, MOE_SUB), BF16),
                        pltpu.VMEM((MOE_NSUB, MOE_SUB, D_MODEL), BF16)],
    )
    return pl.pallas_call(
        _moe_kernel,
        grid_spec=grid_spec,
        out_shape=jax.ShapeDtypeStruct((t, D_MODEL), F32),
        compiler_params=_cparams(("parallel", "arbitrary")),
        name="moe",
    )(counts, x1b, rank_t, w_t, wgu_b, wd_b, pre, gain.astype(F32)[None, :], bias.astype(F32)[None, :])


def _rotary_tables(seq):
    half = HEAD_DIM // 2
    inv = ROPE_BASE ** (-jnp.arange(half, dtype=F32) / half)
    ang = jnp.arange(seq, dtype=jnp.int32).astype(F32)[:, None] * inv[None, :]
    cos, sin = jnp.cos(ang), jnp.sin(ang)
    reps = LANES // HEAD_DIM
    cos_t = jnp.tile(jnp.concatenate([cos, cos], axis=1), (1, reps))
    sin_t = jnp.tile(jnp.concatenate([-sin, sin], axis=1), (1, reps))
    return cos_t, sin_t


def kernel(x, p, w_in, ret_decay_fwd, ret_decay_bwd, ret_gn_gain, na_rpb, w_out, ln1_gain, ln1_bias,
           w_router, router_bias, w_expert_gu, w_expert_down, w_shared_gu, w_shared_down,
           w_ple_proj, w_ple_gate, ln2_gain, ln2_bias):
    batch, seq, d = x.shape
    t = batch * seq
    depth = w_in.shape[0]
    assert depth == 1 and d == D_MODEL
    assert seq % PROJ_TM == 0 and seq % MOE_TM == 0 and seq % RET_CHUNK == 0
    assert MOE_TM % (N_EXPERTS // MOE_EB) == 0
    assert seq % MID_TM == 0 and MOE_TM % MID_TM == 0 and MID_TM % MOE_SUB == 0
    assert (seq // GRID_W) % (NA_ROWS * NA_STEP_BLOCKS) == 0 and seq // GRID_W >= NA_KEY_ROWS
    cos_t, sin_t = _rotary_tables(seq)
    x2 = x.reshape(t, d)
    for i in range(depth):
        rq, rk, rv, rg, nq, nk, nv = _in_proj(x2, w_in[i], cos_t, sin_t, seq)
        ret, wd_b = _retention(rq, rk, rv, rg, ret_decay_fwd[i], ret_decay_bwd[i], ret_gn_gain[i],
                               w_expert_down[i], batch, seq)
        na, wgu_b = _natten(nq, nk, nv, na_rpb[i], w_expert_gu[i], batch, seq)
        pre, x1b, w_t, rank_t, cnt = _mid(ret, na, x2, p[i].reshape(t, -1), w_out[i], ln1_gain[i], ln1_bias[i],
                                          w_router[i], router_bias[i], w_shared_gu[i], w_shared_down[i],
                                          w_ple_proj[i], w_ple_gate[i])
        x2 = _moe(x1b, rank_t, w_t, cnt[:, ::LANES], wgu_b, wd_b, pre, ln2_gain[i], ln2_bias[i])
    return x2.reshape(batch, seq, d)
```

```python
import functools

import numpy as np
import jax
import jax.numpy as jnp
from jax import lax
from jax.experimental import pallas as pl
from jax.experimental.pallas import tpu as pltpu

F32 = jnp.float32
BF16 = jnp.bfloat16
I32 = jnp.int32

D_MODEL = 1024
HEADS = 8
HEAD_DIM = 64
GROUP_W = HEADS * HEAD_DIM
ROPE_BASE = 10000.0
GN_EPS = 1e-6
LN_EPS = 1e-5
GRID_W = 64
NA_WIN_ROWS = 8
NA_WIN_COLS = 16
N_EXPERTS = 64
N_GROUPS = 8
GROUP_SIZE = N_EXPERTS // N_GROUPS
TOPK_GROUPS = 4
TOP_K = 8
EXPERT_DIM = 256
ROUTED_SCALE = 2.5
ALPHA = 2.0 ** 0.25
NEG_BIG = -1e30
LOG2E = 1.4426950408889634
NA_Q_SCALE = HEAD_DIM ** -0.5 * LOG2E

LANES = 128
VMEM_LIMIT_BYTES = 56 * 1024 * 1024

PROJ_TM = 1024
MID_TM = 1024
RET_CHUNK = 128
RET_UNROLL = 32
NA_ROWS = 4
NA_KEY_ROWS = NA_ROWS + NA_WIN_ROWS
NA_STEP_BLOCKS = 8
MOE_TM = 2048
MOE_SUB = 256
MOE_EB = 4
MOE_SLOTS = MOE_SUB // MOE_EB
MOE_SLOTS_SMALL = 48
MOE_NSUB = MOE_TM // MOE_SUB


def _cparams(sem):
    return pltpu.CompilerParams(dimension_semantics=sem, vmem_limit_bytes=VMEM_LIMIT_BYTES)


def _dot(a, b):
    return jnp.dot(a, b, preferred_element_type=F32)


def _dot_nt(a, b):
    return lax.dot_general(a, b, (((1,), (1,)), ((), ())), preferred_element_type=F32)


def _dot_tn(a, b):
    return lax.dot_general(a, b, (((0,), (0,)), ((), ())), preferred_element_type=F32)


def _cast_once(src_ref, dst_ref):
    @pl.when(pl.program_id(0) == 0)
    def _():
        dst_ref[...] = src_ref[...].astype(BF16)


def _resident(shape):
    return pl.BlockSpec(shape, lambda i: (0,) * len(shape), pipeline_mode=pl.Buffered(1))


def _in_proj_kernel(x_ref, wf_ref, cos_ref, sin_ref,
                    rq_ref, rk_ref, rv_ref, rg_ref, nq_ref, nke_ref, nko_ref, nve_ref, nvo_ref, w_ref):
    _cast_once(wf_ref, w_ref)
    xb = x_ref[...].astype(BF16)
    cos = cos_ref[...]
    sin = sin_ref[...]
    lane = lax.broadcasted_iota(I32, (1, LANES), 1)
    first_half = (lane % HEAD_DIM) < (HEAD_DIM // 2)

    def proj(g):
        return _dot(xb, w_ref[:, g * GROUP_W:(g + 1) * GROUP_W])

    def rotary(t, scale):
        outs = []
        for j in range(GROUP_W // LANES):
            c = t[:, j * LANES:(j + 1) * LANES]
            swapped = jnp.where(first_half,
                                pltpu.roll(c, LANES - HEAD_DIM // 2, axis=1),
                                pltpu.roll(c, HEAD_DIM // 2, axis=1))
            outs.append((c * cos + swapped * sin) * scale)
        return jnp.concatenate(outs, axis=1)

    rq_ref[...] = rotary(proj(0), 1.0).astype(BF16)
    rk_ref[...] = rotary(proj(1), HEAD_DIM ** -0.5).astype(BF16)
    rv_ref[...] = proj(2).astype(BF16)
    rg_ref[...] = jax.nn.silu(proj(3)).astype(BF16)
    nq_ref[...] = (proj(4) * NA_Q_SCALE).astype(BF16)
    even_head = (lax.broadcasted_iota(I32, (1, GROUP_W), 1) % (2 * HEAD_DIM)) < HEAD_DIM
    for g, even_ref, odd_ref in ((5, nke_ref, nko_ref), (6, nve_ref, nvo_ref)):
        t = proj(g)
        even_ref[...] = jnp.where(even_head, t, 0.0).astype(BF16)
        odd_ref[...] = jnp.where(even_head, 0.0, t).astype(BF16)


def _in_proj(x2, w_in, cos_t, sin_t, seq):
    t = x2.shape[0]
    tm = PROJ_TM
    n_pos = seq // tm
    out = jax.ShapeDtypeStruct((t, GROUP_W), BF16)
    tok = lambda i: (i, 0)
    return pl.pallas_call(
        _in_proj_kernel,
        grid=(t // tm,),
        in_specs=[
            pl.BlockSpec((tm, D_MODEL), tok),
            _resident(w_in.shape),
            pl.BlockSpec((tm, LANES), lambda i: (i % n_pos, 0)),
            pl.BlockSpec((tm, LANES), lambda i: (i % n_pos, 0)),
        ],
        out_specs=[pl.BlockSpec((tm, GROUP_W), tok)] * 9,
        out_shape=[out] * 9,
        scratch_shapes=[pltpu.VMEM(w_in.shape, BF16)],
        compiler_params=_cparams(("arbitrary",)),
        name="in_proj",
    )(x2, w_in.astype(F32), cos_t, sin_t)


def _log_sigmoid(x):
    return jnp.minimum(x, 0.0) - jnp.log1p(jnp.exp(-jnp.abs(x)))


def _retention_kernel(q_ref, k_ref, v_ref, g_ref, decf_ref, decb_ref, decfd_ref, decbd_ref,
                      gain_ref, wd_f, o_ref, wd_b, kv_ref, st_ref, dmat_ref):
    wd_b[...] = wd_f[...].astype(BF16)
    c = RET_CHUNK
    n_chunks = q_ref.shape[0] // c
    pair_w = 2 * HEAD_DIM

    lgf = _log_sigmoid(decf_ref[...])
    lgb = _log_sigmoid(decb_ref[...])
    row = lax.broadcasted_iota(I32, (c, 1), 0).astype(F32)
    k_dec_f = jnp.exp((c - 1.0 - row) * lgf)
    k_dec_b = jnp.exp(row * lgb)
    q_dec_f = jnp.exp((row + 1.0) * lgf)
    q_dec_b = jnp.exp((c - row) * lgb)
    chunk_dec_f = jnp.exp(c * lgf)
    chunk_dec_b = jnp.exp(c * lgb)

    lane = lax.broadcasted_iota(I32, (1, pair_w), 1)
    head0 = lane < HEAD_DIM
    r2 = lax.broadcasted_iota(I32, (pair_w, pair_w), 0) // HEAD_DIM
    c2 = lax.broadcasted_iota(I32, (pair_w, pair_w), 1) // HEAD_DIM
    same_head = r2 == c2
    block_diag = jnp.where(same_head, 1.0, 0.0)
    seg_avg = jnp.where(same_head, 1.0 / HEAD_DIM, 0.0).astype(BF16)

    lgf_d = _log_sigmoid(decfd_ref[0])
    lgb_d = _log_sigmoid(decbd_ref[0])
    di = lax.broadcasted_iota(I32, (c, 2 * c), 0)
    dj = lax.broadcasted_iota(I32, (c, 2 * c), 1) % c
    diff = (di - dj).astype(F32)
    dmat_ref[...] = jnp.where(diff >= 0.0, jnp.exp(diff * lgf_d), jnp.exp(-diff * lgb_d))

    def chunk(ref, n):
        return ref[pl.ds(pl.multiple_of(n * c, c), c), :]

    unroll = RET_UNROLL

    def summarize(nb, carry):
        for u in range(unroll):
            n = nb * unroll + u
            kf = chunk(k_ref, n).astype(F32)
            kst = jnp.concatenate([kf * k_dec_f, kf * k_dec_b], axis=1).astype(BF16)
            kv_ref[n] = _dot_tn(kst, chunk(v_ref, n))
        return carry

    lax.fori_loop(0, n_chunks // unroll, summarize, 0)

    def scan(i, states):
        fwd, bwd = states
        nb = n_chunks - 1 - i
        st_ref[i, 0:pair_w, :] = fwd.astype(BF16)
        st_ref[nb, pair_w:2 * pair_w, :] = bwd.astype(BF16)
        return (fwd * chunk_dec_f + kv_ref[i, 0:pair_w, :] * block_diag,
                bwd * chunk_dec_b + kv_ref[nb, pair_w:2 * pair_w, :] * block_diag)

    zero_state = jnp.zeros((pair_w, pair_w), F32)
    lax.fori_loop(0, n_chunks, scan, (zero_state, zero_state), unroll=True)

    gain = gain_ref[...]

    seg_avg2 = jnp.concatenate([seg_avg, seg_avg], axis=0)

    def seg_mean(z):
        hi = z.astype(BF16)
        lo = (z - hi.astype(F32)).astype(BF16)
        return _dot(jnp.concatenate([hi, lo], axis=1), seg_avg2)

    def emit(nb, carry):
        ys = []
        for u in range(unroll):
            n = nb * unroll + u
            q = chunk(q_ref, n)
            k = chunk(k_ref, n)
            v = chunk(v_ref, n)
            zero = jnp.zeros_like(k)
            k_st = jnp.concatenate([jnp.where(head0, k, zero), jnp.where(head0, zero, k)], axis=0)
            v_st = jnp.concatenate([jnp.where(head0, v, zero), jnp.where(head0, zero, v)], axis=0)
            scores = _dot_nt(q, k_st) * dmat_ref[...]
            qf = q.astype(F32)
            q_st = jnp.concatenate([qf * q_dec_f, qf * q_dec_b], axis=1).astype(BF16)
            ys.append(_dot(scores.astype(BF16), v_st) + _dot(q_st, st_ref[n]))
        y = jnp.concatenate(ys, axis=0)
        mu = seg_mean(y)
        d = y - mu
        var = seg_mean(d * d)
        yn = d * lax.rsqrt(var + GN_EPS) * gain
        rows = pl.ds(pl.multiple_of(nb * (unroll * c), unroll * c), unroll * c)
        o_ref[rows, :] = (g_ref[rows, :].astype(F32) * yn).astype(BF16)
        return carry

    lax.fori_loop(0, n_chunks // unroll, emit, 0)


def _retention(rq, rk, rv, rg, dec_f, dec_b, gain, w_down, batch, seq):
    t = rq.shape[0]
    c = RET_CHUNK
    n_pairs = HEADS // 2
    pair_w = 2 * HEAD_DIM
    n_exp = w_down.shape[0]
    assert n_exp % (batch * n_pairs) == 0
    epb = n_exp // (batch * n_pairs)
    wd_spec = pl.BlockSpec((epb,) + w_down.shape[1:], lambda b, p: (b * n_pairs + p, 0, 0))
    dec_f_lane = jnp.repeat(dec_f.astype(F32), HEAD_DIM)[None, :]
    dec_b_lane = jnp.repeat(dec_b.astype(F32), HEAD_DIM)[None, :]
    dec_f_col = jnp.repeat(dec_f.astype(F32), c).reshape(n_pairs, 1, 2 * c)
    dec_b_col = jnp.repeat(dec_b.astype(F32), c).reshape(n_pairs, 1, 2 * c)
    tok = pl.BlockSpec((seq, pair_w), lambda b, p: (b, p))
    lane_spec = pl.BlockSpec((1, pair_w), lambda b, p: (0, p))
    col_spec = pl.BlockSpec((1, 1, 2 * c), lambda b, p: (p, 0, 0))
    return pl.pallas_call(
        _retention_kernel,
        grid=(batch, n_pairs),
        in_specs=[tok, tok, tok, tok, lane_spec, lane_spec, col_spec, col_spec, lane_spec, wd_spec],
        out_specs=[tok, wd_spec],
        out_shape=[jax.ShapeDtypeStruct((t, GROUP_W), BF16), jax.ShapeDtypeStruct(w_down.shape, BF16)],
        scratch_shapes=[
            pltpu.VMEM((seq // c, 2 * pair_w, pair_w), F32),
            pltpu.VMEM((seq // c, 2 * pair_w, pair_w), BF16),
            pltpu.VMEM((c, 2 * c), F32),
        ],
        compiler_params=_cparams(("parallel", "parallel")),
        name="retention",
    )(rq, rk, rv, rg, dec_f_lane, dec_b_lane, dec_f_col, dec_b_col, gain.astype(F32)[None, :],
      w_down.astype(F32))


N_ROW_OFFSETS = 2 * NA_WIN_ROWS - 1
N_COL_OFFSETS = 2 * NA_WIN_COLS - 1


def _natten_row_offsets(rows):
    n_blocks = rows // NA_ROWS
    starts = {0: 0, 1: NA_ROWS - NA_WIN_ROWS // 2, 2: rows - NA_KEY_ROWS}
    blocks = {0: 0, 1: 1, 2: n_blocks - 1}
    table = []
    for v in range(3):
        per_a = []
        for a in range(NA_ROWS):
            r = blocks[v] * NA_ROWS + a
            rs = min(max(r - NA_WIN_ROWS // 2, 0), rows - NA_WIN_ROWS)
            per_kl = []
            for kl in range(NA_KEY_ROWS):
                kr = starts[v] + kl
                per_kl.append(kr - r + NA_WIN_ROWS - 1 if rs <= kr < rs + NA_WIN_ROWS else None)
            per_a.append(per_kl)
        table.append(per_a)
    return table


def _natten_live_columns(rows):
    offsets = _natten_row_offsets(rows)
    live = []
    for blk in range(NA_STEP_BLOCKS):
        variants = [1] + ([0] if blk == 0 else []) + ([2] if blk == NA_STEP_BLOCKS - 1 else [])
        live.append([[j for j in range(NA_KEY_ROWS // 2)
                      if any(offsets[v][a][kl] is not None for v in variants for kl in (2 * j, 2 * j + 1))]
                     for a in range(NA_ROWS)])
    return live


def _natten_build_bias(rpb_ref, tab_ref, pair, rows):
    offsets = _natten_row_offsets(rows)
    nk = NA_KEY_ROWS * GRID_W
    shape = (GRID_W, LANES)
    lane = lax.broadcasted_iota(I32, shape, 1)
    c = lax.broadcasted_iota(I32, shape, 0)
    second = lane >= GRID_W
    kc = lane % GRID_W
    cs = jnp.clip(c - NA_WIN_COLS // 2, 0, GRID_W - NA_WIN_COLS)
    col_ok = jnp.logical_and(kc >= cs, kc < cs + NA_WIN_COLS)
    neg = jnp.full(shape, NEG_BIG, F32)
    for hh in range(2):
        toeplitz = []
        for dr in range(N_ROW_OFFSETS):
            x = jnp.broadcast_to(rpb_ref[2 * pair + hh, dr:dr + 1, :] * LOG2E, shape)
            lo = pltpu.roll(x, LANES - (NA_WIN_COLS - 1), axis=1, stride=1, stride_axis=0)
            hi = pltpu.roll(x, GRID_W - (NA_WIN_COLS - 1), axis=1, stride=1, stride_axis=0)
            toeplitz.append(jnp.where(second, hi, lo))
        for v in range(3):
            for a in range(NA_ROWS):
                for j in range(NA_KEY_ROWS // 2):
                    d0, d1 = offsets[v][a][2 * j], offsets[v][a][2 * j + 1]
                    if d0 is None and d1 is None:
                        piece = neg
                    else:
                        t0 = neg if d0 is None else toeplitz[d0]
                        t1 = neg if d1 is None else toeplitz[d1]
                        piece = jnp.where(col_ok, jnp.where(second, t1, t0), neg)
                    tab_ref[v, pair, a * GRID_W:(a + 1) * GRID_W,
                            hh * nk + j * LANES:hh * nk + (j + 1) * LANES] = piece


def _natten_kernel(q_ref, ke_ref, ko_ref, ve_ref, vo_ref, rpb_ref, wgu_f, o_ref, wgu_b, tab_ref, *, rows):
    pair_id = pl.program_id(1)

    @pl.when(jnp.logical_and(pl.program_id(0) == 0, jnp.logical_and(pair_id == 0, pl.program_id(2) == 0)))
    def _():
        for pair in range(HEADS // 2):
            _natten_build_bias(rpb_ref, tab_ref, pair, rows)

    wgu_b[...] = wgu_f[...].astype(BF16)
    nq = NA_ROWS * GRID_W
    nk = NA_KEY_ROWS * GRID_W
    n_blocks = rows // NA_ROWS
    live = _natten_live_columns(rows)
    head0 = lax.broadcasted_iota(I32, (1, 2 * HEAD_DIM), 1) < HEAD_DIM
    ind0 = jnp.broadcast_to(jnp.where(head0, 1.0, 0.0).astype(BF16), (nk, 2 * HEAD_DIM))
    ind1 = jnp.broadcast_to(jnp.where(head0, 0.0, 1.0).astype(BF16), (nk, 2 * HEAD_DIM))

    for blk in range(NA_STEP_BLOCKS):
        rb = pl.program_id(2) * NA_STEP_BLOCKS + blk
        variant = jnp.where(rb == 0, 0, jnp.where(rb == n_blocks - 1, 2, 1))
        start_row = jnp.clip(rb * NA_ROWS - NA_WIN_ROWS // 2, 0, rows - NA_KEY_ROWS)
        win = pl.ds(pl.multiple_of(start_row * GRID_W, GRID_W), nk)
        q = q_ref[blk * nq:(blk + 1) * nq, :]
        k_st = jnp.concatenate([ke_ref[win, :], ko_ref[win, :]], axis=0)
        v_st = jnp.concatenate([jnp.concatenate([ve_ref[win, :], ind0], axis=1),
                                jnp.concatenate([vo_ref[win, :], ind1], axis=1)], axis=0)
        s = _dot_nt(q, k_st) + tab_ref[variant, pair_id]
        dead = jnp.zeros((GRID_W, LANES), BF16)
        slabs = []
        for a in range(NA_ROWS):
            qrows = slice(a * GRID_W, (a + 1) * GRID_W)
            parts = []
            for h in range(2):
                cols = {j: s[qrows, h * nk + j * LANES:h * nk + (j + 1) * LANES] for j in live[blk][a]}
                m = jnp.max(jnp.concatenate(list(cols.values()), axis=1), axis=1, keepdims=True)
                parts += [jnp.exp2(cols[j] - m).astype(BF16) if j in cols else dead
                          for j in range(NA_KEY_ROWS // 2)]
            slabs.append(jnp.concatenate(parts, axis=1))
        out = _dot(jnp.concatenate(slabs, axis=0), v_st)
        o_ref[blk * nq:(blk + 1) * nq, :] = (out[:, :2 * HEAD_DIM] / out[:, 2 * HEAD_DIM:]).astype(BF16)


def _natten(nq, nke, nko, nve, nvo, rpb, w_gu, batch, seq):
    t = nq.shape[0]
    rows = seq // GRID_W
    n_pairs = HEADS // 2
    n_steps = rows // (NA_ROWS * NA_STEP_BLOCKS)
    nqb = NA_ROWS * GRID_W
    nkb = NA_KEY_ROWS * GRID_W
    rpb_pad = jnp.pad(rpb.astype(F32), ((0, 0), (0, 0), (0, LANES - N_COL_OFFSETS)))
    total_steps = batch * n_pairs * n_steps
    n_exp = w_gu.shape[0]
    assert n_exp % total_steps == 0
    epb = n_exp // total_steps

    def expert_spec(shape):
        return pl.BlockSpec((epb,) + shape[1:], lambda b, p, s: ((b * n_pairs + p) * n_steps + s, 0, 0))

    rpb_spec = pl.BlockSpec(rpb_pad.shape, lambda b, p, s: (0, 0, 0))
    kv_spec = pl.BlockSpec((seq, 2 * HEAD_DIM), lambda b, p, s: (b, p))
    q_spec = pl.BlockSpec((NA_STEP_BLOCKS * nqb, 2 * HEAD_DIM), lambda b, p, s: (b * n_steps + s, p))
    return pl.pallas_call(
        functools.partial(_natten_kernel, rows=rows),
        grid=(batch, n_pairs, n_steps),
        in_specs=[q_spec, kv_spec, kv_spec, kv_spec, kv_spec, rpb_spec, expert_spec(w_gu.shape)],
        out_specs=[q_spec, expert_spec(w_gu.shape)],
        out_shape=[jax.ShapeDtypeStruct((t, GROUP_W), BF16), jax.ShapeDtypeStruct(w_gu.shape, BF16)],
        scratch_shapes=[pltpu.VMEM((3, n_pairs, nqb, 2 * nkb), F32)],
        compiler_params=_cparams(("arbitrary", "arbitrary", "arbitrary")),
        name="natten",
    )(nq, nke, nko, nve, nvo, rpb_pad, w_gu.astype(F32))


def _layer_norm(h, gain, bias):
    mu = jnp.mean(h, axis=-1, keepdims=True)
    d = h - mu
    var = jnp.mean(d * d, axis=-1, keepdims=True)
    return d * lax.rsqrt(var + LN_EPS) * gain + bias


def _route_tile(xb, wr, rbias, tri):
    tm = xb.shape[0]
    scores = jax.nn.sigmoid(_dot_nt(wr, xb))
    biased = scores + rbias
    sub = lax.broadcasted_iota(I32, (GROUP_SIZE, tm), 0).astype(F32)
    none = float(N_EXPERTS)
    ninf = -jnp.inf

    def first_max(vals, index):
        m = jnp.max(vals, axis=0, keepdims=True)
        return m, jnp.min(jnp.where(vals == m, index, none), axis=0, keepdims=True)

    groups = [biased[g * GROUP_SIZE:(g + 1) * GROUP_SIZE, :] for g in range(N_GROUPS)]
    group_scores = []
    for g in range(N_GROUPS):
        m1, i1 = first_max(groups[g], sub)
        m2 = jnp.max(jnp.where(sub == i1, ninf, groups[g]), axis=0, keepdims=True)
        group_scores.append(m1 + m2)
    cur = jnp.concatenate(group_scores, axis=0)
    group_sel = jnp.zeros(cur.shape, F32)
    for _ in range(TOPK_GROUPS):
        _, i1 = first_max(cur, sub)
        hit = sub == i1
        group_sel = jnp.where(hit, 1.0, group_sel)
        cur = jnp.where(hit, ninf, cur)

    masked = [jnp.where(group_sel[g:g + 1, :] > 0.5, groups[g], ninf) for g in range(N_GROUPS)]
    ids = [sub + float(g * GROUP_SIZE) for g in range(N_GROUPS)]
    chosen = [jnp.zeros((GROUP_SIZE, tm), F32) for _ in range(N_GROUPS)]
    for _ in range(TOP_K):
        m = masked[0]
        for g in range(1, N_GROUPS):
            m = jnp.maximum(m, masked[g])
        m = jnp.max(m, axis=0, keepdims=True)
        cand = jnp.where(masked[0] == m, ids[0], none)
        for g in range(1, N_GROUPS):
            cand = jnp.minimum(cand, jnp.where(masked[g] == m, ids[g], none))
        first = jnp.min(cand, axis=0, keepdims=True)
        for g in range(N_GROUPS):
            hit = ids[g] == first
            chosen[g] = jnp.where(hit, 1.0, chosen[g])
            masked[g] = jnp.where(hit, ninf, masked[g])

    sel = jnp.concatenate(chosen, axis=0) > 0.5
    picked = jnp.where(sel, scores, 0.0)
    total = jnp.sum(picked, axis=0, keepdims=True)
    weight = picked / total * ROUTED_SCALE
    sel_f = jnp.where(sel, 1.0, 0.0)
    sel_b = sel_f.astype(BF16)
    before, cnt_max = [], None
    for k in range(tm // MOE_SUB):
        cols = slice(k * MOE_SUB, (k + 1) * MOE_SUB)
        before.append(_dot(sel_b[:, cols], tri))
        cnt = jnp.sum(sel_f[:, cols], axis=1, keepdims=True)
        cnt_max = cnt if cnt_max is None else jnp.maximum(cnt_max, cnt)
    rank = jnp.where(sel, jnp.concatenate(before, axis=1).astype(I32), -1)
    return weight, rank, cnt_max


def _mid_kernel(ret_ref, na_ref, x_ref, p_ref, wo_f, wsgu_f, wsd_f, wp_f, wg_f, gain_ref, bias_ref,
                wr_ref, rb_ref, pre_ref, x1b_ref, w_ref, rank_ref, cnt_ref,
                wo_ref, wsgu_ref, wsd_ref, wp_ref, wg_ref, tri_ref):
    for src_ref, dst_ref in ((wo_f, wo_ref), (wsgu_f, wsgu_ref), (wsd_f, wsd_ref), (wp_f, wp_ref),
                             (wg_f, wg_ref)):
        _cast_once(src_ref, dst_ref)

    @pl.when(pl.program_id(0) == 0)
    def _():
        i = lax.broadcasted_iota(I32, (MOE_SUB, MOE_SUB), 0)
        j = lax.broadcasted_iota(I32, (MOE_SUB, MOE_SUB), 1)
        tri_ref[...] = jnp.where(i < j, 1.0, 0.0).astype(BF16)

    sub_blocks = [slice(r, r + MOE_SUB) for r in range(0, x_ref.shape[0], MOE_SUB)]
    for rows in sub_blocks:
        mix = (_dot(ret_ref[rows, :], wo_ref[0:GROUP_W, :])
               + _dot(na_ref[rows, :], wo_ref[GROUP_W:2 * GROUP_W, :]))
        x1 = _layer_norm(ALPHA * x_ref[rows, :] + mix, gain_ref[...], bias_ref[...])
        x1b_ref[rows, :] = x1.astype(BF16)
        pre_ref[rows, :] = ALPHA * x1
    weight, rank, cnt_max = _route_tile(x1b_ref[...], wr_ref[...], rb_ref[...], tri_ref[...])
    w_ref[...] = weight
    rank_ref[...] = rank
    cnt_ref[...] = jnp.broadcast_to(cnt_max, cnt_ref.shape).astype(I32)
    for rows in sub_blocks:
        xb = x1b_ref[rows, :]
        h = _dot(xb, wsgu_ref[...])
        act = (jax.nn.silu(h[:, :EXPERT_DIM]) * h[:, EXPERT_DIM:]).astype(BF16)
        shared = _dot(act, wsd_ref[...])
        ple = _dot(p_ref[rows, :].astype(BF16), wp_ref[...]) * jax.nn.sigmoid(_dot(xb, wg_ref[...]))
        pre_ref[rows, :] += shared + ple


def _mid(ret, na, x2, p2, w_out, gain, bias, w_router, router_bias, wsgu, wsd, wp, wg):
    t = x2.shape[0]
    tm = MID_TM
    nt = t // tm
    tok = lambda i: (i, 0)
    const = lambda i: (0, 0)
    col = lambda i: (0, i)
    weights = [w.astype(F32) for w in (w_out, wsgu, wsd, wp, wg)]
    wr_t = w_router.astype(F32).T.astype(BF16)
    return pl.pallas_call(
        _mid_kernel,
        grid=(nt,),
        in_specs=[pl.BlockSpec((tm, GROUP_W), tok), pl.BlockSpec((tm, GROUP_W), tok),
                  pl.BlockSpec((tm, D_MODEL), tok), pl.BlockSpec((tm, p2.shape[1]), tok)]
                 + [_resident(w.shape) for w in weights]
                 + [pl.BlockSpec((1, D_MODEL), const), pl.BlockSpec((1, D_MODEL), const),
                    pl.BlockSpec((N_EXPERTS, D_MODEL), const), pl.BlockSpec((N_EXPERTS, 1), const)],
        out_specs=[pl.BlockSpec((tm, D_MODEL), tok), pl.BlockSpec((tm, D_MODEL), tok),
                   pl.BlockSpec((N_EXPERTS, tm), col), pl.BlockSpec((N_EXPERTS, tm), col),
                   pl.BlockSpec((N_EXPERTS, LANES), col)],
        out_shape=[jax.ShapeDtypeStruct((t, D_MODEL), F32), jax.ShapeDtypeStruct((t, D_MODEL), BF16),
                   jax.ShapeDtypeStruct((N_EXPERTS, t), F32), jax.ShapeDtypeStruct((N_EXPERTS, t), I32),
                   jax.ShapeDtypeStruct((N_EXPERTS, nt * LANES), I32)],
        scratch_shapes=[pltpu.VMEM(w.shape, BF16) for w in weights] + [pltpu.VMEM((MOE_SUB, MOE_SUB), BF16)],
        compiler_params=_cparams(("arbitrary",)),
        name="mid",
    )(ret, na, x2, p2, *weights, gain.astype(F32)[None, :], bias.astype(F32)[None, :],
      wr_t, router_bias.astype(F32)[:, None])


def _moe_kernel(cnt_ref, x_ref, rank_ref, w_ref, wgu_ref, wd_ref, pre_ref, gain_ref, bias_ref,
                o_ref, oh_ref, g_ref, y_ref):
    i = pl.program_id(0)
    eb = pl.program_id(1)
    n_eb = pl.num_programs(1)

    @pl.when(eb == 0)
    def _():
        o_ref[...] = jnp.zeros_like(o_ref)

    pre_rows = pre_ref.shape[0]
    o_ref[pl.ds(pl.multiple_of(eb * pre_rows, pre_rows), pre_rows), :] += pre_ref[...]

    max_count = cnt_ref[eb, i]

    def one_pass(p, slots):
        n_rows = MOE_EB * slots
        base = p * slots
        slot = lax.broadcasted_iota(I32, (slots, MOE_SUB), 0)
        slot_w = []
        for k in range(MOE_NSUB):
            tok = slice(k * MOE_SUB, (k + 1) * MOE_SUB)
            blocks, weights = [], []
            for j in range(MOE_EB):
                expert = pl.ds(eb * MOE_EB + j, 1)
                match = slot == (rank_ref[expert, tok] - base)
                blocks.append(jnp.where(match, 1.0, 0.0).astype(BF16))
                weights.append(jnp.sum(jnp.where(match, w_ref[expert, tok], 0.0), axis=1, keepdims=True))
            onehot = jnp.concatenate(blocks, axis=0)
            oh_ref[k, 0:n_rows, :] = onehot
            slot_w.append(weights)
            g_ref[k, 0:n_rows, :] = _dot(onehot, x_ref[tok, :]).astype(BF16)
        for j in range(MOE_EB):
            rows = slice(j * slots, (j + 1) * slots)
            xe = jnp.concatenate([g_ref[k, rows, :] for k in range(MOE_NSUB)], axis=0)
            h = _dot(xe, wgu_ref[j])
            act = (jax.nn.silu(h[:, :EXPERT_DIM]) * h[:, EXPERT_DIM:]).astype(BF16)
            wc = jnp.concatenate([slot_w[k][j] for k in range(MOE_NSUB)], axis=0)
            yw = (_dot(act, wd_ref[j]) * wc).astype(BF16)
            for k in range(MOE_NSUB):
                y_ref[k, rows, :] = yw[k * slots:(k + 1) * slots, :]
        for k in range(MOE_NSUB):
            tok = slice(k * MOE_SUB, (k + 1) * MOE_SUB)
            o_ref[tok, :] += _dot_tn(oh_ref[k, 0:n_rows, :], y_ref[k, 0:n_rows, :])

    @pl.when(max_count <= MOE_SLOTS_SMALL)
    def _():
        one_pass(0, MOE_SLOTS_SMALL)

    @pl.when(max_count > MOE_SLOTS_SMALL)
    def _():
        n_pass = lax.shift_right_logical(max_count + (MOE_SLOTS - 1), int(np.log2(MOE_SLOTS)))

        def body(p, carry):
            one_pass(p, MOE_SLOTS)
            return carry

        lax.fori_loop(0, n_pass, body, 0)

    @pl.when(eb == n_eb - 1)
    def _():
        for r in range(0, o_ref.shape[0], MOE_SUB):
            rows = slice(r, r + MOE_SUB)
            o_ref[rows, :] = _layer_norm(o_ref[rows, :], gain_ref[...], bias_ref[...])


def _moe(x1b, rank_t, w_t, counts, wgu_b, wd_b, pre, gain, bias):
    t = x1b.shape[0]
    tm = MOE_TM
    nt = t // tm
    n_eb = N_EXPERTS // MOE_EB
    pre_rows = tm // n_eb
    counts = counts.reshape(N_EXPERTS // MOE_EB, MOE_EB, nt, tm // MID_TM).max(axis=(1, 3))
    grid_spec = pltpu.PrefetchScalarGridSpec(
        num_scalar_prefetch=1,
        grid=(nt, N_EXPERTS // MOE_EB),
        in_specs=[
            pl.BlockSpec((tm, D_MODEL), lambda i, e, c: (i, 0)),
            pl.BlockSpec((N_EXPERTS, tm), lambda i, e, c: (0, i)),
            pl.BlockSpec((N_EXPERTS, tm), lambda i, e, c: (0, i)),
            pl.BlockSpec((MOE_EB, D_MODEL, 2 * EXPERT_DIM), lambda i, e, c: (e, 0, 0)),
            pl.BlockSpec((MOE_EB, EXPERT_DIM, D_MODEL), lambda i, e, c: (e, 0, 0)),
            pl.BlockSpec((pre_rows, D_MODEL), lambda i, e, c: (i * n_eb + e, 0)),
            pl.BlockSpec((1, D_MODEL), lambda i, e, c: (0, 0)),
            pl.BlockSpec((1, D_MODEL), lambda i, e, c: (0, 0)),
        ],
        out_specs=pl.BlockSpec((tm, D_MODEL), lambda i, e, c: (i, 0)),
        scratch_shapes=[pltpu.VMEM((MOE_NSUB, MOE_SUB, MOE_SUB), BF16),
                        pltpu.VMEM((MOE_NSUB, MOE_SUB, D_MODEL), BF16),
                        pltpu.VMEM((MOE_NSUB, MOE_SUB, D_MODEL), BF16)],
    )
    return pl.pallas_call(
        _moe_kernel,
        grid_spec=grid_spec,
        out_shape=jax.ShapeDtypeStruct((t, D_MODEL), F32),
        compiler_params=_cparams(("parallel", "arbitrary")),
        name="moe",
    )(counts, x1b, rank_t, w_t, wgu_b, wd_b, pre, gain.astype(F32)[None, :], bias.astype(F32)[None, :])


def _rotary_tables(seq):
    half = HEAD_DIM // 2
    inv = ROPE_BASE ** (-jnp.arange(half, dtype=F32) / half)
    ang = jnp.arange(seq, dtype=jnp.int32).astype(F32)[:, None] * inv[None, :]
    cos, sin = jnp.cos(ang), jnp.sin(ang)
    reps = LANES // HEAD_DIM
    cos_t = jnp.tile(jnp.concatenate([cos, cos], axis=1), (1, reps))
    sin_t = jnp.tile(jnp.concatenate([-sin, sin], axis=1), (1, reps))
    return cos_t, sin_t


def kernel(x, p, w_in, ret_decay_fwd, ret_decay_bwd, ret_gn_gain, na_rpb, w_out, ln1_gain, ln1_bias,
           w_router, router_bias, w_expert_gu, w_expert_down, w_shared_gu, w_shared_down,
           w_ple_proj, w_ple_gate, ln2_gain, ln2_bias):
    batch, seq, d = x.shape
    t = batch * seq
    depth = w_in.shape[0]
    assert depth == 1 and d == D_MODEL
    assert seq % PROJ_TM == 0 and seq % MOE_TM == 0 and seq % (RET_CHUNK * RET_UNROLL) == 0
    assert MOE_TM % (N_EXPERTS // MOE_EB) == 0
    assert seq % MID_TM == 0 and MOE_TM % MID_TM == 0 and MID_TM % MOE_SUB == 0
    assert (seq // GRID_W) % (NA_ROWS * NA_STEP_BLOCKS) == 0 and seq // GRID_W >= NA_KEY_ROWS
    cos_t, sin_t = _rotary_tables(seq)
    x2 = x.reshape(t, d)
    for i in range(depth):
        rq, rk, rv, rg, nq, nke, nko, nve, nvo = _in_proj(x2, w_in[i], cos_t, sin_t, seq)
        ret, wd_b = _retention(rq, rk, rv, rg, ret_decay_fwd[i], ret_decay_bwd[i], ret_gn_gain[i],
                               w_expert_down[i], batch, seq)
        na, wgu_b = _natten(nq, nke, nko, nve, nvo, na_rpb[i], w_expert_gu[i], batch, seq)
        pre, x1b, w_t, rank_t, cnt = _mid(ret, na, x2, p[i].reshape(t, -1), w_out[i], ln1_gain[i], ln1_bias[i],
                                          w_router[i], router_bias[i], w_shared_gu[i], w_shared_down[i],
                                          w_ple_proj[i], w_ple_gate[i])
        x2 = _moe(x1b, rank_t, w_t, cnt[:, ::LANES], wgu_b, wd_b, pre, ln2_gain[i], ln2_bias[i])
    return x2.reshape(batch, seq, d)
```

```python
import functools

import numpy as np
import jax
import jax.numpy as jnp
from jax import lax
from jax.experimental import pallas as pl
from jax.experimental.pallas import tpu as pltpu

F32 = jnp.float32
BF16 = jnp.bfloat16
I32 = jnp.int32

D_MODEL = 1024
HEADS = 8
HEAD_DIM = 64
GROUP_W = HEADS * HEAD_DIM
ROPE_BASE = 10000.0
GN_EPS = 1e-6
LN_EPS = 1e-5
GRID_W = 64
NA_WIN_ROWS = 8
NA_WIN_COLS = 16
N_EXPERTS = 64
N_GROUPS = 8
GROUP_SIZE = N_EXPERTS // N_GROUPS
TOPK_GROUPS = 4
TOP_K = 8
EXPERT_DIM = 256
ROUTED_SCALE = 2.5
ALPHA = 2.0 ** 0.25
NEG_BIG = -1e30
LOG2E = 1.4426950408889634
NA_Q_SCALE = HEAD_DIM ** -0.5 * LOG2E

LANES = 128
VMEM_LIMIT_BYTES = 56 * 1024 * 1024

PROJ_TM = 1024
MID_TM = 1024
RET_CHUNK = 128
RET_UNROLL = 32
NA_ROWS = 4
NA_KEY_ROWS = NA_ROWS + NA_WIN_ROWS
NA_STEP_BLOCKS = 8
MOE_TM = 2048
MOE_SUB = 256
MOE_EB = 4
MOE_SLOTS = MOE_SUB // MOE_EB
MOE_SLOTS_SMALL = 48
MOE_NSUB = MOE_TM // MOE_SUB


def _cparams(sem):
    return pltpu.CompilerParams(dimension_semantics=sem, vmem_limit_bytes=VMEM_LIMIT_BYTES)


def _dot(a, b):
    return jnp.dot(a, b, preferred_element_type=F32)


def _dot_nt(a, b):
    return lax.dot_general(a, b, (((1,), (1,)), ((), ())), preferred_element_type=F32)


def _dot_tn(a, b):
    return lax.dot_general(a, b, (((0,), (0,)), ((), ())), preferred_element_type=F32)


def _cast_once(src_ref, dst_ref):
    @pl.when(pl.program_id(0) == 0)
    def _():
        dst_ref[...] = src_ref[...].astype(BF16)


def _resident(shape):
    return pl.BlockSpec(shape, lambda i: (0,) * len(shape), pipeline_mode=pl.Buffered(1))


def _in_proj_kernel(x_ref, wf_ref, cos_ref, sin_ref,
                    rq_ref, rk_ref, rv_ref, rg_ref, nq_ref, nke_ref, nko_ref, nve_ref, nvo_ref, w_ref):
    _cast_once(wf_ref, w_ref)
    xb = x_ref[...].astype(BF16)
    cos = cos_ref[...]
    sin = sin_ref[...]
    lane = lax.broadcasted_iota(I32, (1, LANES), 1)
    first_half = (lane % HEAD_DIM) < (HEAD_DIM // 2)

    def proj(g):
        return _dot(xb, w_ref[:, g * GROUP_W:(g + 1) * GROUP_W])

    def rotary(t, scale):
        outs = []
        for j in range(GROUP_W // LANES):
            c = t[:, j * LANES:(j + 1) * LANES]
            swapped = jnp.where(first_half,
                                pltpu.roll(c, LANES - HEAD_DIM // 2, axis=1),
                                pltpu.roll(c, HEAD_DIM // 2, axis=1))
            outs.append((c * cos + swapped * sin) * scale)
        return jnp.concatenate(outs, axis=1)

    rq_ref[...] = rotary(proj(0), 1.0).astype(BF16)
    rk_ref[...] = rotary(proj(1), HEAD_DIM ** -0.5).astype(BF16)
    rv_ref[...] = proj(2).astype(BF16)
    rg_ref[...] = jax.nn.silu(proj(3)).astype(BF16)
    nq_ref[...] = (proj(4) * NA_Q_SCALE).astype(BF16)
    even_head = (lax.broadcasted_iota(I32, (1, GROUP_W), 1) % (2 * HEAD_DIM)) < HEAD_DIM
    for g, even_ref, odd_ref in ((5, nke_ref, nko_ref), (6, nve_ref, nvo_ref)):
        t = proj(g)
        even_ref[...] = jnp.where(even_head, t, 0.0).astype(BF16)
        odd_ref[...] = jnp.where(even_head, 0.0, t).astype(BF16)


def _in_proj(x2, w_in, cos_t, sin_t, seq):
    t = x2.shape[0]
    tm = PROJ_TM
    n_pos = seq // tm
    out = jax.ShapeDtypeStruct((t, GROUP_W), BF16)
    tok = lambda i: (i, 0)
    return pl.pallas_call(
        _in_proj_kernel,
        grid=(t // tm,),
        in_specs=[
            pl.BlockSpec((tm, D_MODEL), tok),
            _resident(w_in.shape),
            pl.BlockSpec((tm, LANES), lambda i: (i % n_pos, 0)),
            pl.BlockSpec((tm, LANES), lambda i: (i % n_pos, 0)),
        ],
        out_specs=[pl.BlockSpec((tm, GROUP_W), tok)] * 9,
        out_shape=[out] * 9,
        scratch_shapes=[pltpu.VMEM(w_in.shape, BF16)],
        compiler_params=_cparams(("arbitrary",)),
        name="in_proj",
    )(x2, w_in.astype(F32), cos_t, sin_t)


def _log_sigmoid(x):
    return jnp.minimum(x, 0.0) - jnp.log1p(jnp.exp(-jnp.abs(x)))


def _retention_kernel(q_ref, k_ref, v_ref, g_ref, decf_ref, decb_ref,
                      gain_ref, wd_f, o_ref, wd_b, kv_ref, st_ref, dmat_ref):
    wd_b[...] = wd_f[...].astype(BF16)
    c = RET_CHUNK
    n_chunks = q_ref.shape[0] // c
    pair_w = 2 * HEAD_DIM

    h0 = 2 * pl.program_id(1)
    head0 = lax.broadcasted_iota(I32, (1, pair_w), 1) < HEAD_DIM
    lgf = _log_sigmoid(jnp.where(head0, decf_ref[h0], decf_ref[h0 + 1]))
    lgb = _log_sigmoid(jnp.where(head0, decb_ref[h0], decb_ref[h0 + 1]))
    row = lax.broadcasted_iota(I32, (c, 1), 0).astype(F32)
    k_dec_f = jnp.exp((c - 1.0 - row) * lgf)
    k_dec_b = jnp.exp(row * lgb)
    q_dec_f = jnp.exp((row + 1.0) * lgf)
    q_dec_b = jnp.exp((c - row) * lgb)
    chunk_dec_f = jnp.exp(c * lgf)
    chunk_dec_b = jnp.exp(c * lgb)

    r2 = lax.broadcasted_iota(I32, (pair_w, pair_w), 0) // HEAD_DIM
    c2 = lax.broadcasted_iota(I32, (pair_w, pair_w), 1) // HEAD_DIM
    same_head = r2 == c2
    block_diag = jnp.where(same_head, 1.0, 0.0)
    seg_avg = jnp.where(same_head, 1.0 / HEAD_DIM, 0.0).astype(BF16)

    head0_col = lax.broadcasted_iota(I32, (1, 2 * c), 1) < c
    lgf_d = _log_sigmoid(jnp.where(head0_col, decf_ref[h0], decf_ref[h0 + 1]))
    lgb_d = _log_sigmoid(jnp.where(head0_col, decb_ref[h0], decb_ref[h0 + 1]))
    di = lax.broadcasted_iota(I32, (c, 2 * c), 0)
    dj = lax.broadcasted_iota(I32, (c, 2 * c), 1) % c
    diff = (di - dj).astype(F32)
    dmat_ref[...] = jnp.where(diff >= 0.0, jnp.exp(diff * lgf_d), jnp.exp(-diff * lgb_d))

    def chunk(ref, n):
        return ref[pl.ds(pl.multiple_of(n * c, c), c), :]

    unroll = RET_UNROLL

    def summarize(nb, carry):
        for u in range(unroll):
            n = nb * unroll + u
            kf = chunk(k_ref, n).astype(F32)
            kst = jnp.concatenate([kf * k_dec_f, kf * k_dec_b], axis=1).astype(BF16)
            kv_ref[n] = _dot_tn(kst, chunk(v_ref, n))
        return carry

    lax.fori_loop(0, n_chunks // unroll, summarize, 0)

    def scan(i, states):
        fwd, bwd = states
        nb = n_chunks - 1 - i
        st_ref[i, 0:pair_w, :] = fwd.astype(BF16)
        st_ref[nb, pair_w:2 * pair_w, :] = bwd.astype(BF16)
        return (fwd * chunk_dec_f + kv_ref[i, 0:pair_w, :] * block_diag,
                bwd * chunk_dec_b + kv_ref[nb, pair_w:2 * pair_w, :] * block_diag)

    zero_state = jnp.zeros((pair_w, pair_w), F32)
    lax.fori_loop(0, n_chunks, scan, (zero_state, zero_state), unroll=True)

    gain = gain_ref[...]

    seg_avg2 = jnp.concatenate([seg_avg, seg_avg], axis=0)

    def seg_mean(z):
        hi = z.astype(BF16)
        lo = (z - hi.astype(F32)).astype(BF16)
        return _dot(jnp.concatenate([hi, lo], axis=1), seg_avg2)

    def emit(nb, carry):
        ys = []
        for u in range(unroll):
            n = nb * unroll + u
            q = chunk(q_ref, n)
            k = chunk(k_ref, n)
            v = chunk(v_ref, n)
            zero = jnp.zeros_like(k)
            k_st = jnp.concatenate([jnp.where(head0, k, zero), jnp.where(head0, zero, k)], axis=0)
            v_st = jnp.concatenate([jnp.where(head0, v, zero), jnp.where(head0, zero, v)], axis=0)
            scores = _dot_nt(q, k_st) * dmat_ref[...]
            qf = q.astype(F32)
            q_st = jnp.concatenate([qf * q_dec_f, qf * q_dec_b], axis=1).astype(BF16)
            ys.append(_dot(scores.astype(BF16), v_st) + _dot(q_st, st_ref[n]))
        y = jnp.concatenate(ys, axis=0)
        mu = seg_mean(y)
        d = y - mu
        var = seg_mean(d * d)
        yn = d * lax.rsqrt(var + GN_EPS) * gain
        rows = pl.ds(pl.multiple_of(nb * (unroll * c), unroll * c), unroll * c)
        o_ref[rows, :] = (g_ref[rows, :].astype(F32) * yn).astype(BF16)
        return carry

    lax.fori_loop(0, n_chunks // unroll, emit, 0)


def _retention(rq, rk, rv, rg, dec_f, dec_b, gain, w_down, batch, seq):
    t = rq.shape[0]
    c = RET_CHUNK
    n_pairs = HEADS // 2
    pair_w = 2 * HEAD_DIM
    n_exp = w_down.shape[0]
    assert n_exp % (batch * n_pairs) == 0
    epb = n_exp // (batch * n_pairs)
    wd_spec = pl.BlockSpec((epb,) + w_down.shape[1:], lambda b, p: (b * n_pairs + p, 0, 0))
    tok = pl.BlockSpec((seq, pair_w), lambda b, p: (b, p))
    lane_spec = pl.BlockSpec((1, pair_w), lambda b, p: (0, p))
    scalars = pl.BlockSpec(memory_space=pltpu.SMEM)
    return pl.pallas_call(
        _retention_kernel,
        grid=(batch, n_pairs),
        in_specs=[tok, tok, tok, tok, scalars, scalars, lane_spec, wd_spec],
        out_specs=[tok, wd_spec],
        out_shape=[jax.ShapeDtypeStruct((t, GROUP_W), BF16), jax.ShapeDtypeStruct(w_down.shape, BF16)],
        scratch_shapes=[
            pltpu.VMEM((seq // c, 2 * pair_w, pair_w), F32),
            pltpu.VMEM((seq // c, 2 * pair_w, pair_w), BF16),
            pltpu.VMEM((c, 2 * c), F32),
        ],
        compiler_params=_cparams(("parallel", "parallel")),
        name="retention",
    )(rq, rk, rv, rg, dec_f.astype(F32), dec_b.astype(F32), gain.astype(F32)[None, :], w_down.astype(F32))


N_ROW_OFFSETS = 2 * NA_WIN_ROWS - 1
N_COL_OFFSETS = 2 * NA_WIN_COLS - 1


def _natten_row_offsets(rows):
    n_blocks = rows // NA_ROWS
    starts = {0: 0, 1: NA_ROWS - NA_WIN_ROWS // 2, 2: rows - NA_KEY_ROWS}
    blocks = {0: 0, 1: 1, 2: n_blocks - 1}
    table = []
    for v in range(3):
        per_a = []
        for a in range(NA_ROWS):
            r = blocks[v] * NA_ROWS + a
            rs = min(max(r - NA_WIN_ROWS // 2, 0), rows - NA_WIN_ROWS)
            per_kl = []
            for kl in range(NA_KEY_ROWS):
                kr = starts[v] + kl
                per_kl.append(kr - r + NA_WIN_ROWS - 1 if rs <= kr < rs + NA_WIN_ROWS else None)
            per_a.append(per_kl)
        table.append(per_a)
    return table


def _natten_live_columns(rows):
    offsets = _natten_row_offsets(rows)
    live = []
    for blk in range(NA_STEP_BLOCKS):
        variants = [1] + ([0] if blk == 0 else []) + ([2] if blk == NA_STEP_BLOCKS - 1 else [])
        live.append([[j for j in range(NA_KEY_ROWS // 2)
                      if any(offsets[v][a][kl] is not None for v in variants for kl in (2 * j, 2 * j + 1))]
                     for a in range(NA_ROWS)])
    return live


def _natten_build_bias(rpb_ref, tab_ref, pair, rows):
    offsets = _natten_row_offsets(rows)
    nk = NA_KEY_ROWS * GRID_W
    shape = (GRID_W, LANES)
    lane = lax.broadcasted_iota(I32, shape, 1)
    c = lax.broadcasted_iota(I32, shape, 0)
    second = lane >= GRID_W
    kc = lane % GRID_W
    cs = jnp.clip(c - NA_WIN_COLS // 2, 0, GRID_W - NA_WIN_COLS)
    col_ok = jnp.logical_and(kc >= cs, kc < cs + NA_WIN_COLS)
    neg = jnp.full(shape, NEG_BIG, F32)
    for hh in range(2):
        toeplitz = []
        for dr in range(N_ROW_OFFSETS):
            x = jnp.broadcast_to(rpb_ref[2 * pair + hh, dr:dr + 1, :] * LOG2E, shape)
            lo = pltpu.roll(x, LANES - (NA_WIN_COLS - 1), axis=1, stride=1, stride_axis=0)
            hi = pltpu.roll(x, GRID_W - (NA_WIN_COLS - 1), axis=1, stride=1, stride_axis=0)
            toeplitz.append(jnp.where(second, hi, lo))
        for v in range(3):
            for a in range(NA_ROWS):
                for j in range(NA_KEY_ROWS // 2):
                    d0, d1 = offsets[v][a][2 * j], offsets[v][a][2 * j + 1]
                    if d0 is None and d1 is None:
                        piece = neg
                    else:
                        t0 = neg if d0 is None else toeplitz[d0]
                        t1 = neg if d1 is None else toeplitz[d1]
                        piece = jnp.where(col_ok, jnp.where(second, t1, t0), neg)
                    tab_ref[v, pair, a * GRID_W:(a + 1) * GRID_W,
                            hh * nk + j * LANES:hh * nk + (j + 1) * LANES] = piece


def _natten_kernel(q_ref, ke_ref, ko_ref, ve_ref, vo_ref, rpb_ref, wgu_f, o_ref, wgu_b, tab_ref, *, rows):
    pair_id = pl.program_id(1)

    @pl.when(jnp.logical_and(pl.program_id(0) == 0, jnp.logical_and(pair_id == 0, pl.program_id(2) == 0)))
    def _():
        for pair in range(HEADS // 2):
            _natten_build_bias(rpb_ref, tab_ref, pair, rows)

    wgu_b[...] = wgu_f[...].astype(BF16)
    nq = NA_ROWS * GRID_W
    nk = NA_KEY_ROWS * GRID_W
    n_blocks = rows // NA_ROWS
    live = _natten_live_columns(rows)
    head0 = lax.broadcasted_iota(I32, (1, 2 * HEAD_DIM), 1) < HEAD_DIM
    ind0 = jnp.broadcast_to(jnp.where(head0, 1.0, 0.0).astype(BF16), (nk, 2 * HEAD_DIM))
    ind1 = jnp.broadcast_to(jnp.where(head0, 0.0, 1.0).astype(BF16), (nk, 2 * HEAD_DIM))

    for blk in range(NA_STEP_BLOCKS):
        rb = pl.program_id(2) * NA_STEP_BLOCKS + blk
        variant = jnp.where(rb == 0, 0, jnp.where(rb == n_blocks - 1, 2, 1))
        start_row = jnp.clip(rb * NA_ROWS - NA_WIN_ROWS // 2, 0, rows - NA_KEY_ROWS)
        win = pl.ds(pl.multiple_of(start_row * GRID_W, GRID_W), nk)
        q = q_ref[blk * nq:(blk + 1) * nq, :]
        k_st = jnp.concatenate([ke_ref[win, :], ko_ref[win, :]], axis=0)
        v_st = jnp.concatenate([jnp.concatenate([ve_ref[win, :], ind0], axis=1),
                                jnp.concatenate([vo_ref[win, :], ind1], axis=1)], axis=0)
        s = _dot_nt(q, k_st) + tab_ref[variant, pair_id]
        dead = jnp.zeros((GRID_W, LANES), BF16)
        slabs = []
        for a in range(NA_ROWS):
            qrows = slice(a * GRID_W, (a + 1) * GRID_W)
            parts = []
            for h in range(2):
                cols = {j: s[qrows, h * nk + j * LANES:h * nk + (j + 1) * LANES] for j in live[blk][a]}
                m = jnp.max(jnp.concatenate(list(cols.values()), axis=1), axis=1, keepdims=True)
                parts += [jnp.exp2(cols[j] - m).astype(BF16) if j in cols else dead
                          for j in range(NA_KEY_ROWS // 2)]
            slabs.append(jnp.concatenate(parts, axis=1))
        out = _dot(jnp.concatenate(slabs, axis=0), v_st)
        o_ref[blk * nq:(blk + 1) * nq, :] = (out[:, :2 * HEAD_DIM] / out[:, 2 * HEAD_DIM:]).astype(BF16)


def _natten(nq, nke, nko, nve, nvo, rpb, w_gu, batch, seq):
    t = nq.shape[0]
    rows = seq // GRID_W
    n_pairs = HEADS // 2
    n_steps = rows // (NA_ROWS * NA_STEP_BLOCKS)
    nqb = NA_ROWS * GRID_W
    nkb = NA_KEY_ROWS * GRID_W
    rpb_pad = jnp.pad(rpb.astype(F32), ((0, 0), (0, 0), (0, LANES - N_COL_OFFSETS)))
    total_steps = batch * n_pairs * n_steps
    n_exp = w_gu.shape[0]
    assert n_exp % total_steps == 0
    epb = n_exp // total_steps

    def expert_spec(shape):
        return pl.BlockSpec((epb,) + shape[1:], lambda b, p, s: ((b * n_pairs + p) * n_steps + s, 0, 0))

    rpb_spec = pl.BlockSpec(rpb_pad.shape, lambda b, p, s: (0, 0, 0))
    kv_spec = pl.BlockSpec((seq, 2 * HEAD_DIM), lambda b, p, s: (b, p))
    q_spec = pl.BlockSpec((NA_STEP_BLOCKS * nqb, 2 * HEAD_DIM), lambda b, p, s: (b * n_steps + s, p))
    return pl.pallas_call(
        functools.partial(_natten_kernel, rows=rows),
        grid=(batch, n_pairs, n_steps),
        in_specs=[q_spec, kv_spec, kv_spec, kv_spec, kv_spec, rpb_spec, expert_spec(w_gu.shape)],
        out_specs=[q_spec, expert_spec(w_gu.shape)],
        out_shape=[jax.ShapeDtypeStruct((t, GROUP_W), BF16), jax.ShapeDtypeStruct(w_gu.shape, BF16)],
        scratch_shapes=[pltpu.VMEM((3, n_pairs, nqb, 2 * nkb), F32)],
        compiler_params=_cparams(("arbitrary", "arbitrary", "arbitrary")),
        name="natten",
    )(nq, nke, nko, nve, nvo, rpb_pad, w_gu.astype(F32))


def _layer_norm(h, gain, bias):
    mu = jnp.mean(h, axis=-1, keepdims=True)
    d = h - mu
    var = jnp.mean(d * d, axis=-1, keepdims=True)
    return d * lax.rsqrt(var + LN_EPS) * gain + bias


def _route_tile(xb, wr, rbias, tri):
    tm = xb.shape[0]
    scores = jax.nn.sigmoid(_dot_nt(wr, xb))
    biased = scores + rbias
    sub = lax.broadcasted_iota(I32, (GROUP_SIZE, tm), 0).astype(F32)
    none = float(N_EXPERTS)
    ninf = -jnp.inf

    def first_max(vals, index):
        m = jnp.max(vals, axis=0, keepdims=True)
        return m, jnp.min(jnp.where(vals == m, index, none), axis=0, keepdims=True)

    groups = [biased[g * GROUP_SIZE:(g + 1) * GROUP_SIZE, :] for g in range(N_GROUPS)]
    group_scores = []
    for g in range(N_GROUPS):
        m1, i1 = first_max(groups[g], sub)
        m2 = jnp.max(jnp.where(sub == i1, ninf, groups[g]), axis=0, keepdims=True)
        group_scores.append(m1 + m2)
    cur = jnp.concatenate(group_scores, axis=0)
    group_sel = jnp.zeros(cur.shape, F32)
    for _ in range(TOPK_GROUPS):
        _, i1 = first_max(cur, sub)
        hit = sub == i1
        group_sel = jnp.where(hit, 1.0, group_sel)
        cur = jnp.where(hit, ninf, cur)

    masked = [jnp.where(group_sel[g:g + 1, :] > 0.5, groups[g], ninf) for g in range(N_GROUPS)]
    ids = [sub + float(g * GROUP_SIZE) for g in range(N_GROUPS)]
    chosen = [jnp.zeros((GROUP_SIZE, tm), F32) for _ in range(N_GROUPS)]
    for _ in range(TOP_K):
        m = masked[0]
        for g in range(1, N_GROUPS):
            m = jnp.maximum(m, masked[g])
        m = jnp.max(m, axis=0, keepdims=True)
        cand = jnp.where(masked[0] == m, ids[0], none)
        for g in range(1, N_GROUPS):
            cand = jnp.minimum(cand, jnp.where(masked[g] == m, ids[g], none))
        first = jnp.min(cand, axis=0, keepdims=True)
        for g in range(N_GROUPS):
            hit = ids[g] == first
            chosen[g] = jnp.where(hit, 1.0, chosen[g])
            masked[g] = jnp.where(hit, ninf, masked[g])

    sel = jnp.concatenate(chosen, axis=0) > 0.5
    picked = jnp.where(sel, scores, 0.0)
    total = jnp.sum(picked, axis=0, keepdims=True)
    weight = picked / total * ROUTED_SCALE
    sel_f = jnp.where(sel, 1.0, 0.0)
    sel_b = sel_f.astype(BF16)
    before, cnt_max = [], None
    for k in range(tm // MOE_SUB):
        cols = slice(k * MOE_SUB, (k + 1) * MOE_SUB)
        before.append(_dot(sel_b[:, cols], tri))
        cnt = jnp.sum(sel_f[:, cols], axis=1, keepdims=True)
        cnt_max = cnt if cnt_max is None else jnp.maximum(cnt_max, cnt)
    rank = jnp.where(sel, jnp.concatenate(before, axis=1).astype(I32), -1)
    return weight, rank, cnt_max


def _mid_kernel(ret_ref, na_ref, x_ref, p_ref, wo_f, wsgu_f, wsd_f, wp_f, wg_f, gain_ref, bias_ref,
                wr_ref, rb_ref, pre_ref, x1b_ref, w_ref, rank_ref, cnt_ref,
                wo_ref, wsgu_ref, wsd_ref, wp_ref, wg_ref, tri_ref):
    for src_ref, dst_ref in ((wo_f, wo_ref), (wsgu_f, wsgu_ref), (wsd_f, wsd_ref), (wp_f, wp_ref),
                             (wg_f, wg_ref)):
        _cast_once(src_ref, dst_ref)

    @pl.when(pl.program_id(0) == 0)
    def _():
        i = lax.broadcasted_iota(I32, (MOE_SUB, MOE_SUB), 0)
        j = lax.broadcasted_iota(I32, (MOE_SUB, MOE_SUB), 1)
        tri_ref[...] = jnp.where(i < j, 1.0, 0.0).astype(BF16)

    sub_blocks = [slice(r, r + MOE_SUB) for r in range(0, x_ref.shape[0], MOE_SUB)]
    for rows in sub_blocks:
        mix = (_dot(ret_ref[rows, :], wo_ref[0:GROUP_W, :])
               + _dot(na_ref[rows, :], wo_ref[GROUP_W:2 * GROUP_W, :]))
        x1 = _layer_norm(ALPHA * x_ref[rows, :] + mix, gain_ref[...], bias_ref[...])
        x1b_ref[rows, :] = x1.astype(BF16)
        pre_ref[rows, :] = ALPHA * x1
    weight, rank, cnt_max = _route_tile(x1b_ref[...], wr_ref[...], rb_ref[...], tri_ref[...])
    w_ref[...] = weight
    rank_ref[...] = rank
    cnt_ref[...] = jnp.broadcast_to(cnt_max, cnt_ref.shape).astype(I32)
    for rows in sub_blocks:
        xb = x1b_ref[rows, :]
        h = _dot(xb, wsgu_ref[...])
        act = (jax.nn.silu(h[:, :EXPERT_DIM]) * h[:, EXPERT_DIM:]).astype(BF16)
        shared = _dot(act, wsd_ref[...])
        ple = _dot(p_ref[rows, :].astype(BF16), wp_ref[...]) * jax.nn.sigmoid(_dot(xb, wg_ref[...]))
        pre_ref[rows, :] += shared + ple


def _mid(ret, na, x2, p2, w_out, gain, bias, w_router, router_bias, wsgu, wsd, wp, wg):
    t = x2.shape[0]
    tm = MID_TM
    nt = t // tm
    tok = lambda i: (i, 0)
    const = lambda i: (0, 0)
    col = lambda i: (0, i)
    weights = [w.astype(F32) for w in (w_out, wsgu, wsd, wp, wg)]
    wr_t = w_router.astype(F32).T.astype(BF16)
    return pl.pallas_call(
        _mid_kernel,
        grid=(nt,),
        in_specs=[pl.BlockSpec((tm, GROUP_W), tok), pl.BlockSpec((tm, GROUP_W), tok),
                  pl.BlockSpec((tm, D_MODEL), tok), pl.BlockSpec((tm, p2.shape[1]), tok)]
                 + [_resident(w.shape) for w in weights]
                 + [pl.BlockSpec((1, D_MODEL), const), pl.BlockSpec((1, D_MODEL), const),
                    pl.BlockSpec((N_EXPERTS, D_MODEL), const), pl.BlockSpec((N_EXPERTS, 1), const)],
        out_specs=[pl.BlockSpec((tm, D_MODEL), tok), pl.BlockSpec((tm, D_MODEL), tok),
                   pl.BlockSpec((N_EXPERTS, tm), col), pl.BlockSpec((N_EXPERTS, tm), col),
                   pl.BlockSpec((N_EXPERTS, LANES), col)],
        out_shape=[jax.ShapeDtypeStruct((t, D_MODEL), F32), jax.ShapeDtypeStruct((t, D_MODEL), BF16),
                   jax.ShapeDtypeStruct((N_EXPERTS, t), F32), jax.ShapeDtypeStruct((N_EXPERTS, t), I32),
                   jax.ShapeDtypeStruct((N_EXPERTS, nt * LANES), I32)],
        scratch_shapes=[pltpu.VMEM(w.shape, BF16) for w in weights] + [pltpu.VMEM((MOE_SUB, MOE_SUB), BF16)],
        compiler_params=_cparams(("arbitrary",)),
        name="mid",
    )(ret, na, x2, p2, *weights, gain.astype(F32)[None, :], bias.astype(F32)[None, :],
      wr_t, router_bias.astype(F32)[:, None])


def _moe_kernel(cnt_ref, x_ref, rank_ref, w_ref, wgu_ref, wd_ref, pre_ref, gain_ref, bias_ref,
                o_ref, oh_ref, g_ref, y_ref):
    i = pl.program_id(0)
    eb = pl.program_id(1)
    n_eb = pl.num_programs(1)

    @pl.when(eb == 0)
    def _():
        o_ref[...] = jnp.zeros_like(o_ref)

    pre_rows = pre_ref.shape[0]
    o_ref[pl.ds(pl.multiple_of(eb * pre_rows, pre_rows), pre_rows), :] += pre_ref[...]

    max_count = cnt_ref[eb, i]

    def one_pass(p, slots):
        n_rows = MOE_EB * slots
        base = p * slots
        slot = lax.broadcasted_iota(I32, (slots, MOE_SUB), 0)
        slot_w = []
        for k in range(MOE_NSUB):
            tok = slice(k * MOE_SUB, (k + 1) * MOE_SUB)
            blocks, weights = [], []
            for j in range(MOE_EB):
                expert = pl.ds(eb * MOE_EB + j, 1)
                match = slot == (rank_ref[expert, tok] - base)
                blocks.append(jnp.where(match, 1.0, 0.0).astype(BF16))
                weights.append(jnp.sum(jnp.where(match, w_ref[expert, tok], 0.0), axis=1, keepdims=True))
            onehot = jnp.concatenate(blocks, axis=0)
            oh_ref[k, 0:n_rows, :] = onehot
            slot_w.append(weights)
            g_ref[k, 0:n_rows, :] = _dot(onehot, x_ref[tok, :]).astype(BF16)
        for j in range(MOE_EB):
            rows = slice(j * slots, (j + 1) * slots)
            xe = jnp.concatenate([g_ref[k, rows, :] for k in range(MOE_NSUB)], axis=0)
            h = _dot(xe, wgu_ref[j])
            act = (jax.nn.silu(h[:, :EXPERT_DIM]) * h[:, EXPERT_DIM:]).astype(BF16)
            wc = jnp.concatenate([slot_w[k][j] for k in range(MOE_NSUB)], axis=0)
            yw = (_dot(act, wd_ref[j]) * wc).astype(BF16)
            for k in range(MOE_NSUB):
                y_ref[k, rows, :] = yw[k * slots:(k + 1) * slots, :]
        for k in range(MOE_NSUB):
            tok = slice(k * MOE_SUB, (k + 1) * MOE_SUB)
            o_ref[tok, :] += _dot_tn(oh_ref[k, 0:n_rows, :], y_ref[k, 0:n_rows, :])

    @pl.when(max_count <= MOE_SLOTS_SMALL)
    def _():
        one_pass(0, MOE_SLOTS_SMALL)

    @pl.when(max_count > MOE_SLOTS_SMALL)
    def _():
        n_pass = lax.shift_right_logical(max_count + (MOE_SLOTS - 1), int(np.log2(MOE_SLOTS)))

        def body(p, carry):
            one_pass(p, MOE_SLOTS)
            return carry

        lax.fori_loop(0, n_pass, body, 0)

    @pl.when(eb == n_eb - 1)
    def _():
        for r in range(0, o_ref.shape[0], MOE_SUB):
            rows = slice(r, r + MOE_SUB)
            o_ref[rows, :] = _layer_norm(o_ref[rows, :], gain_ref[...], bias_ref[...])


def _moe(x1b, rank_t, w_t, counts, wgu_b, wd_b, pre, gain, bias):
    t = x1b.shape[0]
    tm = MOE_TM
    nt = t // tm
    n_eb = N_EXPERTS // MOE_EB
    pre_rows = tm // n_eb
    counts = counts.reshape(N_EXPERTS // MOE_EB, MOE_EB, nt, tm // MID_TM).max(axis=(1, 3))
    grid_spec = pltpu.PrefetchScalarGridSpec(
        num_scalar_prefetch=1,
        grid=(nt, N_EXPERTS // MOE_EB),
        in_specs=[
            pl.BlockSpec((tm, D_MODEL), lambda i, e, c: (i, 0)),
            pl.BlockSpec((N_EXPERTS, tm), lambda i, e, c: (0, i)),
            pl.BlockSpec((N_EXPERTS, tm), lambda i, e, c: (0, i)),
            pl.BlockSpec((MOE_EB, D_MODEL, 2 * EXPERT_DIM), lambda i, e, c: (e, 0, 0)),
            pl.BlockSpec((MOE_EB, EXPERT_DIM, D_MODEL), lambda i, e, c: (e, 0, 0)),
            pl.BlockSpec((pre_rows, D_MODEL), lambda i, e, c: (i * n_eb + e, 0)),
            pl.BlockSpec((1, D_MODEL), lambda i, e, c: (0, 0)),
            pl.BlockSpec((1, D_MODEL), lambda i, e, c: (0, 0)),
        ],
        out_specs=pl.BlockSpec((tm, D_MODEL), lambda i, e, c: (i, 0)),
        scratch_shapes=[pltpu.VMEM((MOE_NSUB, MOE_SUB, MOE_SUB), BF16),
                        pltpu.VMEM((MOE_NSUB, MOE_SUB, D_MODEL), BF16),
                        pltpu.VMEM((MOE_NSUB, MOE_SUB, D_MODEL), BF16)],
    )
    return pl.pallas_call(
        _moe_kernel,
        grid_spec=grid_spec,
        out_shape=jax.ShapeDtypeStruct((t, D_MODEL), F32),
        compiler_params=_cparams(("parallel", "arbitrary")),
        name="moe",
    )(counts, x1b, rank_t, w_t, wgu_b, wd_b, pre, gain.astype(F32)[None, :], bias.astype(F32)[None, :])


def _rotary_tables(seq):
    half = HEAD_DIM // 2
    inv = np.float32(ROPE_BASE) ** (-np.arange(half, dtype=np.float32) / np.float32(half))
    ang = np.arange(seq, dtype=np.float32)[:, None] * inv[None, :]
    cos, sin = np.cos(ang), np.sin(ang)
    reps = LANES // HEAD_DIM
    cos_t = np.tile(np.concatenate([cos, cos], axis=1), (1, reps))
    sin_t = np.tile(np.concatenate([-sin, sin], axis=1), (1, reps))
    return jnp.asarray(cos_t, F32), jnp.asarray(sin_t, F32)


def kernel(x, p, w_in, ret_decay_fwd, ret_decay_bwd, ret_gn_gain, na_rpb, w_out, ln1_gain, ln1_bias,
           w_router, router_bias, w_expert_gu, w_expert_down, w_shared_gu, w_shared_down,
           w_ple_proj, w_ple_gate, ln2_gain, ln2_bias):
    batch, seq, d = x.shape
    t = batch * seq
    depth = w_in.shape[0]
    assert depth == 1 and d == D_MODEL
    assert seq % PROJ_TM == 0 and seq % MOE_TM == 0 and seq % (RET_CHUNK * RET_UNROLL) == 0
    assert MOE_TM % (N_EXPERTS // MOE_EB) == 0
    assert seq % MID_TM == 0 and MOE_TM % MID_TM == 0 and MID_TM % MOE_SUB == 0
    assert (seq // GRID_W) % (NA_ROWS * NA_STEP_BLOCKS) == 0 and seq // GRID_W >= NA_KEY_ROWS
    cos_t, sin_t = _rotary_tables(seq)
    x2 = x.reshape(t, d)
    for i in range(depth):
        rq, rk, rv, rg, nq, nke, nko, nve, nvo = _in_proj(x2, w_in[i], cos_t, sin_t, seq)
        ret, wd_b = _retention(rq, rk, rv, rg, ret_decay_fwd[i], ret_decay_bwd[i], ret_gn_gain[i],
                               w_expert_down[i], batch, seq)
        na, wgu_b = _natten(nq, nke, nko, nve, nvo, na_rpb[i], w_expert_gu[i], batch, seq)
        pre, x1b, w_t, rank_t, cnt = _mid(ret, na, x2, p[i].reshape(t, -1), w_out[i], ln1_gain[i], ln1_bias[i],
                                          w_router[i], router_bias[i], w_shared_gu[i], w_shared_down[i],
                                          w_ple_proj[i], w_ple_gate[i])
        x2 = _moe(x1b, rank_t, w_t, cnt[:, ::LANES], wgu_b, wd_b, pre, ln2_gain[i], ln2_bias[i])
    return x2.reshape(batch, seq, d)
```

```python
import functools

import numpy as np
import jax
import jax.numpy as jnp
from jax import lax
from jax.experimental import pallas as pl
from jax.experimental.pallas import tpu as pltpu

F32 = jnp.float32
BF16 = jnp.bfloat16
I32 = jnp.int32

D_MODEL = 1024
HEADS = 8
HEAD_DIM = 64
GROUP_W = HEADS * HEAD_DIM
ROPE_BASE = 10000.0
GN_EPS = 1e-6
LN_EPS = 1e-5
GRID_W = 64
NA_WIN_ROWS = 8
NA_WIN_COLS = 16
N_EXPERTS = 64
N_GROUPS = 8
GROUP_SIZE = N_EXPERTS // N_GROUPS
TOPK_GROUPS = 4
TOP_K = 8
EXPERT_DIM = 256
ROUTED_SCALE = 2.5
ALPHA = 2.0 ** 0.25
NEG_BIG = -1e30
LOG2E = 1.4426950408889634
NA_Q_SCALE = HEAD_DIM ** -0.5 * LOG2E

LANES = 128
VMEM_LIMIT_BYTES = 56 * 1024 * 1024

PROJ_TM = 1024
MID_TM = 1024
RET_CHUNK = 128
RET_UNROLL = 32
NA_ROWS = 4
NA_KEY_ROWS = NA_ROWS + NA_WIN_ROWS
NA_STEP_BLOCKS = 8
WEIGHT_RING = 3
MOE_TM = 2048
MOE_SUB = 256
MOE_EB = 4
MOE_SLOTS = MOE_SUB // MOE_EB
MOE_SLOTS_SMALL = 48
MOE_NSUB = MOE_TM // MOE_SUB


def _cparams(sem):
    return pltpu.CompilerParams(dimension_semantics=sem, vmem_limit_bytes=VMEM_LIMIT_BYTES)


def _dot(a, b):
    return jnp.dot(a, b, preferred_element_type=F32)


def _dot_nt(a, b):
    return lax.dot_general(a, b, (((1,), (1,)), ((), ())), preferred_element_type=F32)


def _dot_tn(a, b):
    return lax.dot_general(a, b, (((0,), (0,)), ((), ())), preferred_element_type=F32)


def _cast_once(src_ref, dst_ref):
    @pl.when(pl.program_id(0) == 0)
    def _():
        dst_ref[...] = src_ref[...].astype(BF16)


def _resident(shape):
    return pl.BlockSpec(shape, lambda i: (0,) * len(shape), pipeline_mode=pl.Buffered(1))


def _in_proj_kernel(x_ref, wf_ref, cos_ref, sin_ref,
                    rq_ref, rk_ref, rv_ref, rg_ref, nq_ref, nke_ref, nko_ref, nve_ref, nvo_ref, w_ref):
    _cast_once(wf_ref, w_ref)
    xb = x_ref[...].astype(BF16)
    cos = cos_ref[...]
    sin = sin_ref[...]
    lane = lax.broadcasted_iota(I32, (1, LANES), 1)
    first_half = (lane % HEAD_DIM) < (HEAD_DIM // 2)

    def proj(g):
        return _dot(xb, w_ref[:, g * GROUP_W:(g + 1) * GROUP_W])

    def rotary(t, scale):
        outs = []
        for j in range(GROUP_W // LANES):
            c = t[:, j * LANES:(j + 1) * LANES]
            swapped = jnp.where(first_half,
                                pltpu.roll(c, LANES - HEAD_DIM // 2, axis=1),
                                pltpu.roll(c, HEAD_DIM // 2, axis=1))
            outs.append((c * cos + swapped * sin) * scale)
        return jnp.concatenate(outs, axis=1)

    rq_ref[...] = rotary(proj(0), 1.0).astype(BF16)
    rk_ref[...] = rotary(proj(1), HEAD_DIM ** -0.5).astype(BF16)
    rv_ref[...] = proj(2).astype(BF16)
    rg_ref[...] = jax.nn.silu(proj(3)).astype(BF16)
    nq_ref[...] = (proj(4) * NA_Q_SCALE).astype(BF16)
    even_head = (lax.broadcasted_iota(I32, (1, GROUP_W), 1) % (2 * HEAD_DIM)) < HEAD_DIM
    for g, even_ref, odd_ref in ((5, nke_ref, nko_ref), (6, nve_ref, nvo_ref)):
        t = proj(g)
        even_ref[...] = jnp.where(even_head, t, 0.0).astype(BF16)
        odd_ref[...] = jnp.where(even_head, 0.0, t).astype(BF16)


def _in_proj(x2, w_in, cos_t, sin_t, seq):
    t = x2.shape[0]
    tm = PROJ_TM
    n_pos = seq // tm
    out = jax.ShapeDtypeStruct((t, GROUP_W), BF16)
    tok = lambda i: (i, 0)
    return pl.pallas_call(
        _in_proj_kernel,
        grid=(t // tm,),
        in_specs=[
            pl.BlockSpec((tm, D_MODEL), tok),
            _resident(w_in.shape),
            pl.BlockSpec((tm, LANES), lambda i: (i % n_pos, 0)),
            pl.BlockSpec((tm, LANES), lambda i: (i % n_pos, 0)),
        ],
        out_specs=[pl.BlockSpec((tm, GROUP_W), tok)] * 9,
        out_shape=[out] * 9,
        scratch_shapes=[pltpu.VMEM(w_in.shape, BF16)],
        compiler_params=_cparams(("arbitrary",)),
        name="in_proj",
    )(x2, w_in.astype(F32), cos_t, sin_t)


def _log_sigmoid(x):
    return jnp.minimum(x, 0.0) - jnp.log1p(jnp.exp(-jnp.abs(x)))


def _retention_kernel(q_ref, k_ref, v_ref, g_ref, decf_ref, decb_ref,
                      gain_ref, wd_f, o_ref, wd_b, kv_ref, st_ref, dmat_ref):
    wd_b[...] = wd_f[...].astype(BF16)
    c = RET_CHUNK
    n_chunks = q_ref.shape[0] // c
    pair_w = 2 * HEAD_DIM

    h0 = 2 * pl.program_id(1)
    head0 = lax.broadcasted_iota(I32, (1, pair_w), 1) < HEAD_DIM
    lgf = _log_sigmoid(jnp.where(head0, decf_ref[h0], decf_ref[h0 + 1]))
    lgb = _log_sigmoid(jnp.where(head0, decb_ref[h0], decb_ref[h0 + 1]))
    row = lax.broadcasted_iota(I32, (c, 1), 0).astype(F32)
    k_dec_f = jnp.exp((c - 1.0 - row) * lgf)
    k_dec_b = jnp.exp(row * lgb)
    q_dec_f = jnp.exp((row + 1.0) * lgf)
    q_dec_b = jnp.exp((c - row) * lgb)
    chunk_dec_f = jnp.exp(c * lgf)
    chunk_dec_b = jnp.exp(c * lgb)

    r2 = lax.broadcasted_iota(I32, (pair_w, pair_w), 0) // HEAD_DIM
    c2 = lax.broadcasted_iota(I32, (pair_w, pair_w), 1) // HEAD_DIM
    same_head = r2 == c2
    block_diag = jnp.where(same_head, 1.0, 0.0)
    seg_avg = jnp.where(same_head, 1.0 / HEAD_DIM, 0.0).astype(BF16)

    head0_col = lax.broadcasted_iota(I32, (1, 2 * c), 1) < c
    lgf_d = _log_sigmoid(jnp.where(head0_col, decf_ref[h0], decf_ref[h0 + 1]))
    lgb_d = _log_sigmoid(jnp.where(head0_col, decb_ref[h0], decb_ref[h0 + 1]))
    di = lax.broadcasted_iota(I32, (c, 2 * c), 0)
    dj = lax.broadcasted_iota(I32, (c, 2 * c), 1) % c
    diff = (di - dj).astype(F32)
    dmat_ref[...] = jnp.where(diff >= 0.0, jnp.exp(diff * lgf_d), jnp.exp(-diff * lgb_d))

    def chunk(ref, n):
        return ref[pl.ds(pl.multiple_of(n * c, c), c), :]

    unroll = RET_UNROLL

    def summarize(nb, carry):
        for u in range(unroll):
            n = nb * unroll + u
            kf = chunk(k_ref, n).astype(F32)
            kst = jnp.concatenate([kf * k_dec_f, kf * k_dec_b], axis=1).astype(BF16)
            kv_ref[n] = _dot_tn(kst, chunk(v_ref, n))
        return carry

    lax.fori_loop(0, n_chunks // unroll, summarize, 0)

    def scan(i, states):
        fwd, bwd = states
        nb = n_chunks - 1 - i
        st_ref[i, 0:pair_w, :] = fwd.astype(BF16)
        st_ref[nb, pair_w:2 * pair_w, :] = bwd.astype(BF16)
        return (fwd * chunk_dec_f + kv_ref[i, 0:pair_w, :] * block_diag,
                bwd * chunk_dec_b + kv_ref[nb, pair_w:2 * pair_w, :] * block_diag)

    zero_state = jnp.zeros((pair_w, pair_w), F32)
    lax.fori_loop(0, n_chunks, scan, (zero_state, zero_state), unroll=True)

    gain = gain_ref[...]

    seg_avg2 = jnp.concatenate([seg_avg, seg_avg], axis=0)

    def seg_mean(z):
        hi = z.astype(BF16)
        lo = (z - hi.astype(F32)).astype(BF16)
        return _dot(jnp.concatenate([hi, lo], axis=1), seg_avg2)

    def emit(nb, carry):
        ys = []
        for u in range(unroll):
            n = nb * unroll + u
            q = chunk(q_ref, n)
            k = chunk(k_ref, n)
            v = chunk(v_ref, n)
            zero = jnp.zeros_like(k)
            k_st = jnp.concatenate([jnp.where(head0, k, zero), jnp.where(head0, zero, k)], axis=0)
            v_st = jnp.concatenate([jnp.where(head0, v, zero), jnp.where(head0, zero, v)], axis=0)
            scores = _dot_nt(q, k_st) * dmat_ref[...]
            qf = q.astype(F32)
            q_st = jnp.concatenate([qf * q_dec_f, qf * q_dec_b], axis=1).astype(BF16)
            ys.append(_dot(scores.astype(BF16), v_st) + _dot(q_st, st_ref[n]))
        y = jnp.concatenate(ys, axis=0)
        mu = seg_mean(y)
        d = y - mu
        var = seg_mean(d * d)
        yn = d * lax.rsqrt(var + GN_EPS) * gain
        rows = pl.ds(pl.multiple_of(nb * (unroll * c), unroll * c), unroll * c)
        o_ref[rows, :] = (g_ref[rows, :].astype(F32) * yn).astype(BF16)
        return carry

    lax.fori_loop(0, n_chunks // unroll, emit, 0)


def _retention(rq, rk, rv, rg, dec_f, dec_b, gain, w_down, batch, seq):
    t = rq.shape[0]
    c = RET_CHUNK
    n_pairs = HEADS // 2
    pair_w = 2 * HEAD_DIM
    n_exp = w_down.shape[0]
    assert n_exp % (batch * n_pairs) == 0
    epb = n_exp // (batch * n_pairs)
    wd_spec = pl.BlockSpec((epb,) + w_down.shape[1:], lambda b, p: (b * n_pairs + p, 0, 0))
    tok = pl.BlockSpec((seq, pair_w), lambda b, p: (b, p))
    lane_spec = pl.BlockSpec((1, pair_w), lambda b, p: (0, p))
    scalars = pl.BlockSpec(memory_space=pltpu.SMEM)
    return pl.pallas_call(
        _retention_kernel,
        grid=(batch, n_pairs),
        in_specs=[tok, tok, tok, tok, scalars, scalars, lane_spec, wd_spec],
        out_specs=[tok, wd_spec],
        out_shape=[jax.ShapeDtypeStruct((t, GROUP_W), BF16), jax.ShapeDtypeStruct(w_down.shape, BF16)],
        scratch_shapes=[
            pltpu.VMEM((seq // c, 2 * pair_w, pair_w), F32),
            pltpu.VMEM((seq // c, 2 * pair_w, pair_w), BF16),
            pltpu.VMEM((c, 2 * c), F32),
        ],
        compiler_params=_cparams(("parallel", "parallel")),
        name="retention",
    )(rq, rk, rv, rg, dec_f.astype(F32), dec_b.astype(F32), gain.astype(F32)[None, :], w_down.astype(F32))


N_ROW_OFFSETS = 2 * NA_WIN_ROWS - 1
N_COL_OFFSETS = 2 * NA_WIN_COLS - 1


def _natten_row_offsets(rows):
    n_blocks = rows // NA_ROWS
    starts = {0: 0, 1: NA_ROWS - NA_WIN_ROWS // 2, 2: rows - NA_KEY_ROWS}
    blocks = {0: 0, 1: 1, 2: n_blocks - 1}
    table = []
    for v in range(3):
        per_a = []
        for a in range(NA_ROWS):
            r = blocks[v] * NA_ROWS + a
            rs = min(max(r - NA_WIN_ROWS // 2, 0), rows - NA_WIN_ROWS)
            per_kl = []
            for kl in range(NA_KEY_ROWS):
                kr = starts[v] + kl
                per_kl.append(kr - r + NA_WIN_ROWS - 1 if rs <= kr < rs + NA_WIN_ROWS else None)
            per_a.append(per_kl)
        table.append(per_a)
    return table


def _natten_live_columns(rows):
    offsets = _natten_row_offsets(rows)
    live = []
    for blk in range(NA_STEP_BLOCKS):
        variants = [1] + ([0] if blk == 0 else []) + ([2] if blk == NA_STEP_BLOCKS - 1 else [])
        live.append([[j for j in range(NA_KEY_ROWS // 2)
                      if any(offsets[v][a][kl] is not None for v in variants for kl in (2 * j, 2 * j + 1))]
                     for a in range(NA_ROWS)])
    return live


def _natten_build_bias(rpb_ref, tab_ref, pair, rows):
    offsets = _natten_row_offsets(rows)
    nk = NA_KEY_ROWS * GRID_W
    shape = (GRID_W, LANES)
    lane = lax.broadcasted_iota(I32, shape, 1)
    c = lax.broadcasted_iota(I32, shape, 0)
    second = lane >= GRID_W
    kc = lane % GRID_W
    cs = jnp.clip(c - NA_WIN_COLS // 2, 0, GRID_W - NA_WIN_COLS)
    col_ok = jnp.logical_and(kc >= cs, kc < cs + NA_WIN_COLS)
    neg = jnp.full(shape, NEG_BIG, F32)
    for hh in range(2):
        toeplitz = []
        for dr in range(N_ROW_OFFSETS):
            x = jnp.broadcast_to(rpb_ref[2 * pair + hh, dr:dr + 1, :] * LOG2E, shape)
            lo = pltpu.roll(x, LANES - (NA_WIN_COLS - 1), axis=1, stride=1, stride_axis=0)
            hi = pltpu.roll(x, GRID_W - (NA_WIN_COLS - 1), axis=1, stride=1, stride_axis=0)
            toeplitz.append(jnp.where(second, hi, lo))
        for v in range(3):
            for a in range(NA_ROWS):
                for j in range(NA_KEY_ROWS // 2):
                    d0, d1 = offsets[v][a][2 * j], offsets[v][a][2 * j + 1]
                    if d0 is None and d1 is None:
                        piece = neg
                    else:
                        t0 = neg if d0 is None else toeplitz[d0]
                        t1 = neg if d1 is None else toeplitz[d1]
                        piece = jnp.where(col_ok, jnp.where(second, t1, t0), neg)
                    tab_ref[v, pair, a * GRID_W:(a + 1) * GRID_W,
                            hh * nk + j * LANES:hh * nk + (j + 1) * LANES] = piece


def _natten_kernel(q_ref, ke_ref, ko_ref, ve_ref, vo_ref, rpb_ref, wgu_hbm, o_ref, wgu_b, tab_ref,
                   ring_ref, sem_ref, *, rows):
    pair_id = pl.program_id(1)
    step = (pl.program_id(0) * pl.num_programs(1) + pair_id) * pl.num_programs(2) + pl.program_id(2)
    n_steps = pl.num_programs(0) * pl.num_programs(1) * pl.num_programs(2)
    epb = wgu_b.shape[0]

    def weight_copy(block, slot):
        return pltpu.make_async_copy(wgu_hbm.at[pl.ds(block * epb, epb)], ring_ref.at[slot], sem_ref.at[slot])

    @pl.when(step == 0)
    def _():
        for pair in range(HEADS // 2):
            _natten_build_bias(rpb_ref, tab_ref, pair, rows)
        for j in range(WEIGHT_RING):
            weight_copy(j, j).start()

    slot = step % WEIGHT_RING
    weight_copy(step, slot).wait()
    wgu_b[...] = ring_ref[slot].astype(BF16)

    @pl.when(step + WEIGHT_RING < n_steps)
    def _():
        weight_copy(step + WEIGHT_RING, slot).start()

    nq = NA_ROWS * GRID_W
    nk = NA_KEY_ROWS * GRID_W
    n_blocks = rows // NA_ROWS
    live = _natten_live_columns(rows)
    head0 = lax.broadcasted_iota(I32, (1, 2 * HEAD_DIM), 1) < HEAD_DIM
    ind0 = jnp.broadcast_to(jnp.where(head0, 1.0, 0.0).astype(BF16), (nk, 2 * HEAD_DIM))
    ind1 = jnp.broadcast_to(jnp.where(head0, 0.0, 1.0).astype(BF16), (nk, 2 * HEAD_DIM))

    for blk in range(NA_STEP_BLOCKS):
        rb = pl.program_id(2) * NA_STEP_BLOCKS + blk
        variant = jnp.where(rb == 0, 0, jnp.where(rb == n_blocks - 1, 2, 1))
        start_row = jnp.clip(rb * NA_ROWS - NA_WIN_ROWS // 2, 0, rows - NA_KEY_ROWS)
        win = pl.ds(pl.multiple_of(start_row * GRID_W, GRID_W), nk)
        q = q_ref[blk * nq:(blk + 1) * nq, :]
        k_st = jnp.concatenate([ke_ref[win, :], ko_ref[win, :]], axis=0)
        v_st = jnp.concatenate([jnp.concatenate([ve_ref[win, :], ind0], axis=1),
                                jnp.concatenate([vo_ref[win, :], ind1], axis=1)], axis=0)
        s = _dot_nt(q, k_st) + tab_ref[variant, pair_id]
        dead = jnp.zeros((GRID_W, LANES), BF16)
        slabs = []
        for a in range(NA_ROWS):
            qrows = slice(a * GRID_W, (a + 1) * GRID_W)
            parts = []
            for h in range(2):
                cols = {j: s[qrows, h * nk + j * LANES:h * nk + (j + 1) * LANES] for j in live[blk][a]}
                m = jnp.max(jnp.concatenate(list(cols.values()), axis=1), axis=1, keepdims=True)
                parts += [jnp.exp2(cols[j] - m).astype(BF16) if j in cols else dead
                          for j in range(NA_KEY_ROWS // 2)]
            slabs.append(jnp.concatenate(parts, axis=1))
        out = _dot(jnp.concatenate(slabs, axis=0), v_st)
        o_ref[blk * nq:(blk + 1) * nq, :] = (out[:, :2 * HEAD_DIM] / out[:, 2 * HEAD_DIM:]).astype(BF16)


def _natten(nq, nke, nko, nve, nvo, rpb, w_gu, batch, seq):
    t = nq.shape[0]
    rows = seq // GRID_W
    n_pairs = HEADS // 2
    n_steps = rows // (NA_ROWS * NA_STEP_BLOCKS)
    nqb = NA_ROWS * GRID_W
    nkb = NA_KEY_ROWS * GRID_W
    rpb_pad = jnp.pad(rpb.astype(F32), ((0, 0), (0, 0), (0, LANES - N_COL_OFFSETS)))
    total_steps = batch * n_pairs * n_steps
    n_exp = w_gu.shape[0]
    assert n_exp % total_steps == 0
    epb = n_exp // total_steps

    assert total_steps >= WEIGHT_RING

    def expert_spec(shape):
        return pl.BlockSpec((epb,) + shape[1:], lambda b, p, s: ((b * n_pairs + p) * n_steps + s, 0, 0))

    rpb_spec = pl.BlockSpec(rpb_pad.shape, lambda b, p, s: (0, 0, 0))
    kv_spec = pl.BlockSpec((seq, 2 * HEAD_DIM), lambda b, p, s: (b, p))
    q_spec = pl.BlockSpec((NA_STEP_BLOCKS * nqb, 2 * HEAD_DIM), lambda b, p, s: (b * n_steps + s, p))
    return pl.pallas_call(
        functools.partial(_natten_kernel, rows=rows),
        grid=(batch, n_pairs, n_steps),
        in_specs=[q_spec, kv_spec, kv_spec, kv_spec, kv_spec, rpb_spec, pl.BlockSpec(memory_space=pl.ANY)],
        out_specs=[q_spec, expert_spec(w_gu.shape)],
        out_shape=[jax.ShapeDtypeStruct((t, GROUP_W), BF16), jax.ShapeDtypeStruct(w_gu.shape, BF16)],
        scratch_shapes=[pltpu.VMEM((3, n_pairs, nqb, 2 * nkb), F32),
                        pltpu.VMEM((WEIGHT_RING, epb) + w_gu.shape[1:], F32),
                        pltpu.SemaphoreType.DMA((WEIGHT_RING,))],
        compiler_params=_cparams(("arbitrary", "arbitrary", "arbitrary")),
        name="natten",
    )(nq, nke, nko, nve, nvo, rpb_pad, w_gu.astype(F32))


def _layer_norm(h, gain, bias):
    mu = jnp.mean(h, axis=-1, keepdims=True)
    d = h - mu
    var = jnp.mean(d * d, axis=-1, keepdims=True)
    return d * lax.rsqrt(var + LN_EPS) * gain + bias


def _route_tile(xb, wr, rbias, tri):
    tm = xb.shape[0]
    scores = jax.nn.sigmoid(_dot_nt(wr, xb))
    biased = scores + rbias
    sub = lax.broadcasted_iota(I32, (GROUP_SIZE, tm), 0).astype(F32)
    none = float(N_EXPERTS)
    ninf = -jnp.inf

    def first_max(vals, index):
        m = jnp.max(vals, axis=0, keepdims=True)
        return m, jnp.min(jnp.where(vals == m, index, none), axis=0, keepdims=True)

    groups = [biased[g * GROUP_SIZE:(g + 1) * GROUP_SIZE, :] for g in range(N_GROUPS)]
    group_scores = []
    for g in range(N_GROUPS):
        m1, i1 = first_max(groups[g], sub)
        m2 = jnp.max(jnp.where(sub == i1, ninf, groups[g]), axis=0, keepdims=True)
        group_scores.append(m1 + m2)
    cur = jnp.concatenate(group_scores, axis=0)
    group_sel = jnp.zeros(cur.shape, F32)
    for _ in range(TOPK_GROUPS):
        _, i1 = first_max(cur, sub)
        hit = sub == i1
        group_sel = jnp.where(hit, 1.0, group_sel)
        cur = jnp.where(hit, ninf, cur)

    masked = [jnp.where(group_sel[g:g + 1, :] > 0.5, groups[g], ninf) for g in range(N_GROUPS)]
    ids = [sub + float(g * GROUP_SIZE) for g in range(N_GROUPS)]
    chosen = [jnp.zeros((GROUP_SIZE, tm), F32) for _ in range(N_GROUPS)]
    for _ in range(TOP_K):
        m = masked[0]
        for g in range(1, N_GROUPS):
            m = jnp.maximum(m, masked[g])
        m = jnp.max(m, axis=0, keepdims=True)
        cand = jnp.where(masked[0] == m, ids[0], none)
        for g in range(1, N_GROUPS):
            cand = jnp.minimum(cand, jnp.where(masked[g] == m, ids[g], none))
        first = jnp.min(cand, axis=0, keepdims=True)
        for g in range(N_GROUPS):
            hit = ids[g] == first
            chosen[g] = jnp.where(hit, 1.0, chosen[g])
            masked[g] = jnp.where(hit, ninf, masked[g])

    sel = jnp.concatenate(chosen, axis=0) > 0.5
    picked = jnp.where(sel, scores, 0.0)
    total = jnp.sum(picked, axis=0, keepdims=True)
    weight = picked / total * ROUTED_SCALE
    sel_f = jnp.where(sel, 1.0, 0.0)
    sel_b = sel_f.astype(BF16)
    before, cnt_max = [], None
    for k in range(tm // MOE_SUB):
        cols = slice(k * MOE_SUB, (k + 1) * MOE_SUB)
        before.append(_dot(sel_b[:, cols], tri))
        cnt = jnp.sum(sel_f[:, cols], axis=1, keepdims=True)
        cnt_max = cnt if cnt_max is None else jnp.maximum(cnt_max, cnt)
    rank = jnp.where(sel, jnp.concatenate(before, axis=1).astype(I32), -1)
    return weight, rank, cnt_max


def _mid_kernel(ret_ref, na_ref, x_ref, p_ref, wo_f, wsgu_f, wsd_f, wp_f, wg_f, gain_ref, bias_ref,
                wr_ref, rb_ref, pre_ref, x1b_ref, w_ref, rank_ref, cnt_ref,
                wo_ref, wsgu_ref, wsd_ref, wp_ref, wg_ref, tri_ref):
    for src_ref, dst_ref in ((wo_f, wo_ref), (wsgu_f, wsgu_ref), (wsd_f, wsd_ref), (wp_f, wp_ref),
                             (wg_f, wg_ref)):
        _cast_once(src_ref, dst_ref)

    @pl.when(pl.program_id(0) == 0)
    def _():
        i = lax.broadcasted_iota(I32, (MOE_SUB, MOE_SUB), 0)
        j = lax.broadcasted_iota(I32, (MOE_SUB, MOE_SUB), 1)
        tri_ref[...] = jnp.where(i < j, 1.0, 0.0).astype(BF16)

    sub_blocks = [slice(r, r + MOE_SUB) for r in range(0, x_ref.shape[0], MOE_SUB)]
    for rows in sub_blocks:
        mix = (_dot(ret_ref[rows, :], wo_ref[0:GROUP_W, :])
               + _dot(na_ref[rows, :], wo_ref[GROUP_W:2 * GROUP_W, :]))
        x1 = _layer_norm(ALPHA * x_ref[rows, :] + mix, gain_ref[...], bias_ref[...])
        x1b_ref[rows, :] = x1.astype(BF16)
        pre_ref[rows, :] = ALPHA * x1
    weight, rank, cnt_max = _route_tile(x1b_ref[...], wr_ref[...], rb_ref[...], tri_ref[...])
    w_ref[...] = weight
    rank_ref[...] = rank
    cnt_ref[...] = jnp.broadcast_to(cnt_max, cnt_ref.shape).astype(I32)
    for rows in sub_blocks:
        xb = x1b_ref[rows, :]
        h = _dot(xb, wsgu_ref[...])
        act = (jax.nn.silu(h[:, :EXPERT_DIM]) * h[:, EXPERT_DIM:]).astype(BF16)
        shared = _dot(act, wsd_ref[...])
        ple = _dot(p_ref[rows, :].astype(BF16), wp_ref[...]) * jax.nn.sigmoid(_dot(xb, wg_ref[...]))
        pre_ref[rows, :] += shared + ple


def _mid(ret, na, x2, p2, w_out, gain, bias, w_router, router_bias, wsgu, wsd, wp, wg):
    t = x2.shape[0]
    tm = MID_TM
    nt = t // tm
    tok = lambda i: (i, 0)
    const = lambda i: (0, 0)
    col = lambda i: (0, i)
    weights = [w.astype(F32) for w in (w_out, wsgu, wsd, wp, wg)]
    wr_t = w_router.astype(F32).T.astype(BF16)
    return pl.pallas_call(
        _mid_kernel,
        grid=(nt,),
        in_specs=[pl.BlockSpec((tm, GROUP_W), tok), pl.BlockSpec((tm, GROUP_W), tok),
                  pl.BlockSpec((tm, D_MODEL), tok), pl.BlockSpec((tm, p2.shape[1]), tok)]
                 + [_resident(w.shape) for w in weights]
                 + [pl.BlockSpec((1, D_MODEL), const), pl.BlockSpec((1, D_MODEL), const),
                    pl.BlockSpec((N_EXPERTS, D_MODEL), const), pl.BlockSpec((N_EXPERTS, 1), const)],
        out_specs=[pl.BlockSpec((tm, D_MODEL), tok), pl.BlockSpec((tm, D_MODEL), tok),
                   pl.BlockSpec((N_EXPERTS, tm), col), pl.BlockSpec((N_EXPERTS, tm), col),
                   pl.BlockSpec((N_EXPERTS, LANES), col)],
        out_shape=[jax.ShapeDtypeStruct((t, D_MODEL), F32), jax.ShapeDtypeStruct((t, D_MODEL), BF16),
                   jax.ShapeDtypeStruct((N_EXPERTS, t), F32), jax.ShapeDtypeStruct((N_EXPERTS, t), I32),
                   jax.ShapeDtypeStruct((N_EXPERTS, nt * LANES), I32)],
        scratch_shapes=[pltpu.VMEM(w.shape, BF16) for w in weights] + [pltpu.VMEM((MOE_SUB, MOE_SUB), BF16)],
        compiler_params=_cparams(("arbitrary",)),
        name="mid",
    )(ret, na, x2, p2, *weights, gain.astype(F32)[None, :], bias.astype(F32)[None, :],
      wr_t, router_bias.astype(F32)[:, None])


def _moe_kernel(cnt_ref, x_ref, rank_ref, w_ref, wgu_ref, wd_ref, pre_ref, gain_ref, bias_ref,
                o_ref, oh_ref, g_ref, y_ref):
    i = pl.program_id(0)
    eb = pl.program_id(1)
    n_eb = pl.num_programs(1)

    @pl.when(eb == 0)
    def _():
        o_ref[...] = jnp.zeros_like(o_ref)

    pre_rows = pre_ref.shape[0]
    o_ref[pl.ds(pl.multiple_of(eb * pre_rows, pre_rows), pre_rows), :] += pre_ref[...]

    max_count = cnt_ref[eb, i]

    def one_pass(p, slots):
        n_rows = MOE_EB * slots
        base = p * slots
        slot = lax.broadcasted_iota(I32, (slots, MOE_SUB), 0)
        slot_w = []
        for k in range(MOE_NSUB):
            tok = slice(k * MOE_SUB, (k + 1) * MOE_SUB)
            blocks, weights = [], []
            for j in range(MOE_EB):
                expert = pl.ds(eb * MOE_EB + j, 1)
                match = slot == (rank_ref[expert, tok] - base)
                blocks.append(jnp.where(match, 1.0, 0.0).astype(BF16))
                weights.append(jnp.sum(jnp.where(match, w_ref[expert, tok], 0.0), axis=1, keepdims=True))
            onehot = jnp.concatenate(blocks, axis=0)
            oh_ref[k, 0:n_rows, :] = onehot
            slot_w.append(weights)
            g_ref[k, 0:n_rows, :] = _dot(onehot, x_ref[tok, :]).astype(BF16)
        for j in range(MOE_EB):
            rows = slice(j * slots, (j + 1) * slots)
            xe = jnp.concatenate([g_ref[k, rows, :] for k in range(MOE_NSUB)], axis=0)
            h = _dot(xe, wgu_ref[j])
            act = (jax.nn.silu(h[:, :EXPERT_DIM]) * h[:, EXPERT_DIM:]).astype(BF16)
            wc = jnp.concatenate([slot_w[k][j] for k in range(MOE_NSUB)], axis=0)
            yw = (_dot(act, wd_ref[j]) * wc).astype(BF16)
            for k in range(MOE_NSUB):
                y_ref[k, rows, :] = yw[k * slots:(k + 1) * slots, :]
        for k in range(MOE_NSUB):
            tok = slice(k * MOE_SUB, (k + 1) * MOE_SUB)
            o_ref[tok, :] += _dot_tn(oh_ref[k, 0:n_rows, :], y_ref[k, 0:n_rows, :])

    @pl.when(max_count <= MOE_SLOTS_SMALL)
    def _():
        one_pass(0, MOE_SLOTS_SMALL)

    @pl.when(max_count > MOE_SLOTS_SMALL)
    def _():
        n_pass = lax.shift_right_logical(max_count + (MOE_SLOTS - 1), int(np.log2(MOE_SLOTS)))

        def body(p, carry):
            one_pass(p, MOE_SLOTS)
            return carry

        lax.fori_loop(0, n_pass, body, 0)

    @pl.when(eb == n_eb - 1)
    def _():
        for r in range(0, o_ref.shape[0], MOE_SUB):
            rows = slice(r, r + MOE_SUB)
            o_ref[rows, :] = _layer_norm(o_ref[rows, :], gain_ref[...], bias_ref[...])


def _moe(x1b, rank_t, w_t, counts, wgu_b, wd_b, pre, gain, bias):
    t = x1b.shape[0]
    tm = MOE_TM
    nt = t // tm
    n_eb = N_EXPERTS // MOE_EB
    pre_rows = tm // n_eb
    counts = counts.reshape(N_EXPERTS // MOE_EB, MOE_EB, nt, tm // MID_TM).max(axis=(1, 3))
    grid_spec = pltpu.PrefetchScalarGridSpec(
        num_scalar_prefetch=1,
        grid=(nt, N_EXPERTS // MOE_EB),
        in_specs=[
            pl.BlockSpec((tm, D_MODEL), lambda i, e, c: (i, 0)),
            pl.BlockSpec((N_EXPERTS, tm), lambda i, e, c: (0, i)),
            pl.BlockSpec((N_EXPERTS, tm), lambda i, e, c: (0, i)),
            pl.BlockSpec((MOE_EB, D_MODEL, 2 * EXPERT_DIM), lambda i, e, c: (e, 0, 0)),
            pl.BlockSpec((MOE_EB, EXPERT_DIM, D_MODEL), lambda i, e, c: (e, 0, 0)),
            pl.BlockSpec((pre_rows, D_MODEL), lambda i, e, c: (i * n_eb + e, 0)),
            pl.BlockSpec((1, D_MODEL), lambda i, e, c: (0, 0)),
            pl.BlockSpec((1, D_MODEL), lambda i, e, c: (0, 0)),
        ],
        out_specs=pl.BlockSpec((tm, D_MODEL), lambda i, e, c: (i, 0)),
        scratch_shapes=[pltpu.VMEM((MOE_NSUB, MOE_SUB, MOE_SUB), BF16),
                        pltpu.VMEM((MOE_NSUB, MOE_SUB, D_MODEL), BF16),
                        pltpu.VMEM((MOE_NSUB, MOE_SUB, D_MODEL), BF16)],
    )
    return pl.pallas_call(
        _moe_kernel,
        grid_spec=grid_spec,
        out_shape=jax.ShapeDtypeStruct((t, D_MODEL), F32),
        compiler_params=_cparams(("parallel", "arbitrary")),
        name="moe",
    )(counts, x1b, rank_t, w_t, wgu_b, wd_b, pre, gain.astype(F32)[None, :], bias.astype(F32)[None, :])


def _rotary_tables(seq):
    half = HEAD_DIM // 2
    inv = np.float32(ROPE_BASE) ** (-np.arange(half, dtype=np.float32) / np.float32(half))
    ang = np.arange(seq, dtype=np.float32)[:, None] * inv[None, :]
    cos, sin = np.cos(ang), np.sin(ang)
    reps = LANES // HEAD_DIM
    cos_t = np.tile(np.concatenate([cos, cos], axis=1), (1, reps))
    sin_t = np.tile(np.concatenate([-sin, sin], axis=1), (1, reps))
    return jnp.asarray(cos_t, F32), jnp.asarray(sin_t, F32)


def kernel(x, p, w_in, ret_decay_fwd, ret_decay_bwd, ret_gn_gain, na_rpb, w_out, ln1_gain, ln1_bias,
           w_router, router_bias, w_expert_gu, w_expert_down, w_shared_gu, w_shared_down,
           w_ple_proj, w_ple_gate, ln2_gain, ln2_bias):
    batch, seq, d = x.shape
    t = batch * seq
    depth = w_in.shape[0]
    assert depth == 1 and d == D_MODEL
    assert seq % PROJ_TM == 0 and seq % MOE_TM == 0 and seq % (RET_CHUNK * RET_UNROLL) == 0
    assert MOE_TM % (N_EXPERTS // MOE_EB) == 0
    assert seq % MID_TM == 0 and MOE_TM % MID_TM == 0 and MID_TM % MOE_SUB == 0
    assert (seq // GRID_W) % (NA_ROWS * NA_STEP_BLOCKS) == 0 and seq // GRID_W >= NA_KEY_ROWS
    cos_t, sin_t = _rotary_tables(seq)
    x2 = x.reshape(t, d)
    for i in range(depth):
        rq, rk, rv, rg, nq, nke, nko, nve, nvo = _in_proj(x2, w_in[i], cos_t, sin_t, seq)
        ret, wd_b = _retention(rq, rk, rv, rg, ret_decay_fwd[i], ret_decay_bwd[i], ret_gn_gain[i],
                               w_expert_down[i], batch, seq)
        na, wgu_b = _natten(nq, nke, nko, nve, nvo, na_rpb[i], w_expert_gu[i], batch, seq)
        pre, x1b, w_t, rank_t, cnt = _mid(ret, na, x2, p[i].reshape(t, -1), w_out[i], ln1_gain[i], ln1_bias[i],
                                          w_router[i], router_bias[i], w_shared_gu[i], w_shared_down[i],
                                          w_ple_proj[i], w_ple_gate[i])
        x2 = _moe(x1b, rank_t, w_t, cnt[:, ::LANES], wgu_b, wd_b, pre, ln2_gain[i], ln2_bias[i])
    return x2.reshape(batch, seq, d)
```

```python
import functools

import numpy as np
import jax
import jax.numpy as jnp
from jax import lax
from jax.experimental import pallas as pl
from jax.experimental.pallas import tpu as pltpu

F32 = jnp.float32
BF16 = jnp.bfloat16
I32 = jnp.int32

D_MODEL = 1024
HEADS = 8
HEAD_DIM = 64
GROUP_W = HEADS * HEAD_DIM
ROPE_BASE = 10000.0
GN_EPS = 1e-6
LN_EPS = 1e-5
GRID_W = 64
NA_WIN_ROWS = 8
NA_WIN_COLS = 16
N_EXPERTS = 64
N_GROUPS = 8
GROUP_SIZE = N_EXPERTS // N_GROUPS
TOPK_GROUPS = 4
TOP_K = 8
EXPERT_DIM = 256
ROUTED_SCALE = 2.5
ALPHA = 2.0 ** 0.25
NEG_BIG = -1e30
LOG2E = 1.4426950408889634
NA_Q_SCALE = HEAD_DIM ** -0.5 * LOG2E

LANES = 128
VMEM_LIMIT_BYTES = 56 * 1024 * 1024

PROJ_TM = 1024
MID_TM = 1024
RET_CHUNK = 128
RET_UNROLL = 32
NA_ROWS = 4
NA_KEY_ROWS = NA_ROWS + NA_WIN_ROWS
NA_STEP_BLOCKS = 8
MOE_TM = 2048
MOE_SUB = 256
MOE_EB = 4
MOE_SLOTS = MOE_SUB // MOE_EB
MOE_SLOTS_SMALL = 48
MOE_NSUB = MOE_TM // MOE_SUB


def _cparams(sem):
    return pltpu.CompilerParams(dimension_semantics=sem, vmem_limit_bytes=VMEM_LIMIT_BYTES)


def _dot(a, b):
    return jnp.dot(a, b, preferred_element_type=F32)


def _dot_nt(a, b):
    return lax.dot_general(a, b, (((1,), (1,)), ((), ())), preferred_element_type=F32)


def _dot_tn(a, b):
    return lax.dot_general(a, b, (((0,), (0,)), ((), ())), preferred_element_type=F32)


def _cast_once(src_ref, dst_ref):
    @pl.when(pl.program_id(0) == 0)
    def _():
        dst_ref[...] = src_ref[...].astype(BF16)


def _resident(shape):
    return pl.BlockSpec(shape, lambda i: (0,) * len(shape), pipeline_mode=pl.Buffered(1))


def _in_proj_kernel(x_ref, wf_ref, cos_ref, sin_ref,
                    rq_ref, rk_ref, rv_ref, rg_ref, nq_ref, nke_ref, nko_ref, nve_ref, nvo_ref, w_ref):
    _cast_once(wf_ref, w_ref)
    xb = x_ref[...].astype(BF16)
    cos = cos_ref[...]
    sin = sin_ref[...]
    lane = lax.broadcasted_iota(I32, (1, LANES), 1)
    first_half = (lane % HEAD_DIM) < (HEAD_DIM // 2)

    def proj(g):
        return _dot(xb, w_ref[:, g * GROUP_W:(g + 1) * GROUP_W])

    def rotary(t, scale):
        outs = []
        for j in range(GROUP_W // LANES):
            c = t[:, j * LANES:(j + 1) * LANES]
            swapped = jnp.where(first_half,
                                pltpu.roll(c, LANES - HEAD_DIM // 2, axis=1),
                                pltpu.roll(c, HEAD_DIM // 2, axis=1))
            outs.append((c * cos + swapped * sin) * scale)
        return jnp.concatenate(outs, axis=1)

    rq_ref[...] = rotary(proj(0), 1.0).astype(BF16)
    rk_ref[...] = rotary(proj(1), HEAD_DIM ** -0.5).astype(BF16)
    rv_ref[...] = proj(2).astype(BF16)
    rg_ref[...] = jax.nn.silu(proj(3)).astype(BF16)
    nq_ref[...] = (proj(4) * NA_Q_SCALE).astype(BF16)
    even_head = (lax.broadcasted_iota(I32, (1, GROUP_W), 1) % (2 * HEAD_DIM)) < HEAD_DIM
    for g, even_ref, odd_ref in ((5, nke_ref, nko_ref), (6, nve_ref, nvo_ref)):
        t = proj(g)
        even_ref[...] = jnp.where(even_head, t, 0.0).astype(BF16)
        odd_ref[...] = jnp.where(even_head, 0.0, t).astype(BF16)


def _in_proj(x2, w_in, cos_t, sin_t, seq):
    t = x2.shape[0]
    tm = PROJ_TM
    n_pos = seq // tm
    out = jax.ShapeDtypeStruct((t, GROUP_W), BF16)
    tok = lambda i: (i, 0)
    return pl.pallas_call(
        _in_proj_kernel,
        grid=(t // tm,),
        in_specs=[
            pl.BlockSpec((tm, D_MODEL), tok),
            _resident(w_in.shape),
            pl.BlockSpec((tm, LANES), lambda i: (i % n_pos, 0)),
            pl.BlockSpec((tm, LANES), lambda i: (i % n_pos, 0)),
        ],
        out_specs=[pl.BlockSpec((tm, GROUP_W), tok)] * 9,
        out_shape=[out] * 9,
        scratch_shapes=[pltpu.VMEM(w_in.shape, BF16)],
        compiler_params=_cparams(("arbitrary",)),
        name="in_proj",
    )(x2, w_in.astype(F32), cos_t, sin_t)


def _log_sigmoid(x):
    return jnp.minimum(x, 0.0) - jnp.log1p(jnp.exp(-jnp.abs(x)))


def _retention_kernel(q_ref, k_ref, v_ref, g_ref, decf_ref, decb_ref,
                      gain_ref, wd_f, o_ref, wd_b, kv_ref, st_ref, dmat_ref):
    wd_b[...] = wd_f[...].astype(BF16)
    c = RET_CHUNK
    n_chunks = q_ref.shape[0] // c
    pair_w = 2 * HEAD_DIM

    h0 = 2 * pl.program_id(1)
    head0 = lax.broadcasted_iota(I32, (1, pair_w), 1) < HEAD_DIM
    lgf = _log_sigmoid(jnp.where(head0, decf_ref[h0], decf_ref[h0 + 1]))
    lgb = _log_sigmoid(jnp.where(head0, decb_ref[h0], decb_ref[h0 + 1]))
    row = lax.broadcasted_iota(I32, (c, 1), 0).astype(F32)
    k_dec_f = jnp.exp((c - 1.0 - row) * lgf)
    k_dec_b = jnp.exp(row * lgb)
    q_dec_f = jnp.exp((row + 1.0) * lgf)
    q_dec_b = jnp.exp((c - row) * lgb)
    chunk_dec_f = jnp.exp(c * lgf)
    chunk_dec_b = jnp.exp(c * lgb)

    r2 = lax.broadcasted_iota(I32, (pair_w, pair_w), 0) // HEAD_DIM
    c2 = lax.broadcasted_iota(I32, (pair_w, pair_w), 1) // HEAD_DIM
    same_head = r2 == c2
    block_diag = jnp.where(same_head, 1.0, 0.0)
    seg_avg = jnp.where(same_head, 1.0 / HEAD_DIM, 0.0).astype(BF16)

    head0_col = lax.broadcasted_iota(I32, (1, 2 * c), 1) < c
    lgf_d = _log_sigmoid(jnp.where(head0_col, decf_ref[h0], decf_ref[h0 + 1]))
    lgb_d = _log_sigmoid(jnp.where(head0_col, decb_ref[h0], decb_ref[h0 + 1]))
    di = lax.broadcasted_iota(I32, (c, 2 * c), 0)
    dj = lax.broadcasted_iota(I32, (c, 2 * c), 1) % c
    diff = (di - dj).astype(F32)
    dmat_ref[...] = jnp.where(diff >= 0.0, jnp.exp(diff * lgf_d), jnp.exp(-diff * lgb_d))

    def chunk(ref, n):
        return ref[pl.ds(pl.multiple_of(n * c, c), c), :]

    unroll = RET_UNROLL

    def summarize(nb, carry):
        for u in range(unroll):
            n = nb * unroll + u
            kf = chunk(k_ref, n).astype(F32)
            kst = jnp.concatenate([kf * k_dec_f, kf * k_dec_b], axis=1).astype(BF16)
            kv_ref[n] = _dot_tn(kst, chunk(v_ref, n))
        return carry

    lax.fori_loop(0, n_chunks // unroll, summarize, 0)

    def scan(i, states):
        fwd, bwd = states
        nb = n_chunks - 1 - i
        st_ref[i, 0:pair_w, :] = fwd.astype(BF16)
        st_ref[nb, pair_w:2 * pair_w, :] = bwd.astype(BF16)
        return (fwd * chunk_dec_f + kv_ref[i, 0:pair_w, :] * block_diag,
                bwd * chunk_dec_b + kv_ref[nb, pair_w:2 * pair_w, :] * block_diag)

    zero_state = jnp.zeros((pair_w, pair_w), F32)
    lax.fori_loop(0, n_chunks, scan, (zero_state, zero_state), unroll=True)

    gain = gain_ref[...]

    seg_avg2 = jnp.concatenate([seg_avg, seg_avg], axis=0)

    def seg_mean(z):
        hi = z.astype(BF16)
        lo = (z - hi.astype(F32)).astype(BF16)
        return _dot(jnp.concatenate([hi, lo], axis=1), seg_avg2)

    def emit(nb, carry):
        ys = []
        for u in range(unroll):
            n = nb * unroll + u
            q = chunk(q_ref, n)
            k = chunk(k_ref, n)
            v = chunk(v_ref, n)
            zero = jnp.zeros_like(k)
            k_st = jnp.concatenate([jnp.where(head0, k, zero), jnp.where(head0, zero, k)], axis=0)
            v_st = jnp.concatenate([jnp.where(head0, v, zero), jnp.where(head0, zero, v)], axis=0)
            scores = _dot_nt(q, k_st) * dmat_ref[...]
            qf = q.astype(F32)
            q_st = jnp.concatenate([qf * q_dec_f, qf * q_dec_b], axis=1).astype(BF16)
            ys.append(_dot(scores.astype(BF16), v_st) + _dot(q_st, st_ref[n]))
        y = jnp.concatenate(ys, axis=0)
        mu = seg_mean(y)
        d = y - mu
        var = seg_mean(d * d)
        yn = d * lax.rsqrt(var + GN_EPS) * gain
        rows = pl.ds(pl.multiple_of(nb * (unroll * c), unroll * c), unroll * c)
        o_ref[rows, :] = (g_ref[rows, :].astype(F32) * yn).astype(BF16)
        return carry

    lax.fori_loop(0, n_chunks // unroll, emit, 0)


def _retention(rq, rk, rv, rg, dec_f, dec_b, gain, w_down, batch, seq):
    t = rq.shape[0]
    c = RET_CHUNK
    n_pairs = HEADS // 2
    pair_w = 2 * HEAD_DIM
    n_exp = w_down.shape[0]
    assert n_exp % (batch * n_pairs) == 0
    epb = n_exp // (batch * n_pairs)
    wd_spec = pl.BlockSpec((epb,) + w_down.shape[1:], lambda b, p: (b * n_pairs + p, 0, 0))
    tok = pl.BlockSpec((seq, pair_w), lambda b, p: (b, p))
    lane_spec = pl.BlockSpec((1, pair_w), lambda b, p: (0, p))
    scalars = pl.BlockSpec(memory_space=pltpu.SMEM)
    return pl.pallas_call(
        _retention_kernel,
        grid=(batch, n_pairs),
        in_specs=[tok, tok, tok, tok, scalars, scalars, lane_spec, wd_spec],
        out_specs=[tok, wd_spec],
        out_shape=[jax.ShapeDtypeStruct((t, GROUP_W), BF16), jax.ShapeDtypeStruct(w_down.shape, BF16)],
        scratch_shapes=[
            pltpu.VMEM((seq // c, 2 * pair_w, pair_w), F32),
            pltpu.VMEM((seq // c, 2 * pair_w, pair_w), BF16),
            pltpu.VMEM((c, 2 * c), F32),
        ],
        compiler_params=_cparams(("parallel", "parallel")),
        name="retention",
    )(rq, rk, rv, rg, dec_f.astype(F32), dec_b.astype(F32), gain.astype(F32)[None, :], w_down.astype(F32))


N_ROW_OFFSETS = 2 * NA_WIN_ROWS - 1
N_COL_OFFSETS = 2 * NA_WIN_COLS - 1


def _natten_row_offsets(rows):
    n_blocks = rows // NA_ROWS
    starts = {0: 0, 1: NA_ROWS - NA_WIN_ROWS // 2, 2: rows - NA_KEY_ROWS}
    blocks = {0: 0, 1: 1, 2: n_blocks - 1}
    table = []
    for v in range(3):
        per_a = []
        for a in range(NA_ROWS):
            r = blocks[v] * NA_ROWS + a
            rs = min(max(r - NA_WIN_ROWS // 2, 0), rows - NA_WIN_ROWS)
            per_kl = []
            for kl in range(NA_KEY_ROWS):
                kr = starts[v] + kl
                per_kl.append(kr - r + NA_WIN_ROWS - 1 if rs <= kr < rs + NA_WIN_ROWS else None)
            per_a.append(per_kl)
        table.append(per_a)
    return table


def _natten_live_columns(rows):
    offsets = _natten_row_offsets(rows)
    live = []
    for blk in range(NA_STEP_BLOCKS):
        variants = [1] + ([0] if blk == 0 else []) + ([2] if blk == NA_STEP_BLOCKS - 1 else [])
        live.append([[j for j in range(NA_KEY_ROWS // 2)
                      if any(offsets[v][a][kl] is not None for v in variants for kl in (2 * j, 2 * j + 1))]
                     for a in range(NA_ROWS)])
    return live


def _natten_build_bias(rpb_ref, tab_ref, pair, rows):
    offsets = _natten_row_offsets(rows)
    nk = NA_KEY_ROWS * GRID_W
    shape = (GRID_W, LANES)
    lane = lax.broadcasted_iota(I32, shape, 1)
    c = lax.broadcasted_iota(I32, shape, 0)
    second = lane >= GRID_W
    kc = lane % GRID_W
    cs = jnp.clip(c - NA_WIN_COLS // 2, 0, GRID_W - NA_WIN_COLS)
    col_ok = jnp.logical_and(kc >= cs, kc < cs + NA_WIN_COLS)
    neg = jnp.full(shape, NEG_BIG, F32)
    for hh in range(2):
        toeplitz = []
        for dr in range(N_ROW_OFFSETS):
            x = jnp.broadcast_to(rpb_ref[2 * pair + hh, dr:dr + 1, :] * LOG2E, shape)
            lo = pltpu.roll(x, LANES - (NA_WIN_COLS - 1), axis=1, stride=1, stride_axis=0)
            hi = pltpu.roll(x, GRID_W - (NA_WIN_COLS - 1), axis=1, stride=1, stride_axis=0)
            toeplitz.append(jnp.where(second, hi, lo))
        for v in range(3):
            for a in range(NA_ROWS):
                for j in range(NA_KEY_ROWS // 2):
                    d0, d1 = offsets[v][a][2 * j], offsets[v][a][2 * j + 1]
                    if d0 is None and d1 is None:
                        piece = neg
                    else:
                        t0 = neg if d0 is None else toeplitz[d0]
                        t1 = neg if d1 is None else toeplitz[d1]
                        piece = jnp.where(col_ok, jnp.where(second, t1, t0), neg)
                    tab_ref[v, pair, a * GRID_W:(a + 1) * GRID_W,
                            hh * nk + j * LANES:hh * nk + (j + 1) * LANES] = piece


def _natten_kernel(q_ref, ke_ref, ko_ref, ve_ref, vo_ref, rpb_ref, wgu_f, o_ref, wgu_b, tab_ref, *, rows):
    pair_id = pl.program_id(1)

    @pl.when(jnp.logical_and(pl.program_id(0) == 0, jnp.logical_and(pair_id == 0, pl.program_id(2) == 0)))
    def _():
        for pair in range(HEADS // 2):
            _natten_build_bias(rpb_ref, tab_ref, pair, rows)

    wgu_b[...] = wgu_f[...].astype(BF16)
    nq = NA_ROWS * GRID_W
    nk = NA_KEY_ROWS * GRID_W
    n_blocks = rows // NA_ROWS
    live = _natten_live_columns(rows)
    head0 = lax.broadcasted_iota(I32, (1, 2 * HEAD_DIM), 1) < HEAD_DIM
    ind0 = jnp.broadcast_to(jnp.where(head0, 1.0, 0.0).astype(BF16), (nk, 2 * HEAD_DIM))
    ind1 = jnp.broadcast_to(jnp.where(head0, 0.0, 1.0).astype(BF16), (nk, 2 * HEAD_DIM))

    for blk in range(NA_STEP_BLOCKS):
        rb = pl.program_id(2) * NA_STEP_BLOCKS + blk
        variant = jnp.where(rb == 0, 0, jnp.where(rb == n_blocks - 1, 2, 1))
        start_row = jnp.clip(rb * NA_ROWS - NA_WIN_ROWS // 2, 0, rows - NA_KEY_ROWS)
        win = pl.ds(pl.multiple_of(start_row * GRID_W, GRID_W), nk)
        q = q_ref[blk * nq:(blk + 1) * nq, :]
        k_st = jnp.concatenate([ke_ref[win, :], ko_ref[win, :]], axis=0)
        v_st = jnp.concatenate([jnp.concatenate([ve_ref[win, :], ind0], axis=1),
                                jnp.concatenate([vo_ref[win, :], ind1], axis=1)], axis=0)
        s = _dot_nt(q, k_st) + tab_ref[variant, pair_id]
        dead = jnp.zeros((GRID_W, LANES), BF16)
        slabs = []
        for a in range(NA_ROWS):
            qrows = slice(a * GRID_W, (a + 1) * GRID_W)
            parts = []
            for h in range(2):
                cols = {j: s[qrows, h * nk + j * LANES:h * nk + (j + 1) * LANES] for j in live[blk][a]}
                m = jnp.max(jnp.concatenate(list(cols.values()), axis=1), axis=1, keepdims=True)
                parts += [jnp.exp2(cols[j] - m).astype(BF16) if j in cols else dead
                          for j in range(NA_KEY_ROWS // 2)]
            slabs.append(jnp.concatenate(parts, axis=1))
        out = _dot(jnp.concatenate(slabs, axis=0), v_st)
        o_ref[blk * nq:(blk + 1) * nq, :] = (out[:, :2 * HEAD_DIM] / out[:, 2 * HEAD_DIM:]).astype(BF16)


def _natten(nq, nke, nko, nve, nvo, rpb, w_gu, batch, seq):
    t = nq.shape[0]
    rows = seq // GRID_W
    n_pairs = HEADS // 2
    n_steps = rows // (NA_ROWS * NA_STEP_BLOCKS)
    nqb = NA_ROWS * GRID_W
    nkb = NA_KEY_ROWS * GRID_W
    rpb_pad = jnp.pad(rpb.astype(F32), ((0, 0), (0, 0), (0, LANES - N_COL_OFFSETS)))
    total_steps = batch * n_pairs * n_steps
    n_exp = w_gu.shape[0]
    assert n_exp % total_steps == 0
    epb = n_exp // total_steps

    def expert_spec(shape):
        return pl.BlockSpec((epb,) + shape[1:], lambda b, p, s: ((b * n_pairs + p) * n_steps + s, 0, 0))

    rpb_spec = pl.BlockSpec(rpb_pad.shape, lambda b, p, s: (0, 0, 0))
    kv_spec = pl.BlockSpec((seq, 2 * HEAD_DIM), lambda b, p, s: (b, p))
    q_spec = pl.BlockSpec((NA_STEP_BLOCKS * nqb, 2 * HEAD_DIM), lambda b, p, s: (b * n_steps + s, p))
    return pl.pallas_call(
        functools.partial(_natten_kernel, rows=rows),
        grid=(batch, n_pairs, n_steps),
        in_specs=[q_spec, kv_spec, kv_spec, kv_spec, kv_spec, rpb_spec, expert_spec(w_gu.shape)],
        out_specs=[q_spec, expert_spec(w_gu.shape)],
        out_shape=[jax.ShapeDtypeStruct((t, GROUP_W), BF16), jax.ShapeDtypeStruct(w_gu.shape, BF16)],
        scratch_shapes=[pltpu.VMEM((3, n_pairs, nqb, 2 * nkb), F32)],
        compiler_params=_cparams(("arbitrary", "arbitrary", "arbitrary")),
        name="natten",
    )(nq, nke, nko, nve, nvo, rpb_pad, w_gu.astype(F32))


def _layer_norm(h, gain, bias):
    mu = jnp.mean(h, axis=-1, keepdims=True)
    d = h - mu
    var = jnp.mean(d * d, axis=-1, keepdims=True)
    return d * lax.rsqrt(var + LN_EPS) * gain + bias


def _route_tile(xb, wr, rbias, tri):
    tm = xb.shape[0]
    scores = jax.nn.sigmoid(_dot_nt(wr, xb))
    biased = scores + rbias
    sub = lax.broadcasted_iota(I32, (GROUP_SIZE, tm), 0).astype(F32)
    none = float(N_EXPERTS)
    ninf = -jnp.inf

    def first_max(vals, index):
        m = jnp.max(vals, axis=0, keepdims=True)
        return m, jnp.min(jnp.where(vals == m, index, none), axis=0, keepdims=True)

    groups = [biased[g * GROUP_SIZE:(g + 1) * GROUP_SIZE, :] for g in range(N_GROUPS)]
    group_scores = []
    for g in range(N_GROUPS):
        m1, i1 = first_max(groups[g], sub)
        m2 = jnp.max(jnp.where(sub == i1, ninf, groups[g]), axis=0, keepdims=True)
        group_scores.append(m1 + m2)
    cur = jnp.concatenate(group_scores, axis=0)
    group_sel = jnp.zeros(cur.shape, F32)
    for _ in range(TOPK_GROUPS):
        _, i1 = first_max(cur, sub)
        hit = sub == i1
        group_sel = jnp.where(hit, 1.0, group_sel)
        cur = jnp.where(hit, ninf, cur)

    masked = [jnp.where(group_sel[g:g + 1, :] > 0.5, groups[g], ninf) for g in range(N_GROUPS)]
    ids = [sub + float(g * GROUP_SIZE) for g in range(N_GROUPS)]
    chosen = [jnp.zeros((GROUP_SIZE, tm), F32) for _ in range(N_GROUPS)]
    for _ in range(TOP_K):
        m = masked[0]
        for g in range(1, N_GROUPS):
            m = jnp.maximum(m, masked[g])
        m = jnp.max(m, axis=0, keepdims=True)
        cand = jnp.where(masked[0] == m, ids[0], none)
        for g in range(1, N_GROUPS):
            cand = jnp.minimum(cand, jnp.where(masked[g] == m, ids[g], none))
        first = jnp.min(cand, axis=0, keepdims=True)
        for g in range(N_GROUPS):
            hit = ids[g] == first
            chosen[g] = jnp.where(hit, 1.0, chosen[g])
            masked[g] = jnp.where(hit, ninf, masked[g])

    sel = jnp.concatenate(chosen, axis=0) > 0.5
    picked = jnp.where(sel, scores, 0.0)
    total = jnp.sum(picked, axis=0, keepdims=True)
    weight = picked / total * ROUTED_SCALE
    sel_f = jnp.where(sel, 1.0, 0.0)
    sel_b = sel_f.astype(BF16)
    before, cnt_max = [], None
    for k in range(tm // MOE_SUB):
        cols = slice(k * MOE_SUB, (k + 1) * MOE_SUB)
        before.append(_dot(sel_b[:, cols], tri))
        cnt = jnp.sum(sel_f[:, cols], axis=1, keepdims=True)
        cnt_max = cnt if cnt_max is None else jnp.maximum(cnt_max, cnt)
    rank = jnp.where(sel, jnp.concatenate(before, axis=1).astype(I32), -1)
    return weight, rank, cnt_max


def _mid_kernel(ret_ref, na_ref, x_ref, p_ref, wo_f, wsgu_f, wsd_f, wp_f, wg_f, gain_ref, bias_ref,
                wr_ref, rb_ref, pre_ref, x1b_ref, w_ref, rank_ref, cnt_ref,
                wo_ref, wsgu_ref, wsd_ref, wp_ref, wg_ref, tri_ref):
    for src_ref, dst_ref in ((wo_f, wo_ref), (wsgu_f, wsgu_ref), (wsd_f, wsd_ref), (wp_f, wp_ref),
                             (wg_f, wg_ref)):
        _cast_once(src_ref, dst_ref)

    @pl.when(pl.program_id(0) == 0)
    def _():
        i = lax.broadcasted_iota(I32, (MOE_SUB, MOE_SUB), 0)
        j = lax.broadcasted_iota(I32, (MOE_SUB, MOE_SUB), 1)
        tri_ref[...] = jnp.where(i < j, 1.0, 0.0).astype(BF16)

    sub_blocks = [slice(r, r + MOE_SUB) for r in range(0, x_ref.shape[0], MOE_SUB)]
    for rows in sub_blocks:
        mix = (_dot(ret_ref[rows, :], wo_ref[0:GROUP_W, :])
               + _dot(na_ref[rows, :], wo_ref[GROUP_W:2 * GROUP_W, :]))
        x1 = _layer_norm(ALPHA * x_ref[rows, :] + mix, gain_ref[...], bias_ref[...])
        x1b_ref[rows, :] = x1.astype(BF16)
        pre_ref[rows, :] = ALPHA * x1
    weight, rank, cnt_max = _route_tile(x1b_ref[...], wr_ref[...], rb_ref[...], tri_ref[...])
    w_ref[...] = weight
    rank_ref[...] = rank
    cnt_ref[...] = jnp.broadcast_to(cnt_max, cnt_ref.shape).astype(I32)
    for rows in sub_blocks:
        xb = x1b_ref[rows, :]
        h = _dot(xb, wsgu_ref[...])
        act = (jax.nn.silu(h[:, :EXPERT_DIM]) * h[:, EXPERT_DIM:]).astype(BF16)
        shared = _dot(act, wsd_ref[...])
        ple = _dot(p_ref[rows, :].astype(BF16), wp_ref[...]) * jax.nn.sigmoid(_dot(xb, wg_ref[...]))
        pre_ref[rows, :] += shared + ple


def _mid(ret, na, x2, p2, w_out, gain, bias, w_router, router_bias, wsgu, wsd, wp, wg):
    t = x2.shape[0]
    tm = MID_TM
    nt = t // tm
    tok = lambda i: (i, 0)
    const = lambda i: (0, 0)
    col = lambda i: (0, i)
    weights = [w.astype(F32) for w in (w_out, wsgu, wsd, wp, wg)]
    wr_t = w_router.astype(F32).T.astype(BF16)
    return pl.pallas_call(
        _mid_kernel,
        grid=(nt,),
        in_specs=[pl.BlockSpec((tm, GROUP_W), tok), pl.BlockSpec((tm, GROUP_W), tok),
                  pl.BlockSpec((tm, D_MODEL), tok), pl.BlockSpec((tm, p2.shape[1]), tok)]
                 + [_resident(w.shape) for w in weights]
                 + [pl.BlockSpec((1, D_MODEL), const), pl.BlockSpec((1, D_MODEL), const),
                    pl.BlockSpec((N_EXPERTS, D_MODEL), const), pl.BlockSpec((N_EXPERTS, 1), const)],
        out_specs=[pl.BlockSpec((tm, D_MODEL), tok), pl.BlockSpec((tm, D_MODEL), tok),
                   pl.BlockSpec((N_EXPERTS, tm), col), pl.BlockSpec((N_EXPERTS, tm), col),
                   pl.BlockSpec((N_EXPERTS, LANES), col)],
        out_shape=[jax.ShapeDtypeStruct((t, D_MODEL), F32), jax.ShapeDtypeStruct((t, D_MODEL), BF16),
                   jax.ShapeDtypeStruct((N_EXPERTS, t), F32), jax.ShapeDtypeStruct((N_EXPERTS, t), I32),
                   jax.ShapeDtypeStruct((N_EXPERTS, nt * LANES), I32)],
        scratch_shapes=[pltpu.VMEM(w.shape, BF16) for w in weights] + [pltpu.VMEM((MOE_SUB, MOE_SUB), BF16)],
        compiler_params=_cparams(("arbitrary",)),
        name="mid",
    )(ret, na, x2, p2, *weights, gain.astype(F32)[None, :], bias.astype(F32)[None, :],
      wr_t, router_bias.astype(F32)[:, None])


def _moe_kernel(cnt_ref, x_ref, rank_ref, w_ref, wgu_ref, wd_ref, pre_ref, gain_ref, bias_ref,
                o_ref, oh_ref, g_ref, y_ref):
    i = pl.program_id(0)
    eb = pl.program_id(1)
    n_eb = pl.num_programs(1)

    @pl.when(eb == 0)
    def _():
        o_ref[...] = jnp.zeros_like(o_ref)

    pre_rows = pre_ref.shape[0]
    o_ref[pl.ds(pl.multiple_of(eb * pre_rows, pre_rows), pre_rows), :] += pre_ref[...]

    max_count = cnt_ref[eb, i]

    def one_pass(p, slots):
        n_rows = MOE_EB * slots
        base = p * slots
        slot = lax.broadcasted_iota(I32, (slots, MOE_SUB), 0)
        slot_w = []
        for k in range(MOE_NSUB):
            tok = slice(k * MOE_SUB, (k + 1) * MOE_SUB)
            blocks, weights = [], []
            for j in range(MOE_EB):
                expert = pl.ds(eb * MOE_EB + j, 1)
                match = slot == (rank_ref[expert, tok] - base)
                blocks.append(jnp.where(match, 1.0, 0.0).astype(BF16))
                weights.append(jnp.sum(jnp.where(match, w_ref[expert, tok], 0.0), axis=1, keepdims=True))
            onehot = jnp.concatenate(blocks, axis=0)
            oh_ref[k, 0:n_rows, :] = onehot
            slot_w.append(weights)
            g_ref[k, 0:n_rows, :] = _dot(onehot, x_ref[tok, :]).astype(BF16)
        for j in range(MOE_EB):
            rows = slice(j * slots, (j + 1) * slots)
            xe = jnp.concatenate([g_ref[k, rows, :] for k in range(MOE_NSUB)], axis=0)
            h = _dot(xe, wgu_ref[j])
            act = (jax.nn.silu(h[:, :EXPERT_DIM]) * h[:, EXPERT_DIM:]).astype(BF16)
            wc = jnp.concatenate([slot_w[k][j] for k in range(MOE_NSUB)], axis=0)
            yw = (_dot(act, wd_ref[j]) * wc).astype(BF16)
            for k in range(MOE_NSUB):
                y_ref[k, rows, :] = yw[k * slots:(k + 1) * slots, :]
        for k in range(MOE_NSUB):
            tok = slice(k * MOE_SUB, (k + 1) * MOE_SUB)
            o_ref[tok, :] += _dot_tn(oh_ref[k, 0:n_rows, :], y_ref[k, 0:n_rows, :])

    @pl.when(max_count <= MOE_SLOTS_SMALL)
    def _():
        one_pass(0, MOE_SLOTS_SMALL)

    @pl.when(max_count > MOE_SLOTS_SMALL)
    def _():
        n_pass = lax.shift_right_logical(max_count + (MOE_SLOTS - 1), int(np.log2(MOE_SLOTS)))

        def body(p, carry):
            one_pass(p, MOE_SLOTS)
            return carry

        lax.fori_loop(0, n_pass, body, 0)

    @pl.when(eb == n_eb - 1)
    def _():
        for r in range(0, o_ref.shape[0], MOE_SUB):
            rows = slice(r, r + MOE_SUB)
            o_ref[rows, :] = _layer_norm(o_ref[rows, :], gain_ref[...], bias_ref[...])


def _moe(x1b, rank_t, w_t, counts, wgu_b, wd_b, pre, gain, bias):
    t = x1b.shape[0]
    tm = MOE_TM
    nt = t // tm
    n_eb = N_EXPERTS // MOE_EB
    pre_rows = tm // n_eb
    counts = counts.reshape(N_EXPERTS // MOE_EB, MOE_EB, nt, tm // MID_TM).max(axis=(1, 3))
    grid_spec = pltpu.PrefetchScalarGridSpec(
        num_scalar_prefetch=1,
        grid=(nt, N_EXPERTS // MOE_EB),
        in_specs=[
            pl.BlockSpec((tm, D_MODEL), lambda i, e, c: (i, 0)),
            pl.BlockSpec((N_EXPERTS, tm), lambda i, e, c: (0, i)),
            pl.BlockSpec((N_EXPERTS, tm), lambda i, e, c: (0, i)),
            pl.BlockSpec((MOE_EB, D_MODEL, 2 * EXPERT_DIM), lambda i, e, c: (e, 0, 0)),
            pl.BlockSpec((MOE_EB, EXPERT_DIM, D_MODEL), lambda i, e, c: (e, 0, 0)),
            pl.BlockSpec((pre_rows, D_MODEL), lambda i, e, c: (i * n_eb + e, 0)),
            pl.BlockSpec((1, D_MODEL), lambda i, e, c: (0, 0)),
            pl.BlockSpec((1, D_MODEL), lambda i, e, c: (0, 0)),
        ],
        out_specs=pl.BlockSpec((tm, D_MODEL), lambda i, e, c: (i, 0)),
        scratch_shapes=[pltpu.VMEM((MOE_NSUB, MOE_SUB, MOE_SUB), BF16),
                        pltpu.VMEM((MOE_NSUB, MOE_SUB, D_MODEL), BF16),
                        pltpu.VMEM((MOE_NSUB, MOE_SUB, D_MODEL), BF16)],
    )
    return pl.pallas_call(
        _moe_kernel,
        grid_spec=grid_spec,
        out_shape=jax.ShapeDtypeStruct((t, D_MODEL), F32),
        compiler_params=_cparams(("parallel", "arbitrary")),
        name="moe",
    )(counts, x1b, rank_t, w_t, wgu_b, wd_b, pre, gain.astype(F32)[None, :], bias.astype(F32)[None, :])


def _rotary_tables(seq):
    half = HEAD_DIM // 2
    inv = np.float32(ROPE_BASE) ** (-np.arange(half, dtype=np.float32) / np.float32(half))
    ang = np.arange(seq, dtype=np.float32)[:, None] * inv[None, :]
    cos, sin = np.cos(ang), np.sin(ang)
    reps = LANES // HEAD_DIM
    cos_t = np.tile(np.concatenate([cos, cos], axis=1), (1, reps))
    sin_t = np.tile(np.concatenate([-sin, sin], axis=1), (1, reps))
    return jnp.asarray(cos_t, F32), jnp.asarray(sin_t, F32)


def kernel(x, p, w_in, ret_decay_fwd, ret_decay_bwd, ret_gn_gain, na_rpb, w_out, ln1_gain, ln1_bias,
           w_router, router_bias, w_expert_gu, w_expert_down, w_shared_gu, w_shared_down,
           w_ple_proj, w_ple_gate, ln2_gain, ln2_bias):
    batch, seq, d = x.shape
    t = batch * seq
    depth = w_in.shape[0]
    assert depth == 1 and d == D_MODEL
    assert seq % PROJ_TM == 0 and seq % MOE_TM == 0 and seq % (RET_CHUNK * RET_UNROLL) == 0
    assert MOE_TM % (N_EXPERTS // MOE_EB) == 0
    assert seq % MID_TM == 0 and MOE_TM % MID_TM == 0 and MID_TM % MOE_SUB == 0
    assert (seq // GRID_W) % (NA_ROWS * NA_STEP_BLOCKS) == 0 and seq // GRID_W >= NA_KEY_ROWS
    cos_t, sin_t = _rotary_tables(seq)
    x2 = x.reshape(t, d)
    for i in range(depth):
        rq, rk, rv, rg, nq, nke, nko, nve, nvo = _in_proj(x2, w_in[i], cos_t, sin_t, seq)
        ret, wgu_b = _retention(rq, rk, rv, rg, ret_decay_fwd[i], ret_decay_bwd[i], ret_gn_gain[i],
                                w_expert_gu[i], batch, seq)
        na, wd_b = _natten(nq, nke, nko, nve, nvo, na_rpb[i], w_expert_down[i], batch, seq)
        pre, x1b, w_t, rank_t, cnt = _mid(ret, na, x2, p[i].reshape(t, -1), w_out[i], ln1_gain[i], ln1_bias[i],
                                          w_router[i], router_bias[i], w_shared_gu[i], w_shared_down[i],
                                          w_ple_proj[i], w_ple_gate[i])
        x2 = _moe(x1b, rank_t, w_t, cnt[:, ::LANES], wgu_b, wd_b, pre, ln2_gain[i], ln2_bias[i])
    return x2.reshape(batch, seq, d)
```
